```python
import math
import jax
import jax.numpy as jnp
from jax import lax
import numpy as np


D_MODEL = 1024
BATCH = 4
SEQ = 8192
DEPTH = 1
DEC_BATCH = 128
DEC_SEQ = 4
PAST_LEN = 16384
PAGE_SIZE = 128

EPS = 1e-6
ROPE_THETA = 10000.0
N_MEM = 256
WINDOW = 128
A_HD = 64
A_HEADS = D_MODEL // 128
A_KV = A_HEADS // 4
A_GROUP = A_HEADS // A_KV
A_SCALE = A_HD ** -0.5
B_HEADS = D_MODEL // 256
B_DK = 128
B_DV = 128
CONV_W = 4
GDN_CHUNK = 64
B_CONV_CH = B_HEADS * (2 * B_DK + B_DV)
C_HEADS = 4
C_HD = 128
N_BRANCH = 3
BRANCH_W = A_HEADS * A_HD
D_FF = ((8 * D_MODEL + 3 * 256 - 1) // (3 * 256)) * 256

kernel_name = 'hybrid_swa_sink_gdn_memxattn_step'


def _in_sizes():
    return [A_HEADS * A_HD, A_KV * A_HD, A_KV * A_HD,
            B_CONV_CH,
            B_HEADS, B_HEADS,
            B_HEADS * B_DV,
            C_HEADS * C_HD,
            N_BRANCH * D_MODEL]


def _split_points():
    return np.cumsum(_in_sizes())[:-1].tolist()


def _rms_norm(x, gain):
    xf = x.astype(jnp.float32)
    y = xf * lax.rsqrt(jnp.mean(xf * xf, axis=-1, keepdims=True) + EPS)
    return (y * gain.astype(jnp.float32)).astype(x.dtype)


def _l2norm(x):
    return x * lax.rsqrt(jnp.sum(x * x, axis=-1, keepdims=True) + EPS)


def _rope(x, pos):
    half = x.shape[-1] // 2
    inv = ROPE_THETA ** (-jnp.arange(half, dtype=jnp.float32) / half)
    ang = pos.astype(jnp.float32)[:, None] * inv[None, :]
    cos = jnp.cos(ang)[:, None, :]
    sin = jnp.sin(ang)[:, None, :]
    xf = x.astype(jnp.float32)
    x1, x2 = xf[..., :half], xf[..., half:]
    return jnp.concatenate([x1 * cos - x2 * sin, x2 * cos + x1 * sin], axis=-1).astype(x.dtype)


def _window_mask(qpos, kpos):
    d = qpos - kpos
    return (d >= 0) & (d < WINDOW) & (kpos >= 0)


def _sink_softmax(logits, sink):
    s = jnp.broadcast_to(sink.astype(jnp.float32).reshape(A_KV, A_GROUP, 1, 1), logits.shape[:-1] + (1,))
    p = jax.nn.softmax(jnp.concatenate([logits, s], axis=-1), axis=-1)
    return p[..., :-1]


def _swa_prompt(q, k, v, sink):
    Bsz, L = q.shape[:2]
    nb = L // WINDOW
    qb = q.reshape(Bsz, nb, WINDOW, A_KV, A_GROUP, A_HD)
    kb = k.reshape(Bsz, nb, WINDOW, A_KV, A_HD)
    vb = v.reshape(Bsz, nb, WINDOW, A_KV, A_HD)

    def with_prev(t):
        prev = jnp.concatenate([jnp.zeros_like(t[:, :1]), t[:, :-1]], axis=1)
        return jnp.concatenate([prev, t], axis=2)

    kk, vv = with_prev(kb), with_prev(vb)
    logits = jnp.einsum('bnqkgd,bnskd->bnkgqs', qb, kk, preferred_element_type=jnp.float32) * A_SCALE
    blk = jnp.arange(nb, dtype=jnp.int32)[:, None] * WINDOW
    qpos = blk + jnp.arange(WINDOW, dtype=jnp.int32)[None, :]
    kpos = blk - WINDOW + jnp.arange(2 * WINDOW, dtype=jnp.int32)[None, :]
    mask = _window_mask(qpos[:, :, None], kpos[:, None, :])
    logits = jnp.where(mask[None, :, None, None], logits, -jnp.inf)
    p = _sink_softmax(logits, sink)
    o = jnp.einsum('bnkgqs,bnskd->bnqkgd', p.astype(vv.dtype), vv)
    return o.reshape(Bsz, L, A_HEADS * A_HD)


def _swa_decode(q, k, v, win_k, win_v, sink, pos):
    Bsz, L = q.shape[:2]
    kk = jnp.concatenate([win_k.astype(k.dtype), k], axis=1)
    vv = jnp.concatenate([win_v.astype(v.dtype), v], axis=1)
    qg = q.reshape(Bsz, L, A_KV, A_GROUP, A_HD)
    logits = jnp.einsum('bqkgd,bskd->bkgqs', qg, kk, preferred_element_type=jnp.float32) * A_SCALE
    kpos = jnp.concatenate([pos[0] - WINDOW + jnp.arange(WINDOW, dtype=jnp.int32), pos])
    mask = _window_mask(pos[:, None], kpos[None, :])
    logits = jnp.where(mask, logits, -jnp.inf)
    p = _sink_softmax(logits, sink)
    o = jnp.einsum('bkgqs,bskd->bqkgd', p.astype(vv.dtype), vv)
    return o.reshape(Bsz, L, A_HEADS * A_HD), kk[:, -WINDOW:], vv[:, -WINDOW:]


def _gated_delta_rule(q, k, v, g, beta, state):
    Bsz, L, H, DK = q.shape
    DV = v.shape[-1]
    C = min(GDN_CHUNK, L)
    pad = (-L) % C
    N = (L + pad) // C

    def chunks(t):
        t = jnp.pad(t, [(0, 0), (0, pad)] + [(0, 0)] * (t.ndim - 2))
        t = t.reshape((Bsz, N, C) + t.shape[2:])
        return jnp.moveaxis(jnp.moveaxis(t, 1, 0), 3, 2)

    qc, kc, vc, gc, bc = chunks(q), chunks(k), chunks(v), chunks(g), chunks(beta)
    gcum = jnp.cumsum(gc, axis=-1)
    tri = jnp.tril(jnp.ones((C, C), dtype=bool))
    strict = jnp.tril(jnp.ones((C, C), dtype=bool), -1)
    decay = jnp.exp(jnp.where(tri, gcum[..., :, None] - gcum[..., None, :], -jnp.inf))
    kb = kc * bc[..., None]
    a = jnp.where(strict, jnp.einsum('nbhid,nbhjd->nbhij', kb, kc) * decay, 0.0)
    eye = jnp.eye(C, dtype=jnp.float32)
    rhs = jnp.concatenate([vc * bc[..., None], kb * jnp.exp(gcum)[..., None]], axis=-1)
    sol = lax.linalg.triangular_solve(a + eye, rhs, left_side=True, lower=True, unit_diagonal=True)
    u, w = sol[..., :DV], sol[..., DV:]
    qk = jnp.einsum('nbhid,nbhjd->nbhij', qc, kc) * decay

    def step(S, inp):
        q_i, k_i, u_i, w_i, qk_i, g_i = inp
        v_new = u_i - jnp.einsum('bhck,bhkv->bhcv', w_i, S)
        o_i = (jnp.einsum('bhck,bhkv->bhcv', q_i * jnp.exp(g_i)[..., None], S)
               + jnp.einsum('bhij,bhjv->bhiv', qk_i, v_new))
        g_last = g_i[..., -1]
        S = (S * jnp.exp(g_last)[..., None, None]
             + jnp.einsum('bhck,bhcv->bhkv', k_i * jnp.exp(g_last[..., None] - g_i)[..., None], v_new))
        return S, o_i

    S, o = lax.scan(step, state, (qc, kc, u, w, qk, gcum))
    o = jnp.swapaxes(jnp.moveaxis(o, 0, 1), 2, 3).reshape(Bsz, N * C, H, DV)[:, :L]
    return o, S


def _gdn_branch(qkv_raw, b_raw, a_raw, z, conv_hist, rec_state, conv_w, a_log, dt_bias, norm_w):
    Bsz, L, _ = qkv_raw.shape
    xx = jnp.concatenate([conv_hist.astype(qkv_raw.dtype), qkv_raw], axis=1)
    conv = sum(xx[:, i:i + L] * conv_w[i].astype(qkv_raw.dtype) for i in range(CONV_W))
    qkv = jax.nn.silu(conv).astype(jnp.float32)
    q, k, v = jnp.split(qkv, [B_HEADS * B_DK, 2 * B_HEADS * B_DK], axis=-1)
    q = _l2norm(q.reshape(Bsz, L, B_HEADS, B_DK)) * (B_DK ** -0.5)
    k = _l2norm(k.reshape(Bsz, L, B_HEADS, B_DK))
    v = v.reshape(Bsz, L, B_HEADS, B_DV)
    beta = jax.nn.sigmoid(b_raw.astype(jnp.float32))
    g = -jnp.exp(a_log.astype(jnp.float32)) * jax.nn.softplus(a_raw.astype(jnp.float32) + dt_bias.astype(jnp.float32))
    o, S = _gated_delta_rule(q, k, v, g, beta, rec_state.astype(jnp.float32))
    o = _rms_norm(o, norm_w) * jax.nn.silu(z.astype(jnp.float32).reshape(Bsz, L, B_HEADS, B_DV))
    return o.reshape(Bsz, L, B_HEADS * B_DV).astype(qkv_raw.dtype), xx[:, -(CONV_W - 1):], S


def _mem_kv(mem, ln_mem, w_mem_kv):
    Bsz, M, _ = mem.shape
    k, v = jnp.split(_rms_norm(mem, ln_mem) @ w_mem_kv, 2, axis=-1)
    return k.reshape(Bsz, M, C_HEADS, C_HD), v.reshape(Bsz, M, C_HEADS, C_HD)


def _mem_attend(q, mk, mv):
    logits = jnp.einsum('bqhd,bshd->bhqs', q, mk.astype(q.dtype), preferred_element_type=jnp.float32) * (C_HD ** -0.5)
    p = jax.nn.softmax(logits, axis=-1)
    o = jnp.einsum('bhqs,bshd->bqhd', p.astype(q.dtype), mv.astype(q.dtype))
    return o.reshape(q.shape[0], q.shape[1], C_HEADS * C_HD)


def _layer(x, pos, swa_fn, conv_hist, rec_state, mem_k, mem_v, p):
    Bsz, L, _ = x.shape
    h = _rms_norm(x, p['ln_mix_pre'])
    qa, ka, va, qkvb, bb, ab, zb, qc, gates = jnp.split(h @ p['w_in'], _split_points(), axis=-1)
    qa = _rope(qa.reshape(Bsz, L, A_HEADS, A_HD), pos)
    ka = _rope(ka.reshape(Bsz, L, A_KV, A_HD), pos)
    va = va.reshape(Bsz, L, A_KV, A_HD)
    oa, new_wk, new_wv = swa_fn(qa, ka, va)
    ob, new_conv, new_rec = _gdn_branch(qkvb, bb, ab, zb, conv_hist, rec_state, p['gdn_conv_w'],
                                        p['gdn_a_log'], p['gdn_dt_bias'], p['gdn_norm_w'])
    oc = _mem_attend(qc.reshape(Bsz, L, C_HEADS, C_HD), mem_k, mem_v)
    branches = jnp.stack([oa.astype(x.dtype), ob.astype(x.dtype), oc.astype(x.dtype)], axis=2)
    up = jnp.einsum('blnc,ncd->blnd', branches, p['w_branch'])
    gate = jax.nn.sigmoid(gates.reshape(Bsz, L, N_BRANCH, D_MODEL))
    mix = jnp.sum(gate * up, axis=2) @ p['w_out']
    x = x + _rms_norm(mix, p['ln_mix_post'])
    h = _rms_norm(x, p['ln_ffn_pre'])
    gt, uf = jnp.split(h @ p['w_ffn_in'], 2, axis=-1)
    f = (jax.nn.silu(gt) * uf) @ p['w_ffn_out']
    x = x + _rms_norm(f, p['ln_ffn_post'])
    return x, new_wk, new_wv, new_conv, new_rec


def setup_inputs(seed: int = 0) -> dict:
    key = jax.random.key(seed)
    ks = jax.random.split(key, 26)
    f32 = jnp.float32

    def nrm(k, shape, scale):
        return jax.random.normal(k, shape, f32) * scale

    def gain(k, n):
        return 1.0 + 0.05 * jax.random.normal(k, (n,), f32)

    n_in = int(sum(_in_sizes()))
    dt = jnp.exp(jax.random.uniform(ks[13], (B_HEADS,), f32, math.log(1e-3), math.log(1e-1)))
    return {
        'x_prompt': nrm(ks[0], (BATCH, SEQ, D_MODEL), 1.0),
        'x_sample': nrm(ks[1], (DEC_BATCH, DEC_SEQ, D_MODEL), 1.0),
        'mem_prompt': nrm(ks[2], (BATCH, N_MEM, D_MODEL), 1.0),
        'state_win_k': nrm(ks[3], (DEC_BATCH, WINDOW, A_KV, A_HD), 1.0),
        'state_win_v': nrm(ks[4], (DEC_BATCH, WINDOW, A_KV, A_HD), 1.0),
        'state_conv': nrm(ks[5], (DEC_BATCH, CONV_W - 1, B_CONV_CH), 1.0),
        'state_rec': nrm(ks[6], (DEC_BATCH, B_HEADS, B_DK, B_DV), 0.5),
        'cache_mem_k': nrm(ks[7], (DEC_BATCH, N_MEM, C_HEADS, C_HD), 1.0),
        'cache_mem_v': nrm(ks[8], (DEC_BATCH, N_MEM, C_HEADS, C_HD), 1.0),
        'ln_mix_pre': gain(ks[9], D_MODEL),
        'w_in': nrm(ks[10], (D_MODEL, n_in), D_MODEL ** -0.5),
        'attn_sink': nrm(ks[11], (A_HEADS,), 0.5),
        'gdn_conv_w': nrm(ks[12], (CONV_W, B_CONV_CH), CONV_W ** -0.5),
        'gdn_a_log': jnp.log(jax.random.uniform(ks[14], (B_HEADS,), f32, 1.0, 16.0)),
        'gdn_dt_bias': dt + jnp.log(-jnp.expm1(-dt)),
        'gdn_norm_w': gain(ks[15], B_DV),
        'ln_mem': gain(ks[16], D_MODEL),
        'w_mem_kv': nrm(ks[17], (D_MODEL, 2 * C_HEADS * C_HD), D_MODEL ** -0.5),
        'w_branch': nrm(ks[18], (N_BRANCH, BRANCH_W, D_MODEL), BRANCH_W ** -0.5),
        'w_out': nrm(ks[19], (D_MODEL, D_MODEL), D_MODEL ** -0.5),
        'ln_mix_post': gain(ks[20], D_MODEL),
        'ln_ffn_pre': gain(ks[21], D_MODEL),
        'w_ffn_in': nrm(ks[22], (D_MODEL, 2 * D_FF), D_MODEL ** -0.5),
        'w_ffn_out': nrm(ks[23], (D_FF, D_MODEL), D_FF ** -0.5),
        'ln_ffn_post': gain(ks[24], D_MODEL),
    }


def reference(x_prompt, x_sample, mem_prompt, state_win_k, state_win_v, state_conv, state_rec,
              cache_mem_k, cache_mem_v, ln_mix_pre, w_in, attn_sink, gdn_conv_w, gdn_a_log,
              gdn_dt_bias, gdn_norm_w, ln_mem, w_mem_kv, w_branch, w_out, ln_mix_post,
              ln_ffn_pre, w_ffn_in, w_ffn_out, ln_ffn_post):
    p = {'ln_mix_pre': ln_mix_pre, 'w_in': w_in, 'gdn_conv_w': gdn_conv_w, 'gdn_a_log': gdn_a_log,
         'gdn_dt_bias': gdn_dt_bias, 'gdn_norm_w': gdn_norm_w, 'w_branch': w_branch, 'w_out': w_out,
         'ln_mix_post': ln_mix_post, 'ln_ffn_pre': ln_ffn_pre, 'w_ffn_in': w_ffn_in,
         'w_ffn_out': w_ffn_out, 'ln_ffn_post': ln_ffn_post}

    Bp, Lp, _ = x_prompt.shape
    pos_p = jnp.arange(Lp, dtype=jnp.int32)
    mem_k_p, mem_v_p = _mem_kv(mem_prompt, ln_mem, w_mem_kv)

    def swa_prompt_fn(q, k, v):
        return _swa_prompt(q, k, v, attn_sink), k[:, -WINDOW:], v[:, -WINDOW:]

    conv0 = jnp.zeros((Bp, CONV_W - 1, B_CONV_CH), x_prompt.dtype)
    rec0 = jnp.zeros((Bp, B_HEADS, B_DK, B_DV), jnp.float32)
    y_prompt, wk_p, wv_p, conv_p, rec_p = _layer(x_prompt, pos_p, swa_prompt_fn, conv0, rec0,
                                                 mem_k_p, mem_v_p, p)

    pos_s = PAST_LEN + jnp.arange(x_sample.shape[1], dtype=jnp.int32)

    def swa_sample_fn(q, k, v):
        return _swa_decode(q, k, v, state_win_k, state_win_v, attn_sink, pos_s)

    y_sample, wk_s, wv_s, conv_s, rec_s = _layer(x_sample, pos_s, swa_sample_fn, state_conv, state_rec,
                                                 cache_mem_k, cache_mem_v, p)

    return (y_prompt, y_sample, wk_p, wv_p, conv_p, rec_p, mem_k_p, mem_v_p, wk_s, wv_s, conv_s, rec_s)
```

```python
import functools

import numpy as np
import jax
import jax.numpy as jnp
from jax import lax
from jax.experimental import pallas as pl
from jax.experimental.pallas import tpu as pltpu

F32 = jnp.float32
BF16 = jnp.bfloat16

D_MODEL = 1024
PAST_LEN = 16384
EPS = 1e-6
ROPE_THETA = 10000.0
N_MEM = 256
WINDOW = 128
A_HD = 64
A_HEADS = 8
A_KV = 2
A_SCALE = A_HD ** -0.5
B_HEADS = 4
B_DK = 128
B_DV = 128
CONV_W = 4
GDN_CHUNK = 64
B_CONV_CH = B_HEADS * (2 * B_DK + B_DV)
C_HEADS = 4
C_HD = 128
N_BRANCH = 3
BRANCH_W = 512
D_FF = 2816

LANES = 128
SUBLANES = 8
VMEM_LIMIT = 56 * 1024 * 1024
NEG = -1e30
DEC_PAD = SUBLANES

C_QA, C_KA, C_VA, C_QKV, C_Z, C_QC, C_BA, C_END = 0, 512, 640, 768, 2304, 2816, 3328, 3456


def _cparams(sem, vmem=VMEM_LIMIT):
    return pltpu.CompilerParams(dimension_semantics=sem, vmem_limit_bytes=vmem)


def _const_spec(shape):
    nd = len(shape)
    return pl.BlockSpec(shape, lambda *_: (0,) * nd, pipeline_mode=pl.Buffered(1))


def _rms(x, g):
    ms = jnp.mean(x * x, axis=-1, keepdims=True)
    return x * lax.rsqrt(ms + EPS) * g


def _bdot(a, b):
    return jnp.dot(a.astype(BF16), b.astype(BF16), preferred_element_type=F32)


def _bdot_nt(a, b):
    return lax.dot_general(a.astype(BF16), b.astype(BF16), (((1,), (1,)), ((), ())),
                           preferred_element_type=F32)


def _bdot_tn(a, b):
    return lax.dot_general(a.astype(BF16), b.astype(BF16), (((0,), (0,)), ((), ())),
                           preferred_element_type=F32)


def _hdot(a, b):
    return jnp.dot(a, b, precision=lax.Precision.HIGHEST, preferred_element_type=F32)


def _silu(x):
    return x * jax.nn.sigmoid(x)


def _softplus(x):
    return jnp.maximum(x, 0.0) + jnp.log1p(jnp.exp(-jnp.abs(x)))


def _rope128(v, cos, sin):
    lane = lax.broadcasted_iota(jnp.int32, v.shape, 1)
    fwd = pltpu.roll(v, 32, 1)
    bwd = pltpu.roll(v, 96, 1)
    sw = jnp.where((lane & 32) == 0, bwd, fwd)
    return v * cos + sw * sin


def _proj_kernel(x_ref, g_ref, w_ref, cos_ref, sin_ref,
                 qa_ref, ka_ref, va_ref, qkv_ref, z_ref, qc_ref, ba_ref):
    h = _rms(x_ref[...], g_ref[...]).astype(BF16)
    cos = cos_ref[...]
    sin = sin_ref[...]

    def mm(a, b):
        return jnp.dot(h, w_ref[:, a:b], preferred_element_type=F32)

    q = mm(C_QA, C_KA)
    for c in range(4):
        sl = slice(c * LANES, (c + 1) * LANES)
        qa_ref[:, sl] = (_rope128(q[:, sl], cos, sin) * A_SCALE).astype(BF16)
    ka_ref[...] = _rope128(mm(C_KA, C_VA), cos, sin)
    va_ref[...] = mm(C_VA, C_QKV)
    qkv_ref[...] = mm(C_QKV, C_Z)
    z_ref[...] = mm(C_Z, C_QC)
    qc_ref[...] = mm(C_QC, C_BA).astype(BF16)
    ba_ref[...] = mm(C_BA, C_END)


def _proj(x, gain, w1, cos, sin, tm):
    t = x.shape[0]
    ntab = cos.shape[0] // tm
    row = lambda n: pl.BlockSpec((tm, n), lambda i: (i, 0))
    tab = pl.BlockSpec((tm, LANES), lambda i: (i % ntab, 0))
    outs = [(512, BF16), (128, F32), (128, F32), (B_CONV_CH, F32), (512, F32), (512, BF16), (128, F32)]
    return pl.pallas_call(
        _proj_kernel,
        grid=(t // tm,),
        in_specs=[row(D_MODEL), _const_spec((1, D_MODEL)), _const_spec((D_MODEL, C_END)), tab, tab],
        out_specs=[row(n) for n, _ in outs],
        out_shape=[jax.ShapeDtypeStruct((t, n), d) for n, d in outs],
        compiler_params=_cparams(("arbitrary",)),
        name="proj",
    )(x, gain, w1, cos, sin)


def _swa_core(q, k16, v16, mask, sink_ref):
    tq = q.shape[0]
    lo = lax.broadcasted_iota(jnp.int32, (tq, LANES), 1) < A_HD
    blocks = []
    for j in range(4):
        c = q[:, j * LANES:(j + 1) * LANES].astype(F32)
        blocks.append(jnp.where(lo, c, 0.0))
        blocks.append(jnp.where(lo, 0.0, c))
    lhs = jnp.concatenate(blocks, axis=0).astype(BF16)
    logits = lax.dot_general(lhs, k16, (((1,), (1,)), ((), ())), preferred_element_type=F32)
    es, inv = [], []
    for s in range(8):
        l = jnp.where(mask, logits[s * tq:(s + 1) * tq], NEG)
        sk = sink_ref[s]
        m = jnp.maximum(jnp.max(l, axis=-1, keepdims=True), sk)
        e = jnp.exp(l - m)
        den = jnp.sum(e, axis=-1, keepdims=True) + jnp.exp(sk - m)
        es.append(e.astype(BF16))
        inv.append(1.0 / den)
    pv = jnp.dot(jnp.concatenate(es, axis=0), v16, preferred_element_type=F32)
    outs = []
    for j in range(4):
        a = pv[(2 * j) * tq:(2 * j + 1) * tq] * inv[2 * j]
        b = pv[(2 * j + 1) * tq:(2 * j + 2) * tq] * inv[2 * j + 1]
        outs.append(jnp.where(lo, a, b))
    return jnp.concatenate(outs, axis=1)


def _swa_prompt_kernel(sink_ref, q_ref, kc_ref, kp_ref, vc_ref, vp_ref, o_ref, *, nblk):
    i = pl.program_id(1)
    kcat = jnp.concatenate([kp_ref[...], kc_ref[...]], axis=0).astype(BF16)
    vcat = jnp.concatenate([vp_ref[...], vc_ref[...]], axis=0).astype(BF16)
    ii = lax.broadcasted_iota(jnp.int32, (WINDOW, 2 * WINDOW), 0)
    jj = lax.broadcasted_iota(jnp.int32, (WINDOW, 2 * WINDOW), 1)
    band = (jj > ii) & (jj <= ii + WINDOW)
    for jb in range(nblk):
        mask = band
        if jb == 0:
            mask = band & ((jj >= WINDOW) | (i > 0))
        rows = slice(jb * WINDOW, (jb + 1) * WINDOW)
        kv_rows = slice(jb * WINDOW, (jb + 2) * WINDOW)
        o = _swa_core(q_ref[rows, :], kcat[kv_rows], vcat[kv_rows], mask, sink_ref)
        o_ref[rows, :] = o.astype(BF16)


def _swa_prompt(sink, qa, ka, va, batch, seq, tq):
    nq = seq // tq
    nblk = tq // WINDOW
    nw = seq // WINDOW
    cur = lambda n: pl.BlockSpec((tq, n), lambda b, i: (b * nq + i, 0))
    prev = pl.BlockSpec((WINDOW, LANES), lambda b, i: (jnp.maximum(b * nw + i * nblk - 1, 0), 0))
    return pl.pallas_call(
        functools.partial(_swa_prompt_kernel, nblk=nblk),
        grid=(batch, nq),
        in_specs=[pl.BlockSpec(memory_space=pltpu.SMEM), cur(512), cur(LANES), prev, cur(LANES), prev],
        out_specs=cur(512),
        out_shape=jax.ShapeDtypeStruct(qa.shape, BF16),
        compiler_params=_cparams(("arbitrary", "arbitrary")),
        name="swa_prompt",
    )(sink, qa, ka, ka, va, va)


def _swa_decode_kernel(sink_ref, q_ref, kn_ref, vn_ref, wk_ref, wv_ref, o_ref, *, rows):
    ii = lax.broadcasted_iota(jnp.int32, (DEC_PAD, 2 * WINDOW), 0)
    jj = lax.broadcasted_iota(jnp.int32, (DEC_PAD, 2 * WINDOW), 1)
    mask = ((jj < WINDOW) & (jj > ii)) | ((jj >= WINDOW) & (jj - WINDOW <= ii))
    pad = jnp.zeros((WINDOW - DEC_PAD, LANES), F32)

    def body(r, carry):
        kk = jnp.concatenate([wk_ref[r], kn_ref[r], pad], axis=0).astype(BF16)
        vv = jnp.concatenate([wv_ref[r], vn_ref[r], pad], axis=0).astype(BF16)
        o_ref[r] = _swa_core(q_ref[r], kk, vv, mask, sink_ref).astype(BF16)
        return carry

    lax.fori_loop(0, rows, body, 0)


def _swa_decode(sink, qa, kn, vn, wk, wv, rows):
    nb = qa.shape[0]
    blk = lambda a, n: pl.BlockSpec((rows, a, n), lambda i: (i, 0, 0))
    return pl.pallas_call(
        functools.partial(_swa_decode_kernel, rows=rows),
        grid=(nb // rows,),
        in_specs=[pl.BlockSpec(memory_space=pltpu.SMEM), blk(DEC_PAD, 512), blk(DEC_PAD, LANES),
                  blk(DEC_PAD, LANES), blk(WINDOW, LANES), blk(WINDOW, LANES)],
        out_specs=blk(DEC_PAD, 512),
        out_shape=jax.ShapeDtypeStruct(qa.shape, BF16),
        compiler_params=_cparams(("arbitrary",)),
        name="swa_decode",
    )(sink, qa, kn, vn, wk, wv)


def _gdn_gates(ba, alog, dtb, valid):
    beta = jax.nn.sigmoid(ba)
    g = -jnp.exp(alog) * _softplus(ba + dtb)
    if valid is not None:
        beta = jnp.where(valid, beta, 0.0)
        g = jnp.where(valid, g, 0.0)
    return beta, g


def _chunk_masks(chunk):
    sh = chunk.bit_length() - 1
    ri = lax.broadcasted_iota(jnp.int32, (LANES, LANES), 0)
    ci = lax.broadcasted_iota(jnp.int32, (LANES, LANES), 1)
    same = (ri >> sh) == (ci >> sh)
    return same, same & (ri >= ci), same & (ri > ci), ri == ci


def _gdn_cumsums(g_all, chunk):
    same, tri, _, _ = _chunk_masks(chunk)
    lower = jnp.where(tri, 1.0, 0.0)
    gcol = _hdot(lower, g_all)
    grow = _hdot(g_all.T, lower.T)
    gtot = _hdot(jnp.where(same, 1.0, 0.0), g_all)
    return gcol, grow, gtot


def _gdn_phase_a(q, k, v, beta, gcol, grow, gtot, chunk):
    _, tri, strict, _ = _chunk_masks(chunk)
    decay = jnp.exp(jnp.where(tri, gcol - grow, NEG))
    kb = k * beta
    kkqk = _bdot_nt(jnp.concatenate([kb, q], axis=0), k)
    a = jnp.where(strict, kkqk[:LANES] * decay, 0.0)
    qk = kkqk[LANES:] * decay
    n = -a
    apow = a
    for _ in range(chunk.bit_length() - 2):
        apow = _bdot(apow, apow)
        n = n + apow + _bdot(n, apow)
    rhs = jnp.concatenate([v * beta, kb * jnp.exp(gcol)], axis=1)
    sol = rhs + _bdot(n, rhs)
    u, w = sol[:, :B_DV], sol[:, B_DV:]
    return u, w, q * jnp.exp(gcol), k * jnp.exp(gtot - gcol), qk, jnp.exp(gtot)


def _gdn_qkv_heads(qkv, h):
    q = qkv[:, h * B_DK:(h + 1) * B_DK]
    k = qkv[:, (B_HEADS + h) * B_DK:(B_HEADS + h + 1) * B_DK]
    v = qkv[:, (2 * B_HEADS + h) * B_DK:(2 * B_HEADS + h + 1) * B_DK]
    q = q * lax.rsqrt(jnp.sum(q * q, axis=-1, keepdims=True) + EPS) * (B_DK ** -0.5)
    k = k * lax.rsqrt(jnp.sum(k * k, axis=-1, keepdims=True) + EPS)
    return q, k, v


def _gdn_out(o, z, nw):
    return _rms(o, nw) * _silu(z)


def _gdn_prompt_kernel(raw_ref, ba_ref, z_ref, cw_ref, alog_ref, dtb_ref, nw_ref,
                       ob_ref, s_ref, buf_ref, *, tm):
    t = pl.program_id(1)
    hist = SUBLANES

    @pl.when(t == 0)
    def _():
        buf_ref[0:hist, :] = jnp.zeros((hist, B_CONV_CH), F32)
        s_ref[...] = jnp.zeros(s_ref.shape, F32)

    @pl.when(t > 0)
    def _():
        buf_ref[0:hist, :] = buf_ref[tm:tm + hist, :]

    buf_ref[hist:hist + tm, :] = raw_ref[0]
    alog = alog_ref[...]
    dtb = dtb_ref[...]
    nw = nw_ref[...]
    nchunk = LANES // GDN_CHUNK
    for blk in range(tm // LANES):
        r0 = blk * LANES
        conv = None
        for i in range(CONV_W):
            off = hist - (CONV_W - 1) + i + r0
            term = buf_ref[off:off + LANES, :] * cw_ref[i:i + 1, :]
            conv = term if conv is None else conv + term
        qkv = _silu(conv)
        beta_all, g_all = _gdn_gates(ba_ref[0, r0:r0 + LANES, :], alog, dtb, None)
        gcol_all, grow_all, gtot_all = _gdn_cumsums(g_all, GDN_CHUNK)
        for h in range(B_HEADS):
            q, k, v = _gdn_qkv_heads(qkv, h)
            gl = B_HEADS + h
            u, w, qe, kd, qk, egt = _gdn_phase_a(
                q, k, v, beta_all[:, h:h + 1], gcol_all[:, gl:gl + 1], grow_all[gl:gl + 1, :],
                gtot_all[:, gl:gl + 1], GDN_CHUNK)
            outs = []
            for c in range(nchunk):
                rows = slice(c * GDN_CHUNK, (c + 1) * GDN_CHUNK)
                s = s_ref[0, h]
                res = _bdot(jnp.concatenate([w[rows], qe[rows]], axis=0), s)
                v_new = u[rows] - res[:GDN_CHUNK]
                vt = jnp.concatenate([v_new] * nchunk, axis=0)
                outs.append(res[GDN_CHUNK:] + _bdot(qk[rows], vt))
                s_ref[0, h] = s * egt[c * GDN_CHUNK:c * GDN_CHUNK + 1, :] + _bdot_tn(kd[rows], v_new)
            o = jnp.concatenate(outs, axis=0)
            zs = z_ref[0, r0:r0 + LANES, h * B_DV:(h + 1) * B_DV]
            ob_ref[0, r0:r0 + LANES, h * B_DV:(h + 1) * B_DV] = _gdn_out(o, zs, nw).astype(BF16)


def _gdn_prompt(raw, ba, z, cw, alog, dtb, nw, tm):
    batch, seq, _ = raw.shape
    tok = lambda n: pl.BlockSpec((1, tm, n), lambda b, t: (b, t, 0))
    return pl.pallas_call(
        functools.partial(_gdn_prompt_kernel, tm=tm),
        grid=(batch, seq // tm),
        in_specs=[tok(B_CONV_CH), tok(LANES), tok(512), _const_spec((CONV_W, B_CONV_CH)),
                  _const_spec((1, LANES)), _const_spec((1, LANES)), _const_spec((1, B_DV))],
        out_specs=[tok(512), pl.BlockSpec((1, B_HEADS, B_DK, B_DV), lambda b, t: (b, 0, 0, 0))],
        out_shape=[jax.ShapeDtypeStruct((batch, seq, 512), BF16),
                   jax.ShapeDtypeStruct((batch, B_HEADS, B_DK, B_DV), F32)],
        scratch_shapes=[pltpu.VMEM((tm + SUBLANES, B_CONV_CH), F32)],
        compiler_params=_cparams(("arbitrary", "arbitrary")),
        name="gdn_prompt",
    )(raw, ba, z, cw, alog, dtb, nw)


def _gdn_decode_kernel(raw_ref, hist_ref, ba_ref, z_ref, rec_ref, cw_ref, alog_ref, dtb_ref, nw_ref,
                       ob_ref, s_ref, buf_ref, u_s, w_s, qe_s, kd_s, qk_s, eg_s, o_s, *, rows, valid_len):
    buf_ref[:, 0:DEC_PAD, :] = hist_ref[...]
    buf_ref[:, DEC_PAD:2 * DEC_PAD, :] = raw_ref[...]
    conv = None
    for i in range(CONV_W):
        off = DEC_PAD - (CONV_W - 1) + i
        term = buf_ref[:, off:off + DEC_PAD, :] * cw_ref[i:i + 1, :]
        conv = term if conv is None else conv + term
    qkv = _silu(conv.reshape(rows * DEC_PAD, B_CONV_CH))
    tok = lax.broadcasted_iota(jnp.int32, (LANES, LANES), 0) & (DEC_PAD - 1)
    beta_all, g_all = _gdn_gates(ba_ref[...], alog_ref[...], dtb_ref[...], tok < valid_len)
    gcol_all, grow_all, gtot_all = _gdn_cumsums(g_all, DEC_PAD)
    for h in range(B_HEADS):
        q, k, v = _gdn_qkv_heads(qkv, h)
        gl = B_HEADS + h
        u, w, qe, kd, qk, egt = _gdn_phase_a(
            q, k, v, beta_all[:, h:h + 1], gcol_all[:, gl:gl + 1], grow_all[gl:gl + 1, :],
            gtot_all[:, gl:gl + 1], DEC_PAD)
        u_s[h] = u
        w_s[h] = w
        qe_s[h] = qe
        kd_s[h] = kd
        qk_s[h] = qk
        eg_s[h] = jnp.broadcast_to(egt, (LANES, LANES))

    def body(r, carry):
        r0 = pl.multiple_of(r * DEC_PAD, DEC_PAD)
        rr = pl.ds(r0, DEC_PAD)
        for h in range(B_HEADS):
            s = rec_ref[r, h]
            res = _bdot(jnp.concatenate([w_s[h, rr, :], qe_s[h, rr, :]], axis=0), s)
            v_new = u_s[h, rr, :] - res[:DEC_PAD]
            vt = jnp.concatenate([v_new] * (LANES // DEC_PAD), axis=0)
            o_s[h, rr, :] = res[DEC_PAD:] + _bdot(qk_s[h, rr, :], vt)
            s_ref[r, h] = s * eg_s[h, pl.ds(r0, 1), :] + _bdot_tn(kd_s[h, rr, :], v_new)
        return carry

    lax.fori_loop(0, rows, body, 0)
    nw = nw_ref[...]
    for h in range(B_HEADS):
        sl = slice(h * B_DV, (h + 1) * B_DV)
        ob_ref[:, sl] = _gdn_out(o_s[h], z_ref[:, sl], nw).astype(BF16)


def _gdn_decode(raw, histp, ba, z, rec, cw, alog, dtb, nw, rows, valid_len):
    nb = raw.shape[0]
    flat = rows * DEC_PAD
    assert flat == LANES
    sq = lambda: pltpu.VMEM((B_HEADS, LANES, LANES), F32)
    return pl.pallas_call(
        functools.partial(_gdn_decode_kernel, rows=rows, valid_len=valid_len),
        grid=(nb // rows,),
        in_specs=[pl.BlockSpec((rows, DEC_PAD, B_CONV_CH), lambda i: (i, 0, 0)),
                  pl.BlockSpec((rows, DEC_PAD, B_CONV_CH), lambda i: (i, 0, 0)),
                  pl.BlockSpec((flat, LANES), lambda i: (i, 0)),
                  pl.BlockSpec((flat, 512), lambda i: (i, 0)),
                  pl.BlockSpec((rows, B_HEADS, B_DK, B_DV), lambda i: (i, 0, 0, 0)),
                  _const_spec((CONV_W, B_CONV_CH)), _const_spec((1, LANES)), _const_spec((1, LANES)),
                  _const_spec((1, B_DV))],
        out_specs=[pl.BlockSpec((flat, 512), lambda i: (i, 0)),
                   pl.BlockSpec((rows, B_HEADS, B_DK, B_DV), lambda i: (i, 0, 0, 0))],
        out_shape=[jax.ShapeDtypeStruct((nb * DEC_PAD, 512), BF16),
                   jax.ShapeDtypeStruct(rec.shape, F32)],
        scratch_shapes=[pltpu.VMEM((rows, 2 * DEC_PAD, B_CONV_CH), F32)] + [sq() for _ in range(7)],
        compiler_params=_cparams(("arbitrary",)),
        name="gdn_decode",
    )(raw, histp, ba, z, rec, cw, alog, dtb, nw)


def _memkv_kernel(m_ref, g_ref, w_ref, k_ref, v_ref):
    h = _rms(m_ref[...], g_ref[...]).astype(BF16)
    n = C_HEADS * C_HD
    k_ref[...] = jnp.dot(h, w_ref[:, :n], preferred_element_type=F32)
    v_ref[...] = jnp.dot(h, w_ref[:, n:], preferred_element_type=F32)


def _memkv(mem, gain, w):
    t = mem.shape[0]
    tm = 512
    n = C_HEADS * C_HD
    return pl.pallas_call(
        _memkv_kernel,
        grid=(t // tm,),
        in_specs=[pl.BlockSpec((tm, D_MODEL), lambda i: (i, 0)), _const_spec((1, D_MODEL)),
                  _const_spec((D_MODEL, 2 * n))],
        out_specs=[pl.BlockSpec((tm, n), lambda i: (i, 0))] * 2,
        out_shape=[jax.ShapeDtypeStruct((t, n), F32)] * 2,
        compiler_params=_cparams(("arbitrary",)),
        name="memkv",
    )(mem, gain, w)


def _softmax_rows(logits):
    m = jnp.max(logits, axis=-1, keepdims=True)
    e = jnp.exp(logits - m)
    return e, 1.0 / jnp.sum(e, axis=-1, keepdims=True)


def _memattn_prompt_kernel(q_ref, k_ref, v_ref, o_ref):
    for h in range(C_HEADS):
        sl = slice(h * C_HD, (h + 1) * C_HD)
        logits = _bdot_nt(q_ref[:, sl], k_ref[:, sl]) * (C_HD ** -0.5)
        e, inv = _softmax_rows(logits)
        o_ref[:, sl] = (_bdot(e, v_ref[:, sl]) * inv).astype(BF16)


def _memattn_prompt(qc, mk, mv, batch, seq, tm):
    nq = seq // tm
    n = C_HEADS * C_HD
    cur = pl.BlockSpec((tm, n), lambda b, i: (b * nq + i, 0))
    mem = pl.BlockSpec((N_MEM, n), lambda b, i: (b, 0))
    return pl.pallas_call(
        _memattn_prompt_kernel,
        grid=(batch, nq),
        in_specs=[cur, mem, mem],
        out_specs=cur,
        out_shape=jax.ShapeDtypeStruct(qc.shape, BF16),
        compiler_params=_cparams(("arbitrary", "arbitrary")),
        name="memattn_prompt",
    )(qc, mk, mv)


def _memattn_decode_kernel(q_ref, k_ref, v_ref, o_ref, *, rows):
    n = C_HEADS * C_HD
    lane = lax.broadcasted_iota(jnp.int32, (DEC_PAD, n), 1)
    hmask = [(lane >= h * C_HD) & (lane < (h + 1) * C_HD) for h in range(C_HEADS)]

    def body(r, carry):
        q = q_ref[r].astype(F32)
        lhs = jnp.concatenate([jnp.where(hmask[h], q, 0.0) for h in range(C_HEADS)], axis=0)
        logits = _bdot_nt(lhs, k_ref[r]) * (C_HD ** -0.5)
        e, inv = _softmax_rows(logits)
        pv = _bdot(e, v_ref[r]) * inv
        o = jnp.zeros((DEC_PAD, n), F32)
        for h in range(C_HEADS):
            o = jnp.where(hmask[h], pv[h * DEC_PAD:(h + 1) * DEC_PAD], o)
        o_ref[r] = o.astype(BF16)
        return carry

    lax.fori_loop(0, rows, body, 0)


def _memattn_decode(qc, ck, cv, rows):
    nb = qc.shape[0]
    n = C_HEADS * C_HD
    blk = lambda a: pl.BlockSpec((rows, a, n), lambda i: (i, 0, 0))
    return pl.pallas_call(
        functools.partial(_memattn_decode_kernel, rows=rows),
        grid=(nb // rows,),
        in_specs=[blk(DEC_PAD), blk(N_MEM), blk(N_MEM)],
        out_specs=blk(DEC_PAD),
        out_shape=jax.ShapeDtypeStruct(qc.shape, BF16),
        compiler_params=_cparams(("arbitrary",)),
        name="memattn_decode",
    )(qc, ck, cv)


def _post_kernel(x_ref, oa_ref, ob_ref, oc_ref, gpre_ref, wg_ref, wb_ref, wo_ref, gpost_ref,
                 gfpre_ref, wfi_ref, wfo_ref, gfpost_ref, y_ref):
    x = x_ref[...]
    h = _rms(x, gpre_ref[...]).astype(BF16)
    mix = None
    for n, o_ref in enumerate((oa_ref, ob_ref, oc_ref)):
        gate = jax.nn.sigmoid(jnp.dot(h, wg_ref[:, n * D_MODEL:(n + 1) * D_MODEL],
                                      preferred_element_type=F32))
        up = jnp.dot(o_ref[...], wb_ref[n], preferred_element_type=F32)
        mix = gate * up if mix is None else mix + gate * up
    x1 = x + _rms(_bdot(mix, wo_ref[...]), gpost_ref[...])
    h2 = _rms(x1, gfpre_ref[...]).astype(BF16)
    gt = jnp.dot(h2, wfi_ref[:, :D_FF], preferred_element_type=F32)
    uf = jnp.dot(h2, wfi_ref[:, D_FF:], preferred_element_type=F32)
    f = _bdot(_silu(gt) * uf, wfo_ref[...])
    y_ref[...] = x1 + _rms(f, gfpost_ref[...])


def _post(x, oa, ob, oc, gpre, wg, wb, wo, gpost, gfpre, wfi, wfo, gfpost, tm):
    t = x.shape[0]
    row = lambda n: pl.BlockSpec((tm, n), lambda i: (i, 0))
    vec = _const_spec((1, D_MODEL))
    return pl.pallas_call(
        _post_kernel,
        grid=(t // tm,),
        in_specs=[row(D_MODEL), row(512), row(512), row(512), vec,
                  _const_spec((D_MODEL, N_BRANCH * D_MODEL)), _const_spec((N_BRANCH, BRANCH_W, D_MODEL)),
                  _const_spec((D_MODEL, D_MODEL)), vec, vec, _const_spec((D_MODEL, 2 * D_FF)),
                  _const_spec((D_FF, D_MODEL)), vec],
        out_specs=row(D_MODEL),
        out_shape=jax.ShapeDtypeStruct(x.shape, F32),
        compiler_params=_cparams(("arbitrary",)),
        name="post",
    )(x, oa, ob, oc, gpre, wg, wb, wo, gpost, gfpre, wfi, wfo, gfpost)


def _rope_tables(pos):
    half = A_HD // 2
    inv = ROPE_THETA ** (-jnp.arange(half, dtype=F32) / half)
    ang = pos.astype(F32)[:, None] * inv[None, :]
    cos, sin = jnp.cos(ang), jnp.sin(ang)
    cos = jnp.concatenate([cos, cos], axis=-1)
    sin = jnp.concatenate([-sin, sin], axis=-1)
    return jnp.tile(cos, (1, LANES // A_HD)), jnp.tile(sin, (1, LANES // A_HD))


def _lane_row(vals, offset):
    return jnp.zeros((1, LANES), F32).at[0, offset:offset + vals.shape[0]].set(vals.astype(F32))


def kernel(x_prompt, x_sample, mem_prompt, state_win_k, state_win_v, state_conv, state_rec,
           cache_mem_k, cache_mem_v, ln_mix_pre, w_in, attn_sink, gdn_conv_w, gdn_a_log,
           gdn_dt_bias, gdn_norm_w, ln_mem, w_mem_kv, w_branch, w_out, ln_mix_post,
           ln_ffn_pre, w_ffn_in, w_ffn_out, ln_ffn_post):
    bp, lp, _ = x_prompt.shape
    bs, ls, _ = x_sample.shape

    sizes = [512, 128, 128, B_CONV_CH, B_HEADS, B_HEADS, 512, 512, N_BRANCH * D_MODEL]
    o = np.cumsum([0] + sizes)
    hperm = np.concatenate([np.r_[j * A_HD:(j + 1) * A_HD, (j + 4) * A_HD:(j + 5) * A_HD] for j in range(4)])
    w1 = jnp.concatenate([
        w_in[:, o[0]:o[1]][:, hperm], w_in[:, o[1]:o[3]], w_in[:, o[3]:o[4]], w_in[:, o[6]:o[7]],
        w_in[:, o[7]:o[8]], w_in[:, o[4]:o[6]], jnp.zeros((D_MODEL, LANES - 2 * B_HEADS), F32)],
        axis=1).astype(BF16)
    wg = w_in[:, o[8]:o[9]].astype(BF16)
    wb = jnp.concatenate([w_branch[0:1][:, hperm], w_branch[1:]], axis=0).astype(BF16)
    wo = w_out.astype(BF16)
    wfi = w_ffn_in.astype(BF16)
    wfo = w_ffn_out.astype(BF16)
    wmem = w_mem_kv.astype(BF16)
    sink = attn_sink.astype(F32)[np.array([0, 4, 1, 5, 2, 6, 3, 7])]
    vec = lambda g: g.astype(F32).reshape(1, -1)
    alog = _lane_row(gdn_a_log, B_HEADS)
    dtb = _lane_row(gdn_dt_bias, B_HEADS)
    cw = gdn_conv_w.astype(F32)
    nw = vec(gdn_norm_w)

    def post(x, oa, ob, oc, tm):
        return _post(x, oa, ob, oc, vec(ln_mix_pre), wg, wb, wo, vec(ln_mix_post), vec(ln_ffn_pre),
                     wfi, wfo, vec(ln_ffn_post), tm)

    tp = bp * lp
    xp = x_prompt.reshape(tp, D_MODEL)
    cos_p, sin_p = _rope_tables(jnp.arange(lp, dtype=jnp.int32))
    qa, ka, va, qkv, z, qc, ba = _proj(xp, vec(ln_mix_pre), w1, cos_p, sin_p, 512)
    oa = _swa_prompt(sink, qa, ka, va, bp, lp, 512)
    ob, rec_p = _gdn_prompt(qkv.reshape(bp, lp, B_CONV_CH), ba.reshape(bp, lp, LANES),
                            z.reshape(bp, lp, 512), cw, alog, dtb, nw, 256)
    mk, mv = _memkv(mem_prompt.reshape(bp * N_MEM, D_MODEL), vec(ln_mem), wmem)
    oc = _memattn_prompt(qc, mk, mv, bp, lp, 512)
    y_p = post(xp, oa, ob.reshape(tp, 512), oc, 256).reshape(bp, lp, D_MODEL)
    wk_p = ka.reshape(bp, lp, A_KV, A_HD)[:, -WINDOW:]
    wv_p = va.reshape(bp, lp, A_KV, A_HD)[:, -WINDOW:]
    conv_p = qkv.reshape(bp, lp, B_CONV_CH)[:, -(CONV_W - 1):]
    mem_k_p = mk.reshape(bp, N_MEM, C_HEADS, C_HD)
    mem_v_p = mv.reshape(bp, N_MEM, C_HEADS, C_HD)

    ts = bs * DEC_PAD
    xs = jnp.pad(x_sample, ((0, 0), (0, DEC_PAD - ls), (0, 0))).reshape(ts, D_MODEL)
    cos_s, sin_s = _rope_tables(PAST_LEN + jnp.arange(DEC_PAD, dtype=jnp.int32))
    cos_s, sin_s = jnp.tile(cos_s, (bs, 1)), jnp.tile(sin_s, (bs, 1))
    qa, ka, va, qkv, z, qc, ba = _proj(xs, vec(ln_mix_pre), w1, cos_s, sin_s, 512)
    r3 = lambda a: a.reshape(bs, DEC_PAD, a.shape[-1])
    oa = _swa_decode(sink, r3(qa), r3(ka), r3(va), state_win_k.reshape(bs, WINDOW, LANES),
                     state_win_v.reshape(bs, WINDOW, LANES), 16).reshape(ts, 512)
    histp = jnp.pad(state_conv, ((0, 0), (DEC_PAD - (CONV_W - 1), 0), (0, 0)))
    ob, rec_s = _gdn_decode(r3(qkv), histp, ba, z, state_rec, cw, alog, dtb, nw,
                            LANES // DEC_PAD, ls)
    oc = _memattn_decode(r3(qc), cache_mem_k.reshape(bs, N_MEM, C_HEADS * C_HD),
                         cache_mem_v.reshape(bs, N_MEM, C_HEADS * C_HD), 8).reshape(ts, 512)
    y_s = post(xs, oa, ob, oc, 256).reshape(bs, DEC_PAD, D_MODEL)[:, :ls]
    k_new = r3(ka)[:, :ls].reshape(bs, ls, A_KV, A_HD)
    v_new = r3(va)[:, :ls].reshape(bs, ls, A_KV, A_HD)
    wk_s = jnp.concatenate([state_win_k, k_new], axis=1)[:, -WINDOW:]
    wv_s = jnp.concatenate([state_win_v, v_new], axis=1)[:, -WINDOW:]
    conv_s = jnp.concatenate([state_conv, r3(qkv)[:, :ls]], axis=1)[:, -(CONV_W - 1):]

    return (y_p, y_s, wk_p, wv_p, conv_p, rec_p, mem_k_p, mem_v_p, wk_s, wv_s, conv_s, rec_s)
```

```python
import functools

import numpy as np
import jax
import jax.numpy as jnp
from jax import lax
from jax.experimental import pallas as pl
from jax.experimental.pallas import tpu as pltpu

F32 = jnp.float32
BF16 = jnp.bfloat16

D_MODEL = 1024
PAST_LEN = 16384
EPS = 1e-6
ROPE_THETA = 10000.0
N_MEM = 256
WINDOW = 128
A_HD = 64
A_HEADS = 8
A_KV = 2
A_SCALE = A_HD ** -0.5
B_HEADS = 4
B_DK = 128
B_DV = 128
CONV_W = 4
GDN_CHUNK = 64
B_CONV_CH = B_HEADS * (2 * B_DK + B_DV)
C_HEADS = 4
C_HD = 128
N_BRANCH = 3
BRANCH_W = 512
D_FF = 2816

LANES = 128
SUBLANES = 8
VMEM_LIMIT = 56 * 1024 * 1024
NEG = -1e30
DEC_PAD = SUBLANES

C_QA, C_KA, C_VA, C_QKV, C_Z, C_QC, C_BA, C_END = 0, 512, 640, 768, 2304, 2816, 3328, 3456


def _cparams(sem, vmem=VMEM_LIMIT):
    return pltpu.CompilerParams(dimension_semantics=sem, vmem_limit_bytes=vmem)


def _const_spec(shape):
    nd = len(shape)
    return pl.BlockSpec(shape, lambda *_: (0,) * nd, pipeline_mode=pl.Buffered(1))


def _rms(x, g):
    ms = jnp.mean(x * x, axis=-1, keepdims=True)
    return x * lax.rsqrt(ms + EPS) * g


def _bdot(a, b):
    return jnp.dot(a.astype(BF16), b.astype(BF16), preferred_element_type=F32)


def _bdot_nt(a, b):
    return lax.dot_general(a.astype(BF16), b.astype(BF16), (((1,), (1,)), ((), ())),
                           preferred_element_type=F32)


def _bdot_tn(a, b):
    return lax.dot_general(a.astype(BF16), b.astype(BF16), (((0,), (0,)), ((), ())),
                           preferred_element_type=F32)


def _hdot(a, b):
    return jnp.dot(a, b, precision=lax.Precision.HIGHEST, preferred_element_type=F32)


def _silu(x):
    return x * jax.nn.sigmoid(x)


def _softplus(x):
    return jnp.maximum(x, 0.0) + jnp.log1p(jnp.exp(-jnp.abs(x)))


def _rope128(v, cos, sin):
    lane = lax.broadcasted_iota(jnp.int32, v.shape, 1)
    fwd = pltpu.roll(v, 32, 1)
    bwd = pltpu.roll(v, 96, 1)
    sw = jnp.where((lane & 32) == 0, bwd, fwd)
    return v * cos + sw * sin


def _proj_kernel(x_ref, g_ref, w_ref, cos_ref, sin_ref,
                 qa_ref, ka_ref, va_ref, qkv_ref, z_ref, qc_ref, ba_ref):
    h = _rms(x_ref[...], g_ref[...]).astype(BF16)
    cos = cos_ref[...]
    sin = sin_ref[...]

    def mm(a, b):
        return jnp.dot(h, w_ref[:, a:b], preferred_element_type=F32)

    q = mm(C_QA, C_KA)
    for c in range(4):
        sl = slice(c * LANES, (c + 1) * LANES)
        qa_ref[:, sl] = (_rope128(q[:, sl], cos, sin) * A_SCALE).astype(BF16)
    ka_ref[...] = _rope128(mm(C_KA, C_VA), cos, sin)
    va_ref[...] = mm(C_VA, C_QKV)
    qkv_ref[...] = mm(C_QKV, C_Z)
    z_ref[...] = mm(C_Z, C_QC)
    qc_ref[...] = mm(C_QC, C_BA).astype(BF16)
    ba_ref[...] = mm(C_BA, C_END)


def _proj(x, gain, w1, cos, sin, tm):
    t = x.shape[0]
    ntab = cos.shape[0] // tm
    row = lambda n: pl.BlockSpec((tm, n), lambda i: (i, 0))
    tab = pl.BlockSpec((tm, LANES), lambda i: (i % ntab, 0))
    outs = [(512, BF16), (128, F32), (128, F32), (B_CONV_CH, F32), (512, F32), (512, BF16), (128, F32)]
    return pl.pallas_call(
        _proj_kernel,
        grid=(t // tm,),
        in_specs=[row(D_MODEL), _const_spec((1, D_MODEL)), _const_spec((D_MODEL, C_END)), tab, tab],
        out_specs=[row(n) for n, _ in outs],
        out_shape=[jax.ShapeDtypeStruct((t, n), d) for n, d in outs],
        compiler_params=_cparams(("arbitrary",)),
        name="proj",
    )(x, gain, w1, cos, sin)


def _swa_core(q, k16, v16, mask, sink_ref):
    tq = q.shape[0]
    lo = lax.broadcasted_iota(jnp.int32, (tq, LANES), 1) < A_HD
    blocks = []
    for j in range(4):
        c = q[:, j * LANES:(j + 1) * LANES].astype(F32)
        blocks.append(jnp.where(lo, c, 0.0))
        blocks.append(jnp.where(lo, 0.0, c))
    lhs = jnp.concatenate(blocks, axis=0).astype(BF16)
    logits = lax.dot_general(lhs, k16, (((1,), (1,)), ((), ())), preferred_element_type=F32)
    es, inv = [], []
    for s in range(8):
        l = jnp.where(mask, logits[s * tq:(s + 1) * tq], NEG)
        sk = sink_ref[s]
        m = jnp.maximum(jnp.max(l, axis=-1, keepdims=True), sk)
        e = jnp.exp(l - m)
        den = jnp.sum(e, axis=-1, keepdims=True) + jnp.exp(sk - m)
        es.append(e.astype(BF16))
        inv.append(1.0 / den)
    pv = jnp.dot(jnp.concatenate(es, axis=0), v16, preferred_element_type=F32)
    outs = []
    for j in range(4):
        a = pv[(2 * j) * tq:(2 * j + 1) * tq] * inv[2 * j]
        b = pv[(2 * j + 1) * tq:(2 * j + 2) * tq] * inv[2 * j + 1]
        outs.append(jnp.where(lo, a, b))
    return jnp.concatenate(outs, axis=1)


def _swa_prompt_kernel(sink_ref, q_ref, kc_ref, kp_ref, vc_ref, vp_ref, o_ref, *, nblk):
    i = pl.program_id(1)
    kcat = jnp.concatenate([kp_ref[...], kc_ref[...]], axis=0).astype(BF16)
    vcat = jnp.concatenate([vp_ref[...], vc_ref[...]], axis=0).astype(BF16)
    ii = lax.broadcasted_iota(jnp.int32, (WINDOW, 2 * WINDOW), 0)
    jj = lax.broadcasted_iota(jnp.int32, (WINDOW, 2 * WINDOW), 1)
    band = (jj > ii) & (jj <= ii + WINDOW)
    for jb in range(nblk):
        mask = band
        if jb == 0:
            mask = band & ((jj >= WINDOW) | (i > 0))
        rows = slice(jb * WINDOW, (jb + 1) * WINDOW)
        kv_rows = slice(jb * WINDOW, (jb + 2) * WINDOW)
        o = _swa_core(q_ref[rows, :], kcat[kv_rows], vcat[kv_rows], mask, sink_ref)
        o_ref[rows, :] = o.astype(BF16)


def _swa_prompt(sink, qa, ka, va, batch, seq, tq):
    nq = seq // tq
    nblk = tq // WINDOW
    nw = seq // WINDOW
    cur = lambda n: pl.BlockSpec((tq, n), lambda b, i: (b * nq + i, 0))
    prev = pl.BlockSpec((WINDOW, LANES), lambda b, i: (jnp.maximum(b * nw + i * nblk - 1, 0), 0))
    return pl.pallas_call(
        functools.partial(_swa_prompt_kernel, nblk=nblk),
        grid=(batch, nq),
        in_specs=[pl.BlockSpec(memory_space=pltpu.SMEM), cur(512), cur(LANES), prev, cur(LANES), prev],
        out_specs=cur(512),
        out_shape=jax.ShapeDtypeStruct(qa.shape, BF16),
        compiler_params=_cparams(("arbitrary", "arbitrary")),
        name="swa_prompt",
    )(sink, qa, ka, ka, va, va)


def _swa_decode_kernel(sink_ref, q_ref, kn_ref, vn_ref, wk_ref, wv_ref, o_ref, *, rows):
    ii = lax.broadcasted_iota(jnp.int32, (DEC_PAD, 2 * WINDOW), 0)
    jj = lax.broadcasted_iota(jnp.int32, (DEC_PAD, 2 * WINDOW), 1)
    mask = ((jj < WINDOW) & (jj > ii)) | ((jj >= WINDOW) & (jj - WINDOW <= ii))
    pad = jnp.zeros((WINDOW - DEC_PAD, LANES), F32)

    def body(r, carry):
        kk = jnp.concatenate([wk_ref[r], kn_ref[r], pad], axis=0).astype(BF16)
        vv = jnp.concatenate([wv_ref[r], vn_ref[r], pad], axis=0).astype(BF16)
        o_ref[r] = _swa_core(q_ref[r], kk, vv, mask, sink_ref).astype(BF16)
        return carry

    lax.fori_loop(0, rows, body, 0)


def _swa_decode(sink, qa, kn, vn, wk, wv, rows):
    nb = qa.shape[0]
    blk = lambda a, n: pl.BlockSpec((rows, a, n), lambda i: (i, 0, 0))
    return pl.pallas_call(
        functools.partial(_swa_decode_kernel, rows=rows),
        grid=(nb // rows,),
        in_specs=[pl.BlockSpec(memory_space=pltpu.SMEM), blk(DEC_PAD, 512), blk(DEC_PAD, LANES),
                  blk(DEC_PAD, LANES), blk(WINDOW, LANES), blk(WINDOW, LANES)],
        out_specs=blk(DEC_PAD, 512),
        out_shape=jax.ShapeDtypeStruct(qa.shape, BF16),
        compiler_params=_cparams(("arbitrary",)),
        name="swa_decode",
    )(sink, qa, kn, vn, wk, wv)


def _gdn_gates(ba, alog, dtb, valid):
    beta = jax.nn.sigmoid(ba)
    g = -jnp.exp(alog) * _softplus(ba + dtb)
    if valid is not None:
        beta = jnp.where(valid, beta, 0.0)
        g = jnp.where(valid, g, 0.0)
    return beta, g


def _chunk_masks(chunk):
    sh = chunk.bit_length() - 1
    ri = lax.broadcasted_iota(jnp.int32, (LANES, LANES), 0)
    ci = lax.broadcasted_iota(jnp.int32, (LANES, LANES), 1)
    same = (ri >> sh) == (ci >> sh)
    return same, same & (ri >= ci), same & (ri > ci), ri == ci


def _gdn_cumsums(g_all, chunk):
    same, tri, _, _ = _chunk_masks(chunk)
    lower = jnp.where(tri, 1.0, 0.0)
    gcol = _hdot(lower, g_all)
    grow = _hdot(g_all.T, lower.T)
    gtot = _hdot(jnp.where(same, 1.0, 0.0), g_all)
    return gcol, grow, gtot


def _gdn_phase_a(q, k, v, beta, gcol, grow, gtot, chunk):
    _, tri, strict, _ = _chunk_masks(chunk)
    decay = jnp.exp(jnp.where(tri, gcol - grow, NEG))
    kb = k * beta
    kkqk = _bdot_nt(jnp.concatenate([kb, q], axis=0), k)
    a = jnp.where(strict, kkqk[:LANES] * decay, 0.0)
    qk = kkqk[LANES:] * decay
    n = -a
    apow = a
    for _ in range(chunk.bit_length() - 2):
        apow = _bdot(apow, apow)
        n = n + apow + _bdot(n, apow)
    rhs = jnp.concatenate([v * beta, kb * jnp.exp(gcol)], axis=1)
    sol = rhs + _bdot(n, rhs)
    u, w = sol[:, :B_DV], sol[:, B_DV:]
    return u, w, q * jnp.exp(gcol), k * jnp.exp(gtot - gcol), qk, jnp.exp(gtot)


def _gdn_qkv_heads(qkv, h):
    q = qkv[:, h * B_DK:(h + 1) * B_DK]
    k = qkv[:, (B_HEADS + h) * B_DK:(B_HEADS + h + 1) * B_DK]
    v = qkv[:, (2 * B_HEADS + h) * B_DK:(2 * B_HEADS + h + 1) * B_DK]
    q = q * lax.rsqrt(jnp.sum(q * q, axis=-1, keepdims=True) + EPS) * (B_DK ** -0.5)
    k = k * lax.rsqrt(jnp.sum(k * k, axis=-1, keepdims=True) + EPS)
    return q, k, v


def _gdn_out(o, z, nw):
    return _rms(o, nw) * _silu(z)


PAIR = 2 * LANES


def _pair_bd(x2):
    lo = lax.broadcasted_iota(jnp.int32, x2.shape, 1) < LANES
    return jnp.concatenate([jnp.where(lo, x2, 0.0), jnp.where(lo, 0.0, x2)], axis=0).astype(BF16)


def _pair_cols(x, l0, l1):
    lo = lax.broadcasted_iota(jnp.int32, (x.shape[0], PAIR), 1) < LANES
    return jnp.where(lo, x[:, l0:l0 + 1], x[:, l1:l1 + 1])


def _l2n(x):
    return x * lax.rsqrt(jnp.sum(x * x, axis=-1, keepdims=True) + EPS)


def _gdn_prompt_kernel(raw_ref, ba_ref, z_ref, cw_ref, alog_ref, dtb_ref, nw_ref,
                       ob_ref, s_ref, buf_ref, sbd_ref, *, tm):
    t = pl.program_id(1)
    hist = SUBLANES
    npair = B_HEADS // 2

    @pl.when(t == 0)
    def _():
        buf_ref[0:hist, :] = jnp.zeros((hist, B_CONV_CH), F32)
        sbd_ref[...] = jnp.zeros(sbd_ref.shape, F32)

    @pl.when(t > 0)
    def _():
        buf_ref[0:hist, :] = buf_ref[tm:tm + hist, :]

    buf_ref[hist:hist + tm, :] = raw_ref[0]
    alog = alog_ref[...]
    dtb = dtb_ref[...]
    nw = nw_ref[...]
    nchunk = LANES // GDN_CHUNK
    nblk = tm // LANES
    _, tri, strict, _ = _chunk_masks(GDN_CHUNK)
    tri2 = jnp.concatenate([tri, tri], axis=1)
    strict2 = jnp.concatenate([strict, strict], axis=1)

    items = []
    for blk in range(nblk):
        r0 = blk * LANES
        conv = None
        for i in range(CONV_W):
            off = hist - (CONV_W - 1) + i + r0
            term = buf_ref[off:off + LANES, :] * cw_ref[i:i + 1, :]
            conv = term if conv is None else conv + term
        qkv = _silu(conv)
        beta_all, g_all = _gdn_gates(ba_ref[0, r0:r0 + LANES, :], alog, dtb, None)
        gcol_all, grow_all, gtot_all = _gdn_cumsums(g_all, GDN_CHUNK)
        eg_all = jnp.exp(gcol_all)
        ed_all = jnp.exp(gtot_all - gcol_all)
        et_all = jnp.exp(gtot_all)
        for p in range(npair):
            h0, h1 = 2 * p, 2 * p + 1
            g0, g1 = B_HEADS + h0, B_HEADS + h1
            hd = lambda base, h: qkv[:, (base + h) * B_DK:(base + h + 1) * B_DK]
            q2 = jnp.concatenate([_l2n(hd(0, h0)), _l2n(hd(0, h1))], axis=1) * (B_DK ** -0.5)
            k2 = jnp.concatenate([_l2n(hd(B_HEADS, h0)), _l2n(hd(B_HEADS, h1))], axis=1)
            v2 = qkv[:, (2 * B_HEADS + h0) * B_DK:(2 * B_HEADS + h1 + 1) * B_DK]
            beta2 = _pair_cols(beta_all, h0, h1)
            grow2 = jnp.concatenate([grow_all[g0:g0 + 1, :], grow_all[g1:g1 + 1, :]], axis=1)
            decay2 = jnp.exp(jnp.where(tri2, _pair_cols(gcol_all, g0, g1) - grow2, NEG))
            kb2 = k2 * beta2
            items.append(dict(
                blk=blk, p=p, decay2=decay2, kb2=kb2, v2b=v2 * beta2,
                kbe2=kb2 * _pair_cols(eg_all, g0, g1),
                lhs=jnp.concatenate([kb2, q2], axis=0).astype(BF16),
                kbd=_pair_bd(k2),
                qe2=(q2 * _pair_cols(eg_all, g0, g1)).astype(BF16),
                kd2=(k2 * _pair_cols(ed_all, g0, g1)).astype(BF16),
                et2=_pair_cols(et_all, g0, g1)))

    for it in items:
        kkqk = lax.dot_general(it["lhs"], it["kbd"], (((1,), (1,)), ((), ())),
                               preferred_element_type=F32)
        it["a2"] = jnp.where(strict2, kkqk[:LANES] * it["decay2"], 0.0)
        it["qk2"] = (kkqk[LANES:] * it["decay2"]).astype(BF16)
    for it in items:
        it["pw"] = jnp.dot(it["a2"].astype(BF16), _pair_bd(it["a2"]), preferred_element_type=F32)
        it["n"] = -it["a2"]
    nlev = GDN_CHUNK.bit_length() - 2
    for lev in range(nlev):
        for it in items:
            pw, n = it["pw"], it["n"]
            bd = _pair_bd(pw)
            if lev < nlev - 1:
                r = jnp.dot(jnp.concatenate([pw, n], axis=0).astype(BF16), bd, preferred_element_type=F32)
                it["pw"], npw = r[:LANES], r[LANES:]
            else:
                npw = jnp.dot(n.astype(BF16), bd, preferred_element_type=F32)
            it["n"] = n + pw + npw
    for it in items:
        us, ws = [], []
        for s in range(2):
            sl = slice(s * LANES, (s + 1) * LANES)
            rhs = jnp.concatenate([it["v2b"][:, sl], it["kbe2"][:, sl]], axis=1)
            sol = rhs + _bdot(it["n"][:, sl], rhs)
            us.append(sol[:, :B_DV])
            ws.append(sol[:, B_DV:])
        it["u2"] = jnp.concatenate(us, axis=1)
        it["w2"] = jnp.concatenate(ws, axis=1).astype(BF16)

    ri = lax.broadcasted_iota(jnp.int32, (PAIR, PAIR), 0) < LANES
    ci = lax.broadcasted_iota(jnp.int32, (PAIR, PAIR), 1) < LANES
    bdmask = ri == ci
    state = [sbd_ref[p] for p in range(npair)]
    outs = {}
    for blk in range(nblk):
        for c in range(nchunk):
            rows = slice(c * GDN_CHUNK, (c + 1) * GDN_CHUNK)
            for p in range(npair):
                it = items[blk * npair + p]
                s = state[p]
                r = jnp.dot(jnp.concatenate([it["w2"][rows], it["qe2"][rows]], axis=0), s.astype(BF16),
                            preferred_element_type=F32)
                vn2 = it["u2"][rows] - r[:GDN_CHUNK]
                vnb = vn2.astype(BF16)
                vt = _pair_bd(jnp.concatenate([vn2] * nchunk, axis=0))
                outs[(blk, p, c)] = r[GDN_CHUNK:] + jnp.dot(it["qk2"][rows], vt, preferred_element_type=F32)
                upd = lax.dot_general(it["kd2"][rows], vnb, (((0,), (0,)), ((), ())),
                                      preferred_element_type=F32)
                state[p] = s * it["et2"][c * GDN_CHUNK:c * GDN_CHUNK + 1, :] + jnp.where(bdmask, upd, 0.0)
    for p in range(npair):
        sbd_ref[p] = state[p]
        s_ref[0, 2 * p] = state[p][:LANES, :LANES]
        s_ref[0, 2 * p + 1] = state[p][LANES:, LANES:]
    for blk in range(nblk):
        r0 = blk * LANES
        for p in range(npair):
            o2 = jnp.concatenate([outs[(blk, p, c)] for c in range(nchunk)], axis=0)
            for s in range(2):
                h = 2 * p + s
                sl = slice(h * B_DV, (h + 1) * B_DV)
                o = o2[:, s * LANES:(s + 1) * LANES]
                ob_ref[0, r0:r0 + LANES, sl] = _gdn_out(o, z_ref[0, r0:r0 + LANES, sl], nw).astype(BF16)


def _gdn_prompt(raw, ba, z, cw, alog, dtb, nw, tm):
    batch, seq, _ = raw.shape
    tok = lambda n: pl.BlockSpec((1, tm, n), lambda b, t: (b, t, 0))
    return pl.pallas_call(
        functools.partial(_gdn_prompt_kernel, tm=tm),
        grid=(batch, seq // tm),
        in_specs=[tok(B_CONV_CH), tok(LANES), tok(512), _const_spec((CONV_W, B_CONV_CH)),
                  _const_spec((1, LANES)), _const_spec((1, LANES)), _const_spec((1, B_DV))],
        out_specs=[tok(512), pl.BlockSpec((1, B_HEADS, B_DK, B_DV), lambda b, t: (b, 0, 0, 0))],
        out_shape=[jax.ShapeDtypeStruct((batch, seq, 512), BF16),
                   jax.ShapeDtypeStruct((batch, B_HEADS, B_DK, B_DV), F32)],
        scratch_shapes=[pltpu.VMEM((tm + SUBLANES, B_CONV_CH), F32),
                        pltpu.VMEM((B_HEADS // 2, PAIR, PAIR), F32)],
        compiler_params=_cparams(("arbitrary", "arbitrary")),
        name="gdn_prompt",
    )(raw, ba, z, cw, alog, dtb, nw)


def _gdn_decode_kernel(raw_ref, hist_ref, ba_ref, z_ref, rec_ref, cw_ref, alog_ref, dtb_ref, nw_ref,
                       ob_ref, s_ref, buf_ref, u_s, w_s, qe_s, kd_s, qk_s, eg_s, o_s, *, rows, valid_len):
    buf_ref[:, 0:DEC_PAD, :] = hist_ref[...]
    buf_ref[:, DEC_PAD:2 * DEC_PAD, :] = raw_ref[...]
    conv = None
    for i in range(CONV_W):
        off = DEC_PAD - (CONV_W - 1) + i
        term = buf_ref[:, off:off + DEC_PAD, :] * cw_ref[i:i + 1, :]
        conv = term if conv is None else conv + term
    qkv = _silu(conv.reshape(rows * DEC_PAD, B_CONV_CH))
    tok = lax.broadcasted_iota(jnp.int32, (LANES, LANES), 0) & (DEC_PAD - 1)
    beta_all, g_all = _gdn_gates(ba_ref[...], alog_ref[...], dtb_ref[...], tok < valid_len)
    gcol_all, grow_all, gtot_all = _gdn_cumsums(g_all, DEC_PAD)
    for h in range(B_HEADS):
        q, k, v = _gdn_qkv_heads(qkv, h)
        gl = B_HEADS + h
        u, w, qe, kd, qk, egt = _gdn_phase_a(
            q, k, v, beta_all[:, h:h + 1], gcol_all[:, gl:gl + 1], grow_all[gl:gl + 1, :],
            gtot_all[:, gl:gl + 1], DEC_PAD)
        u_s[h] = u
        w_s[h] = w
        qe_s[h] = qe
        kd_s[h] = kd
        qk_s[h] = qk
        eg_s[h] = jnp.broadcast_to(egt, (LANES, LANES))

    def body(r, carry):
        r0 = pl.multiple_of(r * DEC_PAD, DEC_PAD)
        rr = pl.ds(r0, DEC_PAD)
        for h in range(B_HEADS):
            s = rec_ref[r, h]
            res = _bdot(jnp.concatenate([w_s[h, rr, :], qe_s[h, rr, :]], axis=0), s)
            v_new = u_s[h, rr, :] - res[:DEC_PAD]
            vt = jnp.concatenate([v_new] * (LANES // DEC_PAD), axis=0)
            o_s[h, rr, :] = res[DEC_PAD:] + _bdot(qk_s[h, rr, :], vt)
            s_ref[r, h] = s * eg_s[h, pl.ds(r0, 1), :] + _bdot_tn(kd_s[h, rr, :], v_new)
        return carry

    lax.fori_loop(0, rows, body, 0)
    nw = nw_ref[...]
    for h in range(B_HEADS):
        sl = slice(h * B_DV, (h + 1) * B_DV)
        ob_ref[:, sl] = _gdn_out(o_s[h], z_ref[:, sl], nw).astype(BF16)


def _gdn_decode(raw, histp, ba, z, rec, cw, alog, dtb, nw, rows, valid_len):
    nb = raw.shape[0]
    flat = rows * DEC_PAD
    assert flat == LANES
    sq = lambda: pltpu.VMEM((B_HEADS, LANES, LANES), F32)
    return pl.pallas_call(
        functools.partial(_gdn_decode_kernel, rows=rows, valid_len=valid_len),
        grid=(nb // rows,),
        in_specs=[pl.BlockSpec((rows, DEC_PAD, B_CONV_CH), lambda i: (i, 0, 0)),
                  pl.BlockSpec((rows, DEC_PAD, B_CONV_CH), lambda i: (i, 0, 0)),
                  pl.BlockSpec((flat, LANES), lambda i: (i, 0)),
                  pl.BlockSpec((flat, 512), lambda i: (i, 0)),
                  pl.BlockSpec((rows, B_HEADS, B_DK, B_DV), lambda i: (i, 0, 0, 0)),
                  _const_spec((CONV_W, B_CONV_CH)), _const_spec((1, LANES)), _const_spec((1, LANES)),
                  _const_spec((1, B_DV))],
        out_specs=[pl.BlockSpec((flat, 512), lambda i: (i, 0)),
                   pl.BlockSpec((rows, B_HEADS, B_DK, B_DV), lambda i: (i, 0, 0, 0))],
        out_shape=[jax.ShapeDtypeStruct((nb * DEC_PAD, 512), BF16),
                   jax.ShapeDtypeStruct(rec.shape, F32)],
        scratch_shapes=[pltpu.VMEM((rows, 2 * DEC_PAD, B_CONV_CH), F32)] + [sq() for _ in range(7)],
        compiler_params=_cparams(("arbitrary",)),
        name="gdn_decode",
    )(raw, histp, ba, z, rec, cw, alog, dtb, nw)


def _memkv_kernel(m_ref, g_ref, w_ref, k_ref, v_ref):
    h = _rms(m_ref[...], g_ref[...]).astype(BF16)
    n = C_HEADS * C_HD
    k_ref[...] = jnp.dot(h, w_ref[:, :n], preferred_element_type=F32)
    v_ref[...] = jnp.dot(h, w_ref[:, n:], preferred_element_type=F32)


def _memkv(mem, gain, w):
    t = mem.shape[0]
    tm = 512
    n = C_HEADS * C_HD
    return pl.pallas_call(
        _memkv_kernel,
        grid=(t // tm,),
        in_specs=[pl.BlockSpec((tm, D_MODEL), lambda i: (i, 0)), _const_spec((1, D_MODEL)),
                  _const_spec((D_MODEL, 2 * n))],
        out_specs=[pl.BlockSpec((tm, n), lambda i: (i, 0))] * 2,
        out_shape=[jax.ShapeDtypeStruct((t, n), F32)] * 2,
        compiler_params=_cparams(("arbitrary",)),
        name="memkv",
    )(mem, gain, w)


def _softmax_rows(logits):
    m = jnp.max(logits, axis=-1, keepdims=True)
    e = jnp.exp(logits - m)
    return e, 1.0 / jnp.sum(e, axis=-1, keepdims=True)


def _memattn_prompt_kernel(q_ref, k_ref, v_ref, o_ref):
    for h in range(C_HEADS):
        sl = slice(h * C_HD, (h + 1) * C_HD)
        logits = _bdot_nt(q_ref[:, sl], k_ref[:, sl]) * (C_HD ** -0.5)
        e, inv = _softmax_rows(logits)
        o_ref[:, sl] = (_bdot(e, v_ref[:, sl]) * inv).astype(BF16)


def _memattn_prompt(qc, mk, mv, batch, seq, tm):
    nq = seq // tm
    n = C_HEADS * C_HD
    cur = pl.BlockSpec((tm, n), lambda b, i: (b * nq + i, 0))
    mem = pl.BlockSpec((N_MEM, n), lambda b, i: (b, 0))
    return pl.pallas_call(
        _memattn_prompt_kernel,
        grid=(batch, nq),
        in_specs=[cur, mem, mem],
        out_specs=cur,
        out_shape=jax.ShapeDtypeStruct(qc.shape, BF16),
        compiler_params=_cparams(("arbitrary", "arbitrary")),
        name="memattn_prompt",
    )(qc, mk, mv)


def _memattn_decode_kernel(q_ref, k_ref, v_ref, o_ref, *, rows):
    n = C_HEADS * C_HD
    lane = lax.broadcasted_iota(jnp.int32, (DEC_PAD, n), 1)
    hmask = [(lane >= h * C_HD) & (lane < (h + 1) * C_HD) for h in range(C_HEADS)]

    def body(r, carry):
        q = q_ref[r].astype(F32)
        lhs = jnp.concatenate([jnp.where(hmask[h], q, 0.0) for h in range(C_HEADS)], axis=0)
        logits = _bdot_nt(lhs, k_ref[r]) * (C_HD ** -0.5)
        e, inv = _softmax_rows(logits)
        pv = _bdot(e, v_ref[r]) * inv
        o = jnp.zeros((DEC_PAD, n), F32)
        for h in range(C_HEADS):
            o = jnp.where(hmask[h], pv[h * DEC_PAD:(h + 1) * DEC_PAD], o)
        o_ref[r] = o.astype(BF16)
        return carry

    lax.fori_loop(0, rows, body, 0)


def _memattn_decode(qc, ck, cv, rows):
    nb = qc.shape[0]
    n = C_HEADS * C_HD
    blk = lambda a: pl.BlockSpec((rows, a, n), lambda i: (i, 0, 0))
    return pl.pallas_call(
        functools.partial(_memattn_decode_kernel, rows=rows),
        grid=(nb // rows,),
        in_specs=[blk(DEC_PAD), blk(N_MEM), blk(N_MEM)],
        out_specs=blk(DEC_PAD),
        out_shape=jax.ShapeDtypeStruct(qc.shape, BF16),
        compiler_params=_cparams(("arbitrary",)),
        name="memattn_decode",
    )(qc, ck, cv)


def _post_kernel(x_ref, oa_ref, ob_ref, oc_ref, gpre_ref, wg_ref, wb_ref, wo_ref, gpost_ref,
                 gfpre_ref, wfi_ref, wfo_ref, gfpost_ref, y_ref):
    x = x_ref[...]
    h = _rms(x, gpre_ref[...]).astype(BF16)
    mix = None
    for n, o_ref in enumerate((oa_ref, ob_ref, oc_ref)):
        gate = jax.nn.sigmoid(jnp.dot(h, wg_ref[:, n * D_MODEL:(n + 1) * D_MODEL],
                                      preferred_element_type=F32))
        up = jnp.dot(o_ref[...], wb_ref[n], preferred_element_type=F32)
        mix = gate * up if mix is None else mix + gate * up
    x1 = x + _rms(_bdot(mix, wo_ref[...]), gpost_ref[...])
    h2 = _rms(x1, gfpre_ref[...]).astype(BF16)
    gt = jnp.dot(h2, wfi_ref[:, :D_FF], preferred_element_type=F32)
    uf = jnp.dot(h2, wfi_ref[:, D_FF:], preferred_element_type=F32)
    f = _bdot(_silu(gt) * uf, wfo_ref[...])
    y_ref[...] = x1 + _rms(f, gfpost_ref[...])


def _post(x, oa, ob, oc, gpre, wg, wb, wo, gpost, gfpre, wfi, wfo, gfpost, tm):
    t = x.shape[0]
    row = lambda n: pl.BlockSpec((tm, n), lambda i: (i, 0))
    vec = _const_spec((1, D_MODEL))
    return pl.pallas_call(
        _post_kernel,
        grid=(t // tm,),
        in_specs=[row(D_MODEL), row(512), row(512), row(512), vec,
                  _const_spec((D_MODEL, N_BRANCH * D_MODEL)), _const_spec((N_BRANCH, BRANCH_W, D_MODEL)),
                  _const_spec((D_MODEL, D_MODEL)), vec, vec, _const_spec((D_MODEL, 2 * D_FF)),
                  _const_spec((D_FF, D_MODEL)), vec],
        out_specs=row(D_MODEL),
        out_shape=jax.ShapeDtypeStruct(x.shape, F32),
        compiler_params=_cparams(("arbitrary",)),
        name="post",
    )(x, oa, ob, oc, gpre, wg, wb, wo, gpost, gfpre, wfi, wfo, gfpost)


def _rope_tables(pos):
    half = A_HD // 2
    inv = ROPE_THETA ** (-jnp.arange(half, dtype=F32) / half)
    ang = pos.astype(F32)[:, None] * inv[None, :]
    cos, sin = jnp.cos(ang), jnp.sin(ang)
    cos = jnp.concatenate([cos, cos], axis=-1)
    sin = jnp.concatenate([-sin, sin], axis=-1)
    return jnp.tile(cos, (1, LANES // A_HD)), jnp.tile(sin, (1, LANES // A_HD))


def _lane_row(vals, offset):
    return jnp.zeros((1, LANES), F32).at[0, offset:offset + vals.shape[0]].set(vals.astype(F32))


def kernel(x_prompt, x_sample, mem_prompt, state_win_k, state_win_v, state_conv, state_rec,
           cache_mem_k, cache_mem_v, ln_mix_pre, w_in, attn_sink, gdn_conv_w, gdn_a_log,
           gdn_dt_bias, gdn_norm_w, ln_mem, w_mem_kv, w_branch, w_out, ln_mix_post,
           ln_ffn_pre, w_ffn_in, w_ffn_out, ln_ffn_post):
    bp, lp, _ = x_prompt.shape
    bs, ls, _ = x_sample.shape

    sizes = [512, 128, 128, B_CONV_CH, B_HEADS, B_HEADS, 512, 512, N_BRANCH * D_MODEL]
    o = np.cumsum([0] + sizes)
    hperm = np.concatenate([np.r_[j * A_HD:(j + 1) * A_HD, (j + 4) * A_HD:(j + 5) * A_HD] for j in range(4)])
    w1 = jnp.concatenate([
        w_in[:, o[0]:o[1]][:, hperm], w_in[:, o[1]:o[3]], w_in[:, o[3]:o[4]], w_in[:, o[6]:o[7]],
        w_in[:, o[7]:o[8]], w_in[:, o[4]:o[6]], jnp.zeros((D_MODEL, LANES - 2 * B_HEADS), F32)],
        axis=1).astype(BF16)
    wg = w_in[:, o[8]:o[9]].astype(BF16)
    wb = jnp.concatenate([w_branch[0:1][:, hperm], w_branch[1:]], axis=0).astype(BF16)
    wo = w_out.astype(BF16)
    wfi = w_ffn_in.astype(BF16)
    wfo = w_ffn_out.astype(BF16)
    wmem = w_mem_kv.astype(BF16)
    sink = attn_sink.astype(F32)[np.array([0, 4, 1, 5, 2, 6, 3, 7])]
    vec = lambda g: g.astype(F32).reshape(1, -1)
    alog = _lane_row(gdn_a_log, B_HEADS)
    dtb = _lane_row(gdn_dt_bias, B_HEADS)
    cw = gdn_conv_w.astype(F32)
    nw = vec(gdn_norm_w)

    def post(x, oa, ob, oc, tm):
        return _post(x, oa, ob, oc, vec(ln_mix_pre), wg, wb, wo, vec(ln_mix_post), vec(ln_ffn_pre),
                     wfi, wfo, vec(ln_ffn_post), tm)

    tp = bp * lp
    xp = x_prompt.reshape(tp, D_MODEL)
    cos_p, sin_p = _rope_tables(jnp.arange(lp, dtype=jnp.int32))
    qa, ka, va, qkv, z, qc, ba = _proj(xp, vec(ln_mix_pre), w1, cos_p, sin_p, 512)
    oa = _swa_prompt(sink, qa, ka, va, bp, lp, 512)
    ob, rec_p = _gdn_prompt(qkv.reshape(bp, lp, B_CONV_CH), ba.reshape(bp, lp, LANES),
                            z.reshape(bp, lp, 512), cw, alog, dtb, nw, 256)
    mk, mv = _memkv(mem_prompt.reshape(bp * N_MEM, D_MODEL), vec(ln_mem), wmem)
    oc = _memattn_prompt(qc, mk, mv, bp, lp, 512)
    y_p = post(xp, oa, ob.reshape(tp, 512), oc, 256).reshape(bp, lp, D_MODEL)
    wk_p = ka.reshape(bp, lp, A_KV, A_HD)[:, -WINDOW:]
    wv_p = va.reshape(bp, lp, A_KV, A_HD)[:, -WINDOW:]
    conv_p = qkv.reshape(bp, lp, B_CONV_CH)[:, -(CONV_W - 1):]
    mem_k_p = mk.reshape(bp, N_MEM, C_HEADS, C_HD)
    mem_v_p = mv.reshape(bp, N_MEM, C_HEADS, C_HD)

    ts = bs * DEC_PAD
    xs = jnp.pad(x_sample, ((0, 0), (0, DEC_PAD - ls), (0, 0))).reshape(ts, D_MODEL)
    cos_s, sin_s = _rope_tables(PAST_LEN + jnp.arange(DEC_PAD, dtype=jnp.int32))
    cos_s, sin_s = jnp.tile(cos_s, (bs, 1)), jnp.tile(sin_s, (bs, 1))
    qa, ka, va, qkv, z, qc, ba = _proj(xs, vec(ln_mix_pre), w1, cos_s, sin_s, 512)
    r3 = lambda a: a.reshape(bs, DEC_PAD, a.shape[-1])
    oa = _swa_decode(sink, r3(qa), r3(ka), r3(va), state_win_k.reshape(bs, WINDOW, LANES),
                     state_win_v.reshape(bs, WINDOW, LANES), 16).reshape(ts, 512)
    histp = jnp.pad(state_conv, ((0, 0), (DEC_PAD - (CONV_W - 1), 0), (0, 0)))
    ob, rec_s = _gdn_decode(r3(qkv), histp, ba, z, state_rec, cw, alog, dtb, nw,
                            LANES // DEC_PAD, ls)
    oc = _memattn_decode(r3(qc), cache_mem_k.reshape(bs, N_MEM, C_HEADS * C_HD),
                         cache_mem_v.reshape(bs, N_MEM, C_HEADS * C_HD), 8).reshape(ts, 512)
    y_s = post(xs, oa, ob, oc, 256).reshape(bs, DEC_PAD, D_MODEL)[:, :ls]
    k_new = r3(ka)[:, :ls].reshape(bs, ls, A_KV, A_HD)
    v_new = r3(va)[:, :ls].reshape(bs, ls, A_KV, A_HD)
    wk_s = jnp.concatenate([state_win_k, k_new], axis=1)[:, -WINDOW:]
    wv_s = jnp.concatenate([state_win_v, v_new], axis=1)[:, -WINDOW:]
    conv_s = jnp.concatenate([state_conv, r3(qkv)[:, :ls]], axis=1)[:, -(CONV_W - 1):]

    return (y_p, y_s, wk_p, wv_p, conv_p, rec_p, mem_k_p, mem_v_p, wk_s, wv_s, conv_s, rec_s)
```

```python
import functools

import numpy as np
import jax
import jax.numpy as jnp
from jax import lax
from jax.experimental import pallas as pl
from jax.experimental.pallas import tpu as pltpu

F32 = jnp.float32
BF16 = jnp.bfloat16

D_MODEL = 1024
PAST_LEN = 16384
EPS = 1e-6
ROPE_THETA = 10000.0
N_MEM = 256
WINDOW = 128
A_HD = 64
A_HEADS = 8
A_KV = 2
A_SCALE = A_HD ** -0.5
B_HEADS = 4
B_DK = 128
B_DV = 128
CONV_W = 4
GDN_CHUNK = 64
B_CONV_CH = B_HEADS * (2 * B_DK + B_DV)
C_HEADS = 4
C_HD = 128
N_BRANCH = 3
BRANCH_W = 512
D_FF = 2816

LANES = 128
SUBLANES = 8
PAIR = 2 * LANES
VMEM_LIMIT = 56 * 1024 * 1024
NEG = -1e30
DEC_PAD = SUBLANES

C_QA, C_KA, C_VA, C_QKV, C_Z, C_QC, C_BA, C_END = 0, 512, 640, 768, 2304, 2816, 3328, 3456


def _cparams(sem, vmem=VMEM_LIMIT):
    return pltpu.CompilerParams(dimension_semantics=sem, vmem_limit_bytes=vmem)


def _const_spec(shape):
    nd = len(shape)
    return pl.BlockSpec(shape, lambda *_: (0,) * nd, pipeline_mode=pl.Buffered(1))


def _rms(x, g):
    ms = jnp.mean(x * x, axis=-1, keepdims=True)
    return x * lax.rsqrt(ms + EPS) * g


def _bdot(a, b):
    return jnp.dot(a.astype(BF16), b.astype(BF16), preferred_element_type=F32)


def _bdot_nt(a, b):
    return lax.dot_general(a.astype(BF16), b.astype(BF16), (((1,), (1,)), ((), ())),
                           preferred_element_type=F32)


def _bdot_tn(a, b):
    return lax.dot_general(a.astype(BF16), b.astype(BF16), (((0,), (0,)), ((), ())),
                           preferred_element_type=F32)


def _hdot(a, b):
    return jnp.dot(a, b, precision=lax.Precision.HIGHEST, preferred_element_type=F32)


def _silu(x):
    return x * jax.nn.sigmoid(x)


def _softplus(x):
    return jnp.maximum(x, 0.0) + jnp.log1p(jnp.exp(-jnp.abs(x)))


def _rope128(v, cos, sin):
    lane = lax.broadcasted_iota(jnp.int32, v.shape, 1)
    fwd = pltpu.roll(v, 32, 1)
    bwd = pltpu.roll(v, 96, 1)
    sw = jnp.where((lane & 32) == 0, bwd, fwd)
    return v * cos + sw * sin


def _l2n(x):
    return x * lax.rsqrt(jnp.sum(x * x, axis=-1, keepdims=True) + EPS)


def _proj_steps(x_ref, g_ref, w_ref, cos_ref, sin_ref, qa_ref, ka_ref, va_ref, z_ref, qc_ref, ba_ref):
    h = _rms(x_ref[...], g_ref[...]).astype(BF16)
    cos = cos_ref[...]
    sin = sin_ref[...]

    def mm(a, b):
        return jnp.dot(h, w_ref[:, a:b], preferred_element_type=F32)

    def qa_half(c0):
        q = mm(C_QA + c0 * LANES, C_QA + (c0 + 2) * LANES)
        for c in range(2):
            qa_ref[:, (c0 + c) * LANES:(c0 + c + 1) * LANES] = (
                _rope128(q[:, c * LANES:(c + 1) * LANES], cos, sin) * A_SCALE).astype(BF16)

    def kv():
        kv2 = mm(C_KA, C_QKV)
        ka_ref[...] = _rope128(kv2[:, :LANES], cos, sin)
        va_ref[...] = kv2[:, LANES:]

    def z_half(c0):
        z_ref[:, c0:c0 + PAIR] = mm(C_Z + c0, C_Z + c0 + PAIR)

    def qc_half(c0):
        qc_ref[:, c0:c0 + PAIR] = mm(C_QC + c0, C_QC + c0 + PAIR).astype(BF16)

    def ba():
        ba_ref[...] = mm(C_BA, C_END)

    steps = [lambda: qa_half(0), lambda: qa_half(2), kv, lambda: z_half(0), lambda: z_half(PAIR),
             lambda: qc_half(0), lambda: qc_half(PAIR), ba]
    return mm, steps


def _proj_raw_kernel(x_ref, g_ref, w_ref, cos_ref, sin_ref,
                     qa_ref, ka_ref, va_ref, z_ref, qc_ref, ba_ref, qkv_ref):
    mm, steps = _proj_steps(x_ref, g_ref, w_ref, cos_ref, sin_ref,
                            qa_ref, ka_ref, va_ref, z_ref, qc_ref, ba_ref)
    qkv_ref[...] = mm(C_QKV, C_Z)
    for step in steps:
        step()


def _proj_conv_kernel(x_ref, g_ref, w_ref, cos_ref, sin_ref, cw_ref,
                      qa_ref, ka_ref, va_ref, z_ref, qc_ref, ba_ref, qn_ref, kn_ref, vv_ref, tail_ref,
                      buf_ref, *, tm, tiles_per_seq):
    hist = SUBLANES
    first = lax.rem(pl.program_id(0), tiles_per_seq) == 0

    @pl.when(first)
    def _():
        buf_ref[0:hist, :] = jnp.zeros((hist, B_CONV_CH), F32)

    @pl.when(jnp.logical_not(first))
    def _():
        buf_ref[0:hist, :] = buf_ref[tm:tm + hist, :]

    mm, steps = _proj_steps(x_ref, g_ref, w_ref, cos_ref, sin_ref,
                            qa_ref, ka_ref, va_ref, z_ref, qc_ref, ba_ref)
    nq = B_HEADS * B_DK

    def conv_group(c0):
        cs = slice(c0, c0 + PAIR)
        raw = mm(C_QKV + c0, C_QKV + c0 + PAIR)
        buf_ref[hist:hist + tm, cs] = raw
        tail_ref[0, :, cs] = raw[tm - hist:, :]
        conv = None
        for i in range(CONV_W):
            off = hist - (CONV_W - 1) + i
            term = buf_ref[off:off + tm, cs] * cw_ref[i:i + 1, cs]
            conv = term if conv is None else conv + term
        act = _silu(conv)
        if c0 >= 2 * nq:
            vv_ref[:, c0 - 2 * nq:c0 - 2 * nq + PAIR] = act
            return
        out_ref, base, scale = (qn_ref, 0, B_DK ** -0.5) if c0 < nq else (kn_ref, nq, 1.0)
        for s in range(2):
            o0 = c0 - base + s * B_DK
            out_ref[:, o0:o0 + B_DK] = _l2n(act[:, s * B_DK:(s + 1) * B_DK]) * scale

    groups = [functools.partial(conv_group, c0) for c0 in range(0, B_CONV_CH, PAIR)]
    while groups or steps:
        if groups:
            groups.pop(0)()
        if steps:
            steps.pop(0)()


_PROJ_OUTS = [(512, BF16), (128, F32), (128, F32), (512, F32), (512, BF16), (128, F32)]


def _proj(x, gain, w1, cos, sin, tm, cw=None, seq=None):
    t = x.shape[0]
    ntab = cos.shape[0] // tm
    row = lambda n: pl.BlockSpec((tm, n), lambda i: (i, 0))
    tab = pl.BlockSpec((tm, LANES), lambda i: (i % ntab, 0))
    in_specs = [row(D_MODEL), _const_spec((1, D_MODEL)), _const_spec((D_MODEL, C_END)), tab, tab]
    out_specs = [row(n) for n, _ in _PROJ_OUTS]
    out_shape = [jax.ShapeDtypeStruct((t, n), d) for n, d in _PROJ_OUTS]
    if cw is None:
        return pl.pallas_call(
            _proj_raw_kernel, grid=(t // tm,), in_specs=in_specs,
            out_specs=out_specs + [row(B_CONV_CH)],
            out_shape=out_shape + [jax.ShapeDtypeStruct((t, B_CONV_CH), F32)],
            compiler_params=_cparams(("arbitrary",)), name="proj",
        )(x, gain, w1, cos, sin)
    tiles = seq // tm
    return pl.pallas_call(
        functools.partial(_proj_conv_kernel, tm=tm, tiles_per_seq=tiles),
        grid=(t // tm,),
        in_specs=in_specs + [_const_spec((CONV_W, B_CONV_CH))],
        out_specs=out_specs + [row(512)] * 3
        + [pl.BlockSpec((1, SUBLANES, B_CONV_CH), lambda i: (i // tiles, 0, 0))],
        out_shape=out_shape + [jax.ShapeDtypeStruct((t, 512), F32)] * 3
        + [jax.ShapeDtypeStruct((t // seq, SUBLANES, B_CONV_CH), F32)],
        scratch_shapes=[pltpu.VMEM((tm + SUBLANES, B_CONV_CH), F32)],
        compiler_params=_cparams(("arbitrary",)), name="proj_conv",
    )(x, gain, w1, cos, sin, cw)


def _swa_core(q, k16, v16, mask, sink_ref):
    tq = q.shape[0]
    lo = lax.broadcasted_iota(jnp.int32, (tq, LANES), 1) < A_HD
    blocks = []
    for j in range(4):
        c = q[:, j * LANES:(j + 1) * LANES].astype(F32)
        blocks.append(jnp.where(lo, c, 0.0))
        blocks.append(jnp.where(lo, 0.0, c))
    lhs = jnp.concatenate(blocks, axis=0).astype(BF16)
    logits = lax.dot_general(lhs, k16, (((1,), (1,)), ((), ())), preferred_element_type=F32)
    es, inv = [], []
    for s in range(8):
        l = jnp.where(mask, logits[s * tq:(s + 1) * tq], NEG)
        sk = sink_ref[s]
        m = jnp.maximum(jnp.max(l, axis=-1, keepdims=True), sk)
        e = jnp.exp(l - m)
        den = jnp.sum(e, axis=-1, keepdims=True) + jnp.exp(sk - m)
        es.append(e.astype(BF16))
        inv.append(1.0 / den)
    pv = jnp.dot(jnp.concatenate(es, axis=0), v16, preferred_element_type=F32)
    outs = []
    for j in range(4):
        a = pv[(2 * j) * tq:(2 * j + 1) * tq] * inv[2 * j]
        b = pv[(2 * j + 1) * tq:(2 * j + 2) * tq] * inv[2 * j + 1]
        outs.append(jnp.where(lo, a, b))
    return jnp.concatenate(outs, axis=1)


def _swa_prompt_kernel(sink_ref, q_ref, kc_ref, kp_ref, vc_ref, vp_ref, o_ref, *, nblk):
    i = pl.program_id(1)
    kcat = jnp.concatenate([kp_ref[...], kc_ref[...]], axis=0).astype(BF16)
    vcat = jnp.concatenate([vp_ref[...], vc_ref[...]], axis=0).astype(BF16)
    ii = lax.broadcasted_iota(jnp.int32, (WINDOW, 2 * WINDOW), 0)
    jj = lax.broadcasted_iota(jnp.int32, (WINDOW, 2 * WINDOW), 1)
    band = (jj > ii) & (jj <= ii + WINDOW)
    for jb in range(nblk):
        mask = band
        if jb == 0:
            mask = band & ((jj >= WINDOW) | (i > 0))
        rows = slice(jb * WINDOW, (jb + 1) * WINDOW)
        kv_rows = slice(jb * WINDOW, (jb + 2) * WINDOW)
        o = _swa_core(q_ref[rows, :], kcat[kv_rows], vcat[kv_rows], mask, sink_ref)
        o_ref[rows, :] = o.astype(BF16)


def _swa_prompt(sink, qa, ka, va, batch, seq, tq):
    nq = seq // tq
    nblk = tq // WINDOW
    nw = seq // WINDOW
    cur = lambda n: pl.BlockSpec((tq, n), lambda b, i: (b * nq + i, 0))
    prev = pl.BlockSpec((WINDOW, LANES), lambda b, i: (jnp.maximum(b * nw + i * nblk - 1, 0), 0))
    return pl.pallas_call(
        functools.partial(_swa_prompt_kernel, nblk=nblk),
        grid=(batch, nq),
        in_specs=[pl.BlockSpec(memory_space=pltpu.SMEM), cur(512), cur(LANES), prev, cur(LANES), prev],
        out_specs=cur(512),
        out_shape=jax.ShapeDtypeStruct(qa.shape, BF16),
        compiler_params=_cparams(("arbitrary", "arbitrary")),
        name="swa_prompt",
    )(sink, qa, ka, ka, va, va)


def _swa_decode_kernel(sink_ref, q_ref, kn_ref, vn_ref, wk_ref, wv_ref, o_ref, *, rows):
    ii = lax.broadcasted_iota(jnp.int32, (DEC_PAD, 2 * WINDOW), 0)
    jj = lax.broadcasted_iota(jnp.int32, (DEC_PAD, 2 * WINDOW), 1)
    mask = ((jj < WINDOW) & (jj > ii)) | ((jj >= WINDOW) & (jj - WINDOW <= ii))
    pad = jnp.zeros((WINDOW - DEC_PAD, LANES), F32)

    def body(r, carry):
        kk = jnp.concatenate([wk_ref[r], kn_ref[r], pad], axis=0).astype(BF16)
        vv = jnp.concatenate([wv_ref[r], vn_ref[r], pad], axis=0).astype(BF16)
        o_ref[r] = _swa_core(q_ref[r], kk, vv, mask, sink_ref).astype(BF16)
        return carry

    lax.fori_loop(0, rows, body, 0)


def _swa_decode(sink, qa, kn, vn, wk, wv, rows):
    nb = qa.shape[0]
    blk = lambda a, n: pl.BlockSpec((rows, a, n), lambda i: (i, 0, 0))
    return pl.pallas_call(
        functools.partial(_swa_decode_kernel, rows=rows),
        grid=(nb // rows,),
        in_specs=[pl.BlockSpec(memory_space=pltpu.SMEM), blk(DEC_PAD, 512), blk(DEC_PAD, LANES),
                  blk(DEC_PAD, LANES), blk(WINDOW, LANES), blk(WINDOW, LANES)],
        out_specs=blk(DEC_PAD, 512),
        out_shape=jax.ShapeDtypeStruct(qa.shape, BF16),
        compiler_params=_cparams(("arbitrary",)),
        name="swa_decode",
    )(sink, qa, kn, vn, wk, wv)


def _gdn_gates(ba, alog, dtb, valid):
    beta = jax.nn.sigmoid(ba)
    g = -jnp.exp(alog) * _softplus(ba + dtb)
    if valid is not None:
        beta = jnp.where(valid, beta, 0.0)
        g = jnp.where(valid, g, 0.0)
    return beta, g


def _chunk_masks(chunk):
    sh = chunk.bit_length() - 1
    ri = lax.broadcasted_iota(jnp.int32, (LANES, LANES), 0)
    ci = lax.broadcasted_iota(jnp.int32, (LANES, LANES), 1)
    same = (ri >> sh) == (ci >> sh)
    return same, same & (ri >= ci), same & (ri > ci), ri == ci


def _gdn_cumsums(g_all, chunk):
    same, tri, _, _ = _chunk_masks(chunk)
    lower = jnp.where(tri, 1.0, 0.0)
    gcol = _hdot(lower, g_all)
    grow = _hdot(g_all.T, lower.T)
    gtot = _hdot(jnp.where(same, 1.0, 0.0), g_all)
    return gcol, grow, gtot


def _gdn_phase_a(q, k, v, beta, gcol, grow, gtot, chunk):
    _, tri, strict, _ = _chunk_masks(chunk)
    decay = jnp.exp(jnp.where(tri, gcol - grow, NEG))
    kb = k * beta
    kkqk = _bdot_nt(jnp.concatenate([kb, q], axis=0), k)
    a = jnp.where(strict, kkqk[:LANES] * decay, 0.0)
    qk = kkqk[LANES:] * decay
    n = -a
    apow = a
    for _ in range(chunk.bit_length() - 2):
        apow = _bdot(apow, apow)
        n = n + apow + _bdot(n, apow)
    rhs = jnp.concatenate([v * beta, kb * jnp.exp(gcol)], axis=1)
    sol = rhs + _bdot(n, rhs)
    u, w = sol[:, :B_DV], sol[:, B_DV:]
    return u, w, q * jnp.exp(gcol), k * jnp.exp(gtot - gcol), qk, jnp.exp(gtot)


def _gdn_qkv_heads(qkv, h):
    q = qkv[:, h * B_DK:(h + 1) * B_DK]
    k = qkv[:, (B_HEADS + h) * B_DK:(B_HEADS + h + 1) * B_DK]
    v = qkv[:, (2 * B_HEADS + h) * B_DK:(2 * B_HEADS + h + 1) * B_DK]
    q = q * lax.rsqrt(jnp.sum(q * q, axis=-1, keepdims=True) + EPS) * (B_DK ** -0.5)
    k = k * lax.rsqrt(jnp.sum(k * k, axis=-1, keepdims=True) + EPS)
    return q, k, v


def _gdn_out(o, z, nw):
    return _rms(o, nw) * _silu(z)


def _pair_bd(x2):
    lo = lax.broadcasted_iota(jnp.int32, x2.shape, 1) < LANES
    return jnp.concatenate([jnp.where(lo, x2, 0.0), jnp.where(lo, 0.0, x2)], axis=0).astype(BF16)


def _pair_cols(x, l0, l1):
    lo = lax.broadcasted_iota(jnp.int32, (x.shape[0], PAIR), 1) < LANES
    return jnp.where(lo, x[:, l0:l0 + 1], x[:, l1:l1 + 1])


def _gdn_prompt_kernel(q_ref, k_ref, v_ref, ba_ref, z_ref, alog_ref, dtb_ref, nw_ref,
                       ob_ref, s_ref, sbd_ref, *, nb):
    t = pl.program_id(0)
    npair = B_HEADS // 2

    @pl.when(t == 0)
    def _():
        sbd_ref[...] = jnp.zeros(sbd_ref.shape, F32)

    alog = alog_ref[...]
    dtb = dtb_ref[...]
    nw = nw_ref[...]
    nchunk = LANES // GDN_CHUNK
    nblk = nb
    _, tri, strict, _ = _chunk_masks(GDN_CHUNK)
    tri2 = jnp.concatenate([tri, tri], axis=1)
    strict2 = jnp.concatenate([strict, strict], axis=1)

    items = []
    for blk in range(nblk):
        beta_all, g_all = _gdn_gates(ba_ref[blk], alog, dtb, None)
        gcol_all, grow_all, gtot_all = _gdn_cumsums(g_all, GDN_CHUNK)
        eg_all = jnp.exp(gcol_all)
        ed_all = jnp.exp(gtot_all - gcol_all)
        et_all = jnp.exp(gtot_all)
        for p in range(npair):
            h0, h1 = 2 * p, 2 * p + 1
            g0, g1 = B_HEADS + h0, B_HEADS + h1
            psl = slice(p * PAIR, (p + 1) * PAIR)
            q2 = q_ref[blk, :, psl]
            k2 = k_ref[blk, :, psl]
            v2 = v_ref[blk, :, psl]
            beta2 = _pair_cols(beta_all, h0, h1)
            grow2 = jnp.concatenate([grow_all[g0:g0 + 1, :], grow_all[g1:g1 + 1, :]], axis=1)
            decay2 = jnp.exp(jnp.where(tri2, _pair_cols(gcol_all, g0, g1) - grow2, NEG))
            kb2 = k2 * beta2
            items.append(dict(
                blk=blk, p=p, decay2=decay2, kb2=kb2, v2b=v2 * beta2,
                kbe2=kb2 * _pair_cols(eg_all, g0, g1),
                lhs=jnp.concatenate([kb2, q2], axis=0).astype(BF16),
                kbd=_pair_bd(k2),
                qe2=(q2 * _pair_cols(eg_all, g0, g1)).astype(BF16),
                kd2=(k2 * _pair_cols(ed_all, g0, g1)).astype(BF16),
                et2=_pair_cols(et_all, g0, g1)))

    for it in items:
        kkqk = lax.dot_general(it["lhs"], it["kbd"], (((1,), (1,)), ((), ())),
                               preferred_element_type=F32)
        it["a2"] = jnp.where(strict2, kkqk[:LANES] * it["decay2"], 0.0)
        it["qk2"] = (kkqk[LANES:] * it["decay2"]).astype(BF16)
    for it in items:
        it["pw"] = jnp.dot(it["a2"].astype(BF16), _pair_bd(it["a2"]), preferred_element_type=F32)
        it["n"] = -it["a2"]
    nlev = GDN_CHUNK.bit_length() - 2
    for lev in range(nlev):
        for it in items:
            pw, n = it["pw"], it["n"]
            bd = _pair_bd(pw)
            if lev < nlev - 1:
                r = jnp.dot(jnp.concatenate([pw, n], axis=0).astype(BF16), bd, preferred_element_type=F32)
                it["pw"], npw = r[:LANES], r[LANES:]
            else:
                npw = jnp.dot(n.astype(BF16), bd, preferred_element_type=F32)
            it["n"] = n + pw + npw
    for it in items:
        us, ws = [], []
        for s in range(2):
            sl = slice(s * LANES, (s + 1) * LANES)
            rhs = jnp.concatenate([it["v2b"][:, sl], it["kbe2"][:, sl]], axis=1)
            sol = rhs + _bdot(it["n"][:, sl], rhs)
            us.append(sol[:, :B_DV])
            ws.append(sol[:, B_DV:])
        it["u2"] = jnp.concatenate(us, axis=1)
        it["w2"] = jnp.concatenate(ws, axis=1).astype(BF16)

    ri = lax.broadcasted_iota(jnp.int32, (PAIR, PAIR), 0) < LANES
    ci = lax.broadcasted_iota(jnp.int32, (PAIR, PAIR), 1) < LANES
    bdmask = ri == ci
    state = [sbd_ref[i] for i in range(len(items))]
    outs = {}
    for c in range(nchunk):
        rows = slice(c * GDN_CHUNK, (c + 1) * GDN_CHUNK)
        for i, it in enumerate(items):
            s = state[i]
            r = jnp.dot(jnp.concatenate([it["w2"][rows], it["qe2"][rows]], axis=0), s.astype(BF16),
                        preferred_element_type=F32)
            vn2 = it["u2"][rows] - r[:GDN_CHUNK]
            vt = _pair_bd(jnp.concatenate([vn2] * nchunk, axis=0))
            outs[(i, c)] = r[GDN_CHUNK:] + jnp.dot(it["qk2"][rows], vt, preferred_element_type=F32)
            upd = lax.dot_general(it["kd2"][rows], vn2.astype(BF16), (((0,), (0,)), ((), ())),
                                  preferred_element_type=F32)
            state[i] = s * it["et2"][c * GDN_CHUNK:c * GDN_CHUNK + 1, :] + jnp.where(bdmask, upd, 0.0)
    for i, it in enumerate(items):
        sbd_ref[i] = state[i]

    @pl.when(t == pl.num_programs(0) - 1)
    def _():
        for i, it in enumerate(items):
            s_ref[it["blk"], 2 * it["p"]] = state[i][:LANES, :LANES]
            s_ref[it["blk"], 2 * it["p"] + 1] = state[i][LANES:, LANES:]

    for i, it in enumerate(items):
        o2 = jnp.concatenate([outs[(i, c)] for c in range(nchunk)], axis=0)
        for s in range(2):
            h = 2 * it["p"] + s
            sl = slice(h * B_DV, (h + 1) * B_DV)
            o = o2[:, s * LANES:(s + 1) * LANES]
            ob_ref[it["blk"], :, sl] = _gdn_out(o, z_ref[it["blk"], :, sl], nw).astype(BF16)


def _gdn_prompt(q, k, v, ba, z, alog, dtb, nw):
    batch, seq, _ = q.shape
    tok = lambda n: pl.BlockSpec((batch, LANES, n), lambda t: (0, t, 0))
    state = pl.BlockSpec((batch, B_HEADS, B_DK, B_DV), lambda t: (0, 0, 0, 0))
    return pl.pallas_call(
        functools.partial(_gdn_prompt_kernel, nb=batch),
        grid=(seq // LANES,),
        in_specs=[tok(512), tok(512), tok(512), tok(LANES), tok(512),
                  _const_spec((1, LANES)), _const_spec((1, LANES)), _const_spec((1, B_DV))],
        out_specs=[tok(512), state],
        out_shape=[jax.ShapeDtypeStruct((batch, seq, 512), BF16),
                   jax.ShapeDtypeStruct((batch, B_HEADS, B_DK, B_DV), F32)],
        scratch_shapes=[pltpu.VMEM((batch * (B_HEADS // 2), PAIR, PAIR), F32)],
        compiler_params=_cparams(("arbitrary",)),
        name="gdn_prompt",
    )(q, k, v, ba, z, alog, dtb, nw)


def _gdn_decode_kernel(raw_ref, hist_ref, ba_ref, z_ref, rec_ref, cw_ref, alog_ref, dtb_ref, nw_ref,
                       ob_ref, s_ref, buf_ref, u_s, w_s, qe_s, kd_s, qk_s, eg_s, o_s, *, rows, valid_len):
    buf_ref[:, 0:DEC_PAD, :] = hist_ref[...]
    buf_ref[:, DEC_PAD:2 * DEC_PAD, :] = raw_ref[...]
    conv = None
    for i in range(CONV_W):
        off = DEC_PAD - (CONV_W - 1) + i
        term = buf_ref[:, off:off + DEC_PAD, :] * cw_ref[i:i + 1, :]
        conv = term if conv is None else conv + term
    qkv = _silu(conv.reshape(rows * DEC_PAD, B_CONV_CH))
    tok = lax.broadcasted_iota(jnp.int32, (LANES, LANES), 0) & (DEC_PAD - 1)
    beta_all, g_all = _gdn_gates(ba_ref[...], alog_ref[...], dtb_ref[...], tok < valid_len)
    gcol_all, grow_all, gtot_all = _gdn_cumsums(g_all, DEC_PAD)
    for h in range(B_HEADS):
        q, k, v = _gdn_qkv_heads(qkv, h)
        gl = B_HEADS + h
        u, w, qe, kd, qk, egt = _gdn_phase_a(
            q, k, v, beta_all[:, h:h + 1], gcol_all[:, gl:gl + 1], grow_all[gl:gl + 1, :],
            gtot_all[:, gl:gl + 1], DEC_PAD)
        u_s[h] = u
        w_s[h] = w
        qe_s[h] = qe
        kd_s[h] = kd
        qk_s[h] = qk
        eg_s[h] = jnp.broadcast_to(egt, (LANES, LANES))

    def body(r, carry):
        r0 = pl.multiple_of(r * DEC_PAD, DEC_PAD)
        rr = pl.ds(r0, DEC_PAD)
        for h in range(B_HEADS):
            s = rec_ref[r, h]
            res = _bdot(jnp.concatenate([w_s[h, rr, :], qe_s[h, rr, :]], axis=0), s)
            v_new = u_s[h, rr, :] - res[:DEC_PAD]
            vt = jnp.concatenate([v_new] * (LANES // DEC_PAD), axis=0)
            o_s[h, rr, :] = res[DEC_PAD:] + _bdot(qk_s[h, rr, :], vt)
            s_ref[r, h] = s * eg_s[h, pl.ds(r0, 1), :] + _bdot_tn(kd_s[h, rr, :], v_new)
        return carry

    lax.fori_loop(0, rows, body, 0)
    nw = nw_ref[...]
    for h in range(B_HEADS):
        sl = slice(h * B_DV, (h + 1) * B_DV)
        ob_ref[:, sl] = _gdn_out(o_s[h], z_ref[:, sl], nw).astype(BF16)


def _gdn_decode(raw, histp, ba, z, rec, cw, alog, dtb, nw, rows, valid_len):
    nb = raw.shape[0]
    flat = rows * DEC_PAD
    assert flat == LANES
    sq = lambda: pltpu.VMEM((B_HEADS, LANES, LANES), F32)
    return pl.pallas_call(
        functools.partial(_gdn_decode_kernel, rows=rows, valid_len=valid_len),
        grid=(nb // rows,),
        in_specs=[pl.BlockSpec((rows, DEC_PAD, B_CONV_CH), lambda i: (i, 0, 0)),
                  pl.BlockSpec((rows, DEC_PAD, B_CONV_CH), lambda i: (i, 0, 0)),
                  pl.BlockSpec((flat, LANES), lambda i: (i, 0)),
                  pl.BlockSpec((flat, 512), lambda i: (i, 0)),
                  pl.BlockSpec((rows, B_HEADS, B_DK, B_DV), lambda i: (i, 0, 0, 0)),
                  _const_spec((CONV_W, B_CONV_CH)), _const_spec((1, LANES)), _const_spec((1, LANES)),
                  _const_spec((1, B_DV))],
        out_specs=[pl.BlockSpec((flat, 512), lambda i: (i, 0)),
                   pl.BlockSpec((rows, B_HEADS, B_DK, B_DV), lambda i: (i, 0, 0, 0))],
        out_shape=[jax.ShapeDtypeStruct((nb * DEC_PAD, 512), BF16),
                   jax.ShapeDtypeStruct(rec.shape, F32)],
        scratch_shapes=[pltpu.VMEM((rows, 2 * DEC_PAD, B_CONV_CH), F32)] + [sq() for _ in range(7)],
        compiler_params=_cparams(("arbitrary",)),
        name="gdn_decode",
    )(raw, histp, ba, z, rec, cw, alog, dtb, nw)


def _memkv_kernel(m_ref, g_ref, w_ref, k_ref, v_ref):
    h = _rms(m_ref[...], g_ref[...]).astype(BF16)
    n = C_HEADS * C_HD
    k_ref[...] = jnp.dot(h, w_ref[:, :n], preferred_element_type=F32)
    v_ref[...] = jnp.dot(h, w_ref[:, n:], preferred_element_type=F32)


def _memkv(mem, gain, w):
    t = mem.shape[0]
    tm = 512
    n = C_HEADS * C_HD
    return pl.pallas_call(
        _memkv_kernel,
        grid=(t // tm,),
        in_specs=[pl.BlockSpec((tm, D_MODEL), lambda i: (i, 0)), _const_spec((1, D_MODEL)),
                  _const_spec((D_MODEL, 2 * n))],
        out_specs=[pl.BlockSpec((tm, n), lambda i: (i, 0))] * 2,
        out_shape=[jax.ShapeDtypeStruct((t, n), F32)] * 2,
        compiler_params=_cparams(("arbitrary",)),
        name="memkv",
    )(mem, gain, w)


def _softmax_rows(logits):
    m = jnp.max(logits, axis=-1, keepdims=True)
    e = jnp.exp(logits - m)
    return e, 1.0 / jnp.sum(e, axis=-1, keepdims=True)


def _memattn_prompt_kernel(q_ref, k_ref, v_ref, o_ref):
    for h in range(C_HEADS):
        sl = slice(h * C_HD, (h + 1) * C_HD)
        logits = _bdot_nt(q_ref[:, sl], k_ref[:, sl]) * (C_HD ** -0.5)
        e, inv = _softmax_rows(logits)
        o_ref[:, sl] = (_bdot(e, v_ref[:, sl]) * inv).astype(BF16)


def _memattn_prompt(qc, mk, mv, batch, seq, tm):
    nq = seq // tm
    n = C_HEADS * C_HD
    cur = pl.BlockSpec((tm, n), lambda b, i: (b * nq + i, 0))
    mem = pl.BlockSpec((N_MEM, n), lambda b, i: (b, 0))
    return pl.pallas_call(
        _memattn_prompt_kernel,
        grid=(batch, nq),
        in_specs=[cur, mem, mem],
        out_specs=cur,
        out_shape=jax.ShapeDtypeStruct(qc.shape, BF16),
        compiler_params=_cparams(("arbitrary", "arbitrary")),
        name="memattn_prompt",
    )(qc, mk, mv)


def _memattn_decode_kernel(q_ref, k_ref, v_ref, o_ref, *, rows):
    n = C_HEADS * C_HD
    lane = lax.broadcasted_iota(jnp.int32, (DEC_PAD, n), 1)
    hmask = [(lane >= h * C_HD) & (lane < (h + 1) * C_HD) for h in range(C_HEADS)]

    def body(r, carry):
        q = q_ref[r].astype(F32)
        lhs = jnp.concatenate([jnp.where(hmask[h], q, 0.0) for h in range(C_HEADS)], axis=0)
        logits = _bdot_nt(lhs, k_ref[r]) * (C_HD ** -0.5)
        e, inv = _softmax_rows(logits)
        pv = _bdot(e, v_ref[r]) * inv
        o = jnp.zeros((DEC_PAD, n), F32)
        for h in range(C_HEADS):
            o = jnp.where(hmask[h], pv[h * DEC_PAD:(h + 1) * DEC_PAD], o)
        o_ref[r] = o.astype(BF16)
        return carry

    lax.fori_loop(0, rows, body, 0)


def _memattn_decode(qc, ck, cv, rows):
    nb = qc.shape[0]
    n = C_HEADS * C_HD
    blk = lambda a: pl.BlockSpec((rows, a, n), lambda i: (i, 0, 0))
    return pl.pallas_call(
        functools.partial(_memattn_decode_kernel, rows=rows),
        grid=(nb // rows,),
        in_specs=[blk(DEC_PAD), blk(N_MEM), blk(N_MEM)],
        out_specs=blk(DEC_PAD),
        out_shape=jax.ShapeDtypeStruct(qc.shape, BF16),
        compiler_params=_cparams(("arbitrary",)),
        name="memattn_decode",
    )(qc, ck, cv)


def _post_kernel(x_ref, oa_ref, ob_ref, oc_ref, gpre_ref, wg_ref, wb_ref, wo_ref, gpost_ref,
                 gfpre_ref, wfi_ref, wfo_ref, gfpost_ref, y_ref):
    x = x_ref[...]
    h = _rms(x, gpre_ref[...]).astype(BF16)
    mix = None
    for n, o_ref in enumerate((oa_ref, ob_ref, oc_ref)):
        gate = jax.nn.sigmoid(jnp.dot(h, wg_ref[:, n * D_MODEL:(n + 1) * D_MODEL],
                                      preferred_element_type=F32))
        up = jnp.dot(o_ref[...], wb_ref[n], preferred_element_type=F32)
        mix = gate * up if mix is None else mix + gate * up
    x1 = x + _rms(_bdot(mix, wo_ref[...]), gpost_ref[...])
    h2 = _rms(x1, gfpre_ref[...]).astype(BF16)
    gt = jnp.dot(h2, wfi_ref[:, :D_FF], preferred_element_type=F32)
    uf = jnp.dot(h2, wfi_ref[:, D_FF:], preferred_element_type=F32)
    f = _bdot(_silu(gt) * uf, wfo_ref[...])
    y_ref[...] = x1 + _rms(f, gfpost_ref[...])


def _post(x, oa, ob, oc, gpre, wg, wb, wo, gpost, gfpre, wfi, wfo, gfpost, tm):
    t = x.shape[0]
    row = lambda n: pl.BlockSpec((tm, n), lambda i: (i, 0))
    vec = _const_spec((1, D_MODEL))
    return pl.pallas_call(
        _post_kernel,
        grid=(t // tm,),
        in_specs=[row(D_MODEL), row(512), row(512), row(512), vec,
                  _const_spec((D_MODEL, N_BRANCH * D_MODEL)), _const_spec((N_BRANCH, BRANCH_W, D_MODEL)),
                  _const_spec((D_MODEL, D_MODEL)), vec, vec, _const_spec((D_MODEL, 2 * D_FF)),
                  _const_spec((D_FF, D_MODEL)), vec],
        out_specs=row(D_MODEL),
        out_shape=jax.ShapeDtypeStruct(x.shape, F32),
        compiler_params=_cparams(("arbitrary",)),
        name="post",
    )(x, oa, ob, oc, gpre, wg, wb, wo, gpost, gfpre, wfi, wfo, gfpost)


def _rope_tables(pos):
    half = A_HD // 2
    inv = ROPE_THETA ** (-jnp.arange(half, dtype=F32) / half)
    ang = pos.astype(F32)[:, None] * inv[None, :]
    cos, sin = jnp.cos(ang), jnp.sin(ang)
    cos = jnp.concatenate([cos, cos], axis=-1)
    sin = jnp.concatenate([-sin, sin], axis=-1)
    return jnp.tile(cos, (1, LANES // A_HD)), jnp.tile(sin, (1, LANES // A_HD))


def _lane_row(vals, offset):
    return jnp.zeros((1, LANES), F32).at[0, offset:offset + vals.shape[0]].set(vals.astype(F32))


def kernel(x_prompt, x_sample, mem_prompt, state_win_k, state_win_v, state_conv, state_rec,
           cache_mem_k, cache_mem_v, ln_mix_pre, w_in, attn_sink, gdn_conv_w, gdn_a_log,
           gdn_dt_bias, gdn_norm_w, ln_mem, w_mem_kv, w_branch, w_out, ln_mix_post,
           ln_ffn_pre, w_ffn_in, w_ffn_out, ln_ffn_post):
    bp, lp, _ = x_prompt.shape
    bs, ls, _ = x_sample.shape

    sizes = [512, 128, 128, B_CONV_CH, B_HEADS, B_HEADS, 512, 512, N_BRANCH * D_MODEL]
    o = np.cumsum([0] + sizes)
    hperm = np.concatenate([np.r_[j * A_HD:(j + 1) * A_HD, (j + 4) * A_HD:(j + 5) * A_HD] for j in range(4)])
    w1 = jnp.concatenate([
        w_in[:, o[0]:o[1]][:, hperm], w_in[:, o[1]:o[3]], w_in[:, o[3]:o[4]], w_in[:, o[6]:o[7]],
        w_in[:, o[7]:o[8]], w_in[:, o[4]:o[6]], jnp.zeros((D_MODEL, LANES - 2 * B_HEADS), F32)],
        axis=1).astype(BF16)
    wg = w_in[:, o[8]:o[9]].astype(BF16)
    wb = jnp.concatenate([w_branch[0:1][:, hperm], w_branch[1:]], axis=0).astype(BF16)
    wo = w_out.astype(BF16)
    wfi = w_ffn_in.astype(BF16)
    wfo = w_ffn_out.astype(BF16)
    wmem = w_mem_kv.astype(BF16)
    sink = attn_sink.astype(F32)[np.array([0, 4, 1, 5, 2, 6, 3, 7])]
    vec = lambda g: g.astype(F32).reshape(1, -1)
    alog = _lane_row(gdn_a_log, B_HEADS)
    dtb = _lane_row(gdn_dt_bias, B_HEADS)
    cw = gdn_conv_w.astype(F32)
    nw = vec(gdn_norm_w)

    def post(x, oa, ob, oc, tm):
        return _post(x, oa, ob, oc, vec(ln_mix_pre), wg, wb, wo, vec(ln_mix_post), vec(ln_ffn_pre),
                     wfi, wfo, vec(ln_ffn_post), tm)

    tp = bp * lp
    xp = x_prompt.reshape(tp, D_MODEL)
    cos_p, sin_p = _rope_tables(jnp.arange(lp, dtype=jnp.int32))
    qa, ka, va, z, qc, ba, qn, kn, vv, tail = _proj(xp, vec(ln_mix_pre), w1, cos_p, sin_p, 512, cw, lp)
    oa = _swa_prompt(sink, qa, ka, va, bp, lp, 512)
    b3 = lambda a: a.reshape(bp, lp, a.shape[-1])
    ob, rec_p = _gdn_prompt(b3(qn), b3(kn), b3(vv), b3(ba), b3(z), alog, dtb, nw)
    mk, mv = _memkv(mem_prompt.reshape(bp * N_MEM, D_MODEL), vec(ln_mem), wmem)
    oc = _memattn_prompt(qc, mk, mv, bp, lp, 512)
    y_p = post(xp, oa, ob.reshape(tp, 512), oc, 256).reshape(bp, lp, D_MODEL)
    wk_p = ka.reshape(bp, lp, A_KV, A_HD)[:, -WINDOW:]
    wv_p = va.reshape(bp, lp, A_KV, A_HD)[:, -WINDOW:]
    conv_p = tail[:, -(CONV_W - 1):]
    mem_k_p = mk.reshape(bp, N_MEM, C_HEADS, C_HD)
    mem_v_p = mv.reshape(bp, N_MEM, C_HEADS, C_HD)

    ts = bs * DEC_PAD
    xs = jnp.pad(x_sample, ((0, 0), (0, DEC_PAD - ls), (0, 0))).reshape(ts, D_MODEL)
    cos_s, sin_s = _rope_tables(PAST_LEN + jnp.arange(DEC_PAD, dtype=jnp.int32))
    cos_s, sin_s = jnp.tile(cos_s, (bs, 1)), jnp.tile(sin_s, (bs, 1))
    qa, ka, va, z, qc, ba, qkv = _proj(xs, vec(ln_mix_pre), w1, cos_s, sin_s, 512)
    r3 = lambda a: a.reshape(bs, DEC_PAD, a.shape[-1])
    oa = _swa_decode(sink, r3(qa), r3(ka), r3(va), state_win_k.reshape(bs, WINDOW, LANES),
                     state_win_v.reshape(bs, WINDOW, LANES), 16).reshape(ts, 512)
    histp = jnp.pad(state_conv, ((0, 0), (DEC_PAD - (CONV_W - 1), 0), (0, 0)))
    ob, rec_s = _gdn_decode(r3(qkv), histp, ba, z, state_rec, cw, alog, dtb, nw,
                            LANES // DEC_PAD, ls)
    oc = _memattn_decode(r3(qc), cache_mem_k.reshape(bs, N_MEM, C_HEADS * C_HD),
                         cache_mem_v.reshape(bs, N_MEM, C_HEADS * C_HD), 8).reshape(ts, 512)
    y_s = post(xs, oa, ob, oc, 256).reshape(bs, DEC_PAD, D_MODEL)[:, :ls]
    k_new = r3(ka)[:, :ls].reshape(bs, ls, A_KV, A_HD)
    v_new = r3(va)[:, :ls].reshape(bs, ls, A_KV, A_HD)
    wk_s = jnp.concatenate([state_win_k, k_new], axis=1)[:, -WINDOW:]
    wv_s = jnp.concatenate([state_win_v, v_new], axis=1)[:, -WINDOW:]
    conv_s = jnp.concatenate([state_conv, r3(qkv)[:, :ls]], axis=1)[:, -(CONV_W - 1):]

    return (y_p, y_s, wk_p, wv_p, conv_p, rec_p, mem_k_p, mem_v_p, wk_s, wv_s, conv_s, rec_s)
```

```python
import functools

import numpy as np
import jax
import jax.numpy as jnp
from jax import lax
from jax.experimental import pallas as pl
from jax.experimental.pallas import tpu as pltpu

F32 = jnp.float32
BF16 = jnp.bfloat16

D_MODEL = 1024
PAST_LEN = 16384
EPS = 1e-6
ROPE_THETA = 10000.0
N_MEM = 256
WINDOW = 128
A_HD = 64
A_HEADS = 8
A_KV = 2
A_SCALE = A_HD ** -0.5
B_HEADS = 4
B_DK = 128
B_DV = 128
CONV_W = 4
GDN_CHUNK = 64
B_CONV_CH = B_HEADS * (2 * B_DK + B_DV)
C_HEADS = 4
C_HD = 128
N_BRANCH = 3
BRANCH_W = 512
D_FF = 2816

LANES = 128
SUBLANES = 8
PAIR = 2 * LANES
VMEM_LIMIT = 56 * 1024 * 1024
NEG = -1e30
DEC_PAD = SUBLANES

C_QA, C_KA, C_VA, C_QKV, C_Z, C_QC, C_BA, C_END = 0, 512, 640, 768, 2304, 2816, 3328, 3456


def _cparams(sem, vmem=VMEM_LIMIT):
    return pltpu.CompilerParams(dimension_semantics=sem, vmem_limit_bytes=vmem)


def _const_spec(shape):
    nd = len(shape)
    return pl.BlockSpec(shape, lambda *_: (0,) * nd, pipeline_mode=pl.Buffered(1))


def _rms(x, g):
    ms = jnp.mean(x * x, axis=-1, keepdims=True)
    return x * lax.rsqrt(ms + EPS) * g


def _bdot(a, b):
    return jnp.dot(a.astype(BF16), b.astype(BF16), preferred_element_type=F32)


def _bdot_nt(a, b):
    return lax.dot_general(a.astype(BF16), b.astype(BF16), (((1,), (1,)), ((), ())),
                           preferred_element_type=F32)


def _bdot_tn(a, b):
    return lax.dot_general(a.astype(BF16), b.astype(BF16), (((0,), (0,)), ((), ())),
                           preferred_element_type=F32)


def _hdot(a, b):
    return jnp.dot(a, b, precision=lax.Precision.HIGHEST, preferred_element_type=F32)


def _silu(x):
    return x * jax.nn.sigmoid(x)


def _softplus(x):
    return jnp.maximum(x, 0.0) + jnp.log1p(jnp.exp(-jnp.abs(x)))


def _rope128(v, cos, sin):
    lane = lax.broadcasted_iota(jnp.int32, v.shape, 1)
    fwd = pltpu.roll(v, 32, 1)
    bwd = pltpu.roll(v, 96, 1)
    sw = jnp.where((lane & 32) == 0, bwd, fwd)
    return v * cos + sw * sin


def _l2n(x):
    return x * lax.rsqrt(jnp.sum(x * x, axis=-1, keepdims=True) + EPS)


def _proj_steps(x_ref, g_ref, w_ref, cos_ref, sin_ref, qa_ref, ka_ref, va_ref, z_ref, qc_ref, ba_ref):
    h = _rms(x_ref[...], g_ref[...]).astype(BF16)
    cos = cos_ref[...]
    sin = sin_ref[...]

    def mm(a, b):
        return jnp.dot(h, w_ref[:, a:b], preferred_element_type=F32)

    def qa_half(c0):
        q = mm(C_QA + c0 * LANES, C_QA + (c0 + 2) * LANES)
        for c in range(2):
            qa_ref[:, (c0 + c) * LANES:(c0 + c + 1) * LANES] = (
                _rope128(q[:, c * LANES:(c + 1) * LANES], cos, sin) * A_SCALE).astype(BF16)

    def kv():
        kv2 = mm(C_KA, C_QKV)
        ka_ref[...] = _rope128(kv2[:, :LANES], cos, sin)
        va_ref[...] = kv2[:, LANES:]

    def z_half(c0):
        z_ref[:, c0:c0 + PAIR] = mm(C_Z + c0, C_Z + c0 + PAIR)

    def qc_half(c0):
        qc_ref[:, c0:c0 + PAIR] = mm(C_QC + c0, C_QC + c0 + PAIR).astype(BF16)

    def ba():
        ba_ref[...] = mm(C_BA, C_END)

    steps = [lambda: qa_half(0), lambda: qa_half(2), kv, lambda: z_half(0), lambda: z_half(PAIR),
             lambda: qc_half(0), lambda: qc_half(PAIR), ba]
    return mm, steps


def _proj_raw_kernel(x_ref, g_ref, w_ref, cos_ref, sin_ref,
                     qa_ref, ka_ref, va_ref, z_ref, qc_ref, ba_ref, qkv_ref):
    mm, steps = _proj_steps(x_ref, g_ref, w_ref, cos_ref, sin_ref,
                            qa_ref, ka_ref, va_ref, z_ref, qc_ref, ba_ref)
    qkv_ref[...] = mm(C_QKV, C_Z)
    for step in steps:
        step()


def _proj_conv_kernel(x_ref, g_ref, w_ref, cos_ref, sin_ref, cw_ref,
                      qa_ref, ka_ref, va_ref, z_ref, qc_ref, ba_ref, qn_ref, kn_ref, vv_ref, tail_ref,
                      buf_ref, *, tm, tiles_per_seq):
    hist = SUBLANES
    first = lax.rem(pl.program_id(0), tiles_per_seq) == 0

    @pl.when(first)
    def _():
        buf_ref[0:hist, :] = jnp.zeros((hist, B_CONV_CH), F32)

    @pl.when(jnp.logical_not(first))
    def _():
        buf_ref[0:hist, :] = buf_ref[tm:tm + hist, :]

    mm, steps = _proj_steps(x_ref, g_ref, w_ref, cos_ref, sin_ref,
                            qa_ref, ka_ref, va_ref, z_ref, qc_ref, ba_ref)
    nq = B_HEADS * B_DK

    def conv_group(c0):
        cs = slice(c0, c0 + PAIR)
        raw = mm(C_QKV + c0, C_QKV + c0 + PAIR)
        buf_ref[hist:hist + tm, cs] = raw
        tail_ref[0, :, cs] = raw[tm - hist:, :]
        conv = None
        for i in range(CONV_W):
            off = hist - (CONV_W - 1) + i
            term = buf_ref[off:off + tm, cs] * cw_ref[i:i + 1, cs]
            conv = term if conv is None else conv + term
        act = _silu(conv)
        if c0 >= 2 * nq:
            vv_ref[:, c0 - 2 * nq:c0 - 2 * nq + PAIR] = act
            return
        out_ref, base, scale = (qn_ref, 0, B_DK ** -0.5) if c0 < nq else (kn_ref, nq, 1.0)
        for s in range(2):
            o0 = c0 - base + s * B_DK
            out_ref[:, o0:o0 + B_DK] = _l2n(act[:, s * B_DK:(s + 1) * B_DK]) * scale

    groups = [functools.partial(conv_group, c0) for c0 in range(0, B_CONV_CH, PAIR)]
    while groups or steps:
        if groups:
            groups.pop(0)()
        if steps:
            steps.pop(0)()


_PROJ_OUTS = [(512, BF16), (128, F32), (128, F32), (512, F32), (512, BF16), (128, F32)]


def _proj(x, gain, w1, cos, sin, tm, cw=None, seq=None):
    t = x.shape[0]
    ntab = cos.shape[0] // tm
    row = lambda n: pl.BlockSpec((tm, n), lambda i: (i, 0))
    tab = pl.BlockSpec((tm, LANES), lambda i: (i % ntab, 0))
    in_specs = [row(D_MODEL), _const_spec((1, D_MODEL)), _const_spec((D_MODEL, C_END)), tab, tab]
    out_specs = [row(n) for n, _ in _PROJ_OUTS]
    out_shape = [jax.ShapeDtypeStruct((t, n), d) for n, d in _PROJ_OUTS]
    if cw is None:
        return pl.pallas_call(
            _proj_raw_kernel, grid=(t // tm,), in_specs=in_specs,
            out_specs=out_specs + [row(B_CONV_CH)],
            out_shape=out_shape + [jax.ShapeDtypeStruct((t, B_CONV_CH), F32)],
            compiler_params=_cparams(("arbitrary",)), name="proj",
        )(x, gain, w1, cos, sin)
    tiles = seq // tm
    return pl.pallas_call(
        functools.partial(_proj_conv_kernel, tm=tm, tiles_per_seq=tiles),
        grid=(t // tm,),
        in_specs=in_specs + [_const_spec((CONV_W, B_CONV_CH))],
        out_specs=out_specs + [row(512)] * 3
        + [pl.BlockSpec((1, SUBLANES, B_CONV_CH), lambda i: (i // tiles, 0, 0))],
        out_shape=out_shape + [jax.ShapeDtypeStruct((t, 512), F32)] * 3
        + [jax.ShapeDtypeStruct((t // seq, SUBLANES, B_CONV_CH), F32)],
        scratch_shapes=[pltpu.VMEM((tm + SUBLANES, B_CONV_CH), F32)],
        compiler_params=_cparams(("arbitrary",)), name="proj_conv",
    )(x, gain, w1, cos, sin, cw)


def _swa_core(q, k16, v16, mask, sink_ref):
    tq = q.shape[0]
    lo = lax.broadcasted_iota(jnp.int32, (tq, LANES), 1) < A_HD
    blocks = []
    for j in range(4):
        c = q[:, j * LANES:(j + 1) * LANES].astype(F32)
        blocks.append(jnp.where(lo, c, 0.0))
        blocks.append(jnp.where(lo, 0.0, c))
    lhs = jnp.concatenate(blocks, axis=0).astype(BF16)
    logits = lax.dot_general(lhs, k16, (((1,), (1,)), ((), ())), preferred_element_type=F32)
    es, inv = [], []
    for s in range(8):
        l = jnp.where(mask, logits[s * tq:(s + 1) * tq], NEG)
        sk = sink_ref[s]
        m = jnp.maximum(jnp.max(l, axis=-1, keepdims=True), sk)
        e = jnp.exp(l - m)
        den = jnp.sum(e, axis=-1, keepdims=True) + jnp.exp(sk - m)
        es.append(e.astype(BF16))
        inv.append(1.0 / den)
    pv = jnp.dot(jnp.concatenate(es, axis=0), v16, preferred_element_type=F32)
    outs = []
    for j in range(4):
        a = pv[(2 * j) * tq:(2 * j + 1) * tq] * inv[2 * j]
        b = pv[(2 * j + 1) * tq:(2 * j + 2) * tq] * inv[2 * j + 1]
        outs.append(jnp.where(lo, a, b))
    return jnp.concatenate(outs, axis=1)


def _swa_prompt_kernel(sink_ref, q_ref, kc_ref, kp_ref, vc_ref, vp_ref, o_ref, *, nblk):
    i = pl.program_id(1)
    kcat = jnp.concatenate([kp_ref[...], kc_ref[...]], axis=0).astype(BF16)
    vcat = jnp.concatenate([vp_ref[...], vc_ref[...]], axis=0).astype(BF16)
    ii = lax.broadcasted_iota(jnp.int32, (WINDOW, 2 * WINDOW), 0)
    jj = lax.broadcasted_iota(jnp.int32, (WINDOW, 2 * WINDOW), 1)
    band = (jj > ii) & (jj <= ii + WINDOW)
    for jb in range(nblk):
        mask = band
        if jb == 0:
            mask = band & ((jj >= WINDOW) | (i > 0))
        rows = slice(jb * WINDOW, (jb + 1) * WINDOW)
        kv_rows = slice(jb * WINDOW, (jb + 2) * WINDOW)
        o = _swa_core(q_ref[rows, :], kcat[kv_rows], vcat[kv_rows], mask, sink_ref)
        o_ref[rows, :] = o.astype(BF16)


def _swa_prompt(sink, qa, ka, va, batch, seq, tq):
    nq = seq // tq
    nblk = tq // WINDOW
    nw = seq // WINDOW
    cur = lambda n: pl.BlockSpec((tq, n), lambda b, i: (b * nq + i, 0))
    prev = pl.BlockSpec((WINDOW, LANES), lambda b, i: (jnp.maximum(b * nw + i * nblk - 1, 0), 0))
    return pl.pallas_call(
        functools.partial(_swa_prompt_kernel, nblk=nblk),
        grid=(batch, nq),
        in_specs=[pl.BlockSpec(memory_space=pltpu.SMEM), cur(512), cur(LANES), prev, cur(LANES), prev],
        out_specs=cur(512),
        out_shape=jax.ShapeDtypeStruct(qa.shape, BF16),
        compiler_params=_cparams(("arbitrary", "arbitrary")),
        name="swa_prompt",
    )(sink, qa, ka, ka, va, va)


ROW_UNROLL = 4


def _rows_loop(rows, load, compute, store):
    def body(g, carry):
        idx = [g * ROW_UNROLL + j for j in range(ROW_UNROLL)]
        vals = [load(r) for r in idx]
        outs = [compute(v) for v in vals]
        for r, o in zip(idx, outs):
            store(r, o)
        return carry

    lax.fori_loop(0, rows // ROW_UNROLL, body, 0)


def _swa_decode_kernel(sink_ref, q_ref, kn_ref, vn_ref, wk_ref, wv_ref, o_ref, *, rows):
    ii = lax.broadcasted_iota(jnp.int32, (DEC_PAD, 2 * WINDOW), 0)
    jj = lax.broadcasted_iota(jnp.int32, (DEC_PAD, 2 * WINDOW), 1)
    mask = ((jj < WINDOW) & (jj > ii)) | ((jj >= WINDOW) & (jj - WINDOW <= ii))
    pad = jnp.zeros((WINDOW - DEC_PAD, LANES), F32)

    def load(r):
        return q_ref[r], wk_ref[r], kn_ref[r], wv_ref[r], vn_ref[r]

    def compute(args):
        q, wk, kn, wv, vn = args
        kk = jnp.concatenate([wk, kn, pad], axis=0).astype(BF16)
        vv = jnp.concatenate([wv, vn, pad], axis=0).astype(BF16)
        return _swa_core(q, kk, vv, mask, sink_ref).astype(BF16)

    def store(r, o):
        o_ref[r] = o

    _rows_loop(rows, load, compute, store)


def _swa_decode(sink, qa, kn, vn, wk, wv, rows):
    nb = qa.shape[0]
    blk = lambda a, n: pl.BlockSpec((rows, a, n), lambda i: (i, 0, 0))
    return pl.pallas_call(
        functools.partial(_swa_decode_kernel, rows=rows),
        grid=(nb // rows,),
        in_specs=[pl.BlockSpec(memory_space=pltpu.SMEM), blk(DEC_PAD, 512), blk(DEC_PAD, LANES),
                  blk(DEC_PAD, LANES), blk(WINDOW, LANES), blk(WINDOW, LANES)],
        out_specs=blk(DEC_PAD, 512),
        out_shape=jax.ShapeDtypeStruct(qa.shape, BF16),
        compiler_params=_cparams(("arbitrary",)),
        name="swa_decode",
    )(sink, qa, kn, vn, wk, wv)


def _gdn_gates(ba, alog, dtb, valid):
    beta = jax.nn.sigmoid(ba)
    g = -jnp.exp(alog) * _softplus(ba + dtb)
    if valid is not None:
        beta = jnp.where(valid, beta, 0.0)
        g = jnp.where(valid, g, 0.0)
    return beta, g


def _chunk_masks(chunk):
    sh = chunk.bit_length() - 1
    ri = lax.broadcasted_iota(jnp.int32, (LANES, LANES), 0)
    ci = lax.broadcasted_iota(jnp.int32, (LANES, LANES), 1)
    same = (ri >> sh) == (ci >> sh)
    return same, same & (ri >= ci), same & (ri > ci), ri == ci


def _gdn_cumsums(g_all, chunk):
    same, tri, _, _ = _chunk_masks(chunk)
    lower = jnp.where(tri, 1.0, 0.0)
    gcol = _hdot(lower, g_all)
    grow = _hdot(g_all.T, lower.T)
    gtot = _hdot(jnp.where(same, 1.0, 0.0), g_all)
    return gcol, grow, gtot


def _gdn_phase_a(q, k, v, beta, gcol, grow, gtot, chunk):
    _, tri, strict, _ = _chunk_masks(chunk)
    decay = jnp.exp(jnp.where(tri, gcol - grow, NEG))
    kb = k * beta
    kkqk = _bdot_nt(jnp.concatenate([kb, q], axis=0), k)
    a = jnp.where(strict, kkqk[:LANES] * decay, 0.0)
    qk = kkqk[LANES:] * decay
    n = -a
    apow = a
    for _ in range(chunk.bit_length() - 2):
        apow = _bdot(apow, apow)
        n = n + apow + _bdot(n, apow)
    rhs = jnp.concatenate([v * beta, kb * jnp.exp(gcol)], axis=1)
    sol = rhs + _bdot(n, rhs)
    u, w = sol[:, :B_DV], sol[:, B_DV:]
    return u, w, q * jnp.exp(gcol), k * jnp.exp(gtot - gcol), qk, jnp.exp(gtot)


def _gdn_qkv_heads(qkv, h):
    q = qkv[:, h * B_DK:(h + 1) * B_DK]
    k = qkv[:, (B_HEADS + h) * B_DK:(B_HEADS + h + 1) * B_DK]
    v = qkv[:, (2 * B_HEADS + h) * B_DK:(2 * B_HEADS + h + 1) * B_DK]
    q = q * lax.rsqrt(jnp.sum(q * q, axis=-1, keepdims=True) + EPS) * (B_DK ** -0.5)
    k = k * lax.rsqrt(jnp.sum(k * k, axis=-1, keepdims=True) + EPS)
    return q, k, v


def _gdn_out(o, z, nw):
    return _rms(o, nw) * _silu(z)


def _pair_bd(x2):
    lo = lax.broadcasted_iota(jnp.int32, x2.shape, 1) < LANES
    return jnp.concatenate([jnp.where(lo, x2, 0.0), jnp.where(lo, 0.0, x2)], axis=0).astype(BF16)


def _pair_cols(x, l0, l1):
    lo = lax.broadcasted_iota(jnp.int32, (x.shape[0], PAIR), 1) < LANES
    return jnp.where(lo, x[:, l0:l0 + 1], x[:, l1:l1 + 1])


def _gdn_prompt_kernel(q_ref, k_ref, v_ref, ba_ref, z_ref, alog_ref, dtb_ref, nw_ref,
                       ob_ref, s_ref, sbd_ref, *, nb):
    t = pl.program_id(0)
    npair = B_HEADS // 2

    @pl.when(t == 0)
    def _():
        sbd_ref[...] = jnp.zeros(sbd_ref.shape, F32)

    alog = alog_ref[...]
    dtb = dtb_ref[...]
    nw = nw_ref[...]
    nchunk = LANES // GDN_CHUNK
    nblk = nb
    _, tri, strict, _ = _chunk_masks(GDN_CHUNK)
    tri2 = jnp.concatenate([tri, tri], axis=1)
    strict2 = jnp.concatenate([strict, strict], axis=1)

    items = []
    for blk in range(nblk):
        beta_all, g_all = _gdn_gates(ba_ref[blk], alog, dtb, None)
        gcol_all, grow_all, gtot_all = _gdn_cumsums(g_all, GDN_CHUNK)
        eg_all = jnp.exp(gcol_all)
        ed_all = jnp.exp(gtot_all - gcol_all)
        et_all = jnp.exp(gtot_all)
        for p in range(npair):
            h0, h1 = 2 * p, 2 * p + 1
            g0, g1 = B_HEADS + h0, B_HEADS + h1
            psl = slice(p * PAIR, (p + 1) * PAIR)
            q2 = q_ref[blk, :, psl]
            k2 = k_ref[blk, :, psl]
            v2 = v_ref[blk, :, psl]
            beta2 = _pair_cols(beta_all, h0, h1)
            grow2 = jnp.concatenate([grow_all[g0:g0 + 1, :], grow_all[g1:g1 + 1, :]], axis=1)
            decay2 = jnp.exp(jnp.where(tri2, _pair_cols(gcol_all, g0, g1) - grow2, NEG))
            kb2 = k2 * beta2
            items.append(dict(
                blk=blk, p=p, decay2=decay2, kb2=kb2, v2b=v2 * beta2,
                kbe2=kb2 * _pair_cols(eg_all, g0, g1),
                lhs=jnp.concatenate([kb2, q2], axis=0).astype(BF16),
                kbd=_pair_bd(k2),
                qe2=(q2 * _pair_cols(eg_all, g0, g1)).astype(BF16),
                kd2=(k2 * _pair_cols(ed_all, g0, g1)).astype(BF16),
                et2=_pair_cols(et_all, g0, g1)))

    for it in items:
        kkqk = lax.dot_general(it["lhs"], it["kbd"], (((1,), (1,)), ((), ())),
                               preferred_element_type=F32)
        it["a2"] = jnp.where(strict2, kkqk[:LANES] * it["decay2"], 0.0)
        it["qk2"] = (kkqk[LANES:] * it["decay2"]).astype(BF16)
    for it in items:
        it["pw"] = jnp.dot(it["a2"].astype(BF16), _pair_bd(it["a2"]), preferred_element_type=F32)
        it["n"] = -it["a2"]
    nlev = GDN_CHUNK.bit_length() - 2
    for lev in range(nlev):
        for it in items:
            pw, n = it["pw"], it["n"]
            bd = _pair_bd(pw)
            if lev < nlev - 1:
                r = jnp.dot(jnp.concatenate([pw, n], axis=0).astype(BF16), bd, preferred_element_type=F32)
                it["pw"], npw = r[:LANES], r[LANES:]
            else:
                npw = jnp.dot(n.astype(BF16), bd, preferred_element_type=F32)
            it["n"] = n + pw + npw
    for it in items:
        us, ws = [], []
        for s in range(2):
            sl = slice(s * LANES, (s + 1) * LANES)
            rhs = jnp.concatenate([it["v2b"][:, sl], it["kbe2"][:, sl]], axis=1)
            sol = rhs + _bdot(it["n"][:, sl], rhs)
            us.append(sol[:, :B_DV])
            ws.append(sol[:, B_DV:])
        it["u2"] = jnp.concatenate(us, axis=1)
        it["w2"] = jnp.concatenate(ws, axis=1).astype(BF16)

    ri = lax.broadcasted_iota(jnp.int32, (PAIR, PAIR), 0) < LANES
    ci = lax.broadcasted_iota(jnp.int32, (PAIR, PAIR), 1) < LANES
    bdmask = ri == ci
    state = [sbd_ref[i] for i in range(len(items))]
    outs = {}
    for c in range(nchunk):
        rows = slice(c * GDN_CHUNK, (c + 1) * GDN_CHUNK)
        for i, it in enumerate(items):
            s = state[i]
            r = jnp.dot(jnp.concatenate([it["w2"][rows], it["qe2"][rows]], axis=0), s.astype(BF16),
                        preferred_element_type=F32)
            vn2 = it["u2"][rows] - r[:GDN_CHUNK]
            vt = _pair_bd(jnp.concatenate([vn2] * nchunk, axis=0))
            outs[(i, c)] = r[GDN_CHUNK:] + jnp.dot(it["qk2"][rows], vt, preferred_element_type=F32)
            upd = lax.dot_general(it["kd2"][rows], vn2.astype(BF16), (((0,), (0,)), ((), ())),
                                  preferred_element_type=F32)
            state[i] = s * it["et2"][c * GDN_CHUNK:c * GDN_CHUNK + 1, :] + jnp.where(bdmask, upd, 0.0)
    for i, it in enumerate(items):
        sbd_ref[i] = state[i]

    @pl.when(t == pl.num_programs(0) - 1)
    def _():
        for i, it in enumerate(items):
            s_ref[it["blk"], 2 * it["p"]] = state[i][:LANES, :LANES]
            s_ref[it["blk"], 2 * it["p"] + 1] = state[i][LANES:, LANES:]

    for i, it in enumerate(items):
        o2 = jnp.concatenate([outs[(i, c)] for c in range(nchunk)], axis=0)
        for s in range(2):
            h = 2 * it["p"] + s
            sl = slice(h * B_DV, (h + 1) * B_DV)
            o = o2[:, s * LANES:(s + 1) * LANES]
            ob_ref[it["blk"], :, sl] = _gdn_out(o, z_ref[it["blk"], :, sl], nw).astype(BF16)


def _gdn_prompt(q, k, v, ba, z, alog, dtb, nw):
    batch, seq, _ = q.shape
    tok = lambda n: pl.BlockSpec((batch, LANES, n), lambda t: (0, t, 0))
    state = pl.BlockSpec((batch, B_HEADS, B_DK, B_DV), lambda t: (0, 0, 0, 0))
    return pl.pallas_call(
        functools.partial(_gdn_prompt_kernel, nb=batch),
        grid=(seq // LANES,),
        in_specs=[tok(512), tok(512), tok(512), tok(LANES), tok(512),
                  _const_spec((1, LANES)), _const_spec((1, LANES)), _const_spec((1, B_DV))],
        out_specs=[tok(512), state],
        out_shape=[jax.ShapeDtypeStruct((batch, seq, 512), BF16),
                   jax.ShapeDtypeStruct((batch, B_HEADS, B_DK, B_DV), F32)],
        scratch_shapes=[pltpu.VMEM((batch * (B_HEADS // 2), PAIR, PAIR), F32)],
        compiler_params=_cparams(("arbitrary",)),
        name="gdn_prompt",
    )(q, k, v, ba, z, alog, dtb, nw)


def _gdn_decode_kernel(raw_ref, hist_ref, ba_ref, z_ref, rec_ref, cw_ref, alog_ref, dtb_ref, nw_ref,
                       ob_ref, s_ref, buf_ref, u_s, w_s, qe_s, kd_s, qk_s, eg_s, o_s, *, rows, valid_len):
    buf_ref[:, 0:DEC_PAD, :] = hist_ref[...]
    buf_ref[:, DEC_PAD:2 * DEC_PAD, :] = raw_ref[...]
    conv = None
    for i in range(CONV_W):
        off = DEC_PAD - (CONV_W - 1) + i
        term = buf_ref[:, off:off + DEC_PAD, :] * cw_ref[i:i + 1, :]
        conv = term if conv is None else conv + term
    qkv = _silu(conv.reshape(rows * DEC_PAD, B_CONV_CH))
    tok = lax.broadcasted_iota(jnp.int32, (LANES, LANES), 0) & (DEC_PAD - 1)
    beta_all, g_all = _gdn_gates(ba_ref[...], alog_ref[...], dtb_ref[...], tok < valid_len)
    gcol_all, grow_all, gtot_all = _gdn_cumsums(g_all, DEC_PAD)
    for h in range(B_HEADS):
        q, k, v = _gdn_qkv_heads(qkv, h)
        gl = B_HEADS + h
        u, w, qe, kd, qk, egt = _gdn_phase_a(
            q, k, v, beta_all[:, h:h + 1], gcol_all[:, gl:gl + 1], grow_all[gl:gl + 1, :],
            gtot_all[:, gl:gl + 1], DEC_PAD)
        u_s[h] = u
        w_s[h] = w
        qe_s[h] = qe
        kd_s[h] = kd
        qk_s[h] = qk
        eg_s[h] = jnp.broadcast_to(egt, (LANES, LANES))

    def load(r):
        r0 = pl.multiple_of(r * DEC_PAD, DEC_PAD)
        rr = pl.ds(r0, DEC_PAD)
        return [(rec_ref[r, h], w_s[h, rr, :], qe_s[h, rr, :], u_s[h, rr, :], qk_s[h, rr, :],
                 kd_s[h, rr, :], eg_s[h, pl.ds(r0, 1), :]) for h in range(B_HEADS)]

    def compute(heads):
        outs = []
        for s, w, qe, u, qk, kd, eg in heads:
            res = _bdot(jnp.concatenate([w, qe], axis=0), s)
            v_new = u - res[:DEC_PAD]
            vt = jnp.concatenate([v_new] * (LANES // DEC_PAD), axis=0)
            outs.append((res[DEC_PAD:] + _bdot(qk, vt), s * eg + _bdot_tn(kd, v_new)))
        return outs

    def store(r, outs):
        rr = pl.ds(pl.multiple_of(r * DEC_PAD, DEC_PAD), DEC_PAD)
        for h, (o, s_new) in enumerate(outs):
            o_s[h, rr, :] = o
            s_ref[r, h] = s_new

    _rows_loop(rows, load, compute, store)
    nw = nw_ref[...]
    for h in range(B_HEADS):
        sl = slice(h * B_DV, (h + 1) * B_DV)
        ob_ref[:, sl] = _gdn_out(o_s[h], z_ref[:, sl], nw).astype(BF16)


def _gdn_decode(raw, histp, ba, z, rec, cw, alog, dtb, nw, rows, valid_len):
    nb = raw.shape[0]
    flat = rows * DEC_PAD
    assert flat == LANES
    sq = lambda: pltpu.VMEM((B_HEADS, LANES, LANES), F32)
    return pl.pallas_call(
        functools.partial(_gdn_decode_kernel, rows=rows, valid_len=valid_len),
        grid=(nb // rows,),
        in_specs=[pl.BlockSpec((rows, DEC_PAD, B_CONV_CH), lambda i: (i, 0, 0)),
                  pl.BlockSpec((rows, DEC_PAD, B_CONV_CH), lambda i: (i, 0, 0)),
                  pl.BlockSpec((flat, LANES), lambda i: (i, 0)),
                  pl.BlockSpec((flat, 512), lambda i: (i, 0)),
                  pl.BlockSpec((rows, B_HEADS, B_DK, B_DV), lambda i: (i, 0, 0, 0)),
                  _const_spec((CONV_W, B_CONV_CH)), _const_spec((1, LANES)), _const_spec((1, LANES)),
                  _const_spec((1, B_DV))],
        out_specs=[pl.BlockSpec((flat, 512), lambda i: (i, 0)),
                   pl.BlockSpec((rows, B_HEADS, B_DK, B_DV), lambda i: (i, 0, 0, 0))],
        out_shape=[jax.ShapeDtypeStruct((nb * DEC_PAD, 512), BF16),
                   jax.ShapeDtypeStruct(rec.shape, F32)],
        scratch_shapes=[pltpu.VMEM((rows, 2 * DEC_PAD, B_CONV_CH), F32)] + [sq() for _ in range(7)],
        compiler_params=_cparams(("arbitrary",)),
        name="gdn_decode",
    )(raw, histp, ba, z, rec, cw, alog, dtb, nw)


def _memkv_kernel(m_ref, g_ref, w_ref, k_ref, v_ref):
    h = _rms(m_ref[...], g_ref[...]).astype(BF16)
    n = C_HEADS * C_HD
    k_ref[...] = jnp.dot(h, w_ref[:, :n], preferred_element_type=F32)
    v_ref[...] = jnp.dot(h, w_ref[:, n:], preferred_element_type=F32)


def _memkv(mem, gain, w):
    t = mem.shape[0]
    tm = 512
    n = C_HEADS * C_HD
    return pl.pallas_call(
        _memkv_kernel,
        grid=(t // tm,),
        in_specs=[pl.BlockSpec((tm, D_MODEL), lambda i: (i, 0)), _const_spec((1, D_MODEL)),
                  _const_spec((D_MODEL, 2 * n))],
        out_specs=[pl.BlockSpec((tm, n), lambda i: (i, 0))] * 2,
        out_shape=[jax.ShapeDtypeStruct((t, n), F32)] * 2,
        compiler_params=_cparams(("arbitrary",)),
        name="memkv",
    )(mem, gain, w)


def _softmax_rows(logits):
    m = jnp.max(logits, axis=-1, keepdims=True)
    e = jnp.exp(logits - m)
    return e, 1.0 / jnp.sum(e, axis=-1, keepdims=True)


def _memattn_prompt_kernel(q_ref, k_ref, v_ref, o_ref):
    for h in range(C_HEADS):
        sl = slice(h * C_HD, (h + 1) * C_HD)
        logits = _bdot_nt(q_ref[:, sl], k_ref[:, sl]) * (C_HD ** -0.5)
        e, inv = _softmax_rows(logits)
        o_ref[:, sl] = (_bdot(e, v_ref[:, sl]) * inv).astype(BF16)


def _memattn_prompt(qc, mk, mv, batch, seq, tm):
    nq = seq // tm
    n = C_HEADS * C_HD
    cur = pl.BlockSpec((tm, n), lambda b, i: (b * nq + i, 0))
    mem = pl.BlockSpec((N_MEM, n), lambda b, i: (b, 0))
    return pl.pallas_call(
        _memattn_prompt_kernel,
        grid=(batch, nq),
        in_specs=[cur, mem, mem],
        out_specs=cur,
        out_shape=jax.ShapeDtypeStruct(qc.shape, BF16),
        compiler_params=_cparams(("arbitrary", "arbitrary")),
        name="memattn_prompt",
    )(qc, mk, mv)


def _memattn_decode_kernel(q_ref, k_ref, v_ref, o_ref, *, rows):
    nk = N_MEM * C_HEADS
    col = lax.broadcasted_iota(jnp.int32, (C_HEADS * DEC_PAD, nk), 1)
    row = lax.broadcasted_iota(jnp.int32, (C_HEADS * DEC_PAD, nk), 0)
    own = (col & (C_HEADS - 1)) == (row >> (DEC_PAD.bit_length() - 1))

    def load(r):
        return q_ref[r], k_ref[r], v_ref[r]

    def compute(args):
        q, k, v = args
        q = q.astype(F32)
        lhs = jnp.concatenate([q[:, h * C_HD:(h + 1) * C_HD] for h in range(C_HEADS)], axis=0)
        logits = jnp.where(own, _bdot_nt(lhs, k) * (C_HD ** -0.5), NEG)
        e, inv = _softmax_rows(logits)
        pv = _bdot(e, v) * inv
        return jnp.concatenate([pv[h * DEC_PAD:(h + 1) * DEC_PAD] for h in range(C_HEADS)],
                               axis=1).astype(BF16)

    def store(r, o):
        o_ref[r] = o

    _rows_loop(rows, load, compute, store)


def _memattn_decode(qc, ck, cv, rows):
    nb = qc.shape[0]
    n = C_HEADS * C_HD
    blk = pl.BlockSpec((rows, DEC_PAD, n), lambda i: (i, 0, 0))
    cache = pl.BlockSpec((rows, N_MEM * C_HEADS, C_HD), lambda i: (i, 0, 0))
    return pl.pallas_call(
        functools.partial(_memattn_decode_kernel, rows=rows),
        grid=(nb // rows,),
        in_specs=[blk, cache, cache],
        out_specs=blk,
        out_shape=jax.ShapeDtypeStruct(qc.shape, BF16),
        compiler_params=_cparams(("arbitrary",)),
        name="memattn_decode",
    )(qc, ck, cv)


def _post_kernel(x_ref, oa_ref, ob_ref, oc_ref, gpre_ref, wg_ref, wb_ref, wo_ref, gpost_ref,
                 gfpre_ref, wfi_ref, wfo_ref, gfpost_ref, y_ref):
    x = x_ref[...]
    h = _rms(x, gpre_ref[...]).astype(BF16)
    mix = None
    for n, o_ref in enumerate((oa_ref, ob_ref, oc_ref)):
        gate = jax.nn.sigmoid(jnp.dot(h, wg_ref[:, n * D_MODEL:(n + 1) * D_MODEL],
                                      preferred_element_type=F32))
        up = jnp.dot(o_ref[...], wb_ref[n], preferred_element_type=F32)
        mix = gate * up if mix is None else mix + gate * up
    x1 = x + _rms(_bdot(mix, wo_ref[...]), gpost_ref[...])
    h2 = _rms(x1, gfpre_ref[...]).astype(BF16)
    gt = jnp.dot(h2, wfi_ref[:, :D_FF], preferred_element_type=F32)
    uf = jnp.dot(h2, wfi_ref[:, D_FF:], preferred_element_type=F32)
    f = _bdot(_silu(gt) * uf, wfo_ref[...])
    y_ref[...] = x1 + _rms(f, gfpost_ref[...])


def _post(x, oa, ob, oc, gpre, wg, wb, wo, gpost, gfpre, wfi, wfo, gfpost, tm):
    t = x.shape[0]
    row = lambda n: pl.BlockSpec((tm, n), lambda i: (i, 0))
    vec = _const_spec((1, D_MODEL))
    return pl.pallas_call(
        _post_kernel,
        grid=(t // tm,),
        in_specs=[row(D_MODEL), row(512), row(512), row(512), vec,
                  _const_spec((D_MODEL, N_BRANCH * D_MODEL)), _const_spec((N_BRANCH, BRANCH_W, D_MODEL)),
                  _const_spec((D_MODEL, D_MODEL)), vec, vec, _const_spec((D_MODEL, 2 * D_FF)),
                  _const_spec((D_FF, D_MODEL)), vec],
        out_specs=row(D_MODEL),
        out_shape=jax.ShapeDtypeStruct(x.shape, F32),
        compiler_params=_cparams(("arbitrary",)),
        name="post",
    )(x, oa, ob, oc, gpre, wg, wb, wo, gpost, gfpre, wfi, wfo, gfpost)


def _rope_tables(pos):
    half = A_HD // 2
    inv = ROPE_THETA ** (-jnp.arange(half, dtype=F32) / half)
    ang = pos.astype(F32)[:, None] * inv[None, :]
    cos, sin = jnp.cos(ang), jnp.sin(ang)
    cos = jnp.concatenate([cos, cos], axis=-1)
    sin = jnp.concatenate([-sin, sin], axis=-1)
    return jnp.tile(cos, (1, LANES // A_HD)), jnp.tile(sin, (1, LANES // A_HD))


def _lane_row(vals, offset):
    return jnp.zeros((1, LANES), F32).at[0, offset:offset + vals.shape[0]].set(vals.astype(F32))


def kernel(x_prompt, x_sample, mem_prompt, state_win_k, state_win_v, state_conv, state_rec,
           cache_mem_k, cache_mem_v, ln_mix_pre, w_in, attn_sink, gdn_conv_w, gdn_a_log,
           gdn_dt_bias, gdn_norm_w, ln_mem, w_mem_kv, w_branch, w_out, ln_mix_post,
           ln_ffn_pre, w_ffn_in, w_ffn_out, ln_ffn_post):
    bp, lp, _ = x_prompt.shape
    bs, ls, _ = x_sample.shape

    sizes = [512, 128, 128, B_CONV_CH, B_HEADS, B_HEADS, 512, 512, N_BRANCH * D_MODEL]
    o = np.cumsum([0] + sizes)
    hperm = np.concatenate([np.r_[j * A_HD:(j + 1) * A_HD, (j + 4) * A_HD:(j + 5) * A_HD] for j in range(4)])
    w1 = jnp.concatenate([
        w_in[:, o[0]:o[1]][:, hperm], w_in[:, o[1]:o[3]], w_in[:, o[3]:o[4]], w_in[:, o[6]:o[7]],
        w_in[:, o[7]:o[8]], w_in[:, o[4]:o[6]], jnp.zeros((D_MODEL, LANES - 2 * B_HEADS), F32)],
        axis=1).astype(BF16)
    wg = w_in[:, o[8]:o[9]].astype(BF16)
    wb = jnp.concatenate([w_branch[0:1][:, hperm], w_branch[1:]], axis=0).astype(BF16)
    wo = w_out.astype(BF16)
    wfi = w_ffn_in.astype(BF16)
    wfo = w_ffn_out.astype(BF16)
    wmem = w_mem_kv.astype(BF16)
    sink = attn_sink.astype(F32)[np.array([0, 4, 1, 5, 2, 6, 3, 7])]
    vec = lambda g: g.astype(F32).reshape(1, -1)
    alog = _lane_row(gdn_a_log, B_HEADS)
    dtb = _lane_row(gdn_dt_bias, B_HEADS)
    cw = gdn_conv_w.astype(F32)
    nw = vec(gdn_norm_w)

    def post(x, oa, ob, oc, tm):
        return _post(x, oa, ob, oc, vec(ln_mix_pre), wg, wb, wo, vec(ln_mix_post), vec(ln_ffn_pre),
                     wfi, wfo, vec(ln_ffn_post), tm)

    tp = bp * lp
    xp = x_prompt.reshape(tp, D_MODEL)
    cos_p, sin_p = _rope_tables(jnp.arange(lp, dtype=jnp.int32))
    qa, ka, va, z, qc, ba, qn, kn, vv, tail = _proj(xp, vec(ln_mix_pre), w1, cos_p, sin_p, 512, cw, lp)
    oa = _swa_prompt(sink, qa, ka, va, bp, lp, 512)
    b3 = lambda a: a.reshape(bp, lp, a.shape[-1])
    ob, rec_p = _gdn_prompt(b3(qn), b3(kn), b3(vv), b3(ba), b3(z), alog, dtb, nw)
    mk, mv = _memkv(mem_prompt.reshape(bp * N_MEM, D_MODEL), vec(ln_mem), wmem)
    oc = _memattn_prompt(qc, mk, mv, bp, lp, 512)
    y_p = post(xp, oa, ob.reshape(tp, 512), oc, 256).reshape(bp, lp, D_MODEL)
    wk_p = ka.reshape(bp, lp, A_KV, A_HD)[:, -WINDOW:]
    wv_p = va.reshape(bp, lp, A_KV, A_HD)[:, -WINDOW:]
    conv_p = tail[:, -(CONV_W - 1):]
    mem_k_p = mk.reshape(bp, N_MEM, C_HEADS, C_HD)
    mem_v_p = mv.reshape(bp, N_MEM, C_HEADS, C_HD)

    ts = bs * DEC_PAD
    xs = jnp.pad(x_sample, ((0, 0), (0, DEC_PAD - ls), (0, 0))).reshape(ts, D_MODEL)
    cos_s, sin_s = _rope_tables(PAST_LEN + jnp.arange(DEC_PAD, dtype=jnp.int32))
    cos_s, sin_s = jnp.tile(cos_s, (bs, 1)), jnp.tile(sin_s, (bs, 1))
    qa, ka, va, z, qc, ba, qkv = _proj(xs, vec(ln_mix_pre), w1, cos_s, sin_s, 512)
    r3 = lambda a: a.reshape(bs, DEC_PAD, a.shape[-1])
    oa = _swa_decode(sink, r3(qa), r3(ka), r3(va), state_win_k.reshape(bs, WINDOW, LANES),
                     state_win_v.reshape(bs, WINDOW, LANES), 16).reshape(ts, 512)
    histp = jnp.pad(state_conv, ((0, 0), (DEC_PAD - (CONV_W - 1), 0), (0, 0)))
    ob, rec_s = _gdn_decode(r3(qkv), histp, ba, z, state_rec, cw, alog, dtb, nw,
                            LANES // DEC_PAD, ls)
    oc = _memattn_decode(r3(qc), cache_mem_k.reshape(bs, N_MEM * C_HEADS, C_HD),
                         cache_mem_v.reshape(bs, N_MEM * C_HEADS, C_HD), 8).reshape(ts, 512)
    y_s = post(xs, oa, ob, oc, 256).reshape(bs, DEC_PAD, D_MODEL)[:, :ls]
    k_new = r3(ka)[:, :ls].reshape(bs, ls, A_KV, A_HD)
    v_new = r3(va)[:, :ls].reshape(bs, ls, A_KV, A_HD)
    wk_s = jnp.concatenate([state_win_k, k_new], axis=1)[:, -WINDOW:]
    wv_s = jnp.concatenate([state_win_v, v_new], axis=1)[:, -WINDOW:]
    conv_s = jnp.concatenate([state_conv, r3(qkv)[:, :ls]], axis=1)[:, -(CONV_W - 1):]

    return (y_p, y_s, wk_p, wv_p, conv_p, rec_p, mem_k_p, mem_v_p, wk_s, wv_s, conv_s, rec_s)
```

```python
import functools

import numpy as np
import jax
import jax.numpy as jnp
from jax import lax
from jax.experimental import pallas as pl
from jax.experimental.pallas import tpu as pltpu

F32 = jnp.float32
BF16 = jnp.bfloat16

D_MODEL = 1024
PAST_LEN = 16384
EPS = 1e-6
ROPE_THETA = 10000.0
N_MEM = 256
WINDOW = 128
A_HD = 64
A_HEADS = 8
A_KV = 2
A_SCALE = A_HD ** -0.5
B_HEADS = 4
B_DK = 128
B_DV = 128
CONV_W = 4
GDN_CHUNK = 64
B_CONV_CH = B_HEADS * (2 * B_DK + B_DV)
C_HEADS = 4
C_HD = 128
N_BRANCH = 3
BRANCH_W = 512
D_FF = 2816

LANES = 128
SUBLANES = 8
PAIR = 2 * LANES
VMEM_LIMIT = 56 * 1024 * 1024
NEG = -1e30
DEC_PAD = SUBLANES

C_QA, C_KA, C_VA, C_QKV, C_Z, C_QC, C_BA, C_END = 0, 512, 640, 768, 2304, 2816, 3328, 3456


def _cparams(sem, vmem=VMEM_LIMIT):
    return pltpu.CompilerParams(dimension_semantics=sem, vmem_limit_bytes=vmem)


def _const_spec(shape):
    nd = len(shape)
    return pl.BlockSpec(shape, lambda *_: (0,) * nd, pipeline_mode=pl.Buffered(1))


def _rms(x, g):
    ms = jnp.mean(x * x, axis=-1, keepdims=True)
    return x * lax.rsqrt(ms + EPS) * g


def _bdot(a, b):
    return jnp.dot(a.astype(BF16), b.astype(BF16), preferred_element_type=F32)


def _bdot_nt(a, b):
    return lax.dot_general(a.astype(BF16), b.astype(BF16), (((1,), (1,)), ((), ())),
                           preferred_element_type=F32)


def _bdot_tn(a, b):
    return lax.dot_general(a.astype(BF16), b.astype(BF16), (((0,), (0,)), ((), ())),
                           preferred_element_type=F32)


def _hdot(a, b):
    return jnp.dot(a, b, precision=lax.Precision.HIGHEST, preferred_element_type=F32)


def _silu(x):
    return x * jax.nn.sigmoid(x)


def _softplus(x):
    return jnp.maximum(x, 0.0) + jnp.log1p(jnp.exp(-jnp.abs(x)))


def _rope128(v, cos, sin):
    lane = lax.broadcasted_iota(jnp.int32, v.shape, 1)
    fwd = pltpu.roll(v, 32, 1)
    bwd = pltpu.roll(v, 96, 1)
    sw = jnp.where((lane & 32) == 0, bwd, fwd)
    return v * cos + sw * sin


def _l2n(x):
    return x * lax.rsqrt(jnp.sum(x * x, axis=-1, keepdims=True) + EPS)


def _proj_steps(x_ref, g_ref, w_ref, cos_ref, sin_ref, qa_ref, ka_ref, va_ref, z_ref, qc_ref, ba_ref):
    h = _rms(x_ref[...], g_ref[...]).astype(BF16)
    cos = cos_ref[...]
    sin = sin_ref[...]

    def mm(a, b):
        return jnp.dot(h, w_ref[:, a:b], preferred_element_type=F32)

    def qa_half(c0):
        q = mm(C_QA + c0 * LANES, C_QA + (c0 + 2) * LANES)
        for c in range(2):
            qa_ref[:, (c0 + c) * LANES:(c0 + c + 1) * LANES] = (
                _rope128(q[:, c * LANES:(c + 1) * LANES], cos, sin) * A_SCALE).astype(BF16)

    def kv():
        kv2 = mm(C_KA, C_QKV)
        ka_ref[...] = _rope128(kv2[:, :LANES], cos, sin)
        va_ref[...] = kv2[:, LANES:]

    def z_half(c0):
        z_ref[:, c0:c0 + PAIR] = mm(C_Z + c0, C_Z + c0 + PAIR)

    def qc_half(c0):
        qc_ref[:, c0:c0 + PAIR] = mm(C_QC + c0, C_QC + c0 + PAIR).astype(BF16)

    def ba():
        ba_ref[...] = mm(C_BA, C_END)

    steps = [lambda: qa_half(0), lambda: qa_half(2), kv, lambda: z_half(0), lambda: z_half(PAIR),
             lambda: qc_half(0), lambda: qc_half(PAIR), ba]
    return mm, steps


def _proj_raw_kernel(x_ref, g_ref, w_ref, cos_ref, sin_ref,
                     qa_ref, ka_ref, va_ref, z_ref, qc_ref, ba_ref, qkv_ref):
    mm, steps = _proj_steps(x_ref, g_ref, w_ref, cos_ref, sin_ref,
                            qa_ref, ka_ref, va_ref, z_ref, qc_ref, ba_ref)
    qkv_ref[...] = mm(C_QKV, C_Z)
    for step in steps:
        step()


def _proj_conv_kernel(x_ref, g_ref, w_ref, cos_ref, sin_ref, cw_ref,
                      qa_ref, ka_ref, va_ref, z_ref, qc_ref, ba_ref, qn_ref, kn_ref, vv_ref, tail_ref,
                      buf_ref, *, tm, tiles_per_seq):
    hist = SUBLANES
    first = lax.rem(pl.program_id(0), tiles_per_seq) == 0

    @pl.when(first)
    def _():
        buf_ref[0:hist, :] = jnp.zeros((hist, B_CONV_CH), F32)

    @pl.when(jnp.logical_not(first))
    def _():
        buf_ref[0:hist, :] = buf_ref[tm:tm + hist, :]

    mm, steps = _proj_steps(x_ref, g_ref, w_ref, cos_ref, sin_ref,
                            qa_ref, ka_ref, va_ref, z_ref, qc_ref, ba_ref)
    nq = B_HEADS * B_DK

    def conv_group(c0):
        cs = slice(c0, c0 + PAIR)
        raw = mm(C_QKV + c0, C_QKV + c0 + PAIR)
        buf_ref[hist:hist + tm, cs] = raw
        tail_ref[0, :, cs] = raw[tm - hist:, :]
        conv = None
        for i in range(CONV_W):
            off = hist - (CONV_W - 1) + i
            term = buf_ref[off:off + tm, cs] * cw_ref[i:i + 1, cs]
            conv = term if conv is None else conv + term
        act = _silu(conv)
        if c0 >= 2 * nq:
            vv_ref[:, c0 - 2 * nq:c0 - 2 * nq + PAIR] = act
            return
        out_ref, base, scale = (qn_ref, 0, B_DK ** -0.5) if c0 < nq else (kn_ref, nq, 1.0)
        for s in range(2):
            o0 = c0 - base + s * B_DK
            out_ref[:, o0:o0 + B_DK] = _l2n(act[:, s * B_DK:(s + 1) * B_DK]) * scale

    groups = [functools.partial(conv_group, c0) for c0 in range(0, B_CONV_CH, PAIR)]
    while groups or steps:
        if groups:
            groups.pop(0)()
        if steps:
            steps.pop(0)()


_PROJ_OUTS = [(512, BF16), (128, F32), (128, F32), (512, F32), (512, BF16), (128, F32)]


def _proj(x, gain, w1, cos, sin, tm, cw=None, seq=None):
    t = x.shape[0]
    ntab = cos.shape[0] // tm
    row = lambda n: pl.BlockSpec((tm, n), lambda i: (i, 0))
    tab = pl.BlockSpec((tm, LANES), lambda i: (i % ntab, 0))
    in_specs = [row(D_MODEL), _const_spec((1, D_MODEL)), _const_spec((D_MODEL, C_END)), tab, tab]
    out_specs = [row(n) for n, _ in _PROJ_OUTS]
    out_shape = [jax.ShapeDtypeStruct((t, n), d) for n, d in _PROJ_OUTS]
    if cw is None:
        return pl.pallas_call(
            _proj_raw_kernel, grid=(t // tm,), in_specs=in_specs,
            out_specs=out_specs + [row(B_CONV_CH)],
            out_shape=out_shape + [jax.ShapeDtypeStruct((t, B_CONV_CH), F32)],
            compiler_params=_cparams(("arbitrary",)), name="proj",
        )(x, gain, w1, cos, sin)
    tiles = seq // tm
    return pl.pallas_call(
        functools.partial(_proj_conv_kernel, tm=tm, tiles_per_seq=tiles),
        grid=(t // tm,),
        in_specs=in_specs + [_const_spec((CONV_W, B_CONV_CH))],
        out_specs=out_specs + [row(512)] * 3
        + [pl.BlockSpec((1, SUBLANES, B_CONV_CH), lambda i: (i // tiles, 0, 0))],
        out_shape=out_shape + [jax.ShapeDtypeStruct((t, 512), F32)] * 3
        + [jax.ShapeDtypeStruct((t // seq, SUBLANES, B_CONV_CH), F32)],
        scratch_shapes=[pltpu.VMEM((tm + SUBLANES, B_CONV_CH), F32)],
        compiler_params=_cparams(("arbitrary",)), name="proj_conv",
    )(x, gain, w1, cos, sin, cw)


def _swa_scores(q, k16):
    tq = q.shape[0]
    lo = lax.broadcasted_iota(jnp.int32, (tq, LANES), 1) < A_HD
    blocks = []
    for j in range(4):
        c = q[:, j * LANES:(j + 1) * LANES].astype(F32)
        blocks.append(jnp.where(lo, c, 0.0))
        blocks.append(jnp.where(lo, 0.0, c))
    lhs = jnp.concatenate(blocks, axis=0).astype(BF16)
    return lax.dot_general(lhs, k16, (((1,), (1,)), ((), ())), preferred_element_type=F32)


def _swa_finish(logits, v16, mask, sink_ref):
    tq = logits.shape[0] // A_HEADS
    lo = lax.broadcasted_iota(jnp.int32, (tq, LANES), 1) < A_HD
    es, inv = [], []
    for s in range(8):
        l = jnp.where(mask, logits[s * tq:(s + 1) * tq], NEG)
        sk = sink_ref[s]
        m = jnp.maximum(jnp.max(l, axis=-1, keepdims=True), sk)
        e = jnp.exp(l - m)
        den = jnp.sum(e, axis=-1, keepdims=True) + jnp.exp(sk - m)
        es.append(e.astype(BF16))
        inv.append(1.0 / den)
    pv = jnp.dot(jnp.concatenate(es, axis=0), v16, preferred_element_type=F32)
    outs = []
    for j in range(4):
        a = pv[(2 * j) * tq:(2 * j + 1) * tq] * inv[2 * j]
        b = pv[(2 * j + 1) * tq:(2 * j + 2) * tq] * inv[2 * j + 1]
        outs.append(jnp.where(lo, a, b))
    return jnp.concatenate(outs, axis=1)


def _swa_prompt_kernel(sink_ref, q_ref, kc_ref, kp_ref, vc_ref, vp_ref, o_ref, *, nblk):
    i = pl.program_id(1)
    kcat = jnp.concatenate([kp_ref[...], kc_ref[...]], axis=0).astype(BF16)
    vcat = jnp.concatenate([vp_ref[...], vc_ref[...]], axis=0).astype(BF16)
    ii = lax.broadcasted_iota(jnp.int32, (WINDOW, 2 * WINDOW), 0)
    jj = lax.broadcasted_iota(jnp.int32, (WINDOW, 2 * WINDOW), 1)
    band = (jj > ii) & (jj <= ii + WINDOW)
    def scores(jb):
        return _swa_scores(q_ref[jb * WINDOW:(jb + 1) * WINDOW, :], kcat[jb * WINDOW:(jb + 2) * WINDOW])

    logits = scores(0)
    for jb in range(nblk):
        nxt = scores(jb + 1) if jb + 1 < nblk else None
        mask = band
        if jb == 0:
            mask = band & ((jj >= WINDOW) | (i > 0))
        o = _swa_finish(logits, vcat[jb * WINDOW:(jb + 2) * WINDOW], mask, sink_ref)
        o_ref[jb * WINDOW:(jb + 1) * WINDOW, :] = o.astype(BF16)
        logits = nxt


def _swa_prompt(sink, qa, ka, va, batch, seq, tq):
    nq = seq // tq
    nblk = tq // WINDOW
    nw = seq // WINDOW
    cur = lambda n: pl.BlockSpec((tq, n), lambda b, i: (b * nq + i, 0))
    prev = pl.BlockSpec((WINDOW, LANES), lambda b, i: (jnp.maximum(b * nw + i * nblk - 1, 0), 0))
    return pl.pallas_call(
        functools.partial(_swa_prompt_kernel, nblk=nblk),
        grid=(batch, nq),
        in_specs=[pl.BlockSpec(memory_space=pltpu.SMEM), cur(512), cur(LANES), prev, cur(LANES), prev],
        out_specs=cur(512),
        out_shape=jax.ShapeDtypeStruct(qa.shape, BF16),
        compiler_params=_cparams(("arbitrary", "arbitrary")),
        name="swa_prompt",
    )(sink, qa, ka, ka, va, va)


ROW_UNROLL = 4


def _rows_loop(rows, load, stages, store):
    def body(g, carry):
        idx = [g * ROW_UNROLL + j for j in range(ROW_UNROLL)]
        vals = [load(r) for r in idx]
        for stage in stages:
            vals = [stage(v) for v in vals]
        for r, o in zip(idx, vals):
            store(r, o)
        return carry

    lax.fori_loop(0, rows // ROW_UNROLL, body, 0)


def _swa_decode_kernel(sink_ref, q_ref, kn_ref, vn_ref, wk_ref, wv_ref, o_ref, *, rows):
    ii = lax.broadcasted_iota(jnp.int32, (DEC_PAD, 2 * WINDOW), 0)
    jj = lax.broadcasted_iota(jnp.int32, (DEC_PAD, 2 * WINDOW), 1)
    mask = ((jj < WINDOW) & (jj > ii)) | ((jj >= WINDOW) & (jj - WINDOW <= ii))
    pad = jnp.zeros((WINDOW - DEC_PAD, LANES), F32)

    def load(r):
        return q_ref[r], wk_ref[r], kn_ref[r], wv_ref[r], vn_ref[r]

    def scores(args):
        q, wk, kn, wv, vn = args
        kk = jnp.concatenate([wk, kn, pad], axis=0).astype(BF16)
        return _swa_scores(q, kk), wv, vn

    def finish(args):
        logits, wv, vn = args
        vv = jnp.concatenate([wv, vn, pad], axis=0).astype(BF16)
        return _swa_finish(logits, vv, mask, sink_ref).astype(BF16)

    def store(r, o):
        o_ref[r] = o

    _rows_loop(rows, load, [scores, finish], store)


def _swa_decode(sink, qa, kn, vn, wk, wv, rows):
    nb = qa.shape[0]
    blk = lambda a, n: pl.BlockSpec((rows, a, n), lambda i: (i, 0, 0))
    return pl.pallas_call(
        functools.partial(_swa_decode_kernel, rows=rows),
        grid=(nb // rows,),
        in_specs=[pl.BlockSpec(memory_space=pltpu.SMEM), blk(DEC_PAD, 512), blk(DEC_PAD, LANES),
                  blk(DEC_PAD, LANES), blk(WINDOW, LANES), blk(WINDOW, LANES)],
        out_specs=blk(DEC_PAD, 512),
        out_shape=jax.ShapeDtypeStruct(qa.shape, BF16),
        compiler_params=_cparams(("arbitrary",)),
        name="swa_decode",
    )(sink, qa, kn, vn, wk, wv)


def _gdn_gates(ba, alog, dtb, valid):
    beta = jax.nn.sigmoid(ba)
    g = -jnp.exp(alog) * _softplus(ba + dtb)
    if valid is not None:
        beta = jnp.where(valid, beta, 0.0)
        g = jnp.where(valid, g, 0.0)
    return beta, g


def _chunk_masks(chunk):
    sh = chunk.bit_length() - 1
    ri = lax.broadcasted_iota(jnp.int32, (LANES, LANES), 0)
    ci = lax.broadcasted_iota(jnp.int32, (LANES, LANES), 1)
    same = (ri >> sh) == (ci >> sh)
    return same, same & (ri >= ci), same & (ri > ci), ri == ci


def _gdn_cumsums(g_all, chunk):
    same, tri, _, _ = _chunk_masks(chunk)
    lower = jnp.where(tri, 1.0, 0.0)
    gcol = _hdot(lower, g_all)
    grow = _hdot(g_all.T, lower.T)
    gtot = _hdot(jnp.where(same, 1.0, 0.0), g_all)
    return gcol, grow, gtot


def _gdn_phase_a(q, k, v, beta, gcol, grow, gtot, chunk):
    _, tri, strict, _ = _chunk_masks(chunk)
    decay = jnp.exp(jnp.where(tri, gcol - grow, NEG))
    kb = k * beta
    kkqk = _bdot_nt(jnp.concatenate([kb, q], axis=0), k)
    a = jnp.where(strict, kkqk[:LANES] * decay, 0.0)
    qk = kkqk[LANES:] * decay
    n = -a
    apow = a
    for _ in range(chunk.bit_length() - 2):
        apow = _bdot(apow, apow)
        n = n + apow + _bdot(n, apow)
    rhs = jnp.concatenate([v * beta, kb * jnp.exp(gcol)], axis=1)
    sol = rhs + _bdot(n, rhs)
    u, w = sol[:, :B_DV], sol[:, B_DV:]
    return u, w, q * jnp.exp(gcol), k * jnp.exp(gtot - gcol), qk, jnp.exp(gtot)


def _gdn_qkv_heads(qkv, h):
    q = qkv[:, h * B_DK:(h + 1) * B_DK]
    k = qkv[:, (B_HEADS + h) * B_DK:(B_HEADS + h + 1) * B_DK]
    v = qkv[:, (2 * B_HEADS + h) * B_DK:(2 * B_HEADS + h + 1) * B_DK]
    q = q * lax.rsqrt(jnp.sum(q * q, axis=-1, keepdims=True) + EPS) * (B_DK ** -0.5)
    k = k * lax.rsqrt(jnp.sum(k * k, axis=-1, keepdims=True) + EPS)
    return q, k, v


def _gdn_out(o, z, nw):
    return _rms(o, nw) * _silu(z)


def _pair_bd(x2):
    lo = lax.broadcasted_iota(jnp.int32, x2.shape, 1) < LANES
    return jnp.concatenate([jnp.where(lo, x2, 0.0), jnp.where(lo, 0.0, x2)], axis=0).astype(BF16)


def _quad_bd(x4):
    blk = lax.broadcasted_iota(jnp.int32, x4.shape, 1) >> (GDN_CHUNK.bit_length() - 1)
    return jnp.concatenate([jnp.where(blk == j, x4, 0.0) for j in range(PAIR // GDN_CHUNK)],
                           axis=0).astype(BF16)


def _pair_cols(x, l0, l1):
    lo = lax.broadcasted_iota(jnp.int32, (x.shape[0], PAIR), 1) < LANES
    return jnp.where(lo, x[:, l0:l0 + 1], x[:, l1:l1 + 1])


def _gdn_prompt_kernel(q_ref, k_ref, v_ref, ba_ref, z_ref, alog_ref, dtb_ref, nw_ref,
                       ob_ref, s_ref, sbd_ref, *, nb):
    t = pl.program_id(0)
    npair = B_HEADS // 2

    @pl.when(t == 0)
    def _():
        sbd_ref[...] = jnp.zeros(sbd_ref.shape, F32)

    alog = alog_ref[...]
    dtb = dtb_ref[...]
    nw = nw_ref[...]
    nchunk = LANES // GDN_CHUNK
    nblk = nb
    _, tri, strict, _ = _chunk_masks(GDN_CHUNK)
    tri2 = jnp.concatenate([tri, tri], axis=1)
    strict2 = jnp.concatenate([strict, strict], axis=1)

    items = []
    for blk in range(nblk):
        beta_all, g_all = _gdn_gates(ba_ref[blk], alog, dtb, None)
        gcol_all, grow_all, gtot_all = _gdn_cumsums(g_all, GDN_CHUNK)
        eg_all = jnp.exp(gcol_all)
        ed_all = jnp.exp(gtot_all - gcol_all)
        et_all = jnp.exp(gtot_all)
        for p in range(npair):
            h0, h1 = 2 * p, 2 * p + 1
            g0, g1 = B_HEADS + h0, B_HEADS + h1
            psl = slice(p * PAIR, (p + 1) * PAIR)
            q2 = q_ref[blk, :, psl]
            k2 = k_ref[blk, :, psl]
            v2 = v_ref[blk, :, psl]
            beta2 = _pair_cols(beta_all, h0, h1)
            grow2 = jnp.concatenate([grow_all[g0:g0 + 1, :], grow_all[g1:g1 + 1, :]], axis=1)
            decay2 = jnp.exp(jnp.where(tri2, _pair_cols(gcol_all, g0, g1) - grow2, NEG))
            kb2 = k2 * beta2
            items.append(dict(
                blk=blk, p=p, decay2=decay2, kb2=kb2, v2b=v2 * beta2,
                kbe2=kb2 * _pair_cols(eg_all, g0, g1),
                lhs=jnp.concatenate([kb2, q2], axis=0).astype(BF16),
                kbd=_pair_bd(k2),
                qe2=(q2 * _pair_cols(eg_all, g0, g1)).astype(BF16),
                kd2=(k2 * _pair_cols(ed_all, g0, g1)).astype(BF16),
                et2=_pair_cols(et_all, g0, g1)))

    for it in items:
        kkqk = lax.dot_general(it["lhs"], it["kbd"], (((1,), (1,)), ((), ())),
                               preferred_element_type=F32)
        it["a2"] = jnp.where(strict2, kkqk[:LANES] * it["decay2"], 0.0)
        it["qk2"] = (kkqk[LANES:] * it["decay2"]).astype(BF16)
    top = (lax.broadcasted_iota(jnp.int32, (GDN_CHUNK, PAIR), 1) & GDN_CHUNK) == 0
    for it in items:
        a64 = it["a2"][:GDN_CHUNK] + it["a2"][GDN_CHUNK:]
        it["pw"] = jnp.dot(a64.astype(BF16), _quad_bd(a64), preferred_element_type=F32)
        it["n"] = -a64
    nlev = GDN_CHUNK.bit_length() - 2
    for lev in range(nlev):
        for it in items:
            pw, n = it["pw"], it["n"]
            bd = _quad_bd(pw)
            if lev < nlev - 1:
                r = jnp.dot(jnp.concatenate([pw, n], axis=0).astype(BF16), bd, preferred_element_type=F32)
                it["pw"], npw = r[:GDN_CHUNK], r[GDN_CHUNK:]
            else:
                npw = jnp.dot(n.astype(BF16), bd, preferred_element_type=F32)
            it["n"] = n + pw + npw
    for it in items:
        n64 = it["n"]
        it["n"] = jnp.concatenate([jnp.where(top, n64, 0.0), jnp.where(top, 0.0, n64)], axis=0)
    for it in items:
        us, ws = [], []
        for s in range(2):
            sl = slice(s * LANES, (s + 1) * LANES)
            rhs = jnp.concatenate([it["v2b"][:, sl], it["kbe2"][:, sl]], axis=1)
            sol = rhs + _bdot(it["n"][:, sl], rhs)
            us.append(sol[:, :B_DV])
            ws.append(sol[:, B_DV:])
        it["u2"] = jnp.concatenate(us, axis=1)
        it["w2"] = jnp.concatenate(ws, axis=1).astype(BF16)

    ri = lax.broadcasted_iota(jnp.int32, (PAIR, PAIR), 0) < LANES
    ci = lax.broadcasted_iota(jnp.int32, (PAIR, PAIR), 1) < LANES
    bdmask = ri == ci
    state = [sbd_ref[i] for i in range(len(items))]
    outs = {}
    for c in range(nchunk):
        rows = slice(c * GDN_CHUNK, (c + 1) * GDN_CHUNK)
        rs = [jnp.dot(jnp.concatenate([it["w2"][rows], it["qe2"][rows]], axis=0), state[i].astype(BF16),
                      preferred_element_type=F32) for i, it in enumerate(items)]
        for i, it in enumerate(items):
            s = state[i]
            r = rs[i]
            vn2 = it["u2"][rows] - r[:GDN_CHUNK]
            vt = _pair_bd(jnp.concatenate([vn2] * nchunk, axis=0))
            outs[(i, c)] = r[GDN_CHUNK:] + jnp.dot(it["qk2"][rows], vt, preferred_element_type=F32)
            upd = lax.dot_general(it["kd2"][rows], vn2.astype(BF16), (((0,), (0,)), ((), ())),
                                  preferred_element_type=F32)
            state[i] = s * it["et2"][c * GDN_CHUNK:c * GDN_CHUNK + 1, :] + jnp.where(bdmask, upd, 0.0)
    for i, it in enumerate(items):
        sbd_ref[i] = state[i]

    @pl.when(t == pl.num_programs(0) - 1)
    def _():
        for i, it in enumerate(items):
            s_ref[it["blk"], 2 * it["p"]] = state[i][:LANES, :LANES]
            s_ref[it["blk"], 2 * it["p"] + 1] = state[i][LANES:, LANES:]

    for i, it in enumerate(items):
        o2 = jnp.concatenate([outs[(i, c)] for c in range(nchunk)], axis=0)
        for s in range(2):
            h = 2 * it["p"] + s
            sl = slice(h * B_DV, (h + 1) * B_DV)
            o = o2[:, s * LANES:(s + 1) * LANES]
            ob_ref[it["blk"], :, sl] = _gdn_out(o, z_ref[it["blk"], :, sl], nw).astype(BF16)


def _gdn_prompt(q, k, v, ba, z, alog, dtb, nw):
    batch, seq, _ = q.shape
    tok = lambda n: pl.BlockSpec((batch, LANES, n), lambda t: (0, t, 0))
    state = pl.BlockSpec((batch, B_HEADS, B_DK, B_DV), lambda t: (0, 0, 0, 0))
    return pl.pallas_call(
        functools.partial(_gdn_prompt_kernel, nb=batch),
        grid=(seq // LANES,),
        in_specs=[tok(512), tok(512), tok(512), tok(LANES), tok(512),
                  _const_spec((1, LANES)), _const_spec((1, LANES)), _const_spec((1, B_DV))],
        out_specs=[tok(512), state],
        out_shape=[jax.ShapeDtypeStruct((batch, seq, 512), BF16),
                   jax.ShapeDtypeStruct((batch, B_HEADS, B_DK, B_DV), F32)],
        scratch_shapes=[pltpu.VMEM((batch * (B_HEADS // 2), PAIR, PAIR), F32)],
        compiler_params=_cparams(("arbitrary",)),
        name="gdn_prompt",
    )(q, k, v, ba, z, alog, dtb, nw)


def _gdn_decode_kernel(raw_ref, hist_ref, ba_ref, z_ref, rec_ref, cw_ref, alog_ref, dtb_ref, nw_ref,
                       ob_ref, s_ref, buf_ref, u_s, w_s, qe_s, kd_s, qk_s, eg_s, o_s, *, rows, valid_len):
    buf_ref[:, 0:DEC_PAD, :] = hist_ref[...]
    buf_ref[:, DEC_PAD:2 * DEC_PAD, :] = raw_ref[...]
    conv = None
    for i in range(CONV_W):
        off = DEC_PAD - (CONV_W - 1) + i
        term = buf_ref[:, off:off + DEC_PAD, :] * cw_ref[i:i + 1, :]
        conv = term if conv is None else conv + term
    qkv = _silu(conv.reshape(rows * DEC_PAD, B_CONV_CH))
    tok = lax.broadcasted_iota(jnp.int32, (LANES, LANES), 0) & (DEC_PAD - 1)
    beta_all, g_all = _gdn_gates(ba_ref[...], alog_ref[...], dtb_ref[...], tok < valid_len)
    gcol_all, grow_all, gtot_all = _gdn_cumsums(g_all, DEC_PAD)
    for h in range(B_HEADS):
        q, k, v = _gdn_qkv_heads(qkv, h)
        gl = B_HEADS + h
        u, w, qe, kd, qk, egt = _gdn_phase_a(
            q, k, v, beta_all[:, h:h + 1], gcol_all[:, gl:gl + 1], grow_all[gl:gl + 1, :],
            gtot_all[:, gl:gl + 1], DEC_PAD)
        u_s[h] = u
        w_s[h] = w
        qe_s[h] = qe
        kd_s[h] = kd
        qk_s[h] = qk
        eg_s[h] = jnp.broadcast_to(egt, (LANES, LANES))

    def load(r):
        r0 = pl.multiple_of(r * DEC_PAD, DEC_PAD)
        rr = pl.ds(r0, DEC_PAD)
        return [(rec_ref[r, h], w_s[h, rr, :], qe_s[h, rr, :], u_s[h, rr, :], qk_s[h, rr, :],
                 kd_s[h, rr, :], eg_s[h, pl.ds(r0, 1), :]) for h in range(B_HEADS)]

    def read_state(heads):
        return [(_bdot(jnp.concatenate([w, qe], axis=0), s), s, u, qk, kd, eg)
                for s, w, qe, u, qk, kd, eg in heads]

    def update(heads):
        outs = []
        for res, s, u, qk, kd, eg in heads:
            v_new = u - res[:DEC_PAD]
            vt = jnp.concatenate([v_new] * (LANES // DEC_PAD), axis=0)
            outs.append((res[DEC_PAD:] + _bdot(qk, vt), s * eg + _bdot_tn(kd, v_new)))
        return outs

    def store(r, outs):
        rr = pl.ds(pl.multiple_of(r * DEC_PAD, DEC_PAD), DEC_PAD)
        for h, (o, s_new) in enumerate(outs):
            o_s[h, rr, :] = o
            s_ref[r, h] = s_new

    _rows_loop(rows, load, [read_state, update], store)
    nw = nw_ref[...]
    for h in range(B_HEADS):
        sl = slice(h * B_DV, (h + 1) * B_DV)
        ob_ref[:, sl] = _gdn_out(o_s[h], z_ref[:, sl], nw).astype(BF16)


def _gdn_decode(raw, histp, ba, z, rec, cw, alog, dtb, nw, rows, valid_len):
    nb = raw.shape[0]
    flat = rows * DEC_PAD
    assert flat == LANES
    sq = lambda: pltpu.VMEM((B_HEADS, LANES, LANES), F32)
    return pl.pallas_call(
        functools.partial(_gdn_decode_kernel, rows=rows, valid_len=valid_len),
        grid=(nb // rows,),
        in_specs=[pl.BlockSpec((rows, DEC_PAD, B_CONV_CH), lambda i: (i, 0, 0)),
                  pl.BlockSpec((rows, DEC_PAD, B_CONV_CH), lambda i: (i, 0, 0)),
                  pl.BlockSpec((flat, LANES), lambda i: (i, 0)),
                  pl.BlockSpec((flat, 512), lambda i: (i, 0)),
                  pl.BlockSpec((rows, B_HEADS, B_DK, B_DV), lambda i: (i, 0, 0, 0)),
                  _const_spec((CONV_W, B_CONV_CH)), _const_spec((1, LANES)), _const_spec((1, LANES)),
                  _const_spec((1, B_DV))],
        out_specs=[pl.BlockSpec((flat, 512), lambda i: (i, 0)),
                   pl.BlockSpec((rows, B_HEADS, B_DK, B_DV), lambda i: (i, 0, 0, 0))],
        out_shape=[jax.ShapeDtypeStruct((nb * DEC_PAD, 512), BF16),
                   jax.ShapeDtypeStruct(rec.shape, F32)],
        scratch_shapes=[pltpu.VMEM((rows, 2 * DEC_PAD, B_CONV_CH), F32)] + [sq() for _ in range(7)],
        compiler_params=_cparams(("arbitrary",)),
        name="gdn_decode",
    )(raw, histp, ba, z, rec, cw, alog, dtb, nw)


def _memkv_kernel(m_ref, g_ref, w_ref, k_ref, v_ref):
    h = _rms(m_ref[...], g_ref[...]).astype(BF16)
    n = C_HEADS * C_HD
    k_ref[...] = jnp.dot(h, w_ref[:, :n], preferred_element_type=F32)
    v_ref[...] = jnp.dot(h, w_ref[:, n:], preferred_element_type=F32)


def _memkv(mem, gain, w):
    t = mem.shape[0]
    tm = 512
    n = C_HEADS * C_HD
    return pl.pallas_call(
        _memkv_kernel,
        grid=(t // tm,),
        in_specs=[pl.BlockSpec((tm, D_MODEL), lambda i: (i, 0)), _const_spec((1, D_MODEL)),
                  _const_spec((D_MODEL, 2 * n))],
        out_specs=[pl.BlockSpec((tm, n), lambda i: (i, 0))] * 2,
        out_shape=[jax.ShapeDtypeStruct((t, n), F32)] * 2,
        compiler_params=_cparams(("arbitrary",)),
        name="memkv",
    )(mem, gain, w)


def _softmax_rows(logits):
    m = jnp.max(logits, axis=-1, keepdims=True)
    e = jnp.exp(logits - m)
    return e, 1.0 / jnp.sum(e, axis=-1, keepdims=True)


def _memattn_prompt_kernel(q_ref, k_ref, v_ref, o_ref):
    hs = lambda h: slice(h * C_HD, (h + 1) * C_HD)
    scores = lambda h: _bdot_nt(q_ref[:, hs(h)], k_ref[:, hs(h)])
    logits = scores(0)
    for h in range(C_HEADS):
        nxt = scores(h + 1) if h + 1 < C_HEADS else None
        e, inv = _softmax_rows(logits * (C_HD ** -0.5))
        o_ref[:, hs(h)] = (_bdot(e, v_ref[:, hs(h)]) * inv).astype(BF16)
        logits = nxt


def _memattn_prompt(qc, mk, mv, batch, seq, tm):
    nq = seq // tm
    n = C_HEADS * C_HD
    cur = pl.BlockSpec((tm, n), lambda b, i: (b * nq + i, 0))
    mem = pl.BlockSpec((N_MEM, n), lambda b, i: (b, 0))
    return pl.pallas_call(
        _memattn_prompt_kernel,
        grid=(batch, nq),
        in_specs=[cur, mem, mem],
        out_specs=cur,
        out_shape=jax.ShapeDtypeStruct(qc.shape, BF16),
        compiler_params=_cparams(("arbitrary", "arbitrary")),
        name="memattn_prompt",
    )(qc, mk, mv)


def _memattn_decode_kernel(q_ref, k_ref, v_ref, o_ref, *, rows):
    nk = N_MEM * C_HEADS
    col = lax.broadcasted_iota(jnp.int32, (C_HEADS * DEC_PAD, nk), 1)
    row = lax.broadcasted_iota(jnp.int32, (C_HEADS * DEC_PAD, nk), 0)
    own = (col & (C_HEADS - 1)) == (row >> (DEC_PAD.bit_length() - 1))

    def load(r):
        return q_ref[r], k_ref[r], v_ref[r]

    def scores(args):
        q, k, v = args
        q = q.astype(F32)
        lhs = jnp.concatenate([q[:, h * C_HD:(h + 1) * C_HD] for h in range(C_HEADS)], axis=0)
        return _bdot_nt(lhs, k), v

    def finish(args):
        logits, v = args
        e, inv = _softmax_rows(jnp.where(own, logits * (C_HD ** -0.5), NEG))
        pv = _bdot(e, v) * inv
        return jnp.concatenate([pv[h * DEC_PAD:(h + 1) * DEC_PAD] for h in range(C_HEADS)],
                               axis=1).astype(BF16)

    def store(r, o):
        o_ref[r] = o

    _rows_loop(rows, load, [scores, finish], store)


def _memattn_decode(qc, ck, cv, rows):
    nb = qc.shape[0]
    n = C_HEADS * C_HD
    blk = pl.BlockSpec((rows, DEC_PAD, n), lambda i: (i, 0, 0))
    cache = pl.BlockSpec((rows, N_MEM * C_HEADS, C_HD), lambda i: (i, 0, 0))
    return pl.pallas_call(
        functools.partial(_memattn_decode_kernel, rows=rows),
        grid=(nb // rows,),
        in_specs=[blk, cache, cache],
        out_specs=blk,
        out_shape=jax.ShapeDtypeStruct(qc.shape, BF16),
        compiler_params=_cparams(("arbitrary",)),
        name="memattn_decode",
    )(qc, ck, cv)


def _post_kernel(x_ref, oa_ref, ob_ref, oc_ref, gpre_ref, wg_ref, wb_ref, wo_ref, gpost_ref,
                 gfpre_ref, wfi_ref, wfo_ref, gfpost_ref, y_ref):
    x = x_ref[...]
    h = _rms(x, gpre_ref[...]).astype(BF16)
    mix = None
    for n, o_ref in enumerate((oa_ref, ob_ref, oc_ref)):
        gate = jax.nn.sigmoid(jnp.dot(h, wg_ref[:, n * D_MODEL:(n + 1) * D_MODEL],
                                      preferred_element_type=F32))
        up = jnp.dot(o_ref[...], wb_ref[n], preferred_element_type=F32)
        mix = gate * up if mix is None else mix + gate * up
    x1 = x + _rms(_bdot(mix, wo_ref[...]), gpost_ref[...])
    h2 = _rms(x1, gfpre_ref[...]).astype(BF16)
    gt = jnp.dot(h2, wfi_ref[:, :D_FF], preferred_element_type=F32)
    uf = jnp.dot(h2, wfi_ref[:, D_FF:], preferred_element_type=F32)
    f = _bdot(_silu(gt) * uf, wfo_ref[...])
    y_ref[...] = x1 + _rms(f, gfpost_ref[...])


def _post(x, oa, ob, oc, gpre, wg, wb, wo, gpost, gfpre, wfi, wfo, gfpost, tm):
    t = x.shape[0]
    row = lambda n: pl.BlockSpec((tm, n), lambda i: (i, 0))
    vec = _const_spec((1, D_MODEL))
    return pl.pallas_call(
        _post_kernel,
        grid=(t // tm,),
        in_specs=[row(D_MODEL), row(512), row(512), row(512), vec,
                  _const_spec((D_MODEL, N_BRANCH * D_MODEL)), _const_spec((N_BRANCH, BRANCH_W, D_MODEL)),
                  _const_spec((D_MODEL, D_MODEL)), vec, vec, _const_spec((D_MODEL, 2 * D_FF)),
                  _const_spec((D_FF, D_MODEL)), vec],
        out_specs=row(D_MODEL),
        out_shape=jax.ShapeDtypeStruct(x.shape, F32),
        compiler_params=_cparams(("arbitrary",)),
        name="post",
    )(x, oa, ob, oc, gpre, wg, wb, wo, gpost, gfpre, wfi, wfo, gfpost)


def _rope_tables(pos):
    half = A_HD // 2
    inv = ROPE_THETA ** (-jnp.arange(half, dtype=F32) / half)
    ang = pos.astype(F32)[:, None] * inv[None, :]
    cos, sin = jnp.cos(ang), jnp.sin(ang)
    cos = jnp.concatenate([cos, cos], axis=-1)
    sin = jnp.concatenate([-sin, sin], axis=-1)
    return jnp.tile(cos, (1, LANES // A_HD)), jnp.tile(sin, (1, LANES // A_HD))


def _lane_row(vals, offset):
    return jnp.zeros((1, LANES), F32).at[0, offset:offset + vals.shape[0]].set(vals.astype(F32))


def kernel(x_prompt, x_sample, mem_prompt, state_win_k, state_win_v, state_conv, state_rec,
           cache_mem_k, cache_mem_v, ln_mix_pre, w_in, attn_sink, gdn_conv_w, gdn_a_log,
           gdn_dt_bias, gdn_norm_w, ln_mem, w_mem_kv, w_branch, w_out, ln_mix_post,
           ln_ffn_pre, w_ffn_in, w_ffn_out, ln_ffn_post):
    bp, lp, _ = x_prompt.shape
    bs, ls, _ = x_sample.shape

    sizes = [512, 128, 128, B_CONV_CH, B_HEADS, B_HEADS, 512, 512, N_BRANCH * D_MODEL]
    o = np.cumsum([0] + sizes)
    hperm = np.concatenate([np.r_[j * A_HD:(j + 1) * A_HD, (j + 4) * A_HD:(j + 5) * A_HD] for j in range(4)])
    w1 = jnp.concatenate([
        w_in[:, o[0]:o[1]][:, hperm], w_in[:, o[1]:o[3]], w_in[:, o[3]:o[4]], w_in[:, o[6]:o[7]],
        w_in[:, o[7]:o[8]], w_in[:, o[4]:o[6]], jnp.zeros((D_MODEL, LANES - 2 * B_HEADS), F32)],
        axis=1).astype(BF16)
    wg = w_in[:, o[8]:o[9]].astype(BF16)
    wb = jnp.concatenate([w_branch[0:1][:, hperm], w_branch[1:]], axis=0).astype(BF16)
    wo = w_out.astype(BF16)
    wfi = w_ffn_in.astype(BF16)
    wfo = w_ffn_out.astype(BF16)
    wmem = w_mem_kv.astype(BF16)
    sink = attn_sink.astype(F32)[np.array([0, 4, 1, 5, 2, 6, 3, 7])]
    vec = lambda g: g.astype(F32).reshape(1, -1)
    alog = _lane_row(gdn_a_log, B_HEADS)
    dtb = _lane_row(gdn_dt_bias, B_HEADS)
    cw = gdn_conv_w.astype(F32)
    nw = vec(gdn_norm_w)

    def post(x, oa, ob, oc, tm):
        return _post(x, oa, ob, oc, vec(ln_mix_pre), wg, wb, wo, vec(ln_mix_post), vec(ln_ffn_pre),
                     wfi, wfo, vec(ln_ffn_post), tm)

    tp = bp * lp
    xp = x_prompt.reshape(tp, D_MODEL)
    cos_p, sin_p = _rope_tables(jnp.arange(lp, dtype=jnp.int32))
    qa, ka, va, z, qc, ba, qn, kn, vv, tail = _proj(xp, vec(ln_mix_pre), w1, cos_p, sin_p, 512, cw, lp)
    oa = _swa_prompt(sink, qa, ka, va, bp, lp, 512)
    b3 = lambda a: a.reshape(bp, lp, a.shape[-1])
    ob, rec_p = _gdn_prompt(b3(qn), b3(kn), b3(vv), b3(ba), b3(z), alog, dtb, nw)
    mk, mv = _memkv(mem_prompt.reshape(bp * N_MEM, D_MODEL), vec(ln_mem), wmem)
    oc = _memattn_prompt(qc, mk, mv, bp, lp, 512)
    y_p = post(xp, oa, ob.reshape(tp, 512), oc, 256).reshape(bp, lp, D_MODEL)
    wk_p = ka.reshape(bp, lp, A_KV, A_HD)[:, -WINDOW:]
    wv_p = va.reshape(bp, lp, A_KV, A_HD)[:, -WINDOW:]
    conv_p = tail[:, -(CONV_W - 1):]
    mem_k_p = mk.reshape(bp, N_MEM, C_HEADS, C_HD)
    mem_v_p = mv.reshape(bp, N_MEM, C_HEADS, C_HD)

    ts = bs * DEC_PAD
    xs = jnp.pad(x_sample, ((0, 0), (0, DEC_PAD - ls), (0, 0))).reshape(ts, D_MODEL)
    cos_s, sin_s = _rope_tables(PAST_LEN + jnp.arange(DEC_PAD, dtype=jnp.int32))
    cos_s, sin_s = jnp.tile(cos_s, (bs, 1)), jnp.tile(sin_s, (bs, 1))
    qa, ka, va, z, qc, ba, qkv = _proj(xs, vec(ln_mix_pre), w1, cos_s, sin_s, 512)
    r3 = lambda a: a.reshape(bs, DEC_PAD, a.shape[-1])
    oa = _swa_decode(sink, r3(qa), r3(ka), r3(va), state_win_k.reshape(bs, WINDOW, LANES),
                     state_win_v.reshape(bs, WINDOW, LANES), 16).reshape(ts, 512)
    histp = jnp.pad(state_conv, ((0, 0), (DEC_PAD - (CONV_W - 1), 0), (0, 0)))
    ob, rec_s = _gdn_decode(r3(qkv), histp, ba, z, state_rec, cw, alog, dtb, nw,
                            LANES // DEC_PAD, ls)
    oc = _memattn_decode(r3(qc), cache_mem_k.reshape(bs, N_MEM * C_HEADS, C_HD),
                         cache_mem_v.reshape(bs, N_MEM * C_HEADS, C_HD), 8).reshape(ts, 512)
    y_s = post(xs, oa, ob, oc, 256).reshape(bs, DEC_PAD, D_MODEL)[:, :ls]
    k_new = r3(ka)[:, :ls].reshape(bs, ls, A_KV, A_HD)
    v_new = r3(va)[:, :ls].reshape(bs, ls, A_KV, A_HD)
    wk_s = jnp.concatenate([state_win_k[:, ls:], k_new], axis=1)
    wv_s = jnp.concatenate([state_win_v[:, ls:], v_new], axis=1)
    conv_s = r3(qkv)[:, ls - (CONV_W - 1):ls]

    return (y_p, y_s, wk_p, wv_p, conv_p, rec_p, mem_k_p, mem_v_p, wk_s, wv_s, conv_s, rec_s)
```

```python
import functools

import numpy as np
import jax
import jax.numpy as jnp
from jax import lax
from jax.experimental import pallas as pl
from jax.experimental.pallas import tpu as pltpu

F32 = jnp.float32
BF16 = jnp.bfloat16

D_MODEL = 1024
PAST_LEN = 16384
EPS = 1e-6
ROPE_THETA = 10000.0
N_MEM = 256
WINDOW = 128
A_HD = 64
A_HEADS = 8
A_KV = 2
A_SCALE = A_HD ** -0.5
B_HEADS = 4
B_DK = 128
B_DV = 128
CONV_W = 4
GDN_CHUNK = 64
B_CONV_CH = B_HEADS * (2 * B_DK + B_DV)
C_HEADS = 4
C_HD = 128
N_BRANCH = 3
BRANCH_W = 512
D_FF = 2816

LANES = 128
SUBLANES = 8
PAIR = 2 * LANES
VMEM_LIMIT = 56 * 1024 * 1024
NEG = -1e30
DEC_PAD = SUBLANES

C_QA, C_KA, C_VA, C_QKV, C_Z, C_QC, C_BA, C_END = 0, 512, 640, 768, 2304, 2816, 3328, 3456


def _cparams(sem, vmem=VMEM_LIMIT):
    return pltpu.CompilerParams(dimension_semantics=sem, vmem_limit_bytes=vmem)


def _const_spec(shape):
    nd = len(shape)
    return pl.BlockSpec(shape, lambda *_: (0,) * nd, pipeline_mode=pl.Buffered(1))


def _rms(x, g):
    ms = jnp.mean(x * x, axis=-1, keepdims=True)
    return x * lax.rsqrt(ms + EPS) * g


def _bdot(a, b):
    return jnp.dot(a.astype(BF16), b.astype(BF16), preferred_element_type=F32)


def _bdot_nt(a, b):
    return lax.dot_general(a.astype(BF16), b.astype(BF16), (((1,), (1,)), ((), ())),
                           preferred_element_type=F32)


def _bdot_tn(a, b):
    return lax.dot_general(a.astype(BF16), b.astype(BF16), (((0,), (0,)), ((), ())),
                           preferred_element_type=F32)


def _hdot(a, b):
    return jnp.dot(a, b, precision=lax.Precision.HIGHEST, preferred_element_type=F32)


def _silu(x):
    return x * jax.nn.sigmoid(x)


def _softplus(x):
    return jnp.maximum(x, 0.0) + jnp.log1p(jnp.exp(-jnp.abs(x)))


def _rope128(v, cos, sin):
    lane = lax.broadcasted_iota(jnp.int32, v.shape, 1)
    fwd = pltpu.roll(v, 32, 1)
    bwd = pltpu.roll(v, 96, 1)
    sw = jnp.where((lane & 32) == 0, bwd, fwd)
    return v * cos + sw * sin


def _l2n(x):
    return x * lax.rsqrt(jnp.sum(x * x, axis=-1, keepdims=True) + EPS)


def _proj_steps(x_ref, g_ref, w_ref, cos_ref, sin_ref, qa_ref, ka_ref, va_ref, z_ref, qc_ref, ba_ref):
    h = _rms(x_ref[...], g_ref[...]).astype(BF16)
    cos = cos_ref[...]
    sin = sin_ref[...]

    def mm(a, b):
        return jnp.dot(h, w_ref[:, a:b], preferred_element_type=F32)

    def qa_half(c0):
        q = mm(C_QA + c0 * LANES, C_QA + (c0 + 2) * LANES)
        for c in range(2):
            qa_ref[:, (c0 + c) * LANES:(c0 + c + 1) * LANES] = (
                _rope128(q[:, c * LANES:(c + 1) * LANES], cos, sin) * A_SCALE).astype(BF16)

    def kv():
        kv2 = mm(C_KA, C_QKV)
        ka_ref[...] = _rope128(kv2[:, :LANES], cos, sin)
        va_ref[...] = kv2[:, LANES:]

    def z_half(c0):
        z_ref[:, c0:c0 + PAIR] = mm(C_Z + c0, C_Z + c0 + PAIR)

    def qc_half(c0):
        qc_ref[:, c0:c0 + PAIR] = mm(C_QC + c0, C_QC + c0 + PAIR).astype(BF16)

    def ba():
        ba_ref[...] = mm(C_BA, C_END)

    steps = [lambda: qa_half(0), lambda: qa_half(2), kv, lambda: z_half(0), lambda: z_half(PAIR),
             lambda: qc_half(0), lambda: qc_half(PAIR), ba]
    return mm, steps


def _proj_raw_kernel(x_ref, g_ref, w_ref, cos_ref, sin_ref,
                     qa_ref, ka_ref, va_ref, z_ref, qc_ref, ba_ref, qkv_ref):
    mm, steps = _proj_steps(x_ref, g_ref, w_ref, cos_ref, sin_ref,
                            qa_ref, ka_ref, va_ref, z_ref, qc_ref, ba_ref)
    qkv_ref[...] = mm(C_QKV, C_Z)
    for step in steps:
        step()


def _proj_conv_kernel(x_ref, g_ref, w_ref, cos_ref, sin_ref, cw_ref,
                      qa_ref, ka_ref, va_ref, z_ref, qc_ref, ba_ref, qn_ref, kn_ref, vv_ref, tail_ref,
                      buf_ref, *, tm, tiles_per_seq):
    hist = SUBLANES
    first = lax.rem(pl.program_id(0), tiles_per_seq) == 0

    @pl.when(first)
    def _():
        buf_ref[0:hist, :] = jnp.zeros((hist, B_CONV_CH), F32)

    @pl.when(jnp.logical_not(first))
    def _():
        buf_ref[0:hist, :] = buf_ref[tm:tm + hist, :]

    mm, steps = _proj_steps(x_ref, g_ref, w_ref, cos_ref, sin_ref,
                            qa_ref, ka_ref, va_ref, z_ref, qc_ref, ba_ref)
    nq = B_HEADS * B_DK

    def conv_group(c0):
        cs = slice(c0, c0 + PAIR)
        raw = mm(C_QKV + c0, C_QKV + c0 + PAIR)
        buf_ref[hist:hist + tm, cs] = raw
        tail_ref[0, :, cs] = raw[tm - hist:, :]
        conv = None
        for i in range(CONV_W):
            off = hist - (CONV_W - 1) + i
            term = buf_ref[off:off + tm, cs] * cw_ref[i:i + 1, cs]
            conv = term if conv is None else conv + term
        act = _silu(conv)
        if c0 >= 2 * nq:
            vv_ref[:, c0 - 2 * nq:c0 - 2 * nq + PAIR] = act
            return
        out_ref, base, scale = (qn_ref, 0, B_DK ** -0.5) if c0 < nq else (kn_ref, nq, 1.0)
        for s in range(2):
            o0 = c0 - base + s * B_DK
            out_ref[:, o0:o0 + B_DK] = _l2n(act[:, s * B_DK:(s + 1) * B_DK]) * scale

    groups = [functools.partial(conv_group, c0) for c0 in range(0, B_CONV_CH, PAIR)]
    while groups or steps:
        if groups:
            groups.pop(0)()
        if steps:
            steps.pop(0)()


_PROJ_OUTS = [(512, BF16), (128, F32), (128, F32), (512, F32), (512, BF16), (128, F32)]


def _proj(x, gain, w1, cos, sin, tm, cw=None, seq=None):
    t = x.shape[0]
    ntab = cos.shape[0] // tm
    row = lambda n: pl.BlockSpec((tm, n), lambda i: (i, 0))
    tab = pl.BlockSpec((tm, LANES), lambda i: (i % ntab, 0))
    in_specs = [row(D_MODEL), _const_spec((1, D_MODEL)), _const_spec((D_MODEL, C_END)), tab, tab]
    out_specs = [row(n) for n, _ in _PROJ_OUTS]
    out_shape = [jax.ShapeDtypeStruct((t, n), d) for n, d in _PROJ_OUTS]
    if cw is None:
        return pl.pallas_call(
            _proj_raw_kernel, grid=(t // tm,), in_specs=in_specs,
            out_specs=out_specs + [row(B_CONV_CH)],
            out_shape=out_shape + [jax.ShapeDtypeStruct((t, B_CONV_CH), F32)],
            compiler_params=_cparams(("arbitrary",)), name="proj",
        )(x, gain, w1, cos, sin)
    tiles = seq // tm
    return pl.pallas_call(
        functools.partial(_proj_conv_kernel, tm=tm, tiles_per_seq=tiles),
        grid=(t // tm,),
        in_specs=in_specs + [_const_spec((CONV_W, B_CONV_CH))],
        out_specs=out_specs + [row(512)] * 3
        + [pl.BlockSpec((1, SUBLANES, B_CONV_CH), lambda i: (i // tiles, 0, 0))],
        out_shape=out_shape + [jax.ShapeDtypeStruct((t, 512), F32)] * 3
        + [jax.ShapeDtypeStruct((t // seq, SUBLANES, B_CONV_CH), F32)],
        scratch_shapes=[pltpu.VMEM((tm + SUBLANES, B_CONV_CH), F32)],
        compiler_params=_cparams(("arbitrary",)), name="proj_conv",
    )(x, gain, w1, cos, sin, cw)


def _swa_scores(q, k16):
    tq = q.shape[0]
    lo = lax.broadcasted_iota(jnp.int32, (tq, LANES), 1) < A_HD
    blocks = []
    for j in range(4):
        c = q[:, j * LANES:(j + 1) * LANES].astype(F32)
        blocks.append(jnp.where(lo, c, 0.0))
        blocks.append(jnp.where(lo, 0.0, c))
    lhs = jnp.concatenate(blocks, axis=0).astype(BF16)
    return lax.dot_general(lhs, k16, (((1,), (1,)), ((), ())), preferred_element_type=F32)


def _swa_finish(logits, v16, mask, sink_ref):
    tq = logits.shape[0] // A_HEADS
    lo = lax.broadcasted_iota(jnp.int32, (tq, LANES), 1) < A_HD
    es, inv = [], []
    for s in range(8):
        l = jnp.where(mask, logits[s * tq:(s + 1) * tq], NEG)
        sk = sink_ref[s]
        m = jnp.maximum(jnp.max(l, axis=-1, keepdims=True), sk)
        e = jnp.exp(l - m)
        den = jnp.sum(e, axis=-1, keepdims=True) + jnp.exp(sk - m)
        es.append(e.astype(BF16))
        inv.append(1.0 / den)
    pv = jnp.dot(jnp.concatenate(es, axis=0), v16, preferred_element_type=F32)
    outs = []
    for j in range(4):
        a = pv[(2 * j) * tq:(2 * j + 1) * tq] * inv[2 * j]
        b = pv[(2 * j + 1) * tq:(2 * j + 2) * tq] * inv[2 * j + 1]
        outs.append(jnp.where(lo, a, b))
    return jnp.concatenate(outs, axis=1)


def _swa_prompt_kernel(sink_ref, q_ref, kc_ref, kp_ref, vc_ref, vp_ref, o_ref, *, nblk):
    i = pl.program_id(1)
    kcat = jnp.concatenate([kp_ref[...], kc_ref[...]], axis=0).astype(BF16)
    vcat = jnp.concatenate([vp_ref[...], vc_ref[...]], axis=0).astype(BF16)
    ii = lax.broadcasted_iota(jnp.int32, (WINDOW, 2 * WINDOW), 0)
    jj = lax.broadcasted_iota(jnp.int32, (WINDOW, 2 * WINDOW), 1)
    band = (jj > ii) & (jj <= ii + WINDOW)
    for jb in range(nblk):
        mask = band
        if jb == 0:
            mask = band & ((jj >= WINDOW) | (i > 0))
        logits = _swa_scores(q_ref[jb * WINDOW:(jb + 1) * WINDOW, :], kcat[jb * WINDOW:(jb + 2) * WINDOW])
        o = _swa_finish(logits, vcat[jb * WINDOW:(jb + 2) * WINDOW], mask, sink_ref)
        o_ref[jb * WINDOW:(jb + 1) * WINDOW, :] = o.astype(BF16)


def _swa_prompt(sink, qa, ka, va, batch, seq, tq):
    nq = seq // tq
    nblk = tq // WINDOW
    nw = seq // WINDOW
    cur = lambda n: pl.BlockSpec((tq, n), lambda b, i: (b * nq + i, 0))
    prev = pl.BlockSpec((WINDOW, LANES), lambda b, i: (jnp.maximum(b * nw + i * nblk - 1, 0), 0))
    return pl.pallas_call(
        functools.partial(_swa_prompt_kernel, nblk=nblk),
        grid=(batch, nq),
        in_specs=[pl.BlockSpec(memory_space=pltpu.SMEM), cur(512), cur(LANES), prev, cur(LANES), prev],
        out_specs=cur(512),
        out_shape=jax.ShapeDtypeStruct(qa.shape, BF16),
        compiler_params=_cparams(("arbitrary", "arbitrary")),
        name="swa_prompt",
    )(sink, qa, ka, ka, va, va)


ROW_UNROLL = 4


def _rows_loop(rows, load, stages, store):
    def body(g, carry):
        idx = [g * ROW_UNROLL + j for j in range(ROW_UNROLL)]
        vals = [load(r) for r in idx]
        for stage in stages:
            vals = [stage(v) for v in vals]
        for r, o in zip(idx, vals):
            store(r, o)
        return carry

    lax.fori_loop(0, rows // ROW_UNROLL, body, 0)


def _swa_decode_kernel(sink_ref, q_ref, kn_ref, vn_ref, wk_ref, wv_ref, o_ref, *, rows):
    ii = lax.broadcasted_iota(jnp.int32, (DEC_PAD, 2 * WINDOW), 0)
    jj = lax.broadcasted_iota(jnp.int32, (DEC_PAD, 2 * WINDOW), 1)
    mask = ((jj < WINDOW) & (jj > ii)) | ((jj >= WINDOW) & (jj - WINDOW <= ii))
    pad = jnp.zeros((WINDOW - DEC_PAD, LANES), F32)

    def load(r):
        return q_ref[r], wk_ref[r], kn_ref[r], wv_ref[r], vn_ref[r]

    def scores(args):
        q, wk, kn, wv, vn = args
        kk = jnp.concatenate([wk, kn, pad], axis=0).astype(BF16)
        return _swa_scores(q, kk), wv, vn

    def finish(args):
        logits, wv, vn = args
        vv = jnp.concatenate([wv, vn, pad], axis=0).astype(BF16)
        return _swa_finish(logits, vv, mask, sink_ref).astype(BF16)

    def store(r, o):
        o_ref[r] = o

    _rows_loop(rows, load, [scores, finish], store)


def _swa_decode(sink, qa, kn, vn, wk, wv, rows):
    nb = qa.shape[0]
    blk = lambda a, n: pl.BlockSpec((rows, a, n), lambda i: (i, 0, 0))
    return pl.pallas_call(
        functools.partial(_swa_decode_kernel, rows=rows),
        grid=(nb // rows,),
        in_specs=[pl.BlockSpec(memory_space=pltpu.SMEM), blk(DEC_PAD, 512), blk(DEC_PAD, LANES),
                  blk(DEC_PAD, LANES), blk(WINDOW, LANES), blk(WINDOW, LANES)],
        out_specs=blk(DEC_PAD, 512),
        out_shape=jax.ShapeDtypeStruct(qa.shape, BF16),
        compiler_params=_cparams(("arbitrary",)),
        name="swa_decode",
    )(sink, qa, kn, vn, wk, wv)


def _gdn_gates(ba, alog, dtb, valid):
    beta = jax.nn.sigmoid(ba)
    g = -jnp.exp(alog) * _softplus(ba + dtb)
    if valid is not None:
        beta = jnp.where(valid, beta, 0.0)
        g = jnp.where(valid, g, 0.0)
    return beta, g


def _chunk_masks(chunk):
    sh = chunk.bit_length() - 1
    ri = lax.broadcasted_iota(jnp.int32, (LANES, LANES), 0)
    ci = lax.broadcasted_iota(jnp.int32, (LANES, LANES), 1)
    same = (ri >> sh) == (ci >> sh)
    return same, same & (ri >= ci), same & (ri > ci), ri == ci


def _gdn_cumsums(g_all, chunk):
    same, tri, _, _ = _chunk_masks(chunk)
    lower = jnp.where(tri, 1.0, 0.0)
    gcol = _hdot(lower, g_all)
    grow = _hdot(g_all.T, lower.T)
    gtot = _hdot(jnp.where(same, 1.0, 0.0), g_all)
    return gcol, grow, gtot


def _gdn_phase_a(q, k, v, beta, gcol, grow, gtot, chunk):
    _, tri, strict, _ = _chunk_masks(chunk)
    decay = jnp.exp(jnp.where(tri, gcol - grow, NEG))
    kb = k * beta
    kkqk = _bdot_nt(jnp.concatenate([kb, q], axis=0), k)
    a = jnp.where(strict, kkqk[:LANES] * decay, 0.0)
    qk = kkqk[LANES:] * decay
    n = -a
    apow = a
    for _ in range(chunk.bit_length() - 2):
        apow = _bdot(apow, apow)
        n = n + apow + _bdot(n, apow)
    rhs = jnp.concatenate([v * beta, kb * jnp.exp(gcol)], axis=1)
    sol = rhs + _bdot(n, rhs)
    u, w = sol[:, :B_DV], sol[:, B_DV:]
    return u, w, q * jnp.exp(gcol), k * jnp.exp(gtot - gcol), qk, jnp.exp(gtot)


def _gdn_qkv_heads(qkv, h):
    q = qkv[:, h * B_DK:(h + 1) * B_DK]
    k = qkv[:, (B_HEADS + h) * B_DK:(B_HEADS + h + 1) * B_DK]
    v = qkv[:, (2 * B_HEADS + h) * B_DK:(2 * B_HEADS + h + 1) * B_DK]
    q = q * lax.rsqrt(jnp.sum(q * q, axis=-1, keepdims=True) + EPS) * (B_DK ** -0.5)
    k = k * lax.rsqrt(jnp.sum(k * k, axis=-1, keepdims=True) + EPS)
    return q, k, v


def _gdn_out(o, z, nw):
    return _rms(o, nw) * _silu(z)


def _pair_bd(x2):
    lo = lax.broadcasted_iota(jnp.int32, x2.shape, 1) < LANES
    return jnp.concatenate([jnp.where(lo, x2, 0.0), jnp.where(lo, 0.0, x2)], axis=0).astype(BF16)


def _quad_bd(x4):
    blk = lax.broadcasted_iota(jnp.int32, x4.shape, 1) >> (GDN_CHUNK.bit_length() - 1)
    return jnp.concatenate([jnp.where(blk == j, x4, 0.0) for j in range(PAIR // GDN_CHUNK)],
                           axis=0).astype(BF16)


def _pair_cols(x, l0, l1):
    lo = lax.broadcasted_iota(jnp.int32, (x.shape[0], PAIR), 1) < LANES
    return jnp.where(lo, x[:, l0:l0 + 1], x[:, l1:l1 + 1])


def _gdn_prompt_kernel(q_ref, k_ref, v_ref, ba_ref, z_ref, alog_ref, dtb_ref, nw_ref,
                       ob_ref, s_ref, sbd_ref, *, nb):
    t = pl.program_id(0)
    npair = B_HEADS // 2

    @pl.when(t == 0)
    def _():
        sbd_ref[...] = jnp.zeros(sbd_ref.shape, F32)

    alog = alog_ref[...]
    dtb = dtb_ref[...]
    nw = nw_ref[...]
    nchunk = LANES // GDN_CHUNK
    nblk = nb
    _, tri, strict, _ = _chunk_masks(GDN_CHUNK)
    tri2 = jnp.concatenate([tri, tri], axis=1)
    strict2 = jnp.concatenate([strict, strict], axis=1)

    items = []
    for blk in range(nblk):
        beta_all, g_all = _gdn_gates(ba_ref[blk], alog, dtb, None)
        gcol_all, grow_all, gtot_all = _gdn_cumsums(g_all, GDN_CHUNK)
        eg_all = jnp.exp(gcol_all)
        ed_all = jnp.exp(gtot_all - gcol_all)
        et_all = jnp.exp(gtot_all)
        for p in range(npair):
            h0, h1 = 2 * p, 2 * p + 1
            g0, g1 = B_HEADS + h0, B_HEADS + h1
            psl = slice(p * PAIR, (p + 1) * PAIR)
            q2 = q_ref[blk, :, psl]
            k2 = k_ref[blk, :, psl]
            v2 = v_ref[blk, :, psl]
            beta2 = _pair_cols(beta_all, h0, h1)
            grow2 = jnp.concatenate([grow_all[g0:g0 + 1, :], grow_all[g1:g1 + 1, :]], axis=1)
            decay2 = jnp.exp(jnp.where(tri2, _pair_cols(gcol_all, g0, g1) - grow2, NEG))
            kb2 = k2 * beta2
            items.append(dict(
                blk=blk, p=p, decay2=decay2, kb2=kb2, v2b=v2 * beta2,
                kbe2=kb2 * _pair_cols(eg_all, g0, g1),
                lhs=jnp.concatenate([kb2, q2], axis=0).astype(BF16),
                kbd=_pair_bd(k2),
                qe2=(q2 * _pair_cols(eg_all, g0, g1)).astype(BF16),
                kd2=(k2 * _pair_cols(ed_all, g0, g1)).astype(BF16),
                et2=_pair_cols(et_all, g0, g1)))

    for it in items:
        kkqk = lax.dot_general(it["lhs"], it["kbd"], (((1,), (1,)), ((), ())),
                               preferred_element_type=F32)
        it["a2"] = jnp.where(strict2, kkqk[:LANES] * it["decay2"], 0.0)
        it["qk2"] = (kkqk[LANES:] * it["decay2"]).astype(BF16)
    top = (lax.broadcasted_iota(jnp.int32, (GDN_CHUNK, PAIR), 1) & GDN_CHUNK) == 0
    for it in items:
        a64 = it["a2"][:GDN_CHUNK] + it["a2"][GDN_CHUNK:]
        it["pw"] = jnp.dot(a64.astype(BF16), _quad_bd(a64), preferred_element_type=F32)
        it["n"] = -a64
    nlev = GDN_CHUNK.bit_length() - 2
    for lev in range(nlev):
        for it in items:
            pw, n = it["pw"], it["n"]
            bd = _quad_bd(pw)
            if lev < nlev - 1:
                r = jnp.dot(jnp.concatenate([pw, n], axis=0).astype(BF16), bd, preferred_element_type=F32)
                it["pw"], npw = r[:GDN_CHUNK], r[GDN_CHUNK:]
            else:
                npw = jnp.dot(n.astype(BF16), bd, preferred_element_type=F32)
            it["n"] = n + pw + npw
    for it in items:
        n64 = it["n"]
        it["n"] = jnp.concatenate([jnp.where(top, n64, 0.0), jnp.where(top, 0.0, n64)], axis=0)
    for it in items:
        us, ws = [], []
        for s in range(2):
            sl = slice(s * LANES, (s + 1) * LANES)
            rhs = jnp.concatenate([it["v2b"][:, sl], it["kbe2"][:, sl]], axis=1)
            sol = rhs + _bdot(it["n"][:, sl], rhs)
            us.append(sol[:, :B_DV])
            ws.append(sol[:, B_DV:])
        it["u2"] = jnp.concatenate(us, axis=1)
        it["w2"] = jnp.concatenate(ws, axis=1).astype(BF16)

    ri = lax.broadcasted_iota(jnp.int32, (PAIR, PAIR), 0) < LANES
    ci = lax.broadcasted_iota(jnp.int32, (PAIR, PAIR), 1) < LANES
    bdmask = ri == ci
    state = [sbd_ref[i] for i in range(len(items))]
    outs = {}
    for c in range(nchunk):
        rows = slice(c * GDN_CHUNK, (c + 1) * GDN_CHUNK)
        rs = [jnp.dot(jnp.concatenate([it["w2"][rows], it["qe2"][rows]], axis=0), state[i].astype(BF16),
                      preferred_element_type=F32) for i, it in enumerate(items)]
        for i, it in enumerate(items):
            s = state[i]
            r = rs[i]
            vn2 = it["u2"][rows] - r[:GDN_CHUNK]
            vt = _pair_bd(jnp.concatenate([vn2] * nchunk, axis=0))
            outs[(i, c)] = r[GDN_CHUNK:] + jnp.dot(it["qk2"][rows], vt, preferred_element_type=F32)
            upd = lax.dot_general(it["kd2"][rows], vn2.astype(BF16), (((0,), (0,)), ((), ())),
                                  preferred_element_type=F32)
            state[i] = s * it["et2"][c * GDN_CHUNK:c * GDN_CHUNK + 1, :] + jnp.where(bdmask, upd, 0.0)
    for i, it in enumerate(items):
        sbd_ref[i] = state[i]

    @pl.when(t == pl.num_programs(0) - 1)
    def _():
        for i, it in enumerate(items):
            s_ref[it["blk"], 2 * it["p"]] = state[i][:LANES, :LANES]
            s_ref[it["blk"], 2 * it["p"] + 1] = state[i][LANES:, LANES:]

    for i, it in enumerate(items):
        o2 = jnp.concatenate([outs[(i, c)] for c in range(nchunk)], axis=0)
        for s in range(2):
            h = 2 * it["p"] + s
            sl = slice(h * B_DV, (h + 1) * B_DV)
            o = o2[:, s * LANES:(s + 1) * LANES]
            ob_ref[it["blk"], :, sl] = _gdn_out(o, z_ref[it["blk"], :, sl], nw).astype(BF16)


def _gdn_prompt(q, k, v, ba, z, alog, dtb, nw):
    batch, seq, _ = q.shape
    tok = lambda n: pl.BlockSpec((batch, LANES, n), lambda t: (0, t, 0))
    state = pl.BlockSpec((batch, B_HEADS, B_DK, B_DV), lambda t: (0, 0, 0, 0))
    return pl.pallas_call(
        functools.partial(_gdn_prompt_kernel, nb=batch),
        grid=(seq // LANES,),
        in_specs=[tok(512), tok(512), tok(512), tok(LANES), tok(512),
                  _const_spec((1, LANES)), _const_spec((1, LANES)), _const_spec((1, B_DV))],
        out_specs=[tok(512), state],
        out_shape=[jax.ShapeDtypeStruct((batch, seq, 512), BF16),
                   jax.ShapeDtypeStruct((batch, B_HEADS, B_DK, B_DV), F32)],
        scratch_shapes=[pltpu.VMEM((batch * (B_HEADS // 2), PAIR, PAIR), F32)],
        compiler_params=_cparams(("arbitrary",)),
        name="gdn_prompt",
    )(q, k, v, ba, z, alog, dtb, nw)


def _gdn_decode_kernel(raw_ref, hist_ref, ba_ref, z_ref, rec_ref, cw_ref, alog_ref, dtb_ref, nw_ref,
                       ob_ref, s_ref, buf_ref, u_s, w_s, qe_s, kd_s, qk_s, eg_s, o_s, *, rows, valid_len):
    buf_ref[:, 0:DEC_PAD, :] = hist_ref[...]
    buf_ref[:, DEC_PAD:2 * DEC_PAD, :] = raw_ref[...]
    conv = None
    for i in range(CONV_W):
        off = DEC_PAD - (CONV_W - 1) + i
        term = buf_ref[:, off:off + DEC_PAD, :] * cw_ref[i:i + 1, :]
        conv = term if conv is None else conv + term
    qkv = _silu(conv.reshape(rows * DEC_PAD, B_CONV_CH))
    tok = lax.broadcasted_iota(jnp.int32, (LANES, LANES), 0) & (DEC_PAD - 1)
    beta_all, g_all = _gdn_gates(ba_ref[...], alog_ref[...], dtb_ref[...], tok < valid_len)
    gcol_all, grow_all, gtot_all = _gdn_cumsums(g_all, DEC_PAD)
    for h in range(B_HEADS):
        q, k, v = _gdn_qkv_heads(qkv, h)
        gl = B_HEADS + h
        u, w, qe, kd, qk, egt = _gdn_phase_a(
            q, k, v, beta_all[:, h:h + 1], gcol_all[:, gl:gl + 1], grow_all[gl:gl + 1, :],
            gtot_all[:, gl:gl + 1], DEC_PAD)
        u_s[h] = u
        w_s[h] = w
        qe_s[h] = qe
        kd_s[h] = kd
        qk_s[h] = qk
        eg_s[h] = jnp.broadcast_to(egt, (LANES, LANES))

    def load(r):
        r0 = pl.multiple_of(r * DEC_PAD, DEC_PAD)
        rr = pl.ds(r0, DEC_PAD)
        return [(rec_ref[r, h], w_s[h, rr, :], qe_s[h, rr, :], u_s[h, rr, :], qk_s[h, rr, :],
                 kd_s[h, rr, :], eg_s[h, pl.ds(r0, 1), :]) for h in range(B_HEADS)]

    def read_state(heads):
        return [(_bdot(jnp.concatenate([w, qe], axis=0), s), s, u, qk, kd, eg)
                for s, w, qe, u, qk, kd, eg in heads]

    def update(heads):
        outs = []
        for res, s, u, qk, kd, eg in heads:
            v_new = u - res[:DEC_PAD]
            vt = jnp.concatenate([v_new] * (LANES // DEC_PAD), axis=0)
            outs.append((res[DEC_PAD:] + _bdot(qk, vt), s * eg + _bdot_tn(kd, v_new)))
        return outs

    def store(r, outs):
        rr = pl.ds(pl.multiple_of(r * DEC_PAD, DEC_PAD), DEC_PAD)
        for h, (o, s_new) in enumerate(outs):
            o_s[h, rr, :] = o
            s_ref[r, h] = s_new

    _rows_loop(rows, load, [read_state, update], store)
    nw = nw_ref[...]
    for h in range(B_HEADS):
        sl = slice(h * B_DV, (h + 1) * B_DV)
        ob_ref[:, sl] = _gdn_out(o_s[h], z_ref[:, sl], nw).astype(BF16)


def _gdn_decode(raw, histp, ba, z, rec, cw, alog, dtb, nw, rows, valid_len):
    nb = raw.shape[0]
    flat = rows * DEC_PAD
    assert flat == LANES
    sq = lambda: pltpu.VMEM((B_HEADS, LANES, LANES), F32)
    return pl.pallas_call(
        functools.partial(_gdn_decode_kernel, rows=rows, valid_len=valid_len),
        grid=(nb // rows,),
        in_specs=[pl.BlockSpec((rows, DEC_PAD, B_CONV_CH), lambda i: (i, 0, 0)),
                  pl.BlockSpec((rows, DEC_PAD, B_CONV_CH), lambda i: (i, 0, 0)),
                  pl.BlockSpec((flat, LANES), lambda i: (i, 0)),
                  pl.BlockSpec((flat, 512), lambda i: (i, 0)),
                  pl.BlockSpec((rows, B_HEADS, B_DK, B_DV), lambda i: (i, 0, 0, 0)),
                  _const_spec((CONV_W, B_CONV_CH)), _const_spec((1, LANES)), _const_spec((1, LANES)),
                  _const_spec((1, B_DV))],
        out_specs=[pl.BlockSpec((flat, 512), lambda i: (i, 0)),
                   pl.BlockSpec((rows, B_HEADS, B_DK, B_DV), lambda i: (i, 0, 0, 0))],
        out_shape=[jax.ShapeDtypeStruct((nb * DEC_PAD, 512), BF16),
                   jax.ShapeDtypeStruct(rec.shape, F32)],
        scratch_shapes=[pltpu.VMEM((rows, 2 * DEC_PAD, B_CONV_CH), F32)] + [sq() for _ in range(7)],
        compiler_params=_cparams(("arbitrary",)),
        name="gdn_decode",
    )(raw, histp, ba, z, rec, cw, alog, dtb, nw)


def _memkv_kernel(m_ref, g_ref, w_ref, k_ref, v_ref):
    h = _rms(m_ref[...], g_ref[...]).astype(BF16)
    n = C_HEADS * C_HD
    k_ref[...] = jnp.dot(h, w_ref[:, :n], preferred_element_type=F32)
    v_ref[...] = jnp.dot(h, w_ref[:, n:], preferred_element_type=F32)


def _memkv(mem, gain, w):
    t = mem.shape[0]
    tm = 512
    n = C_HEADS * C_HD
    return pl.pallas_call(
        _memkv_kernel,
        grid=(t // tm,),
        in_specs=[pl.BlockSpec((tm, D_MODEL), lambda i: (i, 0)), _const_spec((1, D_MODEL)),
                  _const_spec((D_MODEL, 2 * n))],
        out_specs=[pl.BlockSpec((tm, n), lambda i: (i, 0))] * 2,
        out_shape=[jax.ShapeDtypeStruct((t, n), F32)] * 2,
        compiler_params=_cparams(("arbitrary",)),
        name="memkv",
    )(mem, gain, w)


def _softmax_rows(logits):
    m = jnp.max(logits, axis=-1, keepdims=True)
    e = jnp.exp(logits - m)
    return e, 1.0 / jnp.sum(e, axis=-1, keepdims=True)


def _memattn_prompt_kernel(q_ref, k_ref, v_ref, o_ref):
    hs = lambda h: slice(h * C_HD, (h + 1) * C_HD)
    scores = lambda h: _bdot_nt(q_ref[:, hs(h)], k_ref[:, hs(h)])
    logits = scores(0)
    for h in range(C_HEADS):
        nxt = scores(h + 1) if h + 1 < C_HEADS else None
        e, inv = _softmax_rows(logits * (C_HD ** -0.5))
        o_ref[:, hs(h)] = (_bdot(e, v_ref[:, hs(h)]) * inv).astype(BF16)
        logits = nxt


def _memattn_prompt(qc, mk, mv, batch, seq, tm):
    nq = seq // tm
    n = C_HEADS * C_HD
    cur = pl.BlockSpec((tm, n), lambda b, i: (b * nq + i, 0))
    mem = pl.BlockSpec((N_MEM, n), lambda b, i: (b, 0))
    return pl.pallas_call(
        _memattn_prompt_kernel,
        grid=(batch, nq),
        in_specs=[cur, mem, mem],
        out_specs=cur,
        out_shape=jax.ShapeDtypeStruct(qc.shape, BF16),
        compiler_params=_cparams(("arbitrary", "arbitrary")),
        name="memattn_prompt",
    )(qc, mk, mv)


def _memattn_decode_kernel(q_ref, k_ref, v_ref, o_ref, *, rows):
    nk = N_MEM * C_HEADS
    col = lax.broadcasted_iota(jnp.int32, (C_HEADS * DEC_PAD, nk), 1)
    row = lax.broadcasted_iota(jnp.int32, (C_HEADS * DEC_PAD, nk), 0)
    own = (col & (C_HEADS - 1)) == (row >> (DEC_PAD.bit_length() - 1))

    def load(r):
        return q_ref[r], k_ref[r], v_ref[r]

    def scores(args):
        q, k, v = args
        q = q.astype(F32)
        lhs = jnp.concatenate([q[:, h * C_HD:(h + 1) * C_HD] for h in range(C_HEADS)], axis=0)
        return _bdot_nt(lhs, k), v

    def finish(args):
        logits, v = args
        e, inv = _softmax_rows(jnp.where(own, logits * (C_HD ** -0.5), NEG))
        pv = _bdot(e, v) * inv
        return jnp.concatenate([pv[h * DEC_PAD:(h + 1) * DEC_PAD] for h in range(C_HEADS)],
                               axis=1).astype(BF16)

    def store(r, o):
        o_ref[r] = o

    _rows_loop(rows, load, [scores, finish], store)


def _memattn_decode(qc, ck, cv, rows):
    nb = qc.shape[0]
    n = C_HEADS * C_HD
    blk = pl.BlockSpec((rows, DEC_PAD, n), lambda i: (i, 0, 0))
    cache = pl.BlockSpec((rows, N_MEM * C_HEADS, C_HD), lambda i: (i, 0, 0))
    return pl.pallas_call(
        functools.partial(_memattn_decode_kernel, rows=rows),
        grid=(nb // rows,),
        in_specs=[blk, cache, cache],
        out_specs=blk,
        out_shape=jax.ShapeDtypeStruct(qc.shape, BF16),
        compiler_params=_cparams(("arbitrary",)),
        name="memattn_decode",
    )(qc, ck, cv)


def _merge_kernel(x_ref, oa_ref, ob_ref, oc_ref, gpre_ref, wg_ref, wb_ref, wo_ref, gpost_ref,
                  gfpre_ref, x1_ref, h2_ref):
    x = x_ref[...]
    ups = [jnp.dot(o_ref[...], wb_ref[n], preferred_element_type=F32)
           for n, o_ref in enumerate((oa_ref, ob_ref, oc_ref))]
    h = _rms(x, gpre_ref[...]).astype(BF16)
    mix = None
    for n in range(N_BRANCH):
        gate = jax.nn.sigmoid(jnp.dot(h, wg_ref[:, n * D_MODEL:(n + 1) * D_MODEL],
                                      preferred_element_type=F32))
        mix = gate * ups[n] if mix is None else mix + gate * ups[n]
    x1 = x + _rms(_bdot(mix, wo_ref[...]), gpost_ref[...])
    x1_ref[...] = x1
    h2_ref[...] = _rms(x1, gfpre_ref[...]).astype(BF16)


FF_SPLIT = 6 * PAIR


def _ffn_kernel(x1_ref, h2_ref, wfi_ref, wfo_ref, gfpost_ref, y_ref):
    h2 = h2_ref[...]
    f = None
    halves = []
    for a, b in ((0, FF_SPLIT), (FF_SPLIT, D_FF)):
        gt = jnp.dot(h2, wfi_ref[:, a:b], preferred_element_type=F32)
        uf = jnp.dot(h2, wfi_ref[:, D_FF + a:D_FF + b], preferred_element_type=F32)
        halves.append((a, b, gt, uf))
    for a, b, gt, uf in halves:
        part = _bdot(_silu(gt) * uf, wfo_ref[a:b, :])
        f = part if f is None else f + part
    y_ref[...] = x1_ref[...] + _rms(f, gfpost_ref[...])


def _post(x, oa, ob, oc, gpre, wg, wb, wo, gpost, gfpre, wfi, wfo, gfpost, tm):
    t = x.shape[0]
    row = lambda n: pl.BlockSpec((tm, n), lambda i: (i, 0))
    vec = _const_spec((1, D_MODEL))
    x1, h2 = pl.pallas_call(
        _merge_kernel,
        grid=(t // tm,),
        in_specs=[row(D_MODEL), row(512), row(512), row(512), vec,
                  _const_spec((D_MODEL, N_BRANCH * D_MODEL)), _const_spec((N_BRANCH, BRANCH_W, D_MODEL)),
                  _const_spec((D_MODEL, D_MODEL)), vec, vec],
        out_specs=[row(D_MODEL), row(D_MODEL)],
        out_shape=[jax.ShapeDtypeStruct(x.shape, F32), jax.ShapeDtypeStruct(x.shape, BF16)],
        compiler_params=_cparams(("arbitrary",)),
        name="merge",
    )(x, oa, ob, oc, gpre, wg, wb, wo, gpost, gfpre)
    return pl.pallas_call(
        _ffn_kernel,
        grid=(t // tm,),
        in_specs=[row(D_MODEL), row(D_MODEL), _const_spec((D_MODEL, 2 * D_FF)),
                  _const_spec((D_FF, D_MODEL)), vec],
        out_specs=row(D_MODEL),
        out_shape=jax.ShapeDtypeStruct(x.shape, F32),
        compiler_params=_cparams(("arbitrary",)),
        name="ffn",
    )(x1, h2, wfi, wfo, gfpost)


def _rope_tables(pos):
    half = A_HD // 2
    inv = ROPE_THETA ** (-jnp.arange(half, dtype=F32) / half)
    ang = pos.astype(F32)[:, None] * inv[None, :]
    cos, sin = jnp.cos(ang), jnp.sin(ang)
    cos = jnp.concatenate([cos, cos], axis=-1)
    sin = jnp.concatenate([-sin, sin], axis=-1)
    return jnp.tile(cos, (1, LANES // A_HD)), jnp.tile(sin, (1, LANES // A_HD))


def _lane_row(vals, offset):
    return jnp.zeros((1, LANES), F32).at[0, offset:offset + vals.shape[0]].set(vals.astype(F32))


def kernel(x_prompt, x_sample, mem_prompt, state_win_k, state_win_v, state_conv, state_rec,
           cache_mem_k, cache_mem_v, ln_mix_pre, w_in, attn_sink, gdn_conv_w, gdn_a_log,
           gdn_dt_bias, gdn_norm_w, ln_mem, w_mem_kv, w_branch, w_out, ln_mix_post,
           ln_ffn_pre, w_ffn_in, w_ffn_out, ln_ffn_post):
    bp, lp, _ = x_prompt.shape
    bs, ls, _ = x_sample.shape

    sizes = [512, 128, 128, B_CONV_CH, B_HEADS, B_HEADS, 512, 512, N_BRANCH * D_MODEL]
    o = np.cumsum([0] + sizes)
    hperm = np.concatenate([np.r_[j * A_HD:(j + 1) * A_HD, (j + 4) * A_HD:(j + 5) * A_HD] for j in range(4)])
    w1 = jnp.concatenate([
        w_in[:, o[0]:o[1]][:, hperm], w_in[:, o[1]:o[3]], w_in[:, o[3]:o[4]], w_in[:, o[6]:o[7]],
        w_in[:, o[7]:o[8]], w_in[:, o[4]:o[6]], jnp.zeros((D_MODEL, LANES - 2 * B_HEADS), F32)],
        axis=1).astype(BF16)
    wg = w_in[:, o[8]:o[9]].astype(BF16)
    wb = jnp.concatenate([w_branch[0:1][:, hperm], w_branch[1:]], axis=0).astype(BF16)
    wo = w_out.astype(BF16)
    wfi = w_ffn_in.astype(BF16)
    wfo = w_ffn_out.astype(BF16)
    wmem = w_mem_kv.astype(BF16)
    sink = attn_sink.astype(F32)[np.array([0, 4, 1, 5, 2, 6, 3, 7])]
    vec = lambda g: g.astype(F32).reshape(1, -1)
    alog = _lane_row(gdn_a_log, B_HEADS)
    dtb = _lane_row(gdn_dt_bias, B_HEADS)
    cw = gdn_conv_w.astype(F32)
    nw = vec(gdn_norm_w)

    def post(x, oa, ob, oc, tm):
        return _post(x, oa, ob, oc, vec(ln_mix_pre), wg, wb, wo, vec(ln_mix_post), vec(ln_ffn_pre),
                     wfi, wfo, vec(ln_ffn_post), tm)

    tp = bp * lp
    xp = x_prompt.reshape(tp, D_MODEL)
    cos_p, sin_p = _rope_tables(jnp.arange(lp, dtype=jnp.int32))
    qa, ka, va, z, qc, ba, qn, kn, vv, tail = _proj(xp, vec(ln_mix_pre), w1, cos_p, sin_p, 512, cw, lp)
    oa = _swa_prompt(sink, qa, ka, va, bp, lp, 512)
    b3 = lambda a: a.reshape(bp, lp, a.shape[-1])
    ob, rec_p = _gdn_prompt(b3(qn), b3(kn), b3(vv), b3(ba), b3(z), alog, dtb, nw)
    mk, mv = _memkv(mem_prompt.reshape(bp * N_MEM, D_MODEL), vec(ln_mem), wmem)
    oc = _memattn_prompt(qc, mk, mv, bp, lp, 512)
    y_p = post(xp, oa, ob.reshape(tp, 512), oc, 512).reshape(bp, lp, D_MODEL)
    wk_p = ka.reshape(bp, lp, LANES)[:, -WINDOW:].reshape(bp, WINDOW, A_KV, A_HD)
    wv_p = va.reshape(bp, lp, LANES)[:, -WINDOW:].reshape(bp, WINDOW, A_KV, A_HD)
    conv_p = tail[:, -(CONV_W - 1):]
    mem_k_p = mk.reshape(bp, N_MEM, C_HEADS, C_HD)
    mem_v_p = mv.reshape(bp, N_MEM, C_HEADS, C_HD)

    ts = bs * DEC_PAD
    xs = jnp.pad(x_sample, ((0, 0), (0, DEC_PAD - ls), (0, 0))).reshape(ts, D_MODEL)
    cos_s, sin_s = _rope_tables(PAST_LEN + jnp.arange(DEC_PAD, dtype=jnp.int32))
    cos_s, sin_s = jnp.tile(cos_s, (bs, 1)), jnp.tile(sin_s, (bs, 1))
    qa, ka, va, z, qc, ba, qkv = _proj(xs, vec(ln_mix_pre), w1, cos_s, sin_s, 512)
    r3 = lambda a: a.reshape(bs, DEC_PAD, a.shape[-1])
    oa = _swa_decode(sink, r3(qa), r3(ka), r3(va), state_win_k.reshape(bs, WINDOW, LANES),
                     state_win_v.reshape(bs, WINDOW, LANES), 16).reshape(ts, 512)
    histp = jnp.pad(state_conv, ((0, 0), (DEC_PAD - (CONV_W - 1), 0), (0, 0)))
    ob, rec_s = _gdn_decode(r3(qkv), histp, ba, z, state_rec, cw, alog, dtb, nw,
                            LANES // DEC_PAD, ls)
    oc = _memattn_decode(r3(qc), cache_mem_k.reshape(bs, N_MEM * C_HEADS, C_HD),
                         cache_mem_v.reshape(bs, N_MEM * C_HEADS, C_HD), 8).reshape(ts, 512)
    y_s = post(xs, oa, ob, oc, 512).reshape(bs, DEC_PAD, D_MODEL)[:, :ls]
    k_new = r3(ka)[:, :ls].reshape(bs, ls, A_KV, A_HD)
    v_new = r3(va)[:, :ls].reshape(bs, ls, A_KV, A_HD)
    wk_s = jnp.concatenate([state_win_k[:, ls:], k_new], axis=1)
    wv_s = jnp.concatenate([state_win_v[:, ls:], v_new], axis=1)
    conv_s = r3(qkv)[:, ls - (CONV_W - 1):ls]

    return (y_p, y_s, wk_p, wv_p, conv_p, rec_p, mem_k_p, mem_v_p, wk_s, wv_s, conv_s, rec_s)
```

```python
import functools

import numpy as np
import jax
import jax.numpy as jnp
from jax import lax
from jax.experimental import pallas as pl
from jax.experimental.pallas import tpu as pltpu

F32 = jnp.float32
BF16 = jnp.bfloat16

D_MODEL = 1024
PAST_LEN = 16384
EPS = 1e-6
ROPE_THETA = 10000.0
N_MEM = 256
WINDOW = 128
A_HD = 64
A_HEADS = 8
A_KV = 2
A_SCALE = A_HD ** -0.5
B_HEADS = 4
B_DK = 128
B_DV = 128
CONV_W = 4
GDN_CHUNK = 64
B_CONV_CH = B_HEADS * (2 * B_DK + B_DV)
C_HEADS = 4
C_HD = 128
N_BRANCH = 3
BRANCH_W = 512
D_FF = 2816

LANES = 128
SUBLANES = 8
PAIR = 2 * LANES
VMEM_LIMIT = 56 * 1024 * 1024
NEG = -1e30
DEC_PAD = SUBLANES

C_QA, C_KA, C_VA, C_QKV, C_Z, C_QC, C_BA, C_END = 0, 512, 640, 768, 2304, 2816, 3328, 3456


def _cparams(sem, vmem=VMEM_LIMIT):
    return pltpu.CompilerParams(dimension_semantics=sem, vmem_limit_bytes=vmem)


def _const_spec(shape):
    nd = len(shape)
    return pl.BlockSpec(shape, lambda *_: (0,) * nd, pipeline_mode=pl.Buffered(1))


def _rms(x, g):
    ms = jnp.mean(x * x, axis=-1, keepdims=True)
    return x * lax.rsqrt(ms + EPS) * g


def _bdot(a, b):
    return jnp.dot(a.astype(BF16), b.astype(BF16), preferred_element_type=F32)


def _bdot_nt(a, b):
    return lax.dot_general(a.astype(BF16), b.astype(BF16), (((1,), (1,)), ((), ())),
                           preferred_element_type=F32)


def _bdot_tn(a, b):
    return lax.dot_general(a.astype(BF16), b.astype(BF16), (((0,), (0,)), ((), ())),
                           preferred_element_type=F32)


def _hdot(a, b):
    return jnp.dot(a, b, precision=lax.Precision.HIGHEST, preferred_element_type=F32)


def _silu(x):
    return x * jax.nn.sigmoid(x)


def _softplus(x):
    return jnp.maximum(x, 0.0) + jnp.log1p(jnp.exp(-jnp.abs(x)))


def _rope128(v, cos, sin):
    lane = lax.broadcasted_iota(jnp.int32, v.shape, 1)
    fwd = pltpu.roll(v, 32, 1)
    bwd = pltpu.roll(v, 96, 1)
    sw = jnp.where((lane & 32) == 0, bwd, fwd)
    return v * cos + sw * sin


def _l2n(x):
    return x * lax.rsqrt(jnp.sum(x * x, axis=-1, keepdims=True) + EPS)


def _proj_steps(x_ref, g_ref, w_ref, cos_ref, sin_ref, qa_ref, ka_ref, va_ref, z_ref, qc_ref, ba_ref):
    h = _rms(x_ref[...], g_ref[...]).astype(BF16)
    cos = cos_ref[...]
    sin = sin_ref[...]

    def mm(a, b):
        return jnp.dot(h, w_ref[:, a:b], preferred_element_type=F32)

    def qa_half(c0):
        q = mm(C_QA + c0 * LANES, C_QA + (c0 + 2) * LANES)
        for c in range(2):
            qa_ref[:, (c0 + c) * LANES:(c0 + c + 1) * LANES] = (
                _rope128(q[:, c * LANES:(c + 1) * LANES], cos, sin) * A_SCALE).astype(BF16)

    def kv():
        kv2 = mm(C_KA, C_QKV)
        ka_ref[...] = _rope128(kv2[:, :LANES], cos, sin)
        va_ref[...] = kv2[:, LANES:]

    def z_half(c0):
        z_ref[:, c0:c0 + PAIR] = mm(C_Z + c0, C_Z + c0 + PAIR)

    def qc_half(c0):
        qc_ref[:, c0:c0 + PAIR] = mm(C_QC + c0, C_QC + c0 + PAIR).astype(BF16)

    def ba():
        ba_ref[...] = mm(C_BA, C_END)

    steps = [lambda: qa_half(0), lambda: qa_half(2), kv, lambda: z_half(0), lambda: z_half(PAIR),
             lambda: qc_half(0), lambda: qc_half(PAIR), ba]
    return mm, steps


def _proj_raw_kernel(x_ref, g_ref, w_ref, cos_ref, sin_ref,
                     qa_ref, ka_ref, va_ref, z_ref, qc_ref, ba_ref, qkv_ref):
    mm, steps = _proj_steps(x_ref, g_ref, w_ref, cos_ref, sin_ref,
                            qa_ref, ka_ref, va_ref, z_ref, qc_ref, ba_ref)
    qkv_ref[...] = mm(C_QKV, C_Z)
    for step in steps:
        step()


def _proj_conv_kernel(x_ref, g_ref, w_ref, cos_ref, sin_ref, cw_ref,
                      qa_ref, ka_ref, va_ref, z_ref, qc_ref, ba_ref, qn_ref, kn_ref, vv_ref, tail_ref,
                      buf_ref, *, tm, tiles_per_seq):
    hist = SUBLANES
    first = lax.rem(pl.program_id(0), tiles_per_seq) == 0

    @pl.when(first)
    def _():
        buf_ref[0:hist, :] = jnp.zeros((hist, B_CONV_CH), F32)

    @pl.when(jnp.logical_not(first))
    def _():
        buf_ref[0:hist, :] = buf_ref[tm:tm + hist, :]

    mm, steps = _proj_steps(x_ref, g_ref, w_ref, cos_ref, sin_ref,
                            qa_ref, ka_ref, va_ref, z_ref, qc_ref, ba_ref)
    nq = B_HEADS * B_DK

    def conv_group(c0):
        cs = slice(c0, c0 + PAIR)
        raw = mm(C_QKV + c0, C_QKV + c0 + PAIR)
        buf_ref[hist:hist + tm, cs] = raw
        tail_ref[0, :, cs] = raw[tm - hist:, :]
        xb = buf_ref[:, cs]
        acc = xb * cw_ref[0:1, cs]
        for i in range(1, CONV_W):
            acc = pltpu.roll(acc, 1, 0) + xb * cw_ref[i:i + 1, cs]
        act = _silu(acc[hist:])
        if c0 >= 2 * nq:
            vv_ref[:, c0 - 2 * nq:c0 - 2 * nq + PAIR] = act
            return
        out_ref, base, scale = (qn_ref, 0, B_DK ** -0.5) if c0 < nq else (kn_ref, nq, 1.0)
        for s in range(2):
            o0 = c0 - base + s * B_DK
            out_ref[:, o0:o0 + B_DK] = _l2n(act[:, s * B_DK:(s + 1) * B_DK]) * scale

    groups = [functools.partial(conv_group, c0) for c0 in range(0, B_CONV_CH, PAIR)]
    while groups or steps:
        if groups:
            groups.pop(0)()
        if steps:
            steps.pop(0)()


_PROJ_OUTS = [(512, BF16), (128, F32), (128, F32), (512, F32), (512, BF16), (128, F32)]


def _proj(x, gain, w1, cos, sin, tm, cw=None, seq=None):
    t = x.shape[0]
    ntab = cos.shape[0] // tm
    row = lambda n: pl.BlockSpec((tm, n), lambda i: (i, 0))
    tab = pl.BlockSpec((tm, LANES), lambda i: (i % ntab, 0))
    in_specs = [row(D_MODEL), _const_spec((1, D_MODEL)), _const_spec((D_MODEL, C_END)), tab, tab]
    out_specs = [row(n) for n, _ in _PROJ_OUTS]
    out_shape = [jax.ShapeDtypeStruct((t, n), d) for n, d in _PROJ_OUTS]
    if cw is None:
        return pl.pallas_call(
            _proj_raw_kernel, grid=(t // tm,), in_specs=in_specs,
            out_specs=out_specs + [row(B_CONV_CH)],
            out_shape=out_shape + [jax.ShapeDtypeStruct((t, B_CONV_CH), F32)],
            compiler_params=_cparams(("arbitrary",)), name="proj",
        )(x, gain, w1, cos, sin)
    tiles = seq // tm
    return pl.pallas_call(
        functools.partial(_proj_conv_kernel, tm=tm, tiles_per_seq=tiles),
        grid=(t // tm,),
        in_specs=in_specs + [_const_spec((CONV_W, B_CONV_CH))],
        out_specs=out_specs + [row(512)] * 3
        + [pl.BlockSpec((1, SUBLANES, B_CONV_CH), lambda i: (i // tiles, 0, 0))],
        out_shape=out_shape + [jax.ShapeDtypeStruct((t, 512), F32)] * 3
        + [jax.ShapeDtypeStruct((t // seq, SUBLANES, B_CONV_CH), F32)],
        scratch_shapes=[pltpu.VMEM((tm + SUBLANES, B_CONV_CH), F32)],
        compiler_params=_cparams(("arbitrary",)), name="proj_conv",
    )(x, gain, w1, cos, sin, cw)


def _swa_scores(q, k16):
    tq = q.shape[0]
    lo = lax.broadcasted_iota(jnp.int32, (tq, LANES), 1) < A_HD
    blocks = []
    for j in range(4):
        c = q[:, j * LANES:(j + 1) * LANES].astype(F32)
        blocks.append(jnp.where(lo, c, 0.0))
        blocks.append(jnp.where(lo, 0.0, c))
    lhs = jnp.concatenate(blocks, axis=0).astype(BF16)
    return lax.dot_general(lhs, k16, (((1,), (1,)), ((), ())), preferred_element_type=F32)


def _swa_finish(logits, v16, mask, sink_ref):
    tq = logits.shape[0] // A_HEADS
    lo = lax.broadcasted_iota(jnp.int32, (tq, LANES), 1) < A_HD
    es, inv = [], []
    for s in range(8):
        l = jnp.where(mask, logits[s * tq:(s + 1) * tq], NEG)
        sk = sink_ref[s]
        m = jnp.maximum(jnp.max(l, axis=-1, keepdims=True), sk)
        e = jnp.exp(l - m)
        den = jnp.sum(e, axis=-1, keepdims=True) + jnp.exp(sk - m)
        es.append(e.astype(BF16))
        inv.append(1.0 / den)
    pv = jnp.dot(jnp.concatenate(es, axis=0), v16, preferred_element_type=F32)
    outs = []
    for j in range(4):
        a = pv[(2 * j) * tq:(2 * j + 1) * tq] * inv[2 * j]
        b = pv[(2 * j + 1) * tq:(2 * j + 2) * tq] * inv[2 * j + 1]
        outs.append(jnp.where(lo, a, b))
    return jnp.concatenate(outs, axis=1)


def _swa_prompt_kernel(sink_ref, q_ref, kc_ref, kp_ref, vc_ref, vp_ref, o_ref, *, nblk):
    i = pl.program_id(1)
    kcat = jnp.concatenate([kp_ref[...], kc_ref[...]], axis=0).astype(BF16)
    vcat = jnp.concatenate([vp_ref[...], vc_ref[...]], axis=0).astype(BF16)
    ii = lax.broadcasted_iota(jnp.int32, (WINDOW, 2 * WINDOW), 0)
    jj = lax.broadcasted_iota(jnp.int32, (WINDOW, 2 * WINDOW), 1)
    band = (jj > ii) & (jj <= ii + WINDOW)
    for jb in range(nblk):
        mask = band
        if jb == 0:
            mask = band & ((jj >= WINDOW) | (i > 0))
        logits = _swa_scores(q_ref[jb * WINDOW:(jb + 1) * WINDOW, :], kcat[jb * WINDOW:(jb + 2) * WINDOW])
        o = _swa_finish(logits, vcat[jb * WINDOW:(jb + 2) * WINDOW], mask, sink_ref)
        o_ref[jb * WINDOW:(jb + 1) * WINDOW, :] = o.astype(BF16)


def _swa_prompt(sink, qa, ka, va, batch, seq, tq):
    nq = seq // tq
    nblk = tq // WINDOW
    nw = seq // WINDOW
    cur = lambda n: pl.BlockSpec((tq, n), lambda b, i: (b * nq + i, 0))
    prev = pl.BlockSpec((WINDOW, LANES), lambda b, i: (jnp.maximum(b * nw + i * nblk - 1, 0), 0))
    return pl.pallas_call(
        functools.partial(_swa_prompt_kernel, nblk=nblk),
        grid=(batch, nq),
        in_specs=[pl.BlockSpec(memory_space=pltpu.SMEM), cur(512), cur(LANES), prev, cur(LANES), prev],
        out_specs=cur(512),
        out_shape=jax.ShapeDtypeStruct(qa.shape, BF16),
        compiler_params=_cparams(("arbitrary", "arbitrary")),
        name="swa_prompt",
    )(sink, qa, ka, ka, va, va)


ROW_UNROLL = 4


def _rows_loop(rows, load, stages, store):
    def body(g, carry):
        idx = [g * ROW_UNROLL + j for j in range(ROW_UNROLL)]
        vals = [load(r) for r in idx]
        for stage in stages:
            vals = [stage(v) for v in vals]
        for r, o in zip(idx, vals):
            store(r, o)
        return carry

    lax.fori_loop(0, rows // ROW_UNROLL, body, 0)


def _swa_decode_kernel(sink_ref, q_ref, kn_ref, vn_ref, wk_ref, wv_ref, o_ref, *, rows):
    ii = lax.broadcasted_iota(jnp.int32, (DEC_PAD, 2 * WINDOW), 0)
    jj = lax.broadcasted_iota(jnp.int32, (DEC_PAD, 2 * WINDOW), 1)
    mask = ((jj < WINDOW) & (jj > ii)) | ((jj >= WINDOW) & (jj - WINDOW <= ii))
    pad = jnp.zeros((WINDOW - DEC_PAD, LANES), F32)

    def load(r):
        return q_ref[r], wk_ref[r], kn_ref[r], wv_ref[r], vn_ref[r]

    def scores(args):
        q, wk, kn, wv, vn = args
        kk = jnp.concatenate([wk, kn, pad], axis=0).astype(BF16)
        return _swa_scores(q, kk), wv, vn

    def finish(args):
        logits, wv, vn = args
        vv = jnp.concatenate([wv, vn, pad], axis=0).astype(BF16)
        return _swa_finish(logits, vv, mask, sink_ref).astype(BF16)

    def store(r, o):
        o_ref[r] = o

    _rows_loop(rows, load, [scores, finish], store)


def _swa_decode(sink, qa, kn, vn, wk, wv, rows):
    nb = qa.shape[0]
    blk = lambda a, n: pl.BlockSpec((rows, a, n), lambda i: (i, 0, 0))
    return pl.pallas_call(
        functools.partial(_swa_decode_kernel, rows=rows),
        grid=(nb // rows,),
        in_specs=[pl.BlockSpec(memory_space=pltpu.SMEM), blk(DEC_PAD, 512), blk(DEC_PAD, LANES),
                  blk(DEC_PAD, LANES), blk(WINDOW, LANES), blk(WINDOW, LANES)],
        out_specs=blk(DEC_PAD, 512),
        out_shape=jax.ShapeDtypeStruct(qa.shape, BF16),
        compiler_params=_cparams(("arbitrary",)),
        name="swa_decode",
    )(sink, qa, kn, vn, wk, wv)


def _gdn_gates(ba, alog, dtb, valid):
    beta = jax.nn.sigmoid(ba)
    g = -jnp.exp(alog) * _softplus(ba + dtb)
    if valid is not None:
        beta = jnp.where(valid, beta, 0.0)
        g = jnp.where(valid, g, 0.0)
    return beta, g


def _chunk_masks(chunk):
    sh = chunk.bit_length() - 1
    ri = lax.broadcasted_iota(jnp.int32, (LANES, LANES), 0)
    ci = lax.broadcasted_iota(jnp.int32, (LANES, LANES), 1)
    same = (ri >> sh) == (ci >> sh)
    return same, same & (ri >= ci), same & (ri > ci), ri == ci


def _gdn_cumsums(g_all, chunk):
    same, tri, _, _ = _chunk_masks(chunk)
    lower = jnp.where(tri, 1.0, 0.0)
    gcol = _hdot(lower, g_all)
    grow = _hdot(g_all.T, lower.T)
    gtot = _hdot(jnp.where(same, 1.0, 0.0), g_all)
    return gcol, grow, gtot


def _gdn_phase_a(q, k, v, beta, gcol, grow, gtot, chunk):
    _, tri, strict, _ = _chunk_masks(chunk)
    decay = jnp.exp(jnp.where(tri, gcol - grow, NEG))
    kb = k * beta
    kkqk = _bdot_nt(jnp.concatenate([kb, q], axis=0), k)
    a = jnp.where(strict, kkqk[:LANES] * decay, 0.0)
    qk = kkqk[LANES:] * decay
    n = -a
    apow = a
    for _ in range(chunk.bit_length() - 2):
        apow = _bdot(apow, apow)
        n = n + apow + _bdot(n, apow)
    rhs = jnp.concatenate([v * beta, kb * jnp.exp(gcol)], axis=1)
    sol = rhs + _bdot(n, rhs)
    u, w = sol[:, :B_DV], sol[:, B_DV:]
    return u, w, q * jnp.exp(gcol), k * jnp.exp(gtot - gcol), qk, jnp.exp(gtot)


def _gdn_qkv_heads(qkv, h):
    q = qkv[:, h * B_DK:(h + 1) * B_DK]
    k = qkv[:, (B_HEADS + h) * B_DK:(B_HEADS + h + 1) * B_DK]
    v = qkv[:, (2 * B_HEADS + h) * B_DK:(2 * B_HEADS + h + 1) * B_DK]
    q = q * lax.rsqrt(jnp.sum(q * q, axis=-1, keepdims=True) + EPS) * (B_DK ** -0.5)
    k = k * lax.rsqrt(jnp.sum(k * k, axis=-1, keepdims=True) + EPS)
    return q, k, v


def _gdn_out(o, z, nw):
    return _rms(o, nw) * _silu(z)


def _pair_bd(x2):
    lo = lax.broadcasted_iota(jnp.int32, x2.shape, 1) < LANES
    return jnp.concatenate([jnp.where(lo, x2, 0.0), jnp.where(lo, 0.0, x2)], axis=0).astype(BF16)


def _quad_bd(x4):
    blk = lax.broadcasted_iota(jnp.int32, x4.shape, 1) >> (GDN_CHUNK.bit_length() - 1)
    return jnp.concatenate([jnp.where(blk == j, x4, 0.0) for j in range(PAIR // GDN_CHUNK)],
                           axis=0).astype(BF16)


def _pair_cols(x, l0, l1):
    lo = lax.broadcasted_iota(jnp.int32, (x.shape[0], PAIR), 1) < LANES
    return jnp.where(lo, x[:, l0:l0 + 1], x[:, l1:l1 + 1])


def _gdn_prompt_kernel(q_ref, k_ref, v_ref, ba_ref, z_ref, alog_ref, dtb_ref, nw_ref,
                       ob_ref, s_ref, sbd_ref, *, nb):
    t = pl.program_id(0)
    npair = B_HEADS // 2

    @pl.when(t == 0)
    def _():
        sbd_ref[...] = jnp.zeros(sbd_ref.shape, F32)

    alog = alog_ref[...]
    dtb = dtb_ref[...]
    nw = nw_ref[...]
    nchunk = LANES // GDN_CHUNK
    nblk = nb
    _, tri, strict, _ = _chunk_masks(GDN_CHUNK)
    tri2 = jnp.concatenate([tri, tri], axis=1)
    strict2 = jnp.concatenate([strict, strict], axis=1)

    items = []
    for blk in range(nblk):
        beta_all, g_all = _gdn_gates(ba_ref[blk], alog, dtb, None)
        gcol_all, grow_all, gtot_all = _gdn_cumsums(g_all, GDN_CHUNK)
        eg_all = jnp.exp(gcol_all)
        ed_all = jnp.exp(gtot_all - gcol_all)
        et_all = jnp.exp(gtot_all)
        for p in range(npair):
            h0, h1 = 2 * p, 2 * p + 1
            g0, g1 = B_HEADS + h0, B_HEADS + h1
            psl = slice(p * PAIR, (p + 1) * PAIR)
            q2 = q_ref[blk, :, psl]
            k2 = k_ref[blk, :, psl]
            v2 = v_ref[blk, :, psl]
            beta2 = _pair_cols(beta_all, h0, h1)
            grow2 = jnp.concatenate([grow_all[g0:g0 + 1, :], grow_all[g1:g1 + 1, :]], axis=1)
            decay2 = jnp.exp(jnp.where(tri2, _pair_cols(gcol_all, g0, g1) - grow2, NEG))
            kb2 = k2 * beta2
            items.append(dict(
                blk=blk, p=p, decay2=decay2, kb2=kb2, v2b=v2 * beta2,
                kbe2=kb2 * _pair_cols(eg_all, g0, g1),
                lhs=jnp.concatenate([kb2, q2], axis=0).astype(BF16),
                kbd=_pair_bd(k2),
                qe2=(q2 * _pair_cols(eg_all, g0, g1)).astype(BF16),
                kd2=(k2 * _pair_cols(ed_all, g0, g1)).astype(BF16),
                et2=_pair_cols(et_all, g0, g1)))

    for it in items:
        kkqk = lax.dot_general(it["lhs"], it["kbd"], (((1,), (1,)), ((), ())),
                               preferred_element_type=F32)
        it["a2"] = jnp.where(strict2, kkqk[:LANES] * it["decay2"], 0.0)
        it["qk2"] = (kkqk[LANES:] * it["decay2"]).astype(BF16)
    top = (lax.broadcasted_iota(jnp.int32, (GDN_CHUNK, PAIR), 1) & GDN_CHUNK) == 0
    for it in items:
        a64 = it["a2"][:GDN_CHUNK] + it["a2"][GDN_CHUNK:]
        it["pw"] = jnp.dot(a64.astype(BF16), _quad_bd(a64), preferred_element_type=F32)
        it["n"] = -a64
    nlev = GDN_CHUNK.bit_length() - 2
    for lev in range(nlev):
        for it in items:
            pw, n = it["pw"], it["n"]
            bd = _quad_bd(pw)
            if lev < nlev - 1:
                r = jnp.dot(jnp.concatenate([pw, n], axis=0).astype(BF16), bd, preferred_element_type=F32)
                it["pw"], npw = r[:GDN_CHUNK], r[GDN_CHUNK:]
            else:
                npw = jnp.dot(n.astype(BF16), bd, preferred_element_type=F32)
            it["n"] = n + pw + npw
    for it in items:
        n64 = it["n"]
        it["n"] = jnp.concatenate([jnp.where(top, n64, 0.0), jnp.where(top, 0.0, n64)], axis=0)
    for it in items:
        us, ws = [], []
        for s in range(2):
            sl = slice(s * LANES, (s + 1) * LANES)
            rhs = jnp.concatenate([it["v2b"][:, sl], it["kbe2"][:, sl]], axis=1)
            sol = rhs + _bdot(it["n"][:, sl], rhs)
            us.append(sol[:, :B_DV])
            ws.append(sol[:, B_DV:])
        it["u2"] = jnp.concatenate(us, axis=1)
        it["w2"] = jnp.concatenate(ws, axis=1).astype(BF16)

    ri = lax.broadcasted_iota(jnp.int32, (PAIR, PAIR), 0) < LANES
    ci = lax.broadcasted_iota(jnp.int32, (PAIR, PAIR), 1) < LANES
    bdmask = ri == ci
    state = [sbd_ref[i] for i in range(len(items))]
    outs = {}
    for c in range(nchunk):
        rows = slice(c * GDN_CHUNK, (c + 1) * GDN_CHUNK)
        rs = [jnp.dot(jnp.concatenate([it["w2"][rows], it["qe2"][rows]], axis=0), state[i].astype(BF16),
                      preferred_element_type=F32) for i, it in enumerate(items)]
        for i, it in enumerate(items):
            s = state[i]
            r = rs[i]
            vn2 = it["u2"][rows] - r[:GDN_CHUNK]
            vt = _pair_bd(jnp.concatenate([vn2] * nchunk, axis=0))
            outs[(i, c)] = r[GDN_CHUNK:] + jnp.dot(it["qk2"][rows], vt, preferred_element_type=F32)
            upd = lax.dot_general(it["kd2"][rows], vn2.astype(BF16), (((0,), (0,)), ((), ())),
                                  preferred_element_type=F32)
            state[i] = s * it["et2"][c * GDN_CHUNK:c * GDN_CHUNK + 1, :] + jnp.where(bdmask, upd, 0.0)
    for i, it in enumerate(items):
        sbd_ref[i] = state[i]

    @pl.when(t == pl.num_programs(0) - 1)
    def _():
        for i, it in enumerate(items):
            s_ref[it["blk"], 2 * it["p"]] = state[i][:LANES, :LANES]
            s_ref[it["blk"], 2 * it["p"] + 1] = state[i][LANES:, LANES:]

    for i, it in enumerate(items):
        o2 = jnp.concatenate([outs[(i, c)] for c in range(nchunk)], axis=0)
        for s in range(2):
            h = 2 * it["p"] + s
            sl = slice(h * B_DV, (h + 1) * B_DV)
            o = o2[:, s * LANES:(s + 1) * LANES]
            ob_ref[it["blk"], :, sl] = _gdn_out(o, z_ref[it["blk"], :, sl], nw).astype(BF16)


def _gdn_prompt(q, k, v, ba, z, alog, dtb, nw):
    batch, seq, _ = q.shape
    tok = lambda n: pl.BlockSpec((batch, LANES, n), lambda t: (0, t, 0))
    state = pl.BlockSpec((batch, B_HEADS, B_DK, B_DV), lambda t: (0, 0, 0, 0))
    return pl.pallas_call(
        functools.partial(_gdn_prompt_kernel, nb=batch),
        grid=(seq // LANES,),
        in_specs=[tok(512), tok(512), tok(512), tok(LANES), tok(512),
                  _const_spec((1, LANES)), _const_spec((1, LANES)), _const_spec((1, B_DV))],
        out_specs=[tok(512), state],
        out_shape=[jax.ShapeDtypeStruct((batch, seq, 512), BF16),
                   jax.ShapeDtypeStruct((batch, B_HEADS, B_DK, B_DV), F32)],
        scratch_shapes=[pltpu.VMEM((batch * (B_HEADS // 2), PAIR, PAIR), F32)],
        compiler_params=_cparams(("arbitrary",)),
        name="gdn_prompt",
    )(q, k, v, ba, z, alog, dtb, nw)


def _gdn_decode_kernel(raw_ref, hist_ref, ba_ref, z_ref, rec_ref, cw_ref, alog_ref, dtb_ref, nw_ref,
                       ob_ref, s_ref, buf_ref, u_s, w_s, qe_s, kd_s, qk_s, eg_s, o_s, *, rows, valid_len):
    buf_ref[:, 0:DEC_PAD, :] = hist_ref[...]
    buf_ref[:, DEC_PAD:2 * DEC_PAD, :] = raw_ref[...]
    conv = None
    for i in range(CONV_W):
        off = DEC_PAD - (CONV_W - 1) + i
        term = buf_ref[:, off:off + DEC_PAD, :] * cw_ref[i:i + 1, :]
        conv = term if conv is None else conv + term
    qkv = _silu(conv.reshape(rows * DEC_PAD, B_CONV_CH))
    tok = lax.broadcasted_iota(jnp.int32, (LANES, LANES), 0) & (DEC_PAD - 1)
    beta_all, g_all = _gdn_gates(ba_ref[...], alog_ref[...], dtb_ref[...], tok < valid_len)
    gcol_all, grow_all, gtot_all = _gdn_cumsums(g_all, DEC_PAD)
    for h in range(B_HEADS):
        q, k, v = _gdn_qkv_heads(qkv, h)
        gl = B_HEADS + h
        u, w, qe, kd, qk, egt = _gdn_phase_a(
            q, k, v, beta_all[:, h:h + 1], gcol_all[:, gl:gl + 1], grow_all[gl:gl + 1, :],
            gtot_all[:, gl:gl + 1], DEC_PAD)
        u_s[h] = u
        w_s[h] = w
        qe_s[h] = qe
        kd_s[h] = kd
        qk_s[h] = qk
        eg_s[h] = jnp.broadcast_to(egt, (LANES, LANES))

    def load(r):
        r0 = pl.multiple_of(r * DEC_PAD, DEC_PAD)
        rr = pl.ds(r0, DEC_PAD)
        return [(rec_ref[r, h], w_s[h, rr, :], qe_s[h, rr, :], u_s[h, rr, :], qk_s[h, rr, :],
                 kd_s[h, rr, :], eg_s[h, pl.ds(r0, 1), :]) for h in range(B_HEADS)]

    def read_state(heads):
        return [(_bdot(jnp.concatenate([w, qe], axis=0), s), s, u, qk, kd, eg)
                for s, w, qe, u, qk, kd, eg in heads]

    def update(heads):
        outs = []
        for res, s, u, qk, kd, eg in heads:
            v_new = u - res[:DEC_PAD]
            vt = jnp.concatenate([v_new] * (LANES // DEC_PAD), axis=0)
            outs.append((res[DEC_PAD:] + _bdot(qk, vt), s * eg + _bdot_tn(kd, v_new)))
        return outs

    def store(r, outs):
        rr = pl.ds(pl.multiple_of(r * DEC_PAD, DEC_PAD), DEC_PAD)
        for h, (o, s_new) in enumerate(outs):
            o_s[h, rr, :] = o
            s_ref[r, h] = s_new

    _rows_loop(rows, load, [read_state, update], store)
    nw = nw_ref[...]
    for h in range(B_HEADS):
        sl = slice(h * B_DV, (h + 1) * B_DV)
        ob_ref[:, sl] = _gdn_out(o_s[h], z_ref[:, sl], nw).astype(BF16)


def _gdn_decode(raw, histp, ba, z, rec, cw, alog, dtb, nw, rows, valid_len):
    nb = raw.shape[0]
    flat = rows * DEC_PAD
    assert flat == LANES
    sq = lambda: pltpu.VMEM((B_HEADS, LANES, LANES), F32)
    return pl.pallas_call(
        functools.partial(_gdn_decode_kernel, rows=rows, valid_len=valid_len),
        grid=(nb // rows,),
        in_specs=[pl.BlockSpec((rows, DEC_PAD, B_CONV_CH), lambda i: (i, 0, 0)),
                  pl.BlockSpec((rows, DEC_PAD, B_CONV_CH), lambda i: (i, 0, 0)),
                  pl.BlockSpec((flat, LANES), lambda i: (i, 0)),
                  pl.BlockSpec((flat, 512), lambda i: (i, 0)),
                  pl.BlockSpec((rows, B_HEADS, B_DK, B_DV), lambda i: (i, 0, 0, 0)),
                  _const_spec((CONV_W, B_CONV_CH)), _const_spec((1, LANES)), _const_spec((1, LANES)),
                  _const_spec((1, B_DV))],
        out_specs=[pl.BlockSpec((flat, 512), lambda i: (i, 0)),
                   pl.BlockSpec((rows, B_HEADS, B_DK, B_DV), lambda i: (i, 0, 0, 0))],
        out_shape=[jax.ShapeDtypeStruct((nb * DEC_PAD, 512), BF16),
                   jax.ShapeDtypeStruct(rec.shape, F32)],
        scratch_shapes=[pltpu.VMEM((rows, 2 * DEC_PAD, B_CONV_CH), F32)] + [sq() for _ in range(7)],
        compiler_params=_cparams(("arbitrary",)),
        name="gdn_decode",
    )(raw, histp, ba, z, rec, cw, alog, dtb, nw)


def _memkv_kernel(m_ref, g_ref, w_ref, k_ref, v_ref):
    h = _rms(m_ref[...], g_ref[...]).astype(BF16)
    n = C_HEADS * C_HD
    k_ref[...] = jnp.dot(h, w_ref[:, :n], preferred_element_type=F32)
    v_ref[...] = jnp.dot(h, w_ref[:, n:], preferred_element_type=F32)


def _memkv(mem, gain, w):
    t = mem.shape[0]
    tm = 512
    n = C_HEADS * C_HD
    return pl.pallas_call(
        _memkv_kernel,
        grid=(t // tm,),
        in_specs=[pl.BlockSpec((tm, D_MODEL), lambda i: (i, 0)), _const_spec((1, D_MODEL)),
                  _const_spec((D_MODEL, 2 * n))],
        out_specs=[pl.BlockSpec((tm, n), lambda i: (i, 0))] * 2,
        out_shape=[jax.ShapeDtypeStruct((t, n), F32)] * 2,
        compiler_params=_cparams(("arbitrary",)),
        name="memkv",
    )(mem, gain, w)


def _softmax_rows(logits):
    m = jnp.max(logits, axis=-1, keepdims=True)
    e = jnp.exp(logits - m)
    return e, 1.0 / jnp.sum(e, axis=-1, keepdims=True)


def _memattn_prompt_kernel(q_ref, k_ref, v_ref, o_ref):
    hs = lambda h: slice(h * C_HD, (h + 1) * C_HD)
    scores = lambda h: _bdot_nt(q_ref[:, hs(h)], k_ref[:, hs(h)])
    logits = scores(0)
    for h in range(C_HEADS):
        nxt = scores(h + 1) if h + 1 < C_HEADS else None
        e, inv = _softmax_rows(logits * (C_HD ** -0.5))
        o_ref[:, hs(h)] = (_bdot(e, v_ref[:, hs(h)]) * inv).astype(BF16)
        logits = nxt


def _memattn_prompt(qc, mk, mv, batch, seq, tm):
    nq = seq // tm
    n = C_HEADS * C_HD
    cur = pl.BlockSpec((tm, n), lambda b, i: (b * nq + i, 0))
    mem = pl.BlockSpec((N_MEM, n), lambda b, i: (b, 0))
    return pl.pallas_call(
        _memattn_prompt_kernel,
        grid=(batch, nq),
        in_specs=[cur, mem, mem],
        out_specs=cur,
        out_shape=jax.ShapeDtypeStruct(qc.shape, BF16),
        compiler_params=_cparams(("arbitrary", "arbitrary")),
        name="memattn_prompt",
    )(qc, mk, mv)


def _memattn_decode_kernel(q_ref, k_ref, v_ref, o_ref, *, rows):
    nk = N_MEM * C_HEADS
    col = lax.broadcasted_iota(jnp.int32, (C_HEADS * DEC_PAD, nk), 1)
    row = lax.broadcasted_iota(jnp.int32, (C_HEADS * DEC_PAD, nk), 0)
    own = (col & (C_HEADS - 1)) == (row >> (DEC_PAD.bit_length() - 1))

    def load(r):
        return q_ref[r], k_ref[r], v_ref[r]

    def scores(args):
        q, k, v = args
        q = q.astype(F32)
        lhs = jnp.concatenate([q[:, h * C_HD:(h + 1) * C_HD] for h in range(C_HEADS)], axis=0)
        return _bdot_nt(lhs, k), v

    def finish(args):
        logits, v = args
        e, inv = _softmax_rows(jnp.where(own, logits * (C_HD ** -0.5), NEG))
        pv = _bdot(e, v) * inv
        return jnp.concatenate([pv[h * DEC_PAD:(h + 1) * DEC_PAD] for h in range(C_HEADS)],
                               axis=1).astype(BF16)

    def store(r, o):
        o_ref[r] = o

    _rows_loop(rows, load, [scores, finish], store)


def _memattn_decode(qc, ck, cv, rows):
    nb = qc.shape[0]
    n = C_HEADS * C_HD
    blk = pl.BlockSpec((rows, DEC_PAD, n), lambda i: (i, 0, 0))
    cache = pl.BlockSpec((rows, N_MEM * C_HEADS, C_HD), lambda i: (i, 0, 0))
    return pl.pallas_call(
        functools.partial(_memattn_decode_kernel, rows=rows),
        grid=(nb // rows,),
        in_specs=[blk, cache, cache],
        out_specs=blk,
        out_shape=jax.ShapeDtypeStruct(qc.shape, BF16),
        compiler_params=_cparams(("arbitrary",)),
        name="memattn_decode",
    )(qc, ck, cv)


def _merge_kernel(x_ref, oa_ref, ob_ref, oc_ref, gpre_ref, wg_ref, wb_ref, wo_ref, gpost_ref,
                  gfpre_ref, x1_ref, h2_ref):
    x = x_ref[...]
    ups = [jnp.dot(o_ref[...], wb_ref[n], preferred_element_type=F32)
           for n, o_ref in enumerate((oa_ref, ob_ref, oc_ref))]
    h = _rms(x, gpre_ref[...]).astype(BF16)
    mix = None
    for n in range(N_BRANCH):
        gate = jax.nn.sigmoid(jnp.dot(h, wg_ref[:, n * D_MODEL:(n + 1) * D_MODEL],
                                      preferred_element_type=F32))
        mix = gate * ups[n] if mix is None else mix + gate * ups[n]
    x1 = x + _rms(_bdot(mix, wo_ref[...]), gpost_ref[...])
    x1_ref[...] = x1
    h2_ref[...] = _rms(x1, gfpre_ref[...]).astype(BF16)


FF_SPLIT = 6 * PAIR


def _ffn_kernel(x1_ref, h2_ref, wfi_ref, wfo_ref, gfpost_ref, y_ref):
    h2 = h2_ref[...]
    f = None
    halves = []
    for a, b in ((0, FF_SPLIT), (FF_SPLIT, D_FF)):
        gt = jnp.dot(h2, wfi_ref[:, a:b], preferred_element_type=F32)
        uf = jnp.dot(h2, wfi_ref[:, D_FF + a:D_FF + b], preferred_element_type=F32)
        halves.append((a, b, gt, uf))
    for a, b, gt, uf in halves:
        part = _bdot(_silu(gt) * uf, wfo_ref[a:b, :])
        f = part if f is None else f + part
    y_ref[...] = x1_ref[...] + _rms(f, gfpost_ref[...])


def _post(x, oa, ob, oc, gpre, wg, wb, wo, gpost, gfpre, wfi, wfo, gfpost, tm):
    t = x.shape[0]
    vec = _const_spec((1, D_MODEL))
    tmm = tm // 2
    row = lambda n: pl.BlockSpec((tmm, n), lambda i: (i, 0))
    frow = pl.BlockSpec((tm, D_MODEL), lambda i: (i, 0))
    x1, h2 = pl.pallas_call(
        _merge_kernel,
        grid=(t // tmm,),
        in_specs=[row(D_MODEL), row(512), row(512), row(512), vec,
                  _const_spec((D_MODEL, N_BRANCH * D_MODEL)), _const_spec((N_BRANCH, BRANCH_W, D_MODEL)),
                  _const_spec((D_MODEL, D_MODEL)), vec, vec],
        out_specs=[row(D_MODEL), row(D_MODEL)],
        out_shape=[jax.ShapeDtypeStruct(x.shape, F32), jax.ShapeDtypeStruct(x.shape, BF16)],
        compiler_params=_cparams(("arbitrary",)),
        name="merge",
    )(x, oa, ob, oc, gpre, wg, wb, wo, gpost, gfpre)
    return pl.pallas_call(
        _ffn_kernel,
        grid=(t // tm,),
        in_specs=[frow, frow, _const_spec((D_MODEL, 2 * D_FF)),
                  _const_spec((D_FF, D_MODEL)), vec],
        out_specs=frow,
        out_shape=jax.ShapeDtypeStruct(x.shape, F32),
        compiler_params=_cparams(("arbitrary",)),
        name="ffn",
    )(x1, h2, wfi, wfo, gfpost)


def _rope_tables(pos):
    half = A_HD // 2
    inv = ROPE_THETA ** (-jnp.arange(half, dtype=F32) / half)
    ang = pos.astype(F32)[:, None] * inv[None, :]
    cos, sin = jnp.cos(ang), jnp.sin(ang)
    cos = jnp.concatenate([cos, cos], axis=-1)
    sin = jnp.concatenate([-sin, sin], axis=-1)
    return jnp.tile(cos, (1, LANES // A_HD)), jnp.tile(sin, (1, LANES // A_HD))


def _lane_row(vals, offset):
    return jnp.zeros((1, LANES), F32).at[0, offset:offset + vals.shape[0]].set(vals.astype(F32))


def kernel(x_prompt, x_sample, mem_prompt, state_win_k, state_win_v, state_conv, state_rec,
           cache_mem_k, cache_mem_v, ln_mix_pre, w_in, attn_sink, gdn_conv_w, gdn_a_log,
           gdn_dt_bias, gdn_norm_w, ln_mem, w_mem_kv, w_branch, w_out, ln_mix_post,
           ln_ffn_pre, w_ffn_in, w_ffn_out, ln_ffn_post):
    bp, lp, _ = x_prompt.shape
    bs, ls, _ = x_sample.shape

    sizes = [512, 128, 128, B_CONV_CH, B_HEADS, B_HEADS, 512, 512, N_BRANCH * D_MODEL]
    o = np.cumsum([0] + sizes)
    hperm = np.concatenate([np.r_[j * A_HD:(j + 1) * A_HD, (j + 4) * A_HD:(j + 5) * A_HD] for j in range(4)])
    w1 = jnp.concatenate([
        w_in[:, o[0]:o[1]][:, hperm], w_in[:, o[1]:o[3]], w_in[:, o[3]:o[4]], w_in[:, o[6]:o[7]],
        w_in[:, o[7]:o[8]], w_in[:, o[4]:o[6]], jnp.zeros((D_MODEL, LANES - 2 * B_HEADS), F32)],
        axis=1).astype(BF16)
    wg = w_in[:, o[8]:o[9]].astype(BF16)
    wb = jnp.concatenate([w_branch[0:1][:, hperm], w_branch[1:]], axis=0).astype(BF16)
    wo = w_out.astype(BF16)
    wfi = w_ffn_in.astype(BF16)
    wfo = w_ffn_out.astype(BF16)
    wmem = w_mem_kv.astype(BF16)
    sink = attn_sink.astype(F32)[np.array([0, 4, 1, 5, 2, 6, 3, 7])]
    vec = lambda g: g.astype(F32).reshape(1, -1)
    alog = _lane_row(gdn_a_log, B_HEADS)
    dtb = _lane_row(gdn_dt_bias, B_HEADS)
    cw = gdn_conv_w.astype(F32)
    nw = vec(gdn_norm_w)

    def post(x, oa, ob, oc, tm):
        return _post(x, oa, ob, oc, vec(ln_mix_pre), wg, wb, wo, vec(ln_mix_post), vec(ln_ffn_pre),
                     wfi, wfo, vec(ln_ffn_post), tm)

    tp = bp * lp
    xp = x_prompt.reshape(tp, D_MODEL)
    cos_p, sin_p = _rope_tables(jnp.arange(lp, dtype=jnp.int32))
    qa, ka, va, z, qc, ba, qn, kn, vv, tail = _proj(xp, vec(ln_mix_pre), w1, cos_p, sin_p, 512, cw, lp)
    oa = _swa_prompt(sink, qa, ka, va, bp, lp, 512)
    b3 = lambda a: a.reshape(bp, lp, a.shape[-1])
    ob, rec_p = _gdn_prompt(b3(qn), b3(kn), b3(vv), b3(ba), b3(z), alog, dtb, nw)
    mk, mv = _memkv(mem_prompt.reshape(bp * N_MEM, D_MODEL), vec(ln_mem), wmem)
    oc = _memattn_prompt(qc, mk, mv, bp, lp, 512)
    y_p = post(xp, oa, ob.reshape(tp, 512), oc, 512).reshape(bp, lp, D_MODEL)
    wk_p = ka.reshape(bp, lp, LANES)[:, -WINDOW:].reshape(bp, WINDOW, A_KV, A_HD)
    wv_p = va.reshape(bp, lp, LANES)[:, -WINDOW:].reshape(bp, WINDOW, A_KV, A_HD)
    conv_p = tail[:, -(CONV_W - 1):]
    mem_k_p = mk.reshape(bp, N_MEM, C_HEADS, C_HD)
    mem_v_p = mv.reshape(bp, N_MEM, C_HEADS, C_HD)

    ts = bs * DEC_PAD
    xs = jnp.pad(x_sample, ((0, 0), (0, DEC_PAD - ls), (0, 0))).reshape(ts, D_MODEL)
    cos_s, sin_s = _rope_tables(PAST_LEN + jnp.arange(DEC_PAD, dtype=jnp.int32))
    cos_s, sin_s = jnp.tile(cos_s, (bs, 1)), jnp.tile(sin_s, (bs, 1))
    qa, ka, va, z, qc, ba, qkv = _proj(xs, vec(ln_mix_pre), w1, cos_s, sin_s, 512)
    r3 = lambda a: a.reshape(bs, DEC_PAD, a.shape[-1])
    oa = _swa_decode(sink, r3(qa), r3(ka), r3(va), state_win_k.reshape(bs, WINDOW, LANES),
                     state_win_v.reshape(bs, WINDOW, LANES), 16).reshape(ts, 512)
    histp = jnp.pad(state_conv, ((0, 0), (DEC_PAD - (CONV_W - 1), 0), (0, 0)))
    ob, rec_s = _gdn_decode(r3(qkv), histp, ba, z, state_rec, cw, alog, dtb, nw,
                            LANES // DEC_PAD, ls)
    oc = _memattn_decode(r3(qc), cache_mem_k.reshape(bs, N_MEM * C_HEADS, C_HD),
                         cache_mem_v.reshape(bs, N_MEM * C_HEADS, C_HD), 8).reshape(ts, 512)
    real = lambda a: a.reshape(bs, DEC_PAD, 512)[:, :ls].reshape(bs * ls, 512)
    y_s = post(x_sample.reshape(bs * ls, D_MODEL), real(oa), real(ob), real(oc), 512).reshape(bs, ls, D_MODEL)
    k_new = r3(ka)[:, :ls].reshape(bs, ls, A_KV, A_HD)
    v_new = r3(va)[:, :ls].reshape(bs, ls, A_KV, A_HD)
    wk_s = jnp.concatenate([state_win_k[:, ls:], k_new], axis=1)
    wv_s = jnp.concatenate([state_win_v[:, ls:], v_new], axis=1)
    conv_s = r3(qkv)[:, ls - (CONV_W - 1):ls]

    return (y_p, y_s, wk_p, wv_p, conv_p, rec_p, mem_k_p, mem_v_p, wk_s, wv_s, conv_s, rec_s)
```

```python
import functools

import numpy as np
import jax
import jax.numpy as jnp
from jax import lax
from jax.experimental import pallas as pl
from jax.experimental.pallas import tpu as pltpu

F32 = jnp.float32
BF16 = jnp.bfloat16

D_MODEL = 1024
PAST_LEN = 16384
EPS = 1e-6
ROPE_THETA = 10000.0
N_MEM = 256
WINDOW = 128
A_HD = 64
A_HEADS = 8
A_KV = 2
A_SCALE = A_HD ** -0.5
B_HEADS = 4
B_DK = 128
B_DV = 128
CONV_W = 4
GDN_CHUNK = 64
B_CONV_CH = B_HEADS * (2 * B_DK + B_DV)
C_HEADS = 4
C_HD = 128
N_BRANCH = 3
BRANCH_W = 512
D_FF = 2816

LANES = 128
SUBLANES = 8
PAIR = 2 * LANES
VMEM_LIMIT = 56 * 1024 * 1024
NEG = -1e30
DEC_PAD = SUBLANES


def _cparams(sem, vmem=VMEM_LIMIT):
    return pltpu.CompilerParams(dimension_semantics=sem, vmem_limit_bytes=vmem)


def _const_spec(shape):
    nd = len(shape)
    return pl.BlockSpec(shape, lambda *_: (0,) * nd, pipeline_mode=pl.Buffered(1))


def _rms(x, g):
    ms = jnp.mean(x * x, axis=-1, keepdims=True)
    return x * lax.rsqrt(ms + EPS) * g


def _bdot(a, b):
    return jnp.dot(a.astype(BF16), b.astype(BF16), preferred_element_type=F32)


def _bdot_nt(a, b):
    return lax.dot_general(a.astype(BF16), b.astype(BF16), (((1,), (1,)), ((), ())),
                           preferred_element_type=F32)


def _bdot_tn(a, b):
    return lax.dot_general(a.astype(BF16), b.astype(BF16), (((0,), (0,)), ((), ())),
                           preferred_element_type=F32)


def _hdot(a, b):
    return jnp.dot(a, b, precision=lax.Precision.HIGHEST, preferred_element_type=F32)


def _silu(x):
    return x * jax.nn.sigmoid(x)


def _softplus(x):
    return jnp.maximum(x, 0.0) + jnp.log1p(jnp.exp(-jnp.abs(x)))


def _rope128(v, cos, sin):
    lane = lax.broadcasted_iota(jnp.int32, v.shape, 1)
    fwd = pltpu.roll(v, 32, 1)
    bwd = pltpu.roll(v, 96, 1)
    sw = jnp.where((lane & 32) == 0, bwd, fwd)
    return v * cos + sw * sin


def _l2n(x):
    return x * lax.rsqrt(jnp.sum(x * x, axis=-1, keepdims=True) + EPS)


def _proj_steps(x_ref, g_ref, wq_ref, wr_ref, wzc_ref, wba_ref, cos_ref, sin_ref,
                qa_ref, ka_ref, va_ref, z_ref, qc_ref, ba_ref):
    h = _rms(x_ref[...], g_ref[...]).astype(BF16)
    cos = cos_ref[...]
    sin = sin_ref[...]

    def mm(w_ref, a, b):
        return jnp.dot(h, w_ref[:, a:b], preferred_element_type=F32)

    def qa_half(c0):
        q = mm(wq_ref, c0 * LANES, (c0 + 2) * LANES)
        for c in range(2):
            qa_ref[:, (c0 + c) * LANES:(c0 + c + 1) * LANES] = (
                _rope128(q[:, c * LANES:(c + 1) * LANES], cos, sin) * A_SCALE).astype(BF16)

    def kv():
        kv2 = mm(wr_ref, 0, PAIR)
        ka_ref[...] = _rope128(kv2[:, :LANES], cos, sin)
        va_ref[...] = kv2[:, LANES:]

    def z_half(c0):
        z_ref[:, c0:c0 + PAIR] = mm(wzc_ref, c0, c0 + PAIR)

    def qc_half(c0):
        qc_ref[:, c0:c0 + PAIR] = mm(wzc_ref, 512 + c0, 512 + c0 + PAIR).astype(BF16)

    def ba():
        ba_ref[...] = mm(wba_ref, 0, LANES)

    def qkv(a, b):
        return mm(wr_ref, PAIR + a, PAIR + b)

    steps = [lambda: qa_half(0), lambda: qa_half(2), kv, lambda: z_half(0), lambda: z_half(PAIR),
             lambda: qc_half(0), lambda: qc_half(PAIR), ba]
    return qkv, steps


def _proj_raw_kernel(x_ref, g_ref, wq_ref, wr_ref, wzc_ref, wba_ref, cos_ref, sin_ref,
                     qa_ref, ka_ref, va_ref, z_ref, qc_ref, ba_ref, qkv_ref):
    qkv, steps = _proj_steps(x_ref, g_ref, wq_ref, wr_ref, wzc_ref, wba_ref, cos_ref, sin_ref,
                             qa_ref, ka_ref, va_ref, z_ref, qc_ref, ba_ref)
    qkv_ref[...] = qkv(0, B_CONV_CH)
    for step in steps:
        step()


def _proj_conv_kernel(x_ref, g_ref, wq_ref, wr_ref, wzc_ref, wba_ref, cos_ref, sin_ref, cw_ref,
                      qa_ref, ka_ref, va_ref, z_ref, qc_ref, ba_ref, qn_ref, kn_ref, vv_ref, tail_ref,
                      buf_ref, *, tm, tiles_per_seq):
    hist = SUBLANES
    first = lax.rem(pl.program_id(0), tiles_per_seq) == 0

    @pl.when(first)
    def _():
        buf_ref[0:hist, :] = jnp.zeros((hist, B_CONV_CH), F32)

    @pl.when(jnp.logical_not(first))
    def _():
        buf_ref[0:hist, :] = buf_ref[tm:tm + hist, :]

    qkv, steps = _proj_steps(x_ref, g_ref, wq_ref, wr_ref, wzc_ref, wba_ref, cos_ref, sin_ref,
                             qa_ref, ka_ref, va_ref, z_ref, qc_ref, ba_ref)
    nq = B_HEADS * B_DK

    def conv_group(c0):
        cs = slice(c0, c0 + PAIR)
        raw = qkv(c0, c0 + PAIR)
        buf_ref[hist:hist + tm, cs] = raw
        tail_ref[0, :, cs] = raw[tm - hist:, :]
        xb = buf_ref[:, cs]
        acc = xb * cw_ref[0:1, cs]
        for i in range(1, CONV_W):
            acc = pltpu.roll(acc, 1, 0) + xb * cw_ref[i:i + 1, cs]
        act = _silu(acc[hist:])
        if c0 >= 2 * nq:
            vv_ref[:, c0 - 2 * nq:c0 - 2 * nq + PAIR] = act
            return
        out_ref, base, scale = (qn_ref, 0, B_DK ** -0.5) if c0 < nq else (kn_ref, nq, 1.0)
        for s in range(2):
            o0 = c0 - base + s * B_DK
            out_ref[:, o0:o0 + B_DK] = _l2n(act[:, s * B_DK:(s + 1) * B_DK]) * scale

    groups = [functools.partial(conv_group, c0) for c0 in range(0, B_CONV_CH, PAIR)]
    while groups or steps:
        if groups:
            groups.pop(0)()
        if steps:
            steps.pop(0)()


_PROJ_OUTS = [(512, BF16), (128, F32), (128, F32), (512, F32), (512, BF16), (128, F32)]


def _proj(x, gain, ws, cos, sin, tm, cw=None, seq=None):
    wq, wr, wtail, wba = ws
    t = x.shape[0]
    ntab = cos.shape[0] // tm
    row = lambda n: pl.BlockSpec((tm, n), lambda i: (i, 0))
    tab = pl.BlockSpec((tm, LANES), lambda i: (i % ntab, 0))
    in_specs = [row(D_MODEL), _const_spec((1, D_MODEL)), _const_spec(wq.shape), _const_spec(wr.shape),
                _const_spec((D_MODEL, D_MODEL)), _const_spec(wba.shape), tab, tab]
    out_specs = [row(n) for n, _ in _PROJ_OUTS]
    out_shape = [jax.ShapeDtypeStruct((t, n), d) for n, d in _PROJ_OUTS]
    if cw is None:
        return pl.pallas_call(
            _proj_raw_kernel, grid=(t // tm,), in_specs=in_specs,
            out_specs=out_specs + [row(B_CONV_CH)],
            out_shape=out_shape + [jax.ShapeDtypeStruct((t, B_CONV_CH), F32)],
            compiler_params=_cparams(("arbitrary",)), name="proj",
        )(x, gain, wq, wr, wtail, wba, cos, sin)
    tiles = seq // tm
    return pl.pallas_call(
        functools.partial(_proj_conv_kernel, tm=tm, tiles_per_seq=tiles),
        grid=(t // tm,),
        in_specs=in_specs + [_const_spec((CONV_W, B_CONV_CH))],
        out_specs=out_specs + [row(512)] * 3
        + [pl.BlockSpec((1, SUBLANES, B_CONV_CH), lambda i: (i // tiles, 0, 0))],
        out_shape=out_shape + [jax.ShapeDtypeStruct((t, 512), F32)] * 3
        + [jax.ShapeDtypeStruct((t // seq, SUBLANES, B_CONV_CH), F32)],
        scratch_shapes=[pltpu.VMEM((tm + SUBLANES, B_CONV_CH), F32)],
        compiler_params=_cparams(("arbitrary",)), name="proj_conv",
    )(x, gain, wq, wr, wtail, wba, cos, sin, cw)


def _swa_scores(q, k16):
    tq = q.shape[0]
    lo = lax.broadcasted_iota(jnp.int32, (tq, LANES), 1) < A_HD
    blocks = []
    for j in range(4):
        c = q[:, j * LANES:(j + 1) * LANES].astype(F32)
        blocks.append(jnp.where(lo, c, 0.0))
        blocks.append(jnp.where(lo, 0.0, c))
    lhs = jnp.concatenate(blocks, axis=0).astype(BF16)
    return lax.dot_general(lhs, k16, (((1,), (1,)), ((), ())), preferred_element_type=F32)


def _swa_finish(logits, v16, mask, sink_ref):
    tq = logits.shape[0] // A_HEADS
    lo = lax.broadcasted_iota(jnp.int32, (tq, LANES), 1) < A_HD
    es, inv = [], []
    for s in range(8):
        l = jnp.where(mask, logits[s * tq:(s + 1) * tq], NEG)
        sk = sink_ref[s]
        m = jnp.maximum(jnp.max(l, axis=-1, keepdims=True), sk)
        e = jnp.exp(l - m)
        den = jnp.sum(e, axis=-1, keepdims=True) + jnp.exp(sk - m)
        es.append(e.astype(BF16))
        inv.append(1.0 / den)
    pv = jnp.dot(jnp.concatenate(es, axis=0), v16, preferred_element_type=F32)
    outs = []
    for j in range(4):
        a = pv[(2 * j) * tq:(2 * j + 1) * tq] * inv[2 * j]
        b = pv[(2 * j + 1) * tq:(2 * j + 2) * tq] * inv[2 * j + 1]
        outs.append(jnp.where(lo, a, b))
    return jnp.concatenate(outs, axis=1)


def _swa_prompt_kernel(sink_ref, q_ref, kc_ref, kp_ref, vc_ref, vp_ref, o_ref, *, nblk):
    i = pl.program_id(1)
    kcat = jnp.concatenate([kp_ref[...], kc_ref[...]], axis=0).astype(BF16)
    vcat = jnp.concatenate([vp_ref[...], vc_ref[...]], axis=0).astype(BF16)
    ii = lax.broadcasted_iota(jnp.int32, (WINDOW, 2 * WINDOW), 0)
    jj = lax.broadcasted_iota(jnp.int32, (WINDOW, 2 * WINDOW), 1)
    band = (jj > ii) & (jj <= ii + WINDOW)
    for jb in range(nblk):
        mask = band
        if jb == 0:
            mask = band & ((jj >= WINDOW) | (i > 0))
        logits = _swa_scores(q_ref[jb * WINDOW:(jb + 1) * WINDOW, :], kcat[jb * WINDOW:(jb + 2) * WINDOW])
        o = _swa_finish(logits, vcat[jb * WINDOW:(jb + 2) * WINDOW], mask, sink_ref)
        o_ref[jb * WINDOW:(jb + 1) * WINDOW, :] = o.astype(BF16)


def _swa_prompt(sink, qa, ka, va, batch, seq, tq):
    nq = seq // tq
    nblk = tq // WINDOW
    nw = seq // WINDOW
    cur = lambda n: pl.BlockSpec((tq, n), lambda b, i: (b * nq + i, 0))
    prev = pl.BlockSpec((WINDOW, LANES), lambda b, i: (jnp.maximum(b * nw + i * nblk - 1, 0), 0))
    return pl.pallas_call(
        functools.partial(_swa_prompt_kernel, nblk=nblk),
        grid=(batch, nq),
        in_specs=[pl.BlockSpec(memory_space=pltpu.SMEM), cur(512), cur(LANES), prev, cur(LANES), prev],
        out_specs=cur(512),
        out_shape=jax.ShapeDtypeStruct(qa.shape, BF16),
        compiler_params=_cparams(("arbitrary", "arbitrary")),
        name="swa_prompt",
    )(sink, qa, ka, ka, va, va)


ROW_UNROLL = 4


def _rows_loop(rows, load, stages, store):
    def body(g, carry):
        idx = [g * ROW_UNROLL + j for j in range(ROW_UNROLL)]
        vals = [load(r) for r in idx]
        for stage in stages:
            vals = [stage(v) for v in vals]
        for r, o in zip(idx, vals):
            store(r, o)
        return carry

    lax.fori_loop(0, rows // ROW_UNROLL, body, 0)


def _swa_decode_kernel(sink_ref, q_ref, kn_ref, vn_ref, wk_ref, wv_ref, o_ref, wko_ref, wvo_ref,
                       *, rows, new_len):
    ii = lax.broadcasted_iota(jnp.int32, (DEC_PAD, 2 * WINDOW), 0)
    jj = lax.broadcasted_iota(jnp.int32, (DEC_PAD, 2 * WINDOW), 1)
    mask = ((jj < WINDOW) & (jj > ii)) | ((jj >= WINDOW) & (jj - WINDOW <= ii))
    pad = jnp.zeros((WINDOW - DEC_PAD, LANES), F32)
    keep = lax.broadcasted_iota(jnp.int32, (WINDOW, LANES), 0) < WINDOW - new_len

    def slide(win, new):
        tail = jnp.concatenate([pad, new], axis=0)
        return jnp.where(keep, pltpu.roll(win, WINDOW - new_len, 0), pltpu.roll(tail, DEC_PAD - new_len, 0))

    def load(r):
        return q_ref[r], wk_ref[r], kn_ref[r], wv_ref[r], vn_ref[r]

    def scores(args):
        q, wk, kn, wv, vn = args
        kk = jnp.concatenate([wk, kn, pad], axis=0).astype(BF16)
        return _swa_scores(q, kk), wk, kn, wv, vn

    def finish(args):
        logits, wk, kn, wv, vn = args
        vv = jnp.concatenate([wv, vn, pad], axis=0).astype(BF16)
        return _swa_finish(logits, vv, mask, sink_ref).astype(BF16), slide(wk, kn), slide(wv, vn)

    def store(r, outs):
        o_ref[r], wko_ref[r], wvo_ref[r] = outs

    _rows_loop(rows, load, [scores, finish], store)


def _swa_decode(sink, qa, kn, vn, wk, wv, rows, new_len):
    nb = qa.shape[0]
    blk = lambda a, n: pl.BlockSpec((rows, a, n), lambda i: (i, 0, 0))
    return pl.pallas_call(
        functools.partial(_swa_decode_kernel, rows=rows, new_len=new_len),
        grid=(nb // rows,),
        in_specs=[pl.BlockSpec(memory_space=pltpu.SMEM), blk(DEC_PAD, 512), blk(DEC_PAD, LANES),
                  blk(DEC_PAD, LANES), blk(WINDOW, LANES), blk(WINDOW, LANES)],
        out_specs=[blk(DEC_PAD, 512), blk(WINDOW, LANES), blk(WINDOW, LANES)],
        out_shape=[jax.ShapeDtypeStruct(qa.shape, BF16), jax.ShapeDtypeStruct(wk.shape, F32),
                   jax.ShapeDtypeStruct(wv.shape, F32)],
        compiler_params=_cparams(("arbitrary",)),
        name="swa_decode",
    )(sink, qa, kn, vn, wk, wv)


def _gdn_gates(ba, alog, dtb, valid):
    beta = jax.nn.sigmoid(ba)
    g = -jnp.exp(alog) * _softplus(ba + dtb)
    if valid is not None:
        beta = jnp.where(valid, beta, 0.0)
        g = jnp.where(valid, g, 0.0)
    return beta, g


def _chunk_masks(chunk):
    sh = chunk.bit_length() - 1
    ri = lax.broadcasted_iota(jnp.int32, (LANES, LANES), 0)
    ci = lax.broadcasted_iota(jnp.int32, (LANES, LANES), 1)
    same = (ri >> sh) == (ci >> sh)
    return same, same & (ri >= ci), same & (ri > ci), ri == ci


def _gdn_cumsums(g_all, chunk):
    same, tri, _, _ = _chunk_masks(chunk)
    lower = jnp.where(tri, 1.0, 0.0)
    gcol = _hdot(lower, g_all)
    grow = _hdot(g_all.T, lower.T)
    gtot = _hdot(jnp.where(same, 1.0, 0.0), g_all)
    return gcol, grow, gtot


def _gdn_phase_a(q, k, v, beta, gcol, grow, gtot, chunk):
    _, tri, strict, _ = _chunk_masks(chunk)
    decay = jnp.exp(jnp.where(tri, gcol - grow, NEG))
    kb = k * beta
    kkqk = _bdot_nt(jnp.concatenate([kb, q], axis=0), k)
    a = jnp.where(strict, kkqk[:LANES] * decay, 0.0)
    qk = kkqk[LANES:] * decay
    n = -a
    apow = a
    for _ in range(chunk.bit_length() - 2):
        apow = _bdot(apow, apow)
        n = n + apow + _bdot(n, apow)
    rhs = jnp.concatenate([v * beta, kb * jnp.exp(gcol)], axis=1)
    sol = rhs + _bdot(n, rhs)
    u, w = sol[:, :B_DV], sol[:, B_DV:]
    return u, w, q * jnp.exp(gcol), k * jnp.exp(gtot - gcol), qk, jnp.exp(gtot)


def _gdn_qkv_heads(qkv, h):
    q = qkv[:, h * B_DK:(h + 1) * B_DK]
    k = qkv[:, (B_HEADS + h) * B_DK:(B_HEADS + h + 1) * B_DK]
    v = qkv[:, (2 * B_HEADS + h) * B_DK:(2 * B_HEADS + h + 1) * B_DK]
    q = q * lax.rsqrt(jnp.sum(q * q, axis=-1, keepdims=True) + EPS) * (B_DK ** -0.5)
    k = k * lax.rsqrt(jnp.sum(k * k, axis=-1, keepdims=True) + EPS)
    return q, k, v


def _gdn_out(o, z, nw):
    return _rms(o, nw) * _silu(z)


def _pair_bd(x2):
    lo = lax.broadcasted_iota(jnp.int32, x2.shape, 1) < LANES
    return jnp.concatenate([jnp.where(lo, x2, 0.0), jnp.where(lo, 0.0, x2)], axis=0).astype(BF16)


def _quad_bd(x4):
    blk = lax.broadcasted_iota(jnp.int32, x4.shape, 1) >> (GDN_CHUNK.bit_length() - 1)
    return jnp.concatenate([jnp.where(blk == j, x4, 0.0) for j in range(PAIR // GDN_CHUNK)],
                           axis=0).astype(BF16)


def _pair_cols(x, l0, l1):
    lo = lax.broadcasted_iota(jnp.int32, (x.shape[0], PAIR), 1) < LANES
    return jnp.where(lo, x[:, l0:l0 + 1], x[:, l1:l1 + 1])


def _gdn_prompt_kernel(q_ref, k_ref, v_ref, ba_ref, z_ref, alog_ref, dtb_ref, nw_ref,
                       ob_ref, s_ref, sbd_ref, *, nb):
    t = pl.program_id(0)
    npair = B_HEADS // 2

    @pl.when(t == 0)
    def _():
        sbd_ref[...] = jnp.zeros(sbd_ref.shape, F32)

    alog = alog_ref[...]
    dtb = dtb_ref[...]
    nw = nw_ref[...]
    nchunk = LANES // GDN_CHUNK
    nblk = nb
    _, tri, strict, _ = _chunk_masks(GDN_CHUNK)
    tri2 = jnp.concatenate([tri, tri], axis=1)
    strict2 = jnp.concatenate([strict, strict], axis=1)

    items = []
    for blk in range(nblk):
        beta_all, g_all = _gdn_gates(ba_ref[blk], alog, dtb, None)
        gcol_all, grow_all, gtot_all = _gdn_cumsums(g_all, GDN_CHUNK)
        eg_all = jnp.exp(gcol_all)
        ed_all = jnp.exp(gtot_all - gcol_all)
        et_all = jnp.exp(gtot_all)
        for p in range(npair):
            h0, h1 = 2 * p, 2 * p + 1
            g0, g1 = B_HEADS + h0, B_HEADS + h1
            psl = slice(p * PAIR, (p + 1) * PAIR)
            q2 = q_ref[blk, :, psl]
            k2 = k_ref[blk, :, psl]
            v2 = v_ref[blk, :, psl]
            beta2 = _pair_cols(beta_all, h0, h1)
            grow2 = jnp.concatenate([grow_all[g0:g0 + 1, :], grow_all[g1:g1 + 1, :]], axis=1)
            decay2 = jnp.exp(jnp.where(tri2, _pair_cols(gcol_all, g0, g1) - grow2, NEG))
            kb2 = k2 * beta2
            items.append(dict(
                blk=blk, p=p, decay2=decay2, kb2=kb2, v2b=v2 * beta2,
                kbe2=kb2 * _pair_cols(eg_all, g0, g1),
                lhs=jnp.concatenate([kb2, q2], axis=0).astype(BF16),
                kbd=_pair_bd(k2),
                qe2=(q2 * _pair_cols(eg_all, g0, g1)).astype(BF16),
                kd2=(k2 * _pair_cols(ed_all, g0, g1)).astype(BF16),
                et2=_pair_cols(et_all, g0, g1)))

    for it in items:
        kkqk = lax.dot_general(it["lhs"], it["kbd"], (((1,), (1,)), ((), ())),
                               preferred_element_type=F32)
        it["a2"] = jnp.where(strict2, kkqk[:LANES] * it["decay2"], 0.0)
        it["qk2"] = (kkqk[LANES:] * it["decay2"]).astype(BF16)
    top = (lax.broadcasted_iota(jnp.int32, (GDN_CHUNK, PAIR), 1) & GDN_CHUNK) == 0
    for it in items:
        a64 = it["a2"][:GDN_CHUNK] + it["a2"][GDN_CHUNK:]
        it["pw"] = jnp.dot(a64.astype(BF16), _quad_bd(a64), preferred_element_type=F32)
        it["n"] = -a64
    nlev = GDN_CHUNK.bit_length() - 2
    for lev in range(nlev):
        for it in items:
            pw, n = it["pw"], it["n"]
            bd = _quad_bd(pw)
            if lev < nlev - 1:
                r = jnp.dot(jnp.concatenate([pw, n], axis=0).astype(BF16), bd, preferred_element_type=F32)
                it["pw"], npw = r[:GDN_CHUNK], r[GDN_CHUNK:]
            else:
                npw = jnp.dot(n.astype(BF16), bd, preferred_element_type=F32)
            it["n"] = n + pw + npw
    for it in items:
        n64 = it["n"]
        it["n"] = jnp.concatenate([jnp.where(top, n64, 0.0), jnp.where(top, 0.0, n64)], axis=0)
    for it in items:
        us, ws = [], []
        for s in range(2):
            sl = slice(s * LANES, (s + 1) * LANES)
            rhs = jnp.concatenate([it["v2b"][:, sl], it["kbe2"][:, sl]], axis=1)
            sol = rhs + _bdot(it["n"][:, sl], rhs)
            us.append(sol[:, :B_DV])
            ws.append(sol[:, B_DV:])
        it["u2"] = jnp.concatenate(us, axis=1)
        it["w2"] = jnp.concatenate(ws, axis=1).astype(BF16)

    ri = lax.broadcasted_iota(jnp.int32, (PAIR, PAIR), 0) < LANES
    ci = lax.broadcasted_iota(jnp.int32, (PAIR, PAIR), 1) < LANES
    bdmask = ri == ci
    state = [sbd_ref[i] for i in range(len(items))]
    outs = {}
    for c in range(nchunk):
        rows = slice(c * GDN_CHUNK, (c + 1) * GDN_CHUNK)
        rs = [jnp.dot(jnp.concatenate([it["w2"][rows], it["qe2"][rows]], axis=0), state[i].astype(BF16),
                      preferred_element_type=F32) for i, it in enumerate(items)]
        for i, it in enumerate(items):
            s = state[i]
            r = rs[i]
            vn2 = it["u2"][rows] - r[:GDN_CHUNK]
            vt = _pair_bd(jnp.concatenate([vn2] * nchunk, axis=0))
            outs[(i, c)] = r[GDN_CHUNK:] + jnp.dot(it["qk2"][rows], vt, preferred_element_type=F32)
            upd = lax.dot_general(it["kd2"][rows], vn2.astype(BF16), (((0,), (0,)), ((), ())),
                                  preferred_element_type=F32)
            state[i] = s * it["et2"][c * GDN_CHUNK:c * GDN_CHUNK + 1, :] + jnp.where(bdmask, upd, 0.0)
    for i, it in enumerate(items):
        sbd_ref[i] = state[i]

    @pl.when(t == pl.num_programs(0) - 1)
    def _():
        for i, it in enumerate(items):
            s_ref[it["blk"], 2 * it["p"]] = state[i][:LANES, :LANES]
            s_ref[it["blk"], 2 * it["p"] + 1] = state[i][LANES:, LANES:]

    for i, it in enumerate(items):
        o2 = jnp.concatenate([outs[(i, c)] for c in range(nchunk)], axis=0)
        for s in range(2):
            h = 2 * it["p"] + s
            sl = slice(h * B_DV, (h + 1) * B_DV)
            o = o2[:, s * LANES:(s + 1) * LANES]
            ob_ref[it["blk"], :, sl] = _gdn_out(o, z_ref[it["blk"], :, sl], nw).astype(BF16)


def _gdn_prompt(q, k, v, ba, z, alog, dtb, nw):
    batch, seq, _ = q.shape
    tok = lambda n: pl.BlockSpec((batch, LANES, n), lambda t: (0, t, 0))
    state = pl.BlockSpec((batch, B_HEADS, B_DK, B_DV), lambda t: (0, 0, 0, 0))
    return pl.pallas_call(
        functools.partial(_gdn_prompt_kernel, nb=batch),
        grid=(seq // LANES,),
        in_specs=[tok(512), tok(512), tok(512), tok(LANES), tok(512),
                  _const_spec((1, LANES)), _const_spec((1, LANES)), _const_spec((1, B_DV))],
        out_specs=[tok(512), state],
        out_shape=[jax.ShapeDtypeStruct((batch, seq, 512), BF16),
                   jax.ShapeDtypeStruct((batch, B_HEADS, B_DK, B_DV), F32)],
        scratch_shapes=[pltpu.VMEM((batch * (B_HEADS // 2), PAIR, PAIR), F32)],
        compiler_params=_cparams(("arbitrary",)),
        name="gdn_prompt",
    )(q, k, v, ba, z, alog, dtb, nw)


def _gdn_decode_kernel(raw_ref, hist_ref, ba_ref, z_ref, rec_ref, cw_ref, alog_ref, dtb_ref, nw_ref,
                       ob_ref, s_ref, buf_ref, u_s, w_s, qe_s, kd_s, qk_s, eg_s, o_s, *, rows, valid_len):
    buf_ref[:, 0:DEC_PAD, :] = hist_ref[...]
    buf_ref[:, DEC_PAD:2 * DEC_PAD, :] = raw_ref[...]
    conv = None
    for i in range(CONV_W):
        off = DEC_PAD - (CONV_W - 1) + i
        term = buf_ref[:, off:off + DEC_PAD, :] * cw_ref[i:i + 1, :]
        conv = term if conv is None else conv + term
    qkv = _silu(conv.reshape(rows * DEC_PAD, B_CONV_CH))
    tok = lax.broadcasted_iota(jnp.int32, (LANES, LANES), 0) & (DEC_PAD - 1)
    beta_all, g_all = _gdn_gates(ba_ref[...], alog_ref[...], dtb_ref[...], tok < valid_len)
    gcol_all, grow_all, gtot_all = _gdn_cumsums(g_all, DEC_PAD)
    for h in range(B_HEADS):
        q, k, v = _gdn_qkv_heads(qkv, h)
        gl = B_HEADS + h
        u, w, qe, kd, qk, egt = _gdn_phase_a(
            q, k, v, beta_all[:, h:h + 1], gcol_all[:, gl:gl + 1], grow_all[gl:gl + 1, :],
            gtot_all[:, gl:gl + 1], DEC_PAD)
        u_s[h] = u
        w_s[h] = w
        qe_s[h] = qe
        kd_s[h] = kd
        qk_s[h] = qk
        eg_s[h] = jnp.broadcast_to(egt, (LANES, LANES))

    def load(r):
        r0 = pl.multiple_of(r * DEC_PAD, DEC_PAD)
        rr = pl.ds(r0, DEC_PAD)
        return [(rec_ref[r, h], w_s[h, rr, :], qe_s[h, rr, :], u_s[h, rr, :], qk_s[h, rr, :],
                 kd_s[h, rr, :], eg_s[h, pl.ds(r0, 1), :]) for h in range(B_HEADS)]

    def read_state(heads):
        return [(_bdot(jnp.concatenate([w, qe], axis=0), s), s, u, qk, kd, eg)
                for s, w, qe, u, qk, kd, eg in heads]

    def update(heads):
        outs = []
        for res, s, u, qk, kd, eg in heads:
            v_new = u - res[:DEC_PAD]
            vt = jnp.concatenate([v_new] * (LANES // DEC_PAD), axis=0)
            outs.append((res[DEC_PAD:] + _bdot(qk, vt), s * eg + _bdot_tn(kd, v_new)))
        return outs

    def store(r, outs):
        rr = pl.ds(pl.multiple_of(r * DEC_PAD, DEC_PAD), DEC_PAD)
        for h, (o, s_new) in enumerate(outs):
            o_s[h, rr, :] = o
            s_ref[r, h] = s_new

    _rows_loop(rows, load, [read_state, update], store)
    nw = nw_ref[...]
    for h in range(B_HEADS):
        sl = slice(h * B_DV, (h + 1) * B_DV)
        ob_ref[:, sl] = _gdn_out(o_s[h], z_ref[:, sl], nw).astype(BF16)


def _gdn_decode(raw, histp, ba, z, rec, cw, alog, dtb, nw, rows, valid_len):
    nb = raw.shape[0]
    flat = rows * DEC_PAD
    assert flat == LANES
    sq = lambda: pltpu.VMEM((B_HEADS, LANES, LANES), F32)
    return pl.pallas_call(
        functools.partial(_gdn_decode_kernel, rows=rows, valid_len=valid_len),
        grid=(nb // rows,),
        in_specs=[pl.BlockSpec((rows, DEC_PAD, B_CONV_CH), lambda i: (i, 0, 0)),
                  pl.BlockSpec((rows, DEC_PAD, B_CONV_CH), lambda i: (i, 0, 0)),
                  pl.BlockSpec((flat, LANES), lambda i: (i, 0)),
                  pl.BlockSpec((flat, 512), lambda i: (i, 0)),
                  pl.BlockSpec((rows, B_HEADS, B_DK, B_DV), lambda i: (i, 0, 0, 0)),
                  _const_spec((CONV_W, B_CONV_CH)), _const_spec((1, LANES)), _const_spec((1, LANES)),
                  _const_spec((1, B_DV))],
        out_specs=[pl.BlockSpec((flat, 512), lambda i: (i, 0)),
                   pl.BlockSpec((rows, B_HEADS, B_DK, B_DV), lambda i: (i, 0, 0, 0))],
        out_shape=[jax.ShapeDtypeStruct((nb * DEC_PAD, 512), BF16),
                   jax.ShapeDtypeStruct(rec.shape, F32)],
        scratch_shapes=[pltpu.VMEM((rows, 2 * DEC_PAD, B_CONV_CH), F32)] + [sq() for _ in range(7)],
        compiler_params=_cparams(("arbitrary",)),
        name="gdn_decode",
    )(raw, histp, ba, z, rec, cw, alog, dtb, nw)


def _memkv_kernel(m_ref, g_ref, w_ref, k_ref, v_ref):
    h = _rms(m_ref[...], g_ref[...]).astype(BF16)
    n = C_HEADS * C_HD
    k_ref[...] = jnp.dot(h, w_ref[:, :n], preferred_element_type=F32)
    v_ref[...] = jnp.dot(h, w_ref[:, n:], preferred_element_type=F32)


def _memkv(mem, gain, w):
    t = mem.shape[0]
    tm = 512
    n = C_HEADS * C_HD
    return pl.pallas_call(
        _memkv_kernel,
        grid=(t // tm,),
        in_specs=[pl.BlockSpec((tm, D_MODEL), lambda i: (i, 0)), _const_spec((1, D_MODEL)),
                  _const_spec((D_MODEL, 2 * n))],
        out_specs=[pl.BlockSpec((tm, n), lambda i: (i, 0))] * 2,
        out_shape=[jax.ShapeDtypeStruct((t, n), F32)] * 2,
        compiler_params=_cparams(("arbitrary",)),
        name="memkv",
    )(mem, gain, w)


def _softmax_rows(logits):
    m = jnp.max(logits, axis=-1, keepdims=True)
    e = jnp.exp(logits - m)
    return e, 1.0 / jnp.sum(e, axis=-1, keepdims=True)


def _memattn_prompt_kernel(q_ref, k_ref, v_ref, o_ref):
    hs = lambda h: slice(h * C_HD, (h + 1) * C_HD)
    scores = lambda h: _bdot_nt(q_ref[:, hs(h)], k_ref[:, hs(h)])
    logits = scores(0)
    for h in range(C_HEADS):
        nxt = scores(h + 1) if h + 1 < C_HEADS else None
        e, inv = _softmax_rows(logits * (C_HD ** -0.5))
        o_ref[:, hs(h)] = (_bdot(e, v_ref[:, hs(h)]) * inv).astype(BF16)
        logits = nxt


def _memattn_prompt(qc, mk, mv, batch, seq, tm):
    nq = seq // tm
    n = C_HEADS * C_HD
    cur = pl.BlockSpec((tm, n), lambda b, i: (b * nq + i, 0))
    mem = pl.BlockSpec((N_MEM, n), lambda b, i: (b, 0))
    return pl.pallas_call(
        _memattn_prompt_kernel,
        grid=(batch, nq),
        in_specs=[cur, mem, mem],
        out_specs=cur,
        out_shape=jax.ShapeDtypeStruct(qc.shape, BF16),
        compiler_params=_cparams(("arbitrary", "arbitrary")),
        name="memattn_prompt",
    )(qc, mk, mv)


def _memattn_decode_kernel(q_ref, k_ref, v_ref, o_ref, *, rows):
    nk = N_MEM * C_HEADS
    col = lax.broadcasted_iota(jnp.int32, (C_HEADS * DEC_PAD, nk), 1)
    row = lax.broadcasted_iota(jnp.int32, (C_HEADS * DEC_PAD, nk), 0)
    own = (col & (C_HEADS - 1)) == (row >> (DEC_PAD.bit_length() - 1))

    def load(r):
        return q_ref[r], k_ref[r], v_ref[r]

    def scores(args):
        q, k, v = args
        q = q.astype(F32)
        lhs = jnp.concatenate([q[:, h * C_HD:(h + 1) * C_HD] for h in range(C_HEADS)], axis=0)
        return _bdot_nt(lhs, k), v

    def finish(args):
        logits, v = args
        e, inv = _softmax_rows(jnp.where(own, logits * (C_HD ** -0.5), NEG))
        pv = _bdot(e, v) * inv
        return jnp.concatenate([pv[h * DEC_PAD:(h + 1) * DEC_PAD] for h in range(C_HEADS)],
                               axis=1).astype(BF16)

    def store(r, o):
        o_ref[r] = o

    _rows_loop(rows, load, [scores, finish], store)


def _memattn_decode(qc, ck, cv, rows):
    nb = qc.shape[0]
    n = C_HEADS * C_HD
    blk = pl.BlockSpec((rows, DEC_PAD, n), lambda i: (i, 0, 0))
    cache = pl.BlockSpec((rows, N_MEM * C_HEADS, C_HD), lambda i: (i, 0, 0))
    return pl.pallas_call(
        functools.partial(_memattn_decode_kernel, rows=rows),
        grid=(nb // rows,),
        in_specs=[blk, cache, cache],
        out_specs=blk,
        out_shape=jax.ShapeDtypeStruct(qc.shape, BF16),
        compiler_params=_cparams(("arbitrary",)),
        name="memattn_decode",
    )(qc, ck, cv)


def _merge_kernel(x_ref, oa_ref, ob_ref, oc_ref, gpre_ref, wga_ref, wgb_ref, wgc_ref, wb_ref, wo_ref,
                  gpost_ref, gfpre_ref, x1_ref, h2_ref):
    x = x_ref[...]
    ups = [jnp.dot(o_ref[...], wb_ref[n], preferred_element_type=F32)
           for n, o_ref in enumerate((oa_ref, ob_ref, oc_ref))]
    h = _rms(x, gpre_ref[...]).astype(BF16)
    mix = None
    for n, wg_ref in enumerate((wga_ref, wgb_ref, wgc_ref)):
        gate = jax.nn.sigmoid(jnp.dot(h, wg_ref[...], preferred_element_type=F32))
        mix = gate * ups[n] if mix is None else mix + gate * ups[n]
    x1 = x + _rms(_bdot(mix, wo_ref[...]), gpost_ref[...])
    x1_ref[...] = x1
    h2_ref[...] = _rms(x1, gfpre_ref[...]).astype(BF16)


FF_SPLIT = 6 * PAIR


def _ffn_kernel(x1_ref, h2_ref, wfi_ref, wfo_ref, gfpost_ref, y_ref):
    h2 = h2_ref[...]
    f = None
    halves = []
    for a, b in ((0, FF_SPLIT), (FF_SPLIT, D_FF)):
        gt = jnp.dot(h2, wfi_ref[:, a:b], preferred_element_type=F32)
        uf = jnp.dot(h2, wfi_ref[:, D_FF + a:D_FF + b], preferred_element_type=F32)
        halves.append((a, b, gt, uf))
    for a, b, gt, uf in halves:
        part = _bdot(_silu(gt) * uf, wfo_ref[a:b, :])
        f = part if f is None else f + part
    y_ref[...] = x1_ref[...] + _rms(f, gfpost_ref[...])


def _post(x, oa, ob, oc, gpre, wtail, wb, wo, gpost, gfpre, wfi, wfo, gfpost, tm):
    t = x.shape[0]
    vec = _const_spec((1, D_MODEL))
    tmm = tm // 2
    row = lambda n: pl.BlockSpec((tmm, n), lambda i: (i, 0))
    frow = pl.BlockSpec((tm, D_MODEL), lambda i: (i, 0))
    x1, h2 = pl.pallas_call(
        _merge_kernel,
        grid=(t // tmm,),
        in_specs=[row(D_MODEL), row(512), row(512), row(512), vec]
        + [pl.BlockSpec((D_MODEL, D_MODEL), functools.partial(lambda n, i: (0, n), n + 1),
                        pipeline_mode=pl.Buffered(1)) for n in range(N_BRANCH)]
        + [_const_spec((N_BRANCH, BRANCH_W, D_MODEL)), _const_spec((D_MODEL, D_MODEL)), vec, vec],
        out_specs=[row(D_MODEL), row(D_MODEL)],
        out_shape=[jax.ShapeDtypeStruct(x.shape, F32), jax.ShapeDtypeStruct(x.shape, BF16)],
        compiler_params=_cparams(("arbitrary",)),
        name="merge",
    )(x, oa, ob, oc, gpre, wtail, wtail, wtail, wb, wo, gpost, gfpre)
    return pl.pallas_call(
        _ffn_kernel,
        grid=(t // tm,),
        in_specs=[frow, frow, _const_spec((D_MODEL, 2 * D_FF)),
                  _const_spec((D_FF, D_MODEL)), vec],
        out_specs=frow,
        out_shape=jax.ShapeDtypeStruct(x.shape, F32),
        compiler_params=_cparams(("arbitrary",)),
        name="ffn",
    )(x1, h2, wfi, wfo, gfpost)


def _rope_tables(pos):
    half = A_HD // 2
    inv = ROPE_THETA ** (-jnp.arange(half, dtype=F32) / half)
    ang = pos.astype(F32)[:, None] * inv[None, :]
    cos, sin = jnp.cos(ang), jnp.sin(ang)
    cos = jnp.concatenate([cos, cos], axis=-1)
    sin = jnp.concatenate([-sin, sin], axis=-1)
    return jnp.tile(cos, (1, LANES // A_HD)), jnp.tile(sin, (1, LANES // A_HD))


def _lane_row(vals, offset):
    return jnp.zeros((1, LANES), F32).at[0, offset:offset + vals.shape[0]].set(vals.astype(F32))


def kernel(x_prompt, x_sample, mem_prompt, state_win_k, state_win_v, state_conv, state_rec,
           cache_mem_k, cache_mem_v, ln_mix_pre, w_in, attn_sink, gdn_conv_w, gdn_a_log,
           gdn_dt_bias, gdn_norm_w, ln_mem, w_mem_kv, w_branch, w_out, ln_mix_post,
           ln_ffn_pre, w_ffn_in, w_ffn_out, ln_ffn_post):
    bp, lp, _ = x_prompt.shape
    bs, ls, _ = x_sample.shape

    sizes = [512, 128, 128, B_CONV_CH, B_HEADS, B_HEADS, 512, 512, N_BRANCH * D_MODEL]
    o = np.cumsum([0] + sizes)
    hperm = np.concatenate([np.r_[j * A_HD:(j + 1) * A_HD, (j + 4) * A_HD:(j + 5) * A_HD] for j in range(4)])
    ws = (w_in[:, hperm].astype(BF16), w_in[:, o[1]:o[4]].astype(BF16), w_in[:, o[6]:o[9]].astype(BF16),
          jnp.pad(w_in[:, o[4]:o[6]], ((0, 0), (0, LANES - 2 * B_HEADS))).astype(BF16))
    wb = jnp.concatenate([w_branch[0:1][:, hperm], w_branch[1:]], axis=0).astype(BF16)
    wo = w_out.astype(BF16)
    wfi = w_ffn_in.astype(BF16)
    wfo = w_ffn_out.astype(BF16)
    wmem = w_mem_kv.astype(BF16)
    sink = attn_sink.astype(F32)[np.array([0, 4, 1, 5, 2, 6, 3, 7])]
    vec = lambda g: g.astype(F32).reshape(1, -1)
    alog = _lane_row(gdn_a_log, B_HEADS)
    dtb = _lane_row(gdn_dt_bias, B_HEADS)
    cw = gdn_conv_w.astype(F32)
    nw = vec(gdn_norm_w)

    def post(x, oa, ob, oc, tm):
        return _post(x, oa, ob, oc, vec(ln_mix_pre), ws[2], wb, wo, vec(ln_mix_post), vec(ln_ffn_pre),
                     wfi, wfo, vec(ln_ffn_post), tm)

    tp = bp * lp
    xp = x_prompt.reshape(tp, D_MODEL)
    cos_p, sin_p = _rope_tables(jnp.arange(lp, dtype=jnp.int32))
    qa, ka, va, z, qc, ba, qn, kn, vv, tail = _proj(xp, vec(ln_mix_pre), ws, cos_p, sin_p, 512, cw, lp)
    oa = _swa_prompt(sink, qa, ka, va, bp, lp, 512)
    b3 = lambda a: a.reshape(bp, lp, a.shape[-1])
    ob, rec_p = _gdn_prompt(b3(qn), b3(kn), b3(vv), b3(ba), b3(z), alog, dtb, nw)
    mk, mv = _memkv(mem_prompt.reshape(bp * N_MEM, D_MODEL), vec(ln_mem), wmem)
    oc = _memattn_prompt(qc, mk, mv, bp, lp, 512)
    y_p = post(xp, oa, ob.reshape(tp, 512), oc, 512).reshape(bp, lp, D_MODEL)
    wk_p = ka.reshape(bp, lp, LANES)[:, -WINDOW:].reshape(bp, WINDOW, A_KV, A_HD)
    wv_p = va.reshape(bp, lp, LANES)[:, -WINDOW:].reshape(bp, WINDOW, A_KV, A_HD)
    conv_p = tail[:, -(CONV_W - 1):]
    mem_k_p = mk.reshape(bp, N_MEM, C_HEADS, C_HD)
    mem_v_p = mv.reshape(bp, N_MEM, C_HEADS, C_HD)

    ts = bs * DEC_PAD
    xs = jnp.pad(x_sample, ((0, 0), (0, DEC_PAD - ls), (0, 0))).reshape(ts, D_MODEL)
    cos_s, sin_s = _rope_tables(PAST_LEN + jnp.arange(DEC_PAD, dtype=jnp.int32))
    cos_s, sin_s = jnp.tile(cos_s, (bs, 1)), jnp.tile(sin_s, (bs, 1))
    qa, ka, va, z, qc, ba, qkv = _proj(xs, vec(ln_mix_pre), ws, cos_s, sin_s, 512)
    r3 = lambda a: a.reshape(bs, DEC_PAD, a.shape[-1])
    oa, wk_s, wv_s = _swa_decode(sink, r3(qa), r3(ka), r3(va), state_win_k.reshape(bs, WINDOW, LANES),
                                 state_win_v.reshape(bs, WINDOW, LANES), 16, ls)
    oa = oa.reshape(ts, 512)
    wk_s = wk_s.reshape(state_win_k.shape)
    wv_s = wv_s.reshape(state_win_v.shape)
    histp = jnp.pad(state_conv, ((0, 0), (DEC_PAD - (CONV_W - 1), 0), (0, 0)))
    ob, rec_s = _gdn_decode(r3(qkv), histp, ba, z, state_rec, cw, alog, dtb, nw,
                            LANES // DEC_PAD, ls)
    oc = _memattn_decode(r3(qc), cache_mem_k.reshape(bs, N_MEM * C_HEADS, C_HD),
                         cache_mem_v.reshape(bs, N_MEM * C_HEADS, C_HD), 8).reshape(ts, 512)
    real = lambda a: a.reshape(bs, DEC_PAD, 512)[:, :ls].reshape(bs * ls, 512)
    y_s = post(x_sample.reshape(bs * ls, D_MODEL), real(oa), real(ob), real(oc), 512).reshape(bs, ls, D_MODEL)
    conv_s = r3(qkv)[:, ls - (CONV_W - 1):ls]

    return (y_p, y_s, wk_p, wv_p, conv_p, rec_p, mem_k_p, mem_v_p, wk_s, wv_s, conv_s, rec_s)
```

```python
import functools

import numpy as np
import jax
import jax.numpy as jnp
from jax import lax
from jax.experimental import pallas as pl
from jax.experimental.pallas import tpu as pltpu

F32 = jnp.float32
BF16 = jnp.bfloat16

D_MODEL = 1024
PAST_LEN = 16384
EPS = 1e-6
ROPE_THETA = 10000.0
N_MEM = 256
WINDOW = 128
A_HD = 64
A_HEADS = 8
A_KV = 2
A_SCALE = A_HD ** -0.5
B_HEADS = 4
B_DK = 128
B_DV = 128
CONV_W = 4
GDN_CHUNK = 64
B_CONV_CH = B_HEADS * (2 * B_DK + B_DV)
C_HEADS = 4
C_HD = 128
N_BRANCH = 3
BRANCH_W = 512
D_FF = 2816

LANES = 128
SUBLANES = 8
PAIR = 2 * LANES
VMEM_LIMIT = 56 * 1024 * 1024
NEG = -1e30
DEC_PAD = SUBLANES


def _cparams(sem, vmem=VMEM_LIMIT):
    return pltpu.CompilerParams(dimension_semantics=sem, vmem_limit_bytes=vmem)


def _const_spec(shape):
    nd = len(shape)
    return pl.BlockSpec(shape, lambda *_: (0,) * nd, pipeline_mode=pl.Buffered(1))


def _rms(x, g):
    ms = jnp.mean(x * x, axis=-1, keepdims=True)
    return x * lax.rsqrt(ms + EPS) * g


def _bdot(a, b):
    return jnp.dot(a.astype(BF16), b.astype(BF16), preferred_element_type=F32)


def _bdot_nt(a, b):
    return lax.dot_general(a.astype(BF16), b.astype(BF16), (((1,), (1,)), ((), ())),
                           preferred_element_type=F32)


def _bdot_tn(a, b):
    return lax.dot_general(a.astype(BF16), b.astype(BF16), (((0,), (0,)), ((), ())),
                           preferred_element_type=F32)


def _hdot(a, b):
    return jnp.dot(a, b, precision=lax.Precision.HIGHEST, preferred_element_type=F32)


def _silu(x):
    return x * jax.nn.sigmoid(x)


def _softplus(x):
    return jnp.maximum(x, 0.0) + jnp.log1p(jnp.exp(-jnp.abs(x)))


def _rope128(v, cos, sin):
    lane = lax.broadcasted_iota(jnp.int32, v.shape, 1)
    fwd = pltpu.roll(v, 32, 1)
    bwd = pltpu.roll(v, 96, 1)
    sw = jnp.where((lane & 32) == 0, bwd, fwd)
    return v * cos + sw * sin


def _l2n(x):
    return x * lax.rsqrt(jnp.sum(x * x, axis=-1, keepdims=True) + EPS)


def _proj_steps(x_ref, g_ref, wq_ref, wr_ref, wzc_ref, wba_ref, cos_ref, sin_ref,
                qa_ref, ka_ref, va_ref, z_ref, qc_ref, ba_ref):
    h = _rms(x_ref[...], g_ref[...]).astype(BF16)
    cos = cos_ref[...]
    sin = sin_ref[...]

    def mm(w_ref, a, b):
        return jnp.dot(h, w_ref[:, a:b], preferred_element_type=F32)

    def qa_half(c0):
        q = mm(wq_ref, c0 * LANES, (c0 + 2) * LANES)
        for c in range(2):
            qa_ref[:, (c0 + c) * LANES:(c0 + c + 1) * LANES] = (
                _rope128(q[:, c * LANES:(c + 1) * LANES], cos, sin) * A_SCALE).astype(BF16)

    def kv():
        kv2 = mm(wr_ref, 0, PAIR)
        ka_ref[...] = _rope128(kv2[:, :LANES], cos, sin)
        va_ref[...] = kv2[:, LANES:]

    def z_half(c0):
        z_ref[:, c0:c0 + PAIR] = mm(wzc_ref, c0, c0 + PAIR)

    def qc_half(c0):
        qc_ref[:, c0:c0 + PAIR] = mm(wzc_ref, 512 + c0, 512 + c0 + PAIR).astype(BF16)

    def ba():
        ba_ref[...] = mm(wba_ref, 0, LANES)

    def qkv(a, b):
        return mm(wr_ref, PAIR + a, PAIR + b)

    steps = [lambda: qa_half(0), lambda: qa_half(2), kv, lambda: z_half(0), lambda: z_half(PAIR),
             lambda: qc_half(0), lambda: qc_half(PAIR), ba]
    return qkv, steps


def _proj_raw_kernel(x_ref, g_ref, wq_ref, wr_ref, wzc_ref, wba_ref, cos_ref, sin_ref,
                     qa_ref, ka_ref, va_ref, z_ref, qc_ref, ba_ref, qkv_ref):
    qkv, steps = _proj_steps(x_ref, g_ref, wq_ref, wr_ref, wzc_ref, wba_ref, cos_ref, sin_ref,
                             qa_ref, ka_ref, va_ref, z_ref, qc_ref, ba_ref)
    qkv_ref[...] = qkv(0, B_CONV_CH)
    for step in steps:
        step()


def _proj_conv_kernel(x_ref, g_ref, wq_ref, wr_ref, wzc_ref, wba_ref, cos_ref, sin_ref, cw_ref,
                      qa_ref, ka_ref, va_ref, z_ref, qc_ref, ba_ref, qn_ref, kn_ref, vv_ref, tail_ref,
                      buf_ref, *, tm, tiles_per_seq):
    hist = SUBLANES
    first = lax.rem(pl.program_id(0), tiles_per_seq) == 0

    @pl.when(first)
    def _():
        buf_ref[0:hist, :] = jnp.zeros((hist, B_CONV_CH), F32)

    @pl.when(jnp.logical_not(first))
    def _():
        buf_ref[0:hist, :] = buf_ref[tm:tm + hist, :]

    qkv, steps = _proj_steps(x_ref, g_ref, wq_ref, wr_ref, wzc_ref, wba_ref, cos_ref, sin_ref,
                             qa_ref, ka_ref, va_ref, z_ref, qc_ref, ba_ref)
    nq = B_HEADS * B_DK

    def conv_group(c0):
        cs = slice(c0, c0 + PAIR)
        raw = qkv(c0, c0 + PAIR)
        buf_ref[hist:hist + tm, cs] = raw
        tail_ref[0, :, cs] = raw[tm - hist:, :]
        xb = buf_ref[:, cs]
        acc = xb * cw_ref[0:1, cs]
        for i in range(1, CONV_W):
            acc = pltpu.roll(acc, 1, 0) + xb * cw_ref[i:i + 1, cs]
        act = _silu(acc[hist:])
        if c0 >= 2 * nq:
            vv_ref[:, c0 - 2 * nq:c0 - 2 * nq + PAIR] = act
            return
        out_ref, base, scale = (qn_ref, 0, B_DK ** -0.5) if c0 < nq else (kn_ref, nq, 1.0)
        for s in range(2):
            o0 = c0 - base + s * B_DK
            out_ref[:, o0:o0 + B_DK] = _l2n(act[:, s * B_DK:(s + 1) * B_DK]) * scale

    groups = [functools.partial(conv_group, c0) for c0 in range(0, B_CONV_CH, PAIR)]
    while groups or steps:
        if groups:
            groups.pop(0)()
        if steps:
            steps.pop(0)()


_PROJ_OUTS = [(512, BF16), (128, F32), (128, F32), (512, F32), (512, BF16), (128, F32)]


def _proj(x, gain, ws, cos, sin, tm, cw=None, seq=None):
    wq, wr, wtail, wba = ws
    t = x.shape[0]
    ntab = cos.shape[0] // tm
    row = lambda n: pl.BlockSpec((tm, n), lambda i: (i, 0))
    tab = pl.BlockSpec((tm, LANES), lambda i: (i % ntab, 0))
    in_specs = [row(D_MODEL), _const_spec((1, D_MODEL)), _const_spec(wq.shape), _const_spec(wr.shape),
                _const_spec((D_MODEL, D_MODEL)), _const_spec(wba.shape), tab, tab]
    out_specs = [row(n) for n, _ in _PROJ_OUTS]
    out_shape = [jax.ShapeDtypeStruct((t, n), d) for n, d in _PROJ_OUTS]
    if cw is None:
        return pl.pallas_call(
            _proj_raw_kernel, grid=(t // tm,), in_specs=in_specs,
            out_specs=out_specs + [row(B_CONV_CH)],
            out_shape=out_shape + [jax.ShapeDtypeStruct((t, B_CONV_CH), F32)],
            compiler_params=_cparams(("arbitrary",)), name="proj",
        )(x, gain, wq, wr, wtail, wba, cos, sin)
    tiles = seq // tm
    return pl.pallas_call(
        functools.partial(_proj_conv_kernel, tm=tm, tiles_per_seq=tiles),
        grid=(t // tm,),
        in_specs=in_specs + [_const_spec((CONV_W, B_CONV_CH))],
        out_specs=out_specs + [row(512)] * 3
        + [pl.BlockSpec((1, SUBLANES, B_CONV_CH), lambda i: (i // tiles, 0, 0))],
        out_shape=out_shape + [jax.ShapeDtypeStruct((t, 512), F32)] * 3
        + [jax.ShapeDtypeStruct((t // seq, SUBLANES, B_CONV_CH), F32)],
        scratch_shapes=[pltpu.VMEM((tm + SUBLANES, B_CONV_CH), F32)],
        compiler_params=_cparams(("arbitrary",)), name="proj_conv",
    )(x, gain, wq, wr, wtail, wba, cos, sin, cw)


def _swa_scores(q, k16):
    tq = q.shape[0]
    lo = lax.broadcasted_iota(jnp.int32, (tq, LANES), 1) < A_HD
    blocks = []
    for j in range(4):
        c = q[:, j * LANES:(j + 1) * LANES].astype(F32)
        blocks.append(jnp.where(lo, c, 0.0))
        blocks.append(jnp.where(lo, 0.0, c))
    lhs = jnp.concatenate(blocks, axis=0).astype(BF16)
    return lax.dot_general(lhs, k16, (((1,), (1,)), ((), ())), preferred_element_type=F32)


def _swa_finish(logits, v16, mask, sink_ref):
    tq = logits.shape[0] // A_HEADS
    lo = lax.broadcasted_iota(jnp.int32, (tq, LANES), 1) < A_HD
    es, inv = [], []
    for s in range(8):
        l = jnp.where(mask, logits[s * tq:(s + 1) * tq], NEG)
        sk = sink_ref[s]
        m = jnp.maximum(jnp.max(l, axis=-1, keepdims=True), sk)
        e = jnp.exp(l - m)
        den = jnp.sum(e, axis=-1, keepdims=True) + jnp.exp(sk - m)
        es.append(e.astype(BF16))
        inv.append(1.0 / den)
    pv = jnp.dot(jnp.concatenate(es, axis=0), v16, preferred_element_type=F32)
    outs = []
    for j in range(4):
        a = pv[(2 * j) * tq:(2 * j + 1) * tq] * inv[2 * j]
        b = pv[(2 * j + 1) * tq:(2 * j + 2) * tq] * inv[2 * j + 1]
        outs.append(jnp.where(lo, a, b))
    return jnp.concatenate(outs, axis=1)


def _swa_prompt_kernel(sink_ref, q_ref, kc_ref, kp_ref, vc_ref, vp_ref, o_ref, *, nblk):
    i = pl.program_id(1)
    kcat = jnp.concatenate([kp_ref[...], kc_ref[...]], axis=0).astype(BF16)
    vcat = jnp.concatenate([vp_ref[...], vc_ref[...]], axis=0).astype(BF16)
    ii = lax.broadcasted_iota(jnp.int32, (WINDOW, 2 * WINDOW), 0)
    jj = lax.broadcasted_iota(jnp.int32, (WINDOW, 2 * WINDOW), 1)
    band = (jj > ii) & (jj <= ii + WINDOW)
    for jb in range(nblk):
        mask = band
        if jb == 0:
            mask = band & ((jj >= WINDOW) | (i > 0))
        logits = _swa_scores(q_ref[jb * WINDOW:(jb + 1) * WINDOW, :], kcat[jb * WINDOW:(jb + 2) * WINDOW])
        o = _swa_finish(logits, vcat[jb * WINDOW:(jb + 2) * WINDOW], mask, sink_ref)
        o_ref[jb * WINDOW:(jb + 1) * WINDOW, :] = o.astype(BF16)


def _swa_prompt(sink, qa, ka, va, batch, seq, tq):
    nq = seq // tq
    nblk = tq // WINDOW
    nw = seq // WINDOW
    cur = lambda n: pl.BlockSpec((tq, n), lambda b, i: (b * nq + i, 0))
    prev = pl.BlockSpec((WINDOW, LANES), lambda b, i: (jnp.maximum(b * nw + i * nblk - 1, 0), 0))
    return pl.pallas_call(
        functools.partial(_swa_prompt_kernel, nblk=nblk),
        grid=(batch, nq),
        in_specs=[pl.BlockSpec(memory_space=pltpu.SMEM), cur(512), cur(LANES), prev, cur(LANES), prev],
        out_specs=cur(512),
        out_shape=jax.ShapeDtypeStruct(qa.shape, BF16),
        compiler_params=_cparams(("arbitrary", "arbitrary")),
        name="swa_prompt",
    )(sink, qa, ka, ka, va, va)


ROW_UNROLL = 4


def _rows_loop(rows, load, stages, store):
    def body(g, carry):
        idx = [g * ROW_UNROLL + j for j in range(ROW_UNROLL)]
        vals = [load(r) for r in idx]
        for stage in stages:
            vals = [stage(v) for v in vals]
        for r, o in zip(idx, vals):
            store(r, o)
        return carry

    lax.fori_loop(0, rows // ROW_UNROLL, body, 0)


def _swa_decode_kernel(sink_ref, q_ref, kn_ref, vn_ref, wk_ref, wv_ref, o_ref, wko_ref, wvo_ref,
                       *, rows, new_len):
    ii = lax.broadcasted_iota(jnp.int32, (DEC_PAD, 2 * WINDOW), 0)
    jj = lax.broadcasted_iota(jnp.int32, (DEC_PAD, 2 * WINDOW), 1)
    mask = ((jj < WINDOW) & (jj > ii)) | ((jj >= WINDOW) & (jj - WINDOW <= ii))
    pad = jnp.zeros((WINDOW - DEC_PAD, LANES), F32)
    keep = lax.broadcasted_iota(jnp.int32, (WINDOW, LANES), 0) < WINDOW - new_len

    def slide(win, new):
        tail = jnp.concatenate([pad, new], axis=0)
        return jnp.where(keep, pltpu.roll(win, WINDOW - new_len, 0), pltpu.roll(tail, DEC_PAD - new_len, 0))

    def load(r):
        return q_ref[r], wk_ref[r], kn_ref[r], wv_ref[r], vn_ref[r]

    def scores(args):
        q, wk, kn, wv, vn = args
        kk = jnp.concatenate([wk, kn, pad], axis=0).astype(BF16)
        return _swa_scores(q, kk), wk, kn, wv, vn

    def finish(args):
        logits, wk, kn, wv, vn = args
        vv = jnp.concatenate([wv, vn, pad], axis=0).astype(BF16)
        return _swa_finish(logits, vv, mask, sink_ref).astype(BF16), slide(wk, kn), slide(wv, vn)

    def store(r, outs):
        o_ref[r], wko_ref[r], wvo_ref[r] = outs

    _rows_loop(rows, load, [scores, finish], store)


def _swa_decode(sink, qa, kn, vn, wk, wv, rows, new_len):
    nb = qa.shape[0]
    blk = lambda a, n: pl.BlockSpec((rows, a, n), lambda i: (i, 0, 0))
    return pl.pallas_call(
        functools.partial(_swa_decode_kernel, rows=rows, new_len=new_len),
        grid=(nb // rows,),
        in_specs=[pl.BlockSpec(memory_space=pltpu.SMEM), blk(DEC_PAD, 512), blk(DEC_PAD, LANES),
                  blk(DEC_PAD, LANES), blk(WINDOW, LANES), blk(WINDOW, LANES)],
        out_specs=[blk(DEC_PAD, 512), blk(WINDOW, LANES), blk(WINDOW, LANES)],
        out_shape=[jax.ShapeDtypeStruct(qa.shape, BF16), jax.ShapeDtypeStruct(wk.shape, F32),
                   jax.ShapeDtypeStruct(wv.shape, F32)],
        compiler_params=_cparams(("arbitrary",)),
        name="swa_decode",
    )(sink, qa, kn, vn, wk, wv)


def _gdn_gates(ba, alog, dtb, valid):
    beta = jax.nn.sigmoid(ba)
    g = -jnp.exp(alog) * _softplus(ba + dtb)
    if valid is not None:
        beta = jnp.where(valid, beta, 0.0)
        g = jnp.where(valid, g, 0.0)
    return beta, g


def _chunk_masks(chunk):
    sh = chunk.bit_length() - 1
    ri = lax.broadcasted_iota(jnp.int32, (LANES, LANES), 0)
    ci = lax.broadcasted_iota(jnp.int32, (LANES, LANES), 1)
    same = (ri >> sh) == (ci >> sh)
    return same, same & (ri >= ci), same & (ri > ci), ri == ci


def _gdn_cumsums(g_all, chunk):
    same, tri, _, _ = _chunk_masks(chunk)
    lower = jnp.where(tri, 1.0, 0.0)
    gcol = _hdot(lower, g_all)
    grow = _hdot(g_all.T, lower.T)
    gtot = _hdot(jnp.where(same, 1.0, 0.0), g_all)
    return gcol, grow, gtot


def _gdn_phase_a(q, k, v, beta, gcol, grow, gtot, chunk):
    _, tri, strict, _ = _chunk_masks(chunk)
    decay = jnp.exp(jnp.where(tri, gcol - grow, NEG))
    kb = k * beta
    kkqk = _bdot_nt(jnp.concatenate([kb, q], axis=0), k)
    a = jnp.where(strict, kkqk[:LANES] * decay, 0.0)
    qk = kkqk[LANES:] * decay
    n = -a
    apow = a
    for _ in range(chunk.bit_length() - 2):
        apow = _bdot(apow, apow)
        n = n + apow + _bdot(n, apow)
    rhs = jnp.concatenate([v * beta, kb * jnp.exp(gcol)], axis=1)
    sol = rhs + _bdot(n, rhs)
    u, w = sol[:, :B_DV], sol[:, B_DV:]
    return u, w, q * jnp.exp(gcol), k * jnp.exp(gtot - gcol), qk, jnp.exp(gtot)


def _gdn_qkv_heads(qkv, h):
    q = qkv[:, h * B_DK:(h + 1) * B_DK]
    k = qkv[:, (B_HEADS + h) * B_DK:(B_HEADS + h + 1) * B_DK]
    v = qkv[:, (2 * B_HEADS + h) * B_DK:(2 * B_HEADS + h + 1) * B_DK]
    q = q * lax.rsqrt(jnp.sum(q * q, axis=-1, keepdims=True) + EPS) * (B_DK ** -0.5)
    k = k * lax.rsqrt(jnp.sum(k * k, axis=-1, keepdims=True) + EPS)
    return q, k, v


def _gdn_out(o, z, nw):
    return _rms(o, nw) * _silu(z)


def _pair_bd(x2):
    lo = lax.broadcasted_iota(jnp.int32, x2.shape, 1) < LANES
    return jnp.concatenate([jnp.where(lo, x2, 0.0), jnp.where(lo, 0.0, x2)], axis=0).astype(BF16)


def _quad_bd(x4):
    blk = lax.broadcasted_iota(jnp.int32, x4.shape, 1) >> (GDN_CHUNK.bit_length() - 1)
    return jnp.concatenate([jnp.where(blk == j, x4, 0.0) for j in range(PAIR // GDN_CHUNK)],
                           axis=0).astype(BF16)


def _pair_cols(x, l0, l1):
    lo = lax.broadcasted_iota(jnp.int32, (x.shape[0], PAIR), 1) < LANES
    return jnp.where(lo, x[:, l0:l0 + 1], x[:, l1:l1 + 1])


def _gdn_prompt_kernel(q_ref, k_ref, v_ref, ba_ref, z_ref, alog_ref, dtb_ref, nw_ref,
                       ob_ref, s_ref, sbd_ref, *, nb):
    t = pl.program_id(0)
    npair = B_HEADS // 2

    @pl.when(t == 0)
    def _():
        sbd_ref[...] = jnp.zeros(sbd_ref.shape, F32)

    alog = alog_ref[...]
    dtb = dtb_ref[...]
    nw = nw_ref[...]
    nchunk = LANES // GDN_CHUNK
    nblk = nb
    _, tri, strict, _ = _chunk_masks(GDN_CHUNK)
    tri2 = jnp.concatenate([tri, tri], axis=1)
    strict2 = jnp.concatenate([strict, strict], axis=1)

    items = []
    for blk in range(nblk):
        beta_all, g_all = _gdn_gates(ba_ref[blk], alog, dtb, None)
        gcol_all, grow_all, gtot_all = _gdn_cumsums(g_all, GDN_CHUNK)
        eg_all = jnp.exp(gcol_all)
        ed_all = jnp.exp(gtot_all - gcol_all)
        et_all = jnp.exp(gtot_all)
        for p in range(npair):
            h0, h1 = 2 * p, 2 * p + 1
            g0, g1 = B_HEADS + h0, B_HEADS + h1
            psl = slice(p * PAIR, (p + 1) * PAIR)
            q2 = q_ref[blk, :, psl]
            k2 = k_ref[blk, :, psl]
            v2 = v_ref[blk, :, psl]
            beta2 = _pair_cols(beta_all, h0, h1)
            grow2 = jnp.concatenate([grow_all[g0:g0 + 1, :], grow_all[g1:g1 + 1, :]], axis=1)
            decay2 = jnp.exp(jnp.where(tri2, _pair_cols(gcol_all, g0, g1) - grow2, NEG))
            kb2 = k2 * beta2
            items.append(dict(
                blk=blk, p=p, decay2=decay2, kb2=kb2, v2b=v2 * beta2,
                kbe2=kb2 * _pair_cols(eg_all, g0, g1),
                lhs=jnp.concatenate([kb2, q2], axis=0).astype(BF16),
                kbd=_pair_bd(k2),
                qe2=(q2 * _pair_cols(eg_all, g0, g1)).astype(BF16),
                kd2=(k2 * _pair_cols(ed_all, g0, g1)).astype(BF16),
                et2=_pair_cols(et_all, g0, g1)))

    for it in items:
        kkqk = lax.dot_general(it["lhs"], it["kbd"], (((1,), (1,)), ((), ())),
                               preferred_element_type=F32)
        it["a2"] = jnp.where(strict2, kkqk[:LANES] * it["decay2"], 0.0)
        it["qk2"] = (kkqk[LANES:] * it["decay2"]).astype(BF16)
    top = (lax.broadcasted_iota(jnp.int32, (GDN_CHUNK, PAIR), 1) & GDN_CHUNK) == 0
    for it in items:
        a64 = it["a2"][:GDN_CHUNK] + it["a2"][GDN_CHUNK:]
        it["pw"] = jnp.dot(a64.astype(BF16), _quad_bd(a64), preferred_element_type=F32)
        it["n"] = -a64
    nlev = GDN_CHUNK.bit_length() - 2
    for lev in range(nlev):
        for it in items:
            pw, n = it["pw"], it["n"]
            bd = _quad_bd(pw)
            if lev < nlev - 1:
                r = jnp.dot(jnp.concatenate([pw, n], axis=0).astype(BF16), bd, preferred_element_type=F32)
                it["pw"], npw = r[:GDN_CHUNK], r[GDN_CHUNK:]
            else:
                npw = jnp.dot(n.astype(BF16), bd, preferred_element_type=F32)
            it["n"] = n + pw + npw
    for it in items:
        n64 = it["n"]
        it["n"] = jnp.concatenate([jnp.where(top, n64, 0.0), jnp.where(top, 0.0, n64)], axis=0)
    for it in items:
        us, ws = [], []
        for s in range(2):
            sl = slice(s * LANES, (s + 1) * LANES)
            rhs = jnp.concatenate([it["v2b"][:, sl], it["kbe2"][:, sl]], axis=1)
            sol = rhs + _bdot(it["n"][:, sl], rhs)
            us.append(sol[:, :B_DV])
            ws.append(sol[:, B_DV:])
        it["u2"] = jnp.concatenate(us, axis=1)
        it["w2"] = jnp.concatenate(ws, axis=1).astype(BF16)

    ri = lax.broadcasted_iota(jnp.int32, (PAIR, PAIR), 0) < LANES
    ci = lax.broadcasted_iota(jnp.int32, (PAIR, PAIR), 1) < LANES
    bdmask = ri == ci
    state = [sbd_ref[i] for i in range(len(items))]
    outs = {}
    for c in range(nchunk):
        rows = slice(c * GDN_CHUNK, (c + 1) * GDN_CHUNK)
        rs = [jnp.dot(jnp.concatenate([it["w2"][rows], it["qe2"][rows]], axis=0), state[i].astype(BF16),
                      preferred_element_type=F32) for i, it in enumerate(items)]
        for i, it in enumerate(items):
            s = state[i]
            r = rs[i]
            vn2 = it["u2"][rows] - r[:GDN_CHUNK]
            vt = _pair_bd(jnp.concatenate([vn2] * nchunk, axis=0))
            outs[(i, c)] = r[GDN_CHUNK:] + jnp.dot(it["qk2"][rows], vt, preferred_element_type=F32)
            upd = lax.dot_general(it["kd2"][rows], vn2.astype(BF16), (((0,), (0,)), ((), ())),
                                  preferred_element_type=F32)
            state[i] = s * it["et2"][c * GDN_CHUNK:c * GDN_CHUNK + 1, :] + jnp.where(bdmask, upd, 0.0)
    for i, it in enumerate(items):
        sbd_ref[i] = state[i]

    @pl.when(t == pl.num_programs(0) - 1)
    def _():
        for i, it in enumerate(items):
            s_ref[it["blk"], 2 * it["p"]] = state[i][:LANES, :LANES]
            s_ref[it["blk"], 2 * it["p"] + 1] = state[i][LANES:, LANES:]

    for i, it in enumerate(items):
        o2 = jnp.concatenate([outs[(i, c)] for c in range(nchunk)], axis=0)
        for s in range(2):
            h = 2 * it["p"] + s
            sl = slice(h * B_DV, (h + 1) * B_DV)
            o = o2[:, s * LANES:(s + 1) * LANES]
            ob_ref[it["blk"], :, sl] = _gdn_out(o, z_ref[it["blk"], :, sl], nw).astype(BF16)


def _gdn_prompt(q, k, v, ba, z, alog, dtb, nw):
    batch, seq, _ = q.shape
    tok = lambda n: pl.BlockSpec((batch, LANES, n), lambda t: (0, t, 0))
    state = pl.BlockSpec((batch, B_HEADS, B_DK, B_DV), lambda t: (0, 0, 0, 0))
    return pl.pallas_call(
        functools.partial(_gdn_prompt_kernel, nb=batch),
        grid=(seq // LANES,),
        in_specs=[tok(512), tok(512), tok(512), tok(LANES), tok(512),
                  _const_spec((1, LANES)), _const_spec((1, LANES)), _const_spec((1, B_DV))],
        out_specs=[tok(512), state],
        out_shape=[jax.ShapeDtypeStruct((batch, seq, 512), BF16),
                   jax.ShapeDtypeStruct((batch, B_HEADS, B_DK, B_DV), F32)],
        scratch_shapes=[pltpu.VMEM((batch * (B_HEADS // 2), PAIR, PAIR), F32)],
        compiler_params=_cparams(("arbitrary",)),
        name="gdn_prompt",
    )(q, k, v, ba, z, alog, dtb, nw)


def _gdn_decode_kernel(raw_ref, hist_ref, ba_ref, z_ref, rec_ref, cw_ref, alog_ref, dtb_ref, nw_ref,
                       ob_ref, s_ref, buf_ref, u_s, w_s, qe_s, kd_s, qk_s, eg_s, o_s, *, rows, valid_len):
    buf_ref[:, 0:DEC_PAD, :] = hist_ref[...]
    buf_ref[:, DEC_PAD:2 * DEC_PAD, :] = raw_ref[...]
    conv = None
    for i in range(CONV_W):
        off = DEC_PAD - (CONV_W - 1) + i
        term = buf_ref[:, off:off + DEC_PAD, :] * cw_ref[i:i + 1, :]
        conv = term if conv is None else conv + term
    qkv = _silu(conv.reshape(rows * DEC_PAD, B_CONV_CH))
    tok = lax.broadcasted_iota(jnp.int32, (LANES, LANES), 0) & (DEC_PAD - 1)
    beta_all, g_all = _gdn_gates(ba_ref[...], alog_ref[...], dtb_ref[...], tok < valid_len)
    gcol_all, grow_all, gtot_all = _gdn_cumsums(g_all, DEC_PAD)
    for h in range(B_HEADS):
        q, k, v = _gdn_qkv_heads(qkv, h)
        gl = B_HEADS + h
        u, w, qe, kd, qk, egt = _gdn_phase_a(
            q, k, v, beta_all[:, h:h + 1], gcol_all[:, gl:gl + 1], grow_all[gl:gl + 1, :],
            gtot_all[:, gl:gl + 1], DEC_PAD)
        u_s[h] = u
        w_s[h] = w
        qe_s[h] = qe
        kd_s[h] = kd
        qk_s[h] = qk
        eg_s[h] = jnp.broadcast_to(egt, (LANES, LANES))

    def load(r):
        r0 = pl.multiple_of(r * DEC_PAD, DEC_PAD)
        rr = pl.ds(r0, DEC_PAD)
        return [(rec_ref[r, h], w_s[h, rr, :], qe_s[h, rr, :], u_s[h, rr, :], qk_s[h, rr, :],
                 kd_s[h, rr, :], eg_s[h, pl.ds(r0, 1), :]) for h in range(B_HEADS)]

    def read_state(heads):
        return [(_bdot(jnp.concatenate([w, qe], axis=0), s), s, u, qk, kd, eg)
                for s, w, qe, u, qk, kd, eg in heads]

    def update(heads):
        outs = []
        for res, s, u, qk, kd, eg in heads:
            v_new = u - res[:DEC_PAD]
            vt = jnp.concatenate([v_new] * (LANES // DEC_PAD), axis=0)
            outs.append((res[DEC_PAD:] + _bdot(qk, vt), s * eg + _bdot_tn(kd, v_new)))
        return outs

    def store(r, outs):
        rr = pl.ds(pl.multiple_of(r * DEC_PAD, DEC_PAD), DEC_PAD)
        for h, (o, s_new) in enumerate(outs):
            o_s[h, rr, :] = o
            s_ref[r, h] = s_new

    _rows_loop(rows, load, [read_state, update], store)
    nw = nw_ref[...]
    for h in range(B_HEADS):
        sl = slice(h * B_DV, (h + 1) * B_DV)
        ob_ref[:, sl] = _gdn_out(o_s[h], z_ref[:, sl], nw).astype(BF16)


def _gdn_decode(raw, histp, ba, z, rec, cw, alog, dtb, nw, rows, valid_len):
    nb = raw.shape[0]
    flat = rows * DEC_PAD
    assert flat == LANES
    sq = lambda: pltpu.VMEM((B_HEADS, LANES, LANES), F32)
    return pl.pallas_call(
        functools.partial(_gdn_decode_kernel, rows=rows, valid_len=valid_len),
        grid=(nb // rows,),
        in_specs=[pl.BlockSpec((rows, DEC_PAD, B_CONV_CH), lambda i: (i, 0, 0)),
                  pl.BlockSpec((rows, DEC_PAD, B_CONV_CH), lambda i: (i, 0, 0)),
                  pl.BlockSpec((flat, LANES), lambda i: (i, 0)),
                  pl.BlockSpec((flat, 512), lambda i: (i, 0)),
                  pl.BlockSpec((rows, B_HEADS, B_DK, B_DV), lambda i: (i, 0, 0, 0)),
                  _const_spec((CONV_W, B_CONV_CH)), _const_spec((1, LANES)), _const_spec((1, LANES)),
                  _const_spec((1, B_DV))],
        out_specs=[pl.BlockSpec((flat, 512), lambda i: (i, 0)),
                   pl.BlockSpec((rows, B_HEADS, B_DK, B_DV), lambda i: (i, 0, 0, 0))],
        out_shape=[jax.ShapeDtypeStruct((nb * DEC_PAD, 512), BF16),
                   jax.ShapeDtypeStruct(rec.shape, F32)],
        scratch_shapes=[pltpu.VMEM((rows, 2 * DEC_PAD, B_CONV_CH), F32)] + [sq() for _ in range(7)],
        compiler_params=_cparams(("arbitrary",)),
        name="gdn_decode",
    )(raw, histp, ba, z, rec, cw, alog, dtb, nw)


def _memkv_kernel(m_ref, g_ref, w_ref, k_ref, v_ref):
    h = _rms(m_ref[...], g_ref[...]).astype(BF16)
    n = C_HEADS * C_HD
    k_ref[...] = jnp.dot(h, w_ref[:, :n], preferred_element_type=F32)
    v_ref[...] = jnp.dot(h, w_ref[:, n:], preferred_element_type=F32)


def _memkv(mem, gain, w):
    t = mem.shape[0]
    tm = 512
    n = C_HEADS * C_HD
    return pl.pallas_call(
        _memkv_kernel,
        grid=(t // tm,),
        in_specs=[pl.BlockSpec((tm, D_MODEL), lambda i: (i, 0)), _const_spec((1, D_MODEL)),
                  _const_spec((D_MODEL, 2 * n))],
        out_specs=[pl.BlockSpec((tm, n), lambda i: (i, 0))] * 2,
        out_shape=[jax.ShapeDtypeStruct((t, n), F32)] * 2,
        compiler_params=_cparams(("arbitrary",)),
        name="memkv",
    )(mem, gain, w)


def _softmax_rows(logits):
    m = jnp.max(logits, axis=-1, keepdims=True)
    e = jnp.exp(logits - m)
    return e, 1.0 / jnp.sum(e, axis=-1, keepdims=True)


def _memattn_prompt_kernel(q_ref, k_ref, v_ref, o_ref):
    hs = lambda h: slice(h * C_HD, (h + 1) * C_HD)
    scores = lambda h: _bdot_nt(q_ref[:, hs(h)], k_ref[:, hs(h)])
    logits = scores(0)
    for h in range(C_HEADS):
        nxt = scores(h + 1) if h + 1 < C_HEADS else None
        e, inv = _softmax_rows(logits * (C_HD ** -0.5))
        o_ref[:, hs(h)] = (_bdot(e, v_ref[:, hs(h)]) * inv).astype(BF16)
        logits = nxt


def _memattn_prompt(qc, mk, mv, batch, seq, tm):
    nq = seq // tm
    n = C_HEADS * C_HD
    cur = pl.BlockSpec((tm, n), lambda b, i: (b * nq + i, 0))
    mem = pl.BlockSpec((N_MEM, n), lambda b, i: (b, 0))
    return pl.pallas_call(
        _memattn_prompt_kernel,
        grid=(batch, nq),
        in_specs=[cur, mem, mem],
        out_specs=cur,
        out_shape=jax.ShapeDtypeStruct(qc.shape, BF16),
        compiler_params=_cparams(("arbitrary", "arbitrary")),
        name="memattn_prompt",
    )(qc, mk, mv)


def _memattn_decode_kernel(q_ref, k_ref, v_ref, o_ref, *, rows):
    nk = N_MEM * C_HEADS
    col = lax.broadcasted_iota(jnp.int32, (C_HEADS * DEC_PAD, nk), 1)
    row = lax.broadcasted_iota(jnp.int32, (C_HEADS * DEC_PAD, nk), 0)
    own = (col & (C_HEADS - 1)) == (row >> (DEC_PAD.bit_length() - 1))

    def load(r):
        return q_ref[r], k_ref[r], v_ref[r]

    def scores(args):
        q, k, v = args
        q = q.astype(F32)
        lhs = jnp.concatenate([q[:, h * C_HD:(h + 1) * C_HD] for h in range(C_HEADS)], axis=0)
        return _bdot_nt(lhs, k), v

    def finish(args):
        logits, v = args
        e, inv = _softmax_rows(jnp.where(own, logits * (C_HD ** -0.5), NEG))
        pv = _bdot(e, v) * inv
        return jnp.concatenate([pv[h * DEC_PAD:(h + 1) * DEC_PAD] for h in range(C_HEADS)],
                               axis=1).astype(BF16)

    def store(r, o):
        o_ref[r] = o

    _rows_loop(rows, load, [scores, finish], store)


def _memattn_decode(qc, ck, cv, rows):
    nb = qc.shape[0]
    n = C_HEADS * C_HD
    blk = pl.BlockSpec((rows, DEC_PAD, n), lambda i: (i, 0, 0))
    cache = pl.BlockSpec((rows, N_MEM * C_HEADS, C_HD), lambda i: (i, 0, 0))
    return pl.pallas_call(
        functools.partial(_memattn_decode_kernel, rows=rows),
        grid=(nb // rows,),
        in_specs=[blk, cache, cache],
        out_specs=blk,
        out_shape=jax.ShapeDtypeStruct(qc.shape, BF16),
        compiler_params=_cparams(("arbitrary",)),
        name="memattn_decode",
    )(qc, ck, cv)


def _merge_kernel(x_ref, oa_ref, ob_ref, oc_ref, gpre_ref, wga_ref, wgb_ref, wgc_ref, wb_ref, wo_ref,
                  gpost_ref, gfpre_ref, x1_ref, h2_ref):
    x = x_ref[...]
    ups = [jnp.dot(o_ref[...], wb_ref[n], preferred_element_type=F32)
           for n, o_ref in enumerate((oa_ref, ob_ref, oc_ref))]
    h = _rms(x, gpre_ref[...]).astype(BF16)
    mix = None
    for n, wg_ref in enumerate((wga_ref, wgb_ref, wgc_ref)):
        gate = jax.nn.sigmoid(jnp.dot(h, wg_ref[...], preferred_element_type=F32))
        mix = gate * ups[n] if mix is None else mix + gate * ups[n]
    x1 = x + _rms(_bdot(mix, wo_ref[...]), gpost_ref[...])
    x1_ref[...] = x1
    h2_ref[...] = _rms(x1, gfpre_ref[...]).astype(BF16)


FF_SPLIT = 6 * PAIR


def _ffn_kernel(x1_ref, h2_ref, wfi_ref, wfo_ref, gfpost_ref, y_ref):
    h2 = h2_ref[...]
    f = None
    halves = []
    for a, b in ((0, FF_SPLIT), (FF_SPLIT, D_FF)):
        gt = jnp.dot(h2, wfi_ref[:, a:b], preferred_element_type=F32)
        uf = jnp.dot(h2, wfi_ref[:, D_FF + a:D_FF + b], preferred_element_type=F32)
        halves.append((a, b, gt, uf))
    for a, b, gt, uf in halves:
        part = _bdot(_silu(gt) * uf, wfo_ref[a:b, :])
        f = part if f is None else f + part
    y_ref[...] = x1_ref[...] + _rms(f, gfpost_ref[...])


def _attn_merge_kernel(sink_ref, x_ref, qa_ref, kc_ref, kp_ref, vc_ref, vp_ref, qc_ref, mk_ref, mv_ref,
                       ob_ref, gpre_ref, wga_ref, wgb_ref, wgc_ref, wb_ref, wo_ref, gpost_ref, gfpre_ref,
                       x1_ref, h2_ref, *, nblk):
    i = pl.program_id(1)
    kcat = jnp.concatenate([kp_ref[...], kc_ref[...]], axis=0).astype(BF16)
    vcat = jnp.concatenate([vp_ref[...], vc_ref[...]], axis=0).astype(BF16)
    ii = lax.broadcasted_iota(jnp.int32, (WINDOW, 2 * WINDOW), 0)
    jj = lax.broadcasted_iota(jnp.int32, (WINDOW, 2 * WINDOW), 1)
    band = (jj > ii) & (jj <= ii + WINDOW)
    hs = lambda h: slice(h * C_HD, (h + 1) * C_HD)
    swa_logits = [_swa_scores(qa_ref[jb * WINDOW:(jb + 1) * WINDOW, :], kcat[jb * WINDOW:(jb + 2) * WINDOW])
                  for jb in range(nblk)]
    mem_logits = [_bdot_nt(qc_ref[:, hs(h)], mk_ref[:, hs(h)]) for h in range(C_HEADS)]
    x = x_ref[...]
    up_b = jnp.dot(ob_ref[...], wb_ref[1], preferred_element_type=F32)
    h = _rms(x, gpre_ref[...]).astype(BF16)
    gate = lambda wg_ref: jnp.dot(h, wg_ref[...], preferred_element_type=F32)
    gates = [gate(wga_ref)]
    oa = []
    for jb in range(nblk):
        mask = band
        if jb == 0:
            mask = band & ((jj >= WINDOW) | (i > 0))
        oa.append(_swa_finish(swa_logits[jb], vcat[jb * WINDOW:(jb + 2) * WINDOW], mask, sink_ref))
    oa = jnp.concatenate(oa, axis=0).astype(BF16)
    oc = []
    for hd in range(C_HEADS):
        e, inv = _softmax_rows(mem_logits[hd] * (C_HD ** -0.5))
        oc.append(_bdot(e, mv_ref[:, hs(hd)]) * inv)
    oc = jnp.concatenate(oc, axis=1).astype(BF16)
    gates += [gate(wgb_ref), gate(wgc_ref)]
    ups = [jnp.dot(oa, wb_ref[0], preferred_element_type=F32), up_b,
           jnp.dot(oc, wb_ref[2], preferred_element_type=F32)]
    mix = None
    for n in range(N_BRANCH):
        term = jax.nn.sigmoid(gates[n]) * ups[n]
        mix = term if mix is None else mix + term
    x1 = x + _rms(_bdot(mix, wo_ref[...]), gpost_ref[...])
    x1_ref[...] = x1
    h2_ref[...] = _rms(x1, gfpre_ref[...]).astype(BF16)


def _merge_weight_specs():
    vec = _const_spec((1, D_MODEL))
    return ([vec] + [pl.BlockSpec((D_MODEL, D_MODEL), functools.partial(lambda n, *_: (0, n), n + 1),
                                  pipeline_mode=pl.Buffered(1)) for n in range(N_BRANCH)]
            + [_const_spec((N_BRANCH, BRANCH_W, D_MODEL)), _const_spec((D_MODEL, D_MODEL)), vec, vec])


def _attn_merge(sink, x, qa, ka, va, qc, mk, mv, ob, gpre, wtail, wb, wo, gpost, gfpre, batch, seq, tm):
    nq = seq // tm
    nblk = tm // WINDOW
    nw = seq // WINDOW
    cur = lambda n: pl.BlockSpec((tm, n), lambda b, i: (b * nq + i, 0))
    prev = pl.BlockSpec((WINDOW, LANES), lambda b, i: (jnp.maximum(b * nw + i * nblk - 1, 0), 0))
    mem = pl.BlockSpec((N_MEM, C_HEADS * C_HD), lambda b, i: (b, 0))
    return pl.pallas_call(
        functools.partial(_attn_merge_kernel, nblk=nblk),
        grid=(batch, nq),
        in_specs=[pl.BlockSpec(memory_space=pltpu.SMEM), cur(D_MODEL), cur(512), cur(LANES), prev,
                  cur(LANES), prev, cur(512), mem, mem, cur(512)] + _merge_weight_specs(),
        out_specs=[cur(D_MODEL), cur(D_MODEL)],
        out_shape=[jax.ShapeDtypeStruct(x.shape, F32), jax.ShapeDtypeStruct(x.shape, BF16)],
        compiler_params=_cparams(("arbitrary", "arbitrary")),
        name="attn_merge",
    )(sink, x, qa, ka, ka, va, va, qc, mk, mv, ob, gpre, wtail, wtail, wtail, wb, wo, gpost, gfpre)


def _merge(x, oa, ob, oc, gpre, wtail, wb, wo, gpost, gfpre, tm):
    t = x.shape[0]
    row = lambda n: pl.BlockSpec((tm, n), lambda i: (i, 0))
    return pl.pallas_call(
        _merge_kernel,
        grid=(t // tm,),
        in_specs=[row(D_MODEL), row(512), row(512), row(512)] + _merge_weight_specs(),
        out_specs=[row(D_MODEL), row(D_MODEL)],
        out_shape=[jax.ShapeDtypeStruct(x.shape, F32), jax.ShapeDtypeStruct(x.shape, BF16)],
        compiler_params=_cparams(("arbitrary",)),
        name="merge",
    )(x, oa, ob, oc, gpre, wtail, wtail, wtail, wb, wo, gpost, gfpre)


def _ffn(x1, h2, wfi, wfo, gfpost, tm):
    t = x1.shape[0]
    vec = _const_spec((1, D_MODEL))
    frow = pl.BlockSpec((tm, D_MODEL), lambda i: (i, 0))
    return pl.pallas_call(
        _ffn_kernel,
        grid=(t // tm,),
        in_specs=[frow, frow, _const_spec((D_MODEL, 2 * D_FF)),
                  _const_spec((D_FF, D_MODEL)), vec],
        out_specs=frow,
        out_shape=jax.ShapeDtypeStruct(x1.shape, F32),
        compiler_params=_cparams(("arbitrary",)),
        name="ffn",
    )(x1, h2, wfi, wfo, gfpost)


def _rope_tables(pos):
    half = A_HD // 2
    inv = ROPE_THETA ** (-jnp.arange(half, dtype=F32) / half)
    ang = pos.astype(F32)[:, None] * inv[None, :]
    cos, sin = jnp.cos(ang), jnp.sin(ang)
    cos = jnp.concatenate([cos, cos], axis=-1)
    sin = jnp.concatenate([-sin, sin], axis=-1)
    return jnp.tile(cos, (1, LANES // A_HD)), jnp.tile(sin, (1, LANES // A_HD))


def _lane_row(vals, offset):
    return jnp.zeros((1, LANES), F32).at[0, offset:offset + vals.shape[0]].set(vals.astype(F32))


def kernel(x_prompt, x_sample, mem_prompt, state_win_k, state_win_v, state_conv, state_rec,
           cache_mem_k, cache_mem_v, ln_mix_pre, w_in, attn_sink, gdn_conv_w, gdn_a_log,
           gdn_dt_bias, gdn_norm_w, ln_mem, w_mem_kv, w_branch, w_out, ln_mix_post,
           ln_ffn_pre, w_ffn_in, w_ffn_out, ln_ffn_post):
    bp, lp, _ = x_prompt.shape
    bs, ls, _ = x_sample.shape

    sizes = [512, 128, 128, B_CONV_CH, B_HEADS, B_HEADS, 512, 512, N_BRANCH * D_MODEL]
    o = np.cumsum([0] + sizes)
    hperm = np.concatenate([np.r_[j * A_HD:(j + 1) * A_HD, (j + 4) * A_HD:(j + 5) * A_HD] for j in range(4)])
    ws = (w_in[:, hperm].astype(BF16), w_in[:, o[1]:o[4]].astype(BF16), w_in[:, o[6]:o[9]].astype(BF16),
          jnp.pad(w_in[:, o[4]:o[6]], ((0, 0), (0, LANES - 2 * B_HEADS))).astype(BF16))
    wb = jnp.concatenate([w_branch[0:1][:, hperm], w_branch[1:]], axis=0).astype(BF16)
    wo = w_out.astype(BF16)
    wfi = w_ffn_in.astype(BF16)
    wfo = w_ffn_out.astype(BF16)
    wmem = w_mem_kv.astype(BF16)
    sink = attn_sink.astype(F32)[np.array([0, 4, 1, 5, 2, 6, 3, 7])]
    vec = lambda g: g.astype(F32).reshape(1, -1)
    alog = _lane_row(gdn_a_log, B_HEADS)
    dtb = _lane_row(gdn_dt_bias, B_HEADS)
    cw = gdn_conv_w.astype(F32)
    nw = vec(gdn_norm_w)

    merge_w = (vec(ln_mix_pre), ws[2], wb, wo, vec(ln_mix_post), vec(ln_ffn_pre))
    ffn = lambda x1, h2: _ffn(x1, h2, wfi, wfo, vec(ln_ffn_post), 512)

    tp = bp * lp
    xp = x_prompt.reshape(tp, D_MODEL)
    cos_p, sin_p = _rope_tables(jnp.arange(lp, dtype=jnp.int32))
    qa, ka, va, z, qc, ba, qn, kn, vv, tail = _proj(xp, vec(ln_mix_pre), ws, cos_p, sin_p, 512, cw, lp)
    b3 = lambda a: a.reshape(bp, lp, a.shape[-1])
    ob, rec_p = _gdn_prompt(b3(qn), b3(kn), b3(vv), b3(ba), b3(z), alog, dtb, nw)
    mk, mv = _memkv(mem_prompt.reshape(bp * N_MEM, D_MODEL), vec(ln_mem), wmem)
    x1, h2 = _attn_merge(sink, xp, qa, ka, va, qc, mk, mv, ob.reshape(tp, 512), *merge_w, bp, lp, 256)
    y_p = ffn(x1, h2).reshape(bp, lp, D_MODEL)
    wk_p = ka.reshape(bp, lp, LANES)[:, -WINDOW:].reshape(bp, WINDOW, A_KV, A_HD)
    wv_p = va.reshape(bp, lp, LANES)[:, -WINDOW:].reshape(bp, WINDOW, A_KV, A_HD)
    conv_p = tail[:, -(CONV_W - 1):]
    mem_k_p = mk.reshape(bp, N_MEM, C_HEADS, C_HD)
    mem_v_p = mv.reshape(bp, N_MEM, C_HEADS, C_HD)

    ts = bs * DEC_PAD
    xs = jnp.pad(x_sample, ((0, 0), (0, DEC_PAD - ls), (0, 0))).reshape(ts, D_MODEL)
    cos_s, sin_s = _rope_tables(PAST_LEN + jnp.arange(DEC_PAD, dtype=jnp.int32))
    cos_s, sin_s = jnp.tile(cos_s, (bs, 1)), jnp.tile(sin_s, (bs, 1))
    qa, ka, va, z, qc, ba, qkv = _proj(xs, vec(ln_mix_pre), ws, cos_s, sin_s, 512)
    r3 = lambda a: a.reshape(bs, DEC_PAD, a.shape[-1])
    oa, wk_s, wv_s = _swa_decode(sink, r3(qa), r3(ka), r3(va), state_win_k.reshape(bs, WINDOW, LANES),
                                 state_win_v.reshape(bs, WINDOW, LANES), 16, ls)
    oa = oa.reshape(ts, 512)
    wk_s = wk_s.reshape(state_win_k.shape)
    wv_s = wv_s.reshape(state_win_v.shape)
    histp = jnp.pad(state_conv, ((0, 0), (DEC_PAD - (CONV_W - 1), 0), (0, 0)))
    ob, rec_s = _gdn_decode(r3(qkv), histp, ba, z, state_rec, cw, alog, dtb, nw,
                            LANES // DEC_PAD, ls)
    oc = _memattn_decode(r3(qc), cache_mem_k.reshape(bs, N_MEM * C_HEADS, C_HD),
                         cache_mem_v.reshape(bs, N_MEM * C_HEADS, C_HD), 8).reshape(ts, 512)
    real = lambda a: a.reshape(bs, DEC_PAD, 512)[:, :ls].reshape(bs * ls, 512)
    x1, h2 = _merge(x_sample.reshape(bs * ls, D_MODEL), real(oa), real(ob), real(oc), *merge_w, 256)
    y_s = ffn(x1, h2).reshape(bs, ls, D_MODEL)
    conv_s = r3(qkv)[:, ls - (CONV_W - 1):ls]

    return (y_p, y_s, wk_p, wv_p, conv_p, rec_p, mem_k_p, mem_v_p, wk_s, wv_s, conv_s, rec_s)
```

```python
import functools

import numpy as np
import jax
import jax.numpy as jnp
from jax import lax
from jax.experimental import pallas as pl
from jax.experimental.pallas import tpu as pltpu

F32 = jnp.float32
BF16 = jnp.bfloat16

D_MODEL = 1024
PAST_LEN = 16384
EPS = 1e-6
ROPE_THETA = 10000.0
N_MEM = 256
WINDOW = 128
A_HD = 64
A_HEADS = 8
A_KV = 2
A_SCALE = A_HD ** -0.5
B_HEADS = 4
B_DK = 128
B_DV = 128
CONV_W = 4
GDN_CHUNK = 64
B_CONV_CH = B_HEADS * (2 * B_DK + B_DV)
C_HEADS = 4
C_HD = 128
N_BRANCH = 3
BRANCH_W = 512
D_FF = 2816

LANES = 128
SUBLANES = 8
PAIR = 2 * LANES
VMEM_LIMIT = 56 * 1024 * 1024
NEG = -1e30
DEC_PAD = SUBLANES


def _cparams(sem, vmem=VMEM_LIMIT):
    return pltpu.CompilerParams(dimension_semantics=sem, vmem_limit_bytes=vmem)


def _const_spec(shape):
    nd = len(shape)
    return pl.BlockSpec(shape, lambda *_: (0,) * nd, pipeline_mode=pl.Buffered(1))


def _rms(x, g):
    ms = jnp.mean(x * x, axis=-1, keepdims=True)
    return x * lax.rsqrt(ms + EPS) * g


def _bdot(a, b):
    return jnp.dot(a.astype(BF16), b.astype(BF16), preferred_element_type=F32)


def _bdot_nt(a, b):
    return lax.dot_general(a.astype(BF16), b.astype(BF16), (((1,), (1,)), ((), ())),
                           preferred_element_type=F32)


def _bdot_tn(a, b):
    return lax.dot_general(a.astype(BF16), b.astype(BF16), (((0,), (0,)), ((), ())),
                           preferred_element_type=F32)


def _hdot(a, b):
    return jnp.dot(a, b, precision=lax.Precision.HIGHEST, preferred_element_type=F32)


def _silu(x):
    return x * jax.nn.sigmoid(x)


def _softplus(x):
    return jnp.maximum(x, 0.0) + jnp.log1p(jnp.exp(-jnp.abs(x)))


def _rope128(v, cos, sin):
    lane = lax.broadcasted_iota(jnp.int32, v.shape, 1)
    fwd = pltpu.roll(v, 32, 1)
    bwd = pltpu.roll(v, 96, 1)
    sw = jnp.where((lane & 32) == 0, bwd, fwd)
    return v * cos + sw * sin


def _l2n(x):
    return x * lax.rsqrt(jnp.sum(x * x, axis=-1, keepdims=True) + EPS)

def _proj_steps(x_ref, g_ref, wq_ref, wr_ref, wzc_ref, wba_ref, cos_ref, sin_ref,
                qa_ref, ka_ref, va_ref, z_ref, qc_ref, ba_ref):
    h = _rms(x_ref[...], g_ref[...]).astype(BF16)
    cos = cos_ref[...]
    sin = sin_ref[...]

    def mm(w_ref, a, b):
        return jnp.dot(h, w_ref[:, a:b], preferred_element_type=F32)

    def qa_half(c0):
        q = mm(wq_ref, c0 * LANES, (c0 + 2) * LANES)
        for c in range(2):
            qa_ref[:, (c0 + c) * LANES:(c0 + c + 1) * LANES] = (
                _rope128(q[:, c * LANES:(c + 1) * LANES], cos, sin) * A_SCALE).astype(BF16)

    def kv():
        kv2 = mm(wr_ref, 0, PAIR)
        ka_ref[...] = _rope128(kv2[:, :LANES], cos, sin)
        va_ref[...] = kv2[:, LANES:]

    def z_half(c0):
        z_ref[:, c0:c0 + PAIR] = mm(wzc_ref, c0, c0 + PAIR)

    def qc_half(c0):
        qc_ref[:, c0:c0 + PAIR] = mm(wzc_ref, 512 + c0, 512 + c0 + PAIR).astype(BF16)

    def ba():
        ba_ref[...] = mm(wba_ref, 0, LANES)

    def qkv(a, b):
        return mm(wr_ref, PAIR + a, PAIR + b)

    steps = [lambda: qa_half(0), lambda: qa_half(2), kv, lambda: z_half(0), lambda: z_half(PAIR),
             lambda: qc_half(0), lambda: qc_half(PAIR), ba]
    return qkv, steps


def _proj_raw_kernel(x_ref, g_ref, wq_ref, wr_ref, wzc_ref, wba_ref, cos_ref, sin_ref,
                     qa_ref, ka_ref, va_ref, z_ref, qc_ref, ba_ref, qkv_ref):
    qkv, steps = _proj_steps(x_ref, g_ref, wq_ref, wr_ref, wzc_ref, wba_ref, cos_ref, sin_ref,
                             qa_ref, ka_ref, va_ref, z_ref, qc_ref, ba_ref)
    qkv_ref[...] = qkv(0, B_CONV_CH)
    for step in steps:
        step()


def _proj_conv_kernel(x_ref, g_ref, wq_ref, wr_ref, wzc_ref, wba_ref, cos_ref, sin_ref, cw_ref,
                      qa_ref, ka_ref, va_ref, z_ref, qc_ref, ba_ref, qn_ref, kn_ref, vv_ref, tail_ref,
                      buf_ref, *, tm, tiles_per_seq):
    hist = SUBLANES
    first = lax.rem(pl.program_id(0), tiles_per_seq) == 0

    @pl.when(first)
    def _():
        buf_ref[0:hist, :] = jnp.zeros((hist, B_CONV_CH), F32)

    @pl.when(jnp.logical_not(first))
    def _():
        buf_ref[0:hist, :] = buf_ref[tm:tm + hist, :]

    qkv, steps = _proj_steps(x_ref, g_ref, wq_ref, wr_ref, wzc_ref, wba_ref, cos_ref, sin_ref,
                             qa_ref, ka_ref, va_ref, z_ref, qc_ref, ba_ref)
    nq = B_HEADS * B_DK

    def conv_group(c0):
        cs = slice(c0, c0 + PAIR)
        raw = qkv(c0, c0 + PAIR)
        buf_ref[hist:hist + tm, cs] = raw
        tail_ref[0, :, cs] = raw[tm - hist:, :]
        xb = buf_ref[:, cs]
        acc = xb * cw_ref[0:1, cs]
        for i in range(1, CONV_W):
            acc = pltpu.roll(acc, 1, 0) + xb * cw_ref[i:i + 1, cs]
        act = _silu(acc[hist:])
        if c0 >= 2 * nq:
            vv_ref[:, c0 - 2 * nq:c0 - 2 * nq + PAIR] = act
            return
        out_ref, base, scale = (qn_ref, 0, B_DK ** -0.5) if c0 < nq else (kn_ref, nq, 1.0)
        for s in range(2):
            o0 = c0 - base + s * B_DK
            out_ref[:, o0:o0 + B_DK] = _l2n(act[:, s * B_DK:(s + 1) * B_DK]) * scale

    groups = [functools.partial(conv_group, c0) for c0 in range(0, B_CONV_CH, PAIR)]
    while groups or steps:
        if groups:
            groups.pop(0)()
        if steps:
            steps.pop(0)()


_PROJ_OUTS = [(512, BF16), (128, F32), (128, F32), (512, F32), (512, BF16), (128, F32)]


def _proj(x, gain, ws, cos, sin, tm, cw=None, seq=None):
    wq, wr, wtail, wba = ws
    t = x.shape[0]
    ntab = cos.shape[0] // tm
    row = lambda n: pl.BlockSpec((tm, n), lambda i: (i, 0))
    tab = pl.BlockSpec((tm, LANES), lambda i: (i % ntab, 0))
    in_specs = [row(D_MODEL), _const_spec((1, D_MODEL)), _const_spec(wq.shape), _const_spec(wr.shape),
                _const_spec((D_MODEL, D_MODEL)), _const_spec(wba.shape), tab, tab]
    out_specs = [row(n) for n, _ in _PROJ_OUTS]
    out_shape = [jax.ShapeDtypeStruct((t, n), d) for n, d in _PROJ_OUTS]
    if cw is None:
        return pl.pallas_call(
            _proj_raw_kernel, grid=(t // tm,), in_specs=in_specs,
            out_specs=out_specs + [row(B_CONV_CH)],
            out_shape=out_shape + [jax.ShapeDtypeStruct((t, B_CONV_CH), F32)],
            compiler_params=_cparams(("arbitrary",)), name="proj",
        )(x, gain, wq, wr, wtail, wba, cos, sin)
    tiles = seq // tm
    return pl.pallas_call(
        functools.partial(_proj_conv_kernel, tm=tm, tiles_per_seq=tiles),
        grid=(t // tm,),
        in_specs=in_specs + [_const_spec((CONV_W, B_CONV_CH))],
        out_specs=out_specs + [row(512)] * 3
        + [pl.BlockSpec((1, SUBLANES, B_CONV_CH), lambda i: (i // tiles, 0, 0))],
        out_shape=out_shape + [jax.ShapeDtypeStruct((t, 512), F32)] * 3
        + [jax.ShapeDtypeStruct((t // seq, SUBLANES, B_CONV_CH), F32)],
        scratch_shapes=[pltpu.VMEM((tm + SUBLANES, B_CONV_CH), F32)],
        compiler_params=_cparams(("arbitrary",)), name="proj_conv",
    )(x, gain, wq, wr, wtail, wba, cos, sin, cw)


def _swa_scores(q, k16):
    tq = q.shape[0]
    lo = lax.broadcasted_iota(jnp.int32, (tq, LANES), 1) < A_HD
    blocks = []
    for j in range(4):
        c = q[:, j * LANES:(j + 1) * LANES].astype(F32)
        blocks.append(jnp.where(lo, c, 0.0))
        blocks.append(jnp.where(lo, 0.0, c))
    lhs = jnp.concatenate(blocks, axis=0).astype(BF16)
    return lax.dot_general(lhs, k16, (((1,), (1,)), ((), ())), preferred_element_type=F32)


def _swa_finish(logits, v16, mask, sink_ref):
    tq = logits.shape[0] // A_HEADS
    lo = lax.broadcasted_iota(jnp.int32, (tq, LANES), 1) < A_HD
    es, inv = [], []
    for s in range(8):
        l = jnp.where(mask, logits[s * tq:(s + 1) * tq], NEG)
        sk = sink_ref[s]
        m = jnp.maximum(jnp.max(l, axis=-1, keepdims=True), sk)
        e = jnp.exp(l - m)
        den = jnp.sum(e, axis=-1, keepdims=True) + jnp.exp(sk - m)
        es.append(e.astype(BF16))
        inv.append(1.0 / den)
    pv = jnp.dot(jnp.concatenate(es, axis=0), v16, preferred_element_type=F32)
    outs = []
    for j in range(4):
        a = pv[(2 * j) * tq:(2 * j + 1) * tq] * inv[2 * j]
        b = pv[(2 * j + 1) * tq:(2 * j + 2) * tq] * inv[2 * j + 1]
        outs.append(jnp.where(lo, a, b))
    return jnp.concatenate(outs, axis=1)


def _swa_prompt_kernel(sink_ref, q_ref, kc_ref, kp_ref, vc_ref, vp_ref, o_ref, *, nblk):
    i = pl.program_id(1)
    kcat = jnp.concatenate([kp_ref[...], kc_ref[...]], axis=0).astype(BF16)
    vcat = jnp.concatenate([vp_ref[...], vc_ref[...]], axis=0).astype(BF16)
    ii = lax.broadcasted_iota(jnp.int32, (WINDOW, 2 * WINDOW), 0)
    jj = lax.broadcasted_iota(jnp.int32, (WINDOW, 2 * WINDOW), 1)
    band = (jj > ii) & (jj <= ii + WINDOW)
    for jb in range(nblk):
        mask = band
        if jb == 0:
            mask = band & ((jj >= WINDOW) | (i > 0))
        logits = _swa_scores(q_ref[jb * WINDOW:(jb + 1) * WINDOW, :], kcat[jb * WINDOW:(jb + 2) * WINDOW])
        o = _swa_finish(logits, vcat[jb * WINDOW:(jb + 2) * WINDOW], mask, sink_ref)
        o_ref[jb * WINDOW:(jb + 1) * WINDOW, :] = o.astype(BF16)


def _swa_prompt(sink, qa, ka, va, batch, seq, tq):
    nq = seq // tq
    nblk = tq // WINDOW
    nw = seq // WINDOW
    cur = lambda n: pl.BlockSpec((tq, n), lambda b, i: (b * nq + i, 0))
    prev = pl.BlockSpec((WINDOW, LANES), lambda b, i: (jnp.maximum(b * nw + i * nblk - 1, 0), 0))
    return pl.pallas_call(
        functools.partial(_swa_prompt_kernel, nblk=nblk),
        grid=(batch, nq),
        in_specs=[pl.BlockSpec(memory_space=pltpu.SMEM), cur(512), cur(LANES), prev, cur(LANES), prev],
        out_specs=cur(512),
        out_shape=jax.ShapeDtypeStruct(qa.shape, BF16),
        compiler_params=_cparams(("arbitrary", "arbitrary")),
        name="swa_prompt",
    )(sink, qa, ka, ka, va, va)


ROW_UNROLL = 4


def _rows_loop(rows, load, stages, store):
    def body(g, carry):
        idx = [g * ROW_UNROLL + j for j in range(ROW_UNROLL)]
        vals = [load(r) for r in idx]
        for stage in stages:
            vals = [stage(v) for v in vals]
        for r, o in zip(idx, vals):
            store(r, o)
        return carry

    lax.fori_loop(0, rows // ROW_UNROLL, body, 0)


def _swa_decode_kernel(sink_ref, q_ref, kn_ref, vn_ref, wk_ref, wv_ref, o_ref, wko_ref, wvo_ref,
                       *, rows, new_len):
    ii = lax.broadcasted_iota(jnp.int32, (DEC_PAD, 2 * WINDOW), 0)
    jj = lax.broadcasted_iota(jnp.int32, (DEC_PAD, 2 * WINDOW), 1)
    mask = ((jj < WINDOW) & (jj > ii)) | ((jj >= WINDOW) & (jj - WINDOW <= ii))
    pad = jnp.zeros((WINDOW - DEC_PAD, LANES), F32)
    keep = lax.broadcasted_iota(jnp.int32, (WINDOW, LANES), 0) < WINDOW - new_len

    def slide(win, new):
        tail = jnp.concatenate([pad, new], axis=0)
        return jnp.where(keep, pltpu.roll(win, WINDOW - new_len, 0), pltpu.roll(tail, DEC_PAD - new_len, 0))

    def load(r):
        return q_ref[r], wk_ref[r], kn_ref[r], wv_ref[r], vn_ref[r]

    def scores(args):
        q, wk, kn, wv, vn = args
        kk = jnp.concatenate([wk, kn, pad], axis=0).astype(BF16)
        return _swa_scores(q, kk), wk, kn, wv, vn

    def finish(args):
        logits, wk, kn, wv, vn = args
        vv = jnp.concatenate([wv, vn, pad], axis=0).astype(BF16)
        return _swa_finish(logits, vv, mask, sink_ref).astype(BF16), slide(wk, kn), slide(wv, vn)

    def store(r, outs):
        o_ref[r], wko_ref[r], wvo_ref[r] = outs

    _rows_loop(rows, load, [scores, finish], store)


def _swa_decode(sink, qa, kn, vn, wk, wv, rows, new_len):
    nb = qa.shape[0]
    blk = lambda a, n: pl.BlockSpec((rows, a, n), lambda i: (i, 0, 0))
    return pl.pallas_call(
        functools.partial(_swa_decode_kernel, rows=rows, new_len=new_len),
        grid=(nb // rows,),
        in_specs=[pl.BlockSpec(memory_space=pltpu.SMEM), blk(DEC_PAD, 512), blk(DEC_PAD, LANES),
                  blk(DEC_PAD, LANES), blk(WINDOW, LANES), blk(WINDOW, LANES)],
        out_specs=[blk(DEC_PAD, 512), blk(WINDOW, LANES), blk(WINDOW, LANES)],
        out_shape=[jax.ShapeDtypeStruct(qa.shape, BF16), jax.ShapeDtypeStruct(wk.shape, F32),
                   jax.ShapeDtypeStruct(wv.shape, F32)],
        compiler_params=_cparams(("arbitrary",)),
        name="swa_decode",
    )(sink, qa, kn, vn, wk, wv)


def _gdn_gates(ba, alog, dtb, valid):
    beta = jax.nn.sigmoid(ba)
    g = -jnp.exp(alog) * _softplus(ba + dtb)
    if valid is not None:
        beta = jnp.where(valid, beta, 0.0)
        g = jnp.where(valid, g, 0.0)
    return beta, g


def _chunk_masks(chunk):
    sh = chunk.bit_length() - 1
    ri = lax.broadcasted_iota(jnp.int32, (LANES, LANES), 0)
    ci = lax.broadcasted_iota(jnp.int32, (LANES, LANES), 1)
    same = (ri >> sh) == (ci >> sh)
    return same, same & (ri >= ci), same & (ri > ci), ri == ci


def _gdn_cumsums(g_all, chunk):
    same, tri, _, _ = _chunk_masks(chunk)
    lower = jnp.where(tri, 1.0, 0.0)
    gcol = _hdot(lower, g_all)
    grow = _hdot(g_all.T, lower.T)
    gtot = _hdot(jnp.where(same, 1.0, 0.0), g_all)
    return gcol, grow, gtot


def _sibling_mask(ri, ci, lvl):
    rb = ri >> lvl
    return ((rb & 1) == 1) & ((ci >> lvl) == rb - 1)


def _gdn_phase_a(q, k, v, beta, gcol, grow, gtot, chunk):
    _, tri, strict, _ = _chunk_masks(chunk)
    decay = jnp.exp(jnp.where(tri, gcol - grow, NEG))
    kb = k * beta
    kkqk = _bdot_nt(jnp.concatenate([kb, q], axis=0), k)
    a = jnp.where(strict, kkqk[:LANES] * decay, 0.0)
    qk = kkqk[LANES:] * decay
    ri = lax.broadcasted_iota(jnp.int32, a.shape, 0)
    ci = lax.broadcasted_iota(jnp.int32, a.shape, 1)
    n = -jnp.where(_sibling_mask(ri, ci, 0), a, 0.0)
    for lvl in range(1, chunk.bit_length() - 1):
        al = jnp.where(_sibling_mask(ri, ci, lvl), a, 0.0)
        x = al + _bdot(n, al)
        n = n - (x + _bdot(x, n))
    rhs = jnp.concatenate([v * beta, kb * jnp.exp(gcol)], axis=1)
    sol = rhs + _bdot(n, rhs)
    u, w = sol[:, :B_DV], sol[:, B_DV:]
    return u, w, q * jnp.exp(gcol), k * jnp.exp(gtot - gcol), qk, jnp.exp(gtot)


def _gdn_qkv_heads(qkv, h):
    q = qkv[:, h * B_DK:(h + 1) * B_DK]
    k = qkv[:, (B_HEADS + h) * B_DK:(B_HEADS + h + 1) * B_DK]
    v = qkv[:, (2 * B_HEADS + h) * B_DK:(2 * B_HEADS + h + 1) * B_DK]
    q = q * lax.rsqrt(jnp.sum(q * q, axis=-1, keepdims=True) + EPS) * (B_DK ** -0.5)
    k = k * lax.rsqrt(jnp.sum(k * k, axis=-1, keepdims=True) + EPS)
    return q, k, v


def _gdn_out(o, z, nw):
    return _rms(o, nw) * _silu(z)


def _pair_bd(x2):
    lo = lax.broadcasted_iota(jnp.int32, x2.shape, 1) < LANES
    return jnp.concatenate([jnp.where(lo, x2, 0.0), jnp.where(lo, 0.0, x2)], axis=0).astype(BF16)


def _quad_bd(x4):
    blk = lax.broadcasted_iota(jnp.int32, x4.shape, 1) >> (GDN_CHUNK.bit_length() - 1)
    return jnp.concatenate([jnp.where(blk == j, x4, 0.0) for j in range(PAIR // GDN_CHUNK)],
                           axis=0).astype(BF16)


def _pair_cols(x, l0, l1):
    lo = lax.broadcasted_iota(jnp.int32, (x.shape[0], PAIR), 1) < LANES
    return jnp.where(lo, x[:, l0:l0 + 1], x[:, l1:l1 + 1])


def _gdn_prompt_kernel(q_ref, k_ref, v_ref, ba_ref, z_ref, alog_ref, dtb_ref, nw_ref,
                       ob_ref, s_ref, sbd_ref, *, nb, ntb):
    t = pl.program_id(0)
    npair = B_HEADS // 2

    @pl.when(t == 0)
    def _():
        sbd_ref[...] = jnp.zeros(sbd_ref.shape, F32)

    alog = alog_ref[...]
    dtb = dtb_ref[...]
    nw = nw_ref[...]
    nchunk = LANES // GDN_CHUNK
    _, tri, strict, _ = _chunk_masks(GDN_CHUNK)
    tri2 = jnp.concatenate([tri, tri], axis=1)
    strict2 = jnp.concatenate([strict, strict], axis=1)

    items = []
    for b, tb in [(b, tb) for b in range(nb) for tb in range(ntb)]:
        rsl = slice(tb * LANES, (tb + 1) * LANES)
        beta_all, g_all = _gdn_gates(ba_ref[b, rsl, :], alog, dtb, None)
        gcol_all, grow_all, gtot_all = _gdn_cumsums(g_all, GDN_CHUNK)
        eg_all = jnp.exp(gcol_all)
        ed_all = jnp.exp(gtot_all - gcol_all)
        et_all = jnp.exp(gtot_all)
        for p in range(npair):
            h0, h1 = 2 * p, 2 * p + 1
            g0, g1 = B_HEADS + h0, B_HEADS + h1
            psl = slice(p * PAIR, (p + 1) * PAIR)
            q2 = q_ref[b, rsl, psl]
            k2 = k_ref[b, rsl, psl]
            v2 = v_ref[b, rsl, psl]
            beta2 = _pair_cols(beta_all, h0, h1)
            grow2 = jnp.concatenate([grow_all[g0:g0 + 1, :], grow_all[g1:g1 + 1, :]], axis=1)
            decay2 = jnp.exp(jnp.where(tri2, _pair_cols(gcol_all, g0, g1) - grow2, NEG))
            kb2 = k2 * beta2
            items.append(dict(
                b=b, tb=tb, p=p, rsl=rsl, decay2=decay2, kb2=kb2, v2b=v2 * beta2,
                kbe2=kb2 * _pair_cols(eg_all, g0, g1),
                lhs=jnp.concatenate([kb2, q2], axis=0).astype(BF16),
                kbd=_pair_bd(k2),
                qe2=(q2 * _pair_cols(eg_all, g0, g1)).astype(BF16),
                kd2=(k2 * _pair_cols(ed_all, g0, g1)).astype(BF16),
                et2=_pair_cols(et_all, g0, g1)))

    for it in items:
        kkqk = lax.dot_general(it["lhs"], it["kbd"], (((1,), (1,)), ((), ())),
                               preferred_element_type=F32)
        it["a2"] = jnp.where(strict2, kkqk[:LANES] * it["decay2"], 0.0)
        it["qk2"] = (kkqk[LANES:] * it["decay2"]).astype(BF16)
    qlane = lax.broadcasted_iota(jnp.int32, (GDN_CHUNK, PAIR), 1)
    qrow = lax.broadcasted_iota(jnp.int32, (GDN_CHUNK, PAIR), 0)
    qcol = qlane & (GDN_CHUNK - 1)
    top = (qlane & GDN_CHUNK) == 0
    for it in items:
        it["a64"] = it["a2"][:GDN_CHUNK] + it["a2"][GDN_CHUNK:]
        it["n"] = -jnp.where(_sibling_mask(qrow, qcol, 0), it["a64"], 0.0)
    for lvl in range(1, GDN_CHUNK.bit_length() - 1):
        sib = _sibling_mask(qrow, qcol, lvl)
        for it in items:
            it["al"] = jnp.where(sib, it["a64"], 0.0)
            it["x"] = it["al"] + jnp.dot(it["n"].astype(BF16), _quad_bd(it["al"]), preferred_element_type=F32)
        for it in items:
            x = it["x"]
            it["n"] = it["n"] - (x + jnp.dot(x.astype(BF16), _quad_bd(it["n"]), preferred_element_type=F32))
    for it in items:
        n64 = it["n"]
        it["n"] = jnp.concatenate([jnp.where(top, n64, 0.0), jnp.where(top, 0.0, n64)], axis=0)
    for it in items:
        us, ws = [], []
        for s in range(2):
            sl = slice(s * LANES, (s + 1) * LANES)
            rhs = jnp.concatenate([it["v2b"][:, sl], it["kbe2"][:, sl]], axis=1)
            sol = rhs + _bdot(it["n"][:, sl], rhs)
            us.append(sol[:, :B_DV])
            ws.append(sol[:, B_DV:])
        it["u2"] = jnp.concatenate(us, axis=1)
        it["w2"] = jnp.concatenate(ws, axis=1).astype(BF16)

    ri = lax.broadcasted_iota(jnp.int32, (PAIR, PAIR), 0) < LANES
    ci = lax.broadcasted_iota(jnp.int32, (PAIR, PAIR), 1) < LANES
    bdmask = ri == ci
    nchain = nb * npair
    state = [sbd_ref[i] for i in range(nchain)]
    outs = {}
    for tb in range(ntb):
        cur = [(it["b"] * npair + it["p"], i, it) for i, it in enumerate(items) if it["tb"] == tb]
        for c in range(nchunk):
            rows = slice(c * GDN_CHUNK, (c + 1) * GDN_CHUNK)
            rs = [jnp.dot(jnp.concatenate([it["w2"][rows], it["qe2"][rows]], axis=0), state[ch].astype(BF16),
                          preferred_element_type=F32) for ch, _, it in cur]
            for (ch, i, it), r in zip(cur, rs):
                s = state[ch]
                vn2 = it["u2"][rows] - r[:GDN_CHUNK]
                vt = _pair_bd(jnp.concatenate([vn2] * nchunk, axis=0))
                outs[(i, c)] = r[GDN_CHUNK:] + jnp.dot(it["qk2"][rows], vt, preferred_element_type=F32)
                upd = lax.dot_general(it["kd2"][rows], vn2.astype(BF16), (((0,), (0,)), ((), ())),
                                      preferred_element_type=F32)
                state[ch] = s * it["et2"][c * GDN_CHUNK:c * GDN_CHUNK + 1, :] + jnp.where(bdmask, upd, 0.0)
    for ch in range(nchain):
        sbd_ref[ch] = state[ch]

    @pl.when(t == pl.num_programs(0) - 1)
    def _():
        for ch in range(nchain):
            b, p = divmod(ch, npair)
            s_ref[b, 2 * p] = state[ch][:LANES, :LANES]
            s_ref[b, 2 * p + 1] = state[ch][LANES:, LANES:]

    for i, it in enumerate(items):
        o2 = jnp.concatenate([outs[(i, c)] for c in range(nchunk)], axis=0)
        for s in range(2):
            h = 2 * it["p"] + s
            sl = slice(h * B_DV, (h + 1) * B_DV)
            o = o2[:, s * LANES:(s + 1) * LANES]
            ob_ref[it["b"], it["rsl"], sl] = _gdn_out(o, z_ref[it["b"], it["rsl"], sl], nw).astype(BF16)


GDN_TOKEN_BLOCKS = 2


def _gdn_prompt(q, k, v, ba, z, alog, dtb, nw):
    batch, seq, _ = q.shape
    tm = GDN_TOKEN_BLOCKS * LANES
    tok = lambda n: pl.BlockSpec((batch, tm, n), lambda t: (0, t, 0))
    state = pl.BlockSpec((batch, B_HEADS, B_DK, B_DV), lambda t: (0, 0, 0, 0))
    return pl.pallas_call(
        functools.partial(_gdn_prompt_kernel, nb=batch, ntb=GDN_TOKEN_BLOCKS),
        grid=(seq // tm,),
        in_specs=[tok(512), tok(512), tok(512), tok(LANES), tok(512),
                  _const_spec((1, LANES)), _const_spec((1, LANES)), _const_spec((1, B_DV))],
        out_specs=[tok(512), state],
        out_shape=[jax.ShapeDtypeStruct((batch, seq, 512), BF16),
                   jax.ShapeDtypeStruct((batch, B_HEADS, B_DK, B_DV), F32)],
        scratch_shapes=[pltpu.VMEM((batch * (B_HEADS // 2), PAIR, PAIR), F32)],
        compiler_params=_cparams(("arbitrary",)),
        name="gdn_prompt",
    )(q, k, v, ba, z, alog, dtb, nw)


def _gdn_decode_kernel(raw_ref, hist_ref, ba_ref, z_ref, rec_ref, cw_ref, alog_ref, dtb_ref, nw_ref,
                       ob_ref, s_ref, buf_ref, u_s, w_s, qe_s, kd_s, qk_s, eg_s, o_s, *, rows, valid_len):
    buf_ref[:, 0:DEC_PAD, :] = hist_ref[...]
    buf_ref[:, DEC_PAD:2 * DEC_PAD, :] = raw_ref[...]
    conv = None
    for i in range(CONV_W):
        off = DEC_PAD - (CONV_W - 1) + i
        term = buf_ref[:, off:off + DEC_PAD, :] * cw_ref[i:i + 1, :]
        conv = term if conv is None else conv + term
    qkv = _silu(conv.reshape(rows * DEC_PAD, B_CONV_CH))
    tok = lax.broadcasted_iota(jnp.int32, (LANES, LANES), 0) & (DEC_PAD - 1)
    beta_all, g_all = _gdn_gates(ba_ref[...], alog_ref[...], dtb_ref[...], tok < valid_len)
    gcol_all, grow_all, gtot_all = _gdn_cumsums(g_all, DEC_PAD)
    for h in range(B_HEADS):
        q, k, v = _gdn_qkv_heads(qkv, h)
        gl = B_HEADS + h
        u, w, qe, kd, qk, egt = _gdn_phase_a(
            q, k, v, beta_all[:, h:h + 1], gcol_all[:, gl:gl + 1], grow_all[gl:gl + 1, :],
            gtot_all[:, gl:gl + 1], DEC_PAD)
        u_s[h] = u
        w_s[h] = w
        qe_s[h] = qe
        kd_s[h] = kd
        qk_s[h] = qk
        eg_s[h] = jnp.broadcast_to(egt, (LANES, LANES))

    def load(r):
        r0 = pl.multiple_of(r * DEC_PAD, DEC_PAD)
        rr = pl.ds(r0, DEC_PAD)
        return [(rec_ref[r, h], w_s[h, rr, :], qe_s[h, rr, :], u_s[h, rr, :], qk_s[h, rr, :],
                 kd_s[h, rr, :], eg_s[h, pl.ds(r0, 1), :]) for h in range(B_HEADS)]

    def read_state(heads):
        return [(_bdot(jnp.concatenate([w, qe], axis=0), s), s, u, qk, kd, eg)
                for s, w, qe, u, qk, kd, eg in heads]

    def update(heads):
        outs = []
        for res, s, u, qk, kd, eg in heads:
            v_new = u - res[:DEC_PAD]
            vt = jnp.concatenate([v_new] * (LANES // DEC_PAD), axis=0)
            outs.append((res[DEC_PAD:] + _bdot(qk, vt), s * eg + _bdot_tn(kd, v_new)))
        return outs

    def store(r, outs):
        rr = pl.ds(pl.multiple_of(r * DEC_PAD, DEC_PAD), DEC_PAD)
        for h, (o, s_new) in enumerate(outs):
            o_s[h, rr, :] = o
            s_ref[r, h] = s_new

    _rows_loop(rows, load, [read_state, update], store)
    nw = nw_ref[...]
    for h in range(B_HEADS):
        sl = slice(h * B_DV, (h + 1) * B_DV)
        ob_ref[:, sl] = _gdn_out(o_s[h], z_ref[:, sl], nw).astype(BF16)


def _gdn_decode(raw, histp, ba, z, rec, cw, alog, dtb, nw, rows, valid_len):
    nb = raw.shape[0]
    flat = rows * DEC_PAD
    assert flat == LANES
    sq = lambda: pltpu.VMEM((B_HEADS, LANES, LANES), F32)
    return pl.pallas_call(
        functools.partial(_gdn_decode_kernel, rows=rows, valid_len=valid_len),
        grid=(nb // rows,),
        in_specs=[pl.BlockSpec((rows, DEC_PAD, B_CONV_CH), lambda i: (i, 0, 0)),
                  pl.BlockSpec((rows, DEC_PAD, B_CONV_CH), lambda i: (i, 0, 0)),
                  pl.BlockSpec((flat, LANES), lambda i: (i, 0)),
                  pl.BlockSpec((flat, 512), lambda i: (i, 0)),
                  pl.BlockSpec((rows, B_HEADS, B_DK, B_DV), lambda i: (i, 0, 0, 0)),
                  _const_spec((CONV_W, B_CONV_CH)), _const_spec((1, LANES)), _const_spec((1, LANES)),
                  _const_spec((1, B_DV))],
        out_specs=[pl.BlockSpec((flat, 512), lambda i: (i, 0)),
                   pl.BlockSpec((rows, B_HEADS, B_DK, B_DV), lambda i: (i, 0, 0, 0))],
        out_shape=[jax.ShapeDtypeStruct((nb * DEC_PAD, 512), BF16),
                   jax.ShapeDtypeStruct(rec.shape, F32)],
        scratch_shapes=[pltpu.VMEM((rows, 2 * DEC_PAD, B_CONV_CH), F32)] + [sq() for _ in range(7)],
        compiler_params=_cparams(("arbitrary",)),
        name="gdn_decode",
    )(raw, histp, ba, z, rec, cw, alog, dtb, nw)


def _memkv_kernel(m_ref, g_ref, w_ref, k_ref, v_ref):
    h = _rms(m_ref[...], g_ref[...]).astype(BF16)
    n = C_HEADS * C_HD
    k_ref[...] = jnp.dot(h, w_ref[:, :n], preferred_element_type=F32)
    v_ref[...] = jnp.dot(h, w_ref[:, n:], preferred_element_type=F32)


def _memkv(mem, gain, w):
    t = mem.shape[0]
    tm = 512
    n = C_HEADS * C_HD
    return pl.pallas_call(
        _memkv_kernel,
        grid=(t // tm,),
        in_specs=[pl.BlockSpec((tm, D_MODEL), lambda i: (i, 0)), _const_spec((1, D_MODEL)),
                  _const_spec((D_MODEL, 2 * n))],
        out_specs=[pl.BlockSpec((tm, n), lambda i: (i, 0))] * 2,
        out_shape=[jax.ShapeDtypeStruct((t, n), F32)] * 2,
        compiler_params=_cparams(("arbitrary",)),
        name="memkv",
    )(mem, gain, w)


def _softmax_rows(logits):
    m = jnp.max(logits, axis=-1, keepdims=True)
    e = jnp.exp(logits - m)
    return e, 1.0 / jnp.sum(e, axis=-1, keepdims=True)


def _memattn_prompt_kernel(q_ref, k_ref, v_ref, o_ref):
    hs = lambda h: slice(h * C_HD, (h + 1) * C_HD)
    scores = lambda h: _bdot_nt(q_ref[:, hs(h)], k_ref[:, hs(h)])
    logits = scores(0)
    for h in range(C_HEADS):
        nxt = scores(h + 1) if h + 1 < C_HEADS else None
        e, inv = _softmax_rows(logits * (C_HD ** -0.5))
        o_ref[:, hs(h)] = (_bdot(e, v_ref[:, hs(h)]) * inv).astype(BF16)
        logits = nxt


def _memattn_prompt(qc, mk, mv, batch, seq, tm):
    nq = seq // tm
    n = C_HEADS * C_HD
    cur = pl.BlockSpec((tm, n), lambda b, i: (b * nq + i, 0))
    mem = pl.BlockSpec((N_MEM, n), lambda b, i: (b, 0))
    return pl.pallas_call(
        _memattn_prompt_kernel,
        grid=(batch, nq),
        in_specs=[cur, mem, mem],
        out_specs=cur,
        out_shape=jax.ShapeDtypeStruct(qc.shape, BF16),
        compiler_params=_cparams(("arbitrary", "arbitrary")),
        name="memattn_prompt",
    )(qc, mk, mv)


def _memattn_decode_kernel(q_ref, k_ref, v_ref, o_ref, *, rows):
    nk = N_MEM * C_HEADS
    col = lax.broadcasted_iota(jnp.int32, (C_HEADS * DEC_PAD, nk), 1)
    row = lax.broadcasted_iota(jnp.int32, (C_HEADS * DEC_PAD, nk), 0)
    own = (col & (C_HEADS - 1)) == (row >> (DEC_PAD.bit_length() - 1))

    def load(r):
        return q_ref[r], k_ref[r], v_ref[r]

    def scores(args):
        q, k, v = args
        q = q.astype(F32)
        lhs = jnp.concatenate([q[:, h * C_HD:(h + 1) * C_HD] for h in range(C_HEADS)], axis=0)
        return _bdot_nt(lhs, k), v

    def finish(args):
        logits, v = args
        e, inv = _softmax_rows(jnp.where(own, logits * (C_HD ** -0.5), NEG))
        pv = _bdot(e, v) * inv
        return jnp.concatenate([pv[h * DEC_PAD:(h + 1) * DEC_PAD] for h in range(C_HEADS)],
                               axis=1).astype(BF16)

    def store(r, o):
        o_ref[r] = o

    _rows_loop(rows, load, [scores, finish], store)


def _memattn_decode(qc, ck, cv, rows):
    nb = qc.shape[0]
    n = C_HEADS * C_HD
    blk = pl.BlockSpec((rows, DEC_PAD, n), lambda i: (i, 0, 0))
    cache = pl.BlockSpec((rows, N_MEM * C_HEADS, C_HD), lambda i: (i, 0, 0))
    return pl.pallas_call(
        functools.partial(_memattn_decode_kernel, rows=rows),
        grid=(nb // rows,),
        in_specs=[blk, cache, cache],
        out_specs=blk,
        out_shape=jax.ShapeDtypeStruct(qc.shape, BF16),
        compiler_params=_cparams(("arbitrary",)),
        name="memattn_decode",
    )(qc, ck, cv)


def _merge_kernel(x_ref, oa_ref, ob_ref, oc_ref, gpre_ref, wga_ref, wgb_ref, wgc_ref, wb_ref, wo_ref,
                  gpost_ref, gfpre_ref, x1_ref, h2_ref):
    x = x_ref[...]
    ups = [jnp.dot(o_ref[...], wb_ref[n], preferred_element_type=F32)
           for n, o_ref in enumerate((oa_ref, ob_ref, oc_ref))]
    h = _rms(x, gpre_ref[...]).astype(BF16)
    mix = None
    for n, wg_ref in enumerate((wga_ref, wgb_ref, wgc_ref)):
        gate = jax.nn.sigmoid(jnp.dot(h, wg_ref[...], preferred_element_type=F32))
        mix = gate * ups[n] if mix is None else mix + gate * ups[n]
    x1 = x + _rms(_bdot(mix, wo_ref[...]), gpost_ref[...])
    x1_ref[...] = x1
    h2_ref[...] = _rms(x1, gfpre_ref[...]).astype(BF16)


FF_SPLIT = 6 * PAIR


def _ffn_kernel(x1_ref, h2_ref, wfi_ref, wfo_ref, gfpost_ref, y_ref):
    h2 = h2_ref[...]
    f = None
    halves = []
    for a, b in ((0, FF_SPLIT), (FF_SPLIT, D_FF)):
        gt = jnp.dot(h2, wfi_ref[:, a:b], preferred_element_type=F32)
        uf = jnp.dot(h2, wfi_ref[:, D_FF + a:D_FF + b], preferred_element_type=F32)
        halves.append((a, b, gt, uf))
    for a, b, gt, uf in halves:
        part = _bdot(_silu(gt) * uf, wfo_ref[a:b, :])
        f = part if f is None else f + part
    y_ref[...] = x1_ref[...] + _rms(f, gfpost_ref[...])


def _merge_weight_specs():
    vec = _const_spec((1, D_MODEL))
    return ([vec] + [pl.BlockSpec((D_MODEL, D_MODEL), functools.partial(lambda n, *_: (0, n), n + 1),
                                  pipeline_mode=pl.Buffered(1)) for n in range(N_BRANCH)]
            + [_const_spec((N_BRANCH, BRANCH_W, D_MODEL)), _const_spec((D_MODEL, D_MODEL)), vec, vec])


def _merge(x, oa, ob, oc, gpre, wtail, wb, wo, gpost, gfpre, tm):
    t = x.shape[0]
    row = lambda n: pl.BlockSpec((tm, n), lambda i: (i, 0))
    return pl.pallas_call(
        _merge_kernel,
        grid=(t // tm,),
        in_specs=[row(D_MODEL), row(512), row(512), row(512)] + _merge_weight_specs(),
        out_specs=[row(D_MODEL), row(D_MODEL)],
        out_shape=[jax.ShapeDtypeStruct(x.shape, F32), jax.ShapeDtypeStruct(x.shape, BF16)],
        compiler_params=_cparams(("arbitrary",)),
        name="merge",
    )(x, oa, ob, oc, gpre, wtail, wtail, wtail, wb, wo, gpost, gfpre)


def _ffn(x1, h2, wfi, wfo, gfpost, tm):
    t = x1.shape[0]
    vec = _const_spec((1, D_MODEL))
    frow = pl.BlockSpec((tm, D_MODEL), lambda i: (i, 0))
    return pl.pallas_call(
        _ffn_kernel,
        grid=(t // tm,),
        in_specs=[frow, frow, _const_spec((D_MODEL, 2 * D_FF)),
                  _const_spec((D_FF, D_MODEL)), vec],
        out_specs=frow,
        out_shape=jax.ShapeDtypeStruct(x1.shape, F32),
        compiler_params=_cparams(("arbitrary",)),
        name="ffn",
    )(x1, h2, wfi, wfo, gfpost)


def _rope_tables(pos):
    half = A_HD // 2
    inv = ROPE_THETA ** (-jnp.arange(half, dtype=F32) / half)
    ang = pos.astype(F32)[:, None] * inv[None, :]
    cos, sin = jnp.cos(ang), jnp.sin(ang)
    cos = jnp.concatenate([cos, cos], axis=-1)
    sin = jnp.concatenate([-sin, sin], axis=-1)
    return jnp.tile(cos, (1, LANES // A_HD)), jnp.tile(sin, (1, LANES // A_HD))


def _lane_row(vals, offset):
    return jnp.zeros((1, LANES), F32).at[0, offset:offset + vals.shape[0]].set(vals.astype(F32))


def kernel(x_prompt, x_sample, mem_prompt, state_win_k, state_win_v, state_conv, state_rec,
           cache_mem_k, cache_mem_v, ln_mix_pre, w_in, attn_sink, gdn_conv_w, gdn_a_log,
           gdn_dt_bias, gdn_norm_w, ln_mem, w_mem_kv, w_branch, w_out, ln_mix_post,
           ln_ffn_pre, w_ffn_in, w_ffn_out, ln_ffn_post):
    bp, lp, _ = x_prompt.shape
    bs, ls, _ = x_sample.shape

    sizes = [512, 128, 128, B_CONV_CH, B_HEADS, B_HEADS, 512, 512, N_BRANCH * D_MODEL]
    o = np.cumsum([0] + sizes)
    hperm = np.concatenate([np.r_[j * A_HD:(j + 1) * A_HD, (j + 4) * A_HD:(j + 5) * A_HD] for j in range(4)])
    ws = (w_in[:, hperm].astype(BF16), w_in[:, o[1]:o[4]].astype(BF16), w_in[:, o[6]:o[9]].astype(BF16),
          jnp.pad(w_in[:, o[4]:o[6]], ((0, 0), (0, LANES - 2 * B_HEADS))).astype(BF16))
    wb = jnp.concatenate([w_branch[0:1][:, hperm], w_branch[1:]], axis=0).astype(BF16)
    wo = w_out.astype(BF16)
    wfi = w_ffn_in.astype(BF16)
    wfo = w_ffn_out.astype(BF16)
    wmem = w_mem_kv.astype(BF16)
    sink = attn_sink.astype(F32)[np.array([0, 4, 1, 5, 2, 6, 3, 7])]
    vec = lambda g: g.astype(F32).reshape(1, -1)
    alog = _lane_row(gdn_a_log, B_HEADS)
    dtb = _lane_row(gdn_dt_bias, B_HEADS)
    cw = gdn_conv_w.astype(F32)
    nw = vec(gdn_norm_w)

    merge_w = (vec(ln_mix_pre), ws[2], wb, wo, vec(ln_mix_post), vec(ln_ffn_pre))
    ffn = lambda x1, h2: _ffn(x1, h2, wfi, wfo, vec(ln_ffn_post), 512)

    tp = bp * lp
    xp = x_prompt.reshape(tp, D_MODEL)
    cos_p, sin_p = _rope_tables(jnp.arange(lp, dtype=jnp.int32))
    qa, ka, va, z, qc, ba, qn, kn, vv, tail = _proj(xp, vec(ln_mix_pre), ws, cos_p, sin_p, 512, cw, lp)
    b3 = lambda a: a.reshape(bp, lp, a.shape[-1])
    ob, rec_p = _gdn_prompt(b3(qn), b3(kn), b3(vv), b3(ba), b3(z), alog, dtb, nw)
    mk, mv = _memkv(mem_prompt.reshape(bp * N_MEM, D_MODEL), vec(ln_mem), wmem)
    oa = _swa_prompt(sink, qa, ka, va, bp, lp, 512)
    oc = _memattn_prompt(qc, mk, mv, bp, lp, 512)
    x1, h2 = _merge(xp, oa, ob.reshape(tp, 512), oc, *merge_w, 256)
    y_p = ffn(x1, h2).reshape(bp, lp, D_MODEL)
    wk_p = ka.reshape(bp, lp, LANES)[:, -WINDOW:].reshape(bp, WINDOW, A_KV, A_HD)
    wv_p = va.reshape(bp, lp, LANES)[:, -WINDOW:].reshape(bp, WINDOW, A_KV, A_HD)
    conv_p = tail[:, -(CONV_W - 1):]
    mem_k_p = mk.reshape(bp, N_MEM, C_HEADS, C_HD)
    mem_v_p = mv.reshape(bp, N_MEM, C_HEADS, C_HD)

    ts = bs * DEC_PAD
    xs = jnp.pad(x_sample, ((0, 0), (0, DEC_PAD - ls), (0, 0))).reshape(ts, D_MODEL)
    cos_s, sin_s = _rope_tables(PAST_LEN + jnp.arange(DEC_PAD, dtype=jnp.int32))
    cos_s, sin_s = jnp.tile(cos_s, (bs, 1)), jnp.tile(sin_s, (bs, 1))
    qa, ka, va, z, qc, ba, qkv = _proj(xs, vec(ln_mix_pre), ws, cos_s, sin_s, 512)
    r3 = lambda a: a.reshape(bs, DEC_PAD, a.shape[-1])
    oa, wk_s, wv_s = _swa_decode(sink, r3(qa), r3(ka), r3(va), state_win_k.reshape(bs, WINDOW, LANES),
                                 state_win_v.reshape(bs, WINDOW, LANES), 16, ls)
    oa = oa.reshape(ts, 512)
    wk_s = wk_s.reshape(state_win_k.shape)
    wv_s = wv_s.reshape(state_win_v.shape)
    histp = jnp.pad(state_conv, ((0, 0), (DEC_PAD - (CONV_W - 1), 0), (0, 0)))
    ob, rec_s = _gdn_decode(r3(qkv), histp, ba, z, state_rec, cw, alog, dtb, nw,
                            LANES // DEC_PAD, ls)
    oc = _memattn_decode(r3(qc), cache_mem_k.reshape(bs, N_MEM * C_HEADS, C_HD),
                         cache_mem_v.reshape(bs, N_MEM * C_HEADS, C_HD), 8).reshape(ts, 512)
    real = lambda a: a.reshape(bs, DEC_PAD, 512)[:, :ls].reshape(bs * ls, 512)
    x1, h2 = _merge(x_sample.reshape(bs * ls, D_MODEL), real(oa), real(ob), real(oc), *merge_w, 256)
    y_s = ffn(x1, h2).reshape(bs, ls, D_MODEL)
    conv_s = r3(qkv)[:, ls - (CONV_W - 1):ls]

    return (y_p, y_s, wk_p, wv_p, conv_p, rec_p, mem_k_p, mem_v_p, wk_s, wv_s, conv_s, rec_s)
```

```python
import functools

import numpy as np
import jax
import jax.numpy as jnp
from jax import lax
from jax.experimental import pallas as pl
from jax.experimental.pallas import tpu as pltpu

F32 = jnp.float32
BF16 = jnp.bfloat16

D_MODEL = 1024
PAST_LEN = 16384
EPS = 1e-6
ROPE_THETA = 10000.0
N_MEM = 256
WINDOW = 128
A_HD = 64
A_HEADS = 8
A_KV = 2
A_SCALE = A_HD ** -0.5
B_HEADS = 4
B_DK = 128
B_DV = 128
CONV_W = 4
GDN_CHUNK = 64
B_CONV_CH = B_HEADS * (2 * B_DK + B_DV)
C_HEADS = 4
C_HD = 128
N_BRANCH = 3
BRANCH_W = 512
D_FF = 2816

LANES = 128
SUBLANES = 8
PAIR = 2 * LANES
VMEM_LIMIT = 56 * 1024 * 1024
NEG = -1e30
DEC_PAD = SUBLANES


def _cparams(sem, vmem=VMEM_LIMIT):
    return pltpu.CompilerParams(dimension_semantics=sem, vmem_limit_bytes=vmem)


def _const_spec(shape):
    nd = len(shape)
    return pl.BlockSpec(shape, lambda *_: (0,) * nd, pipeline_mode=pl.Buffered(1))


def _rms(x, g):
    ms = jnp.mean(x * x, axis=-1, keepdims=True)
    return x * lax.rsqrt(ms + EPS) * g


def _bdot(a, b):
    return jnp.dot(a.astype(BF16), b.astype(BF16), preferred_element_type=F32)


def _bdot_nt(a, b):
    return lax.dot_general(a.astype(BF16), b.astype(BF16), (((1,), (1,)), ((), ())),
                           preferred_element_type=F32)


def _bdot_tn(a, b):
    return lax.dot_general(a.astype(BF16), b.astype(BF16), (((0,), (0,)), ((), ())),
                           preferred_element_type=F32)


def _hdot(a, b):
    return jnp.dot(a, b, precision=lax.Precision.HIGHEST, preferred_element_type=F32)


def _silu(x):
    return x * jax.nn.sigmoid(x)


def _softplus(x):
    return jnp.maximum(x, 0.0) + jnp.log1p(jnp.exp(-jnp.abs(x)))


def _rope128(v, cos, sin):
    lane = lax.broadcasted_iota(jnp.int32, v.shape, 1)
    fwd = pltpu.roll(v, 32, 1)
    bwd = pltpu.roll(v, 96, 1)
    sw = jnp.where((lane & 32) == 0, bwd, fwd)
    return v * cos + sw * sin


def _l2n(x):
    return x * lax.rsqrt(jnp.sum(x * x, axis=-1, keepdims=True) + EPS)

def _proj_steps(x_ref, g_ref, wq_ref, wr_ref, wzc_ref, wba_ref, cos_ref, sin_ref,
                qa_ref, ka_ref, va_ref, z_ref, qc_ref, ba_ref):
    h = _rms(x_ref[...], g_ref[...]).astype(BF16)
    cos = cos_ref[...]
    sin = sin_ref[...]

    def mm(w_ref, a, b):
        return jnp.dot(h, w_ref[:, a:b], preferred_element_type=F32)

    def qa_half(c0):
        q = mm(wq_ref, c0 * LANES, (c0 + 2) * LANES)
        for c in range(2):
            qa_ref[:, (c0 + c) * LANES:(c0 + c + 1) * LANES] = (
                _rope128(q[:, c * LANES:(c + 1) * LANES], cos, sin) * A_SCALE).astype(BF16)

    def kv():
        kv2 = mm(wr_ref, 0, PAIR)
        ka_ref[...] = _rope128(kv2[:, :LANES], cos, sin)
        va_ref[...] = kv2[:, LANES:]

    def z_half(c0):
        z_ref[:, c0:c0 + PAIR] = mm(wzc_ref, c0, c0 + PAIR)

    def qc_half(c0):
        qc_ref[:, c0:c0 + PAIR] = mm(wzc_ref, 512 + c0, 512 + c0 + PAIR).astype(BF16)

    def ba():
        ba_ref[...] = mm(wba_ref, 0, LANES)

    def qkv(a, b):
        return mm(wr_ref, PAIR + a, PAIR + b)

    steps = [lambda: qa_half(0), lambda: qa_half(2), kv, lambda: z_half(0), lambda: z_half(PAIR),
             lambda: qc_half(0), lambda: qc_half(PAIR), ba]
    return qkv, steps


def _proj_raw_kernel(x_ref, g_ref, wq_ref, wr_ref, wzc_ref, wba_ref, cos_ref, sin_ref,
                     qa_ref, ka_ref, va_ref, z_ref, qc_ref, ba_ref, qkv_ref):
    qkv, steps = _proj_steps(x_ref, g_ref, wq_ref, wr_ref, wzc_ref, wba_ref, cos_ref, sin_ref,
                             qa_ref, ka_ref, va_ref, z_ref, qc_ref, ba_ref)
    qkv_ref[...] = qkv(0, B_CONV_CH)
    for step in steps:
        step()


def _proj_conv_kernel(x_ref, g_ref, wq_ref, wr_ref, wzc_ref, wba_ref, cos_ref, sin_ref, cw_ref,
                      qa_ref, ka_ref, va_ref, z_ref, qc_ref, ba_ref, qn_ref, kn_ref, vv_ref, tail_ref,
                      buf_ref, *, tm, tiles_per_seq):
    hist = SUBLANES
    first = lax.rem(pl.program_id(0), tiles_per_seq) == 0

    @pl.when(first)
    def _():
        buf_ref[0:hist, :] = jnp.zeros((hist, B_CONV_CH), F32)

    @pl.when(jnp.logical_not(first))
    def _():
        buf_ref[0:hist, :] = buf_ref[tm:tm + hist, :]

    qkv, steps = _proj_steps(x_ref, g_ref, wq_ref, wr_ref, wzc_ref, wba_ref, cos_ref, sin_ref,
                             qa_ref, ka_ref, va_ref, z_ref, qc_ref, ba_ref)
    nq = B_HEADS * B_DK

    def conv_group(c0):
        cs = slice(c0, c0 + PAIR)
        raw = qkv(c0, c0 + PAIR)
        buf_ref[hist:hist + tm, cs] = raw
        tail_ref[0, :, cs] = raw[tm - hist:, :]
        xb = buf_ref[:, cs]
        acc = xb * cw_ref[0:1, cs]
        for i in range(1, CONV_W):
            acc = pltpu.roll(acc, 1, 0) + xb * cw_ref[i:i + 1, cs]
        act = _silu(acc[hist:])
        if c0 >= 2 * nq:
            vv_ref[:, c0 - 2 * nq:c0 - 2 * nq + PAIR] = act
            return
        out_ref, base, scale = (qn_ref, 0, B_DK ** -0.5) if c0 < nq else (kn_ref, nq, 1.0)
        for s in range(2):
            o0 = c0 - base + s * B_DK
            out_ref[:, o0:o0 + B_DK] = _l2n(act[:, s * B_DK:(s + 1) * B_DK]) * scale

    groups = [functools.partial(conv_group, c0) for c0 in range(0, B_CONV_CH, PAIR)]
    while groups or steps:
        if groups:
            groups.pop(0)()
        if steps:
            steps.pop(0)()


_PROJ_OUTS = [(512, BF16), (128, F32), (128, F32), (512, F32), (512, BF16), (128, F32)]


def _proj(x, gain, ws, cos, sin, tm, cw=None, seq=None):
    wq, wr, wtail, wba = ws
    t = x.shape[0]
    ntab = cos.shape[0] // tm
    row = lambda n: pl.BlockSpec((tm, n), lambda i: (i, 0))
    tab = pl.BlockSpec((tm, LANES), lambda i: (i % ntab, 0))
    in_specs = [row(D_MODEL), _const_spec((1, D_MODEL)), _const_spec(wq.shape), _const_spec(wr.shape),
                _const_spec((D_MODEL, D_MODEL)), _const_spec(wba.shape), tab, tab]
    out_specs = [row(n) for n, _ in _PROJ_OUTS]
    out_shape = [jax.ShapeDtypeStruct((t, n), d) for n, d in _PROJ_OUTS]
    if cw is None:
        return pl.pallas_call(
            _proj_raw_kernel, grid=(t // tm,), in_specs=in_specs,
            out_specs=out_specs + [row(B_CONV_CH)],
            out_shape=out_shape + [jax.ShapeDtypeStruct((t, B_CONV_CH), F32)],
            compiler_params=_cparams(("arbitrary",)), name="proj",
        )(x, gain, wq, wr, wtail, wba, cos, sin)
    tiles = seq // tm
    return pl.pallas_call(
        functools.partial(_proj_conv_kernel, tm=tm, tiles_per_seq=tiles),
        grid=(t // tm,),
        in_specs=in_specs + [_const_spec((CONV_W, B_CONV_CH))],
        out_specs=out_specs + [row(512)] * 3
        + [pl.BlockSpec((1, SUBLANES, B_CONV_CH), lambda i: (i // tiles, 0, 0))],
        out_shape=out_shape + [jax.ShapeDtypeStruct((t, 512), F32)] * 3
        + [jax.ShapeDtypeStruct((t // seq, SUBLANES, B_CONV_CH), F32)],
        scratch_shapes=[pltpu.VMEM((tm + SUBLANES, B_CONV_CH), F32)],
        compiler_params=_cparams(("arbitrary",)), name="proj_conv",
    )(x, gain, wq, wr, wtail, wba, cos, sin, cw)


def _swa_scores(q, k16):
    tq = q.shape[0]
    lo = lax.broadcasted_iota(jnp.int32, (tq, LANES), 1) < A_HD
    blocks = []
    for j in range(4):
        c = q[:, j * LANES:(j + 1) * LANES].astype(F32)
        blocks.append(jnp.where(lo, c, 0.0))
        blocks.append(jnp.where(lo, 0.0, c))
    lhs = jnp.concatenate(blocks, axis=0).astype(BF16)
    return lax.dot_general(lhs, k16, (((1,), (1,)), ((), ())), preferred_element_type=F32)


def _swa_finish(logits, v16, mask, sink_ref):
    tq = logits.shape[0] // A_HEADS
    lo = lax.broadcasted_iota(jnp.int32, (tq, LANES), 1) < A_HD
    es, inv = [], []
    for s in range(8):
        l = jnp.where(mask, logits[s * tq:(s + 1) * tq], NEG)
        sk = sink_ref[s]
        m = jnp.maximum(jnp.max(l, axis=-1, keepdims=True), sk)
        e = jnp.exp(l - m)
        den = jnp.sum(e, axis=-1, keepdims=True) + jnp.exp(sk - m)
        es.append(e.astype(BF16))
        inv.append(1.0 / den)
    pv = jnp.dot(jnp.concatenate(es, axis=0), v16, preferred_element_type=F32)
    outs = []
    for j in range(4):
        a = pv[(2 * j) * tq:(2 * j + 1) * tq] * inv[2 * j]
        b = pv[(2 * j + 1) * tq:(2 * j + 2) * tq] * inv[2 * j + 1]
        outs.append(jnp.where(lo, a, b))
    return jnp.concatenate(outs, axis=1)


def _swa_prompt_kernel(sink_ref, q_ref, kc_ref, kp_ref, vc_ref, vp_ref, o_ref, *, nblk):
    i = pl.program_id(1)
    kcat = jnp.concatenate([kp_ref[...], kc_ref[...]], axis=0).astype(BF16)
    vcat = jnp.concatenate([vp_ref[...], vc_ref[...]], axis=0).astype(BF16)
    ii = lax.broadcasted_iota(jnp.int32, (WINDOW, 2 * WINDOW), 0)
    jj = lax.broadcasted_iota(jnp.int32, (WINDOW, 2 * WINDOW), 1)
    band = (jj > ii) & (jj <= ii + WINDOW)
    for jb in range(nblk):
        mask = band
        if jb == 0:
            mask = band & ((jj >= WINDOW) | (i > 0))
        logits = _swa_scores(q_ref[jb * WINDOW:(jb + 1) * WINDOW, :], kcat[jb * WINDOW:(jb + 2) * WINDOW])
        o = _swa_finish(logits, vcat[jb * WINDOW:(jb + 2) * WINDOW], mask, sink_ref)
        o_ref[jb * WINDOW:(jb + 1) * WINDOW, :] = o.astype(BF16)


def _swa_prompt(sink, qa, ka, va, batch, seq, tq):
    nq = seq // tq
    nblk = tq // WINDOW
    nw = seq // WINDOW
    cur = lambda n: pl.BlockSpec((tq, n), lambda b, i: (b * nq + i, 0))
    prev = pl.BlockSpec((WINDOW, LANES), lambda b, i: (jnp.maximum(b * nw + i * nblk - 1, 0), 0))
    return pl.pallas_call(
        functools.partial(_swa_prompt_kernel, nblk=nblk),
        grid=(batch, nq),
        in_specs=[pl.BlockSpec(memory_space=pltpu.SMEM), cur(512), cur(LANES), prev, cur(LANES), prev],
        out_specs=cur(512),
        out_shape=jax.ShapeDtypeStruct(qa.shape, BF16),
        compiler_params=_cparams(("arbitrary", "arbitrary")),
        name="swa_prompt",
    )(sink, qa, ka, ka, va, va)


ROW_UNROLL = 8


def _rows_loop(rows, load, stages, store):
    def body(g, carry):
        idx = [g * ROW_UNROLL + j for j in range(ROW_UNROLL)]
        vals = [load(r) for r in idx]
        for stage in stages:
            vals = [stage(v) for v in vals]
        for r, o in zip(idx, vals):
            store(r, o)
        return carry

    lax.fori_loop(0, rows // ROW_UNROLL, body, 0)


def _swa_decode_kernel(sink_ref, q_ref, kn_ref, vn_ref, wk_ref, wv_ref, o_ref, wko_ref, wvo_ref,
                       *, rows, new_len):
    ii = lax.broadcasted_iota(jnp.int32, (DEC_PAD, 2 * WINDOW), 0)
    jj = lax.broadcasted_iota(jnp.int32, (DEC_PAD, 2 * WINDOW), 1)
    mask = ((jj < WINDOW) & (jj > ii)) | ((jj >= WINDOW) & (jj - WINDOW <= ii))
    pad = jnp.zeros((WINDOW - DEC_PAD, LANES), F32)
    keep = lax.broadcasted_iota(jnp.int32, (WINDOW, LANES), 0) < WINDOW - new_len

    def slide(win, new):
        tail = jnp.concatenate([pad, new], axis=0)
        return jnp.where(keep, pltpu.roll(win, WINDOW - new_len, 0), pltpu.roll(tail, DEC_PAD - new_len, 0))

    def load(r):
        return q_ref[r], wk_ref[r], kn_ref[r], wv_ref[r], vn_ref[r]

    def scores(args):
        q, wk, kn, wv, vn = args
        kk = jnp.concatenate([wk, kn, pad], axis=0).astype(BF16)
        return _swa_scores(q, kk), wk, kn, wv, vn

    def finish(args):
        logits, wk, kn, wv, vn = args
        vv = jnp.concatenate([wv, vn, pad], axis=0).astype(BF16)
        return _swa_finish(logits, vv, mask, sink_ref).astype(BF16), slide(wk, kn), slide(wv, vn)

    def store(r, outs):
        o_ref[r], wko_ref[r], wvo_ref[r] = outs

    _rows_loop(rows, load, [scores, finish], store)


def _swa_decode(sink, qa, kn, vn, wk, wv, rows, new_len):
    nb = qa.shape[0]
    blk = lambda a, n: pl.BlockSpec((rows, a, n), lambda i: (i, 0, 0))
    return pl.pallas_call(
        functools.partial(_swa_decode_kernel, rows=rows, new_len=new_len),
        grid=(nb // rows,),
        in_specs=[pl.BlockSpec(memory_space=pltpu.SMEM), blk(DEC_PAD, 512), blk(DEC_PAD, LANES),
                  blk(DEC_PAD, LANES), blk(WINDOW, LANES), blk(WINDOW, LANES)],
        out_specs=[blk(DEC_PAD, 512), blk(WINDOW, LANES), blk(WINDOW, LANES)],
        out_shape=[jax.ShapeDtypeStruct(qa.shape, BF16), jax.ShapeDtypeStruct(wk.shape, F32),
                   jax.ShapeDtypeStruct(wv.shape, F32)],
        compiler_params=_cparams(("arbitrary",)),
        name="swa_decode",
    )(sink, qa, kn, vn, wk, wv)


def _gdn_gates(ba, alog, dtb, valid):
    beta = jax.nn.sigmoid(ba)
    g = -jnp.exp(alog) * _softplus(ba + dtb)
    if valid is not None:
        beta = jnp.where(valid, beta, 0.0)
        g = jnp.where(valid, g, 0.0)
    return beta, g


def _chunk_masks(chunk):
    sh = chunk.bit_length() - 1
    ri = lax.broadcasted_iota(jnp.int32, (LANES, LANES), 0)
    ci = lax.broadcasted_iota(jnp.int32, (LANES, LANES), 1)
    same = (ri >> sh) == (ci >> sh)
    return same, same & (ri >= ci), same & (ri > ci), ri == ci


def _gdn_cumsums(g_all, chunk):
    same, tri, _, _ = _chunk_masks(chunk)
    lower = jnp.where(tri, 1.0, 0.0)
    gcol = _hdot(lower, g_all)
    grow = _hdot(g_all.T, lower.T)
    gtot = _hdot(jnp.where(same, 1.0, 0.0), g_all)
    return gcol, grow, gtot


def _sibling_mask(ri, ci, lvl):
    rb = ri >> lvl
    return ((rb & 1) == 1) & ((ci >> lvl) == rb - 1)


def _gdn_phase_a(heads, chunk):
    _, tri, strict, _ = _chunk_masks(chunk)
    ri = lax.broadcasted_iota(jnp.int32, (LANES, LANES), 0)
    ci = lax.broadcasted_iota(jnp.int32, (LANES, LANES), 1)
    kbs = [k * beta for q, k, v, beta, gcol, grow, gtot in heads]
    kkqk = [_bdot_nt(jnp.concatenate([kb, hd[0]], axis=0), hd[1]) for kb, hd in zip(kbs, heads)]
    a, qk = [], []
    for r, (q, k, v, beta, gcol, grow, gtot) in zip(kkqk, heads):
        decay = jnp.exp(jnp.where(tri, gcol - grow, NEG))
        a.append(jnp.where(strict, r[:LANES] * decay, 0.0))
        qk.append(r[LANES:] * decay)
    n = [-jnp.where(_sibling_mask(ri, ci, 0), ah, 0.0) for ah in a]
    for lvl in range(1, chunk.bit_length() - 1):
        sib = _sibling_mask(ri, ci, lvl)
        al = [jnp.where(sib, ah, 0.0) for ah in a]
        x = [alh + _bdot(nh, alh) for alh, nh in zip(al, n)]
        n = [nh - (xh + _bdot(xh, nh)) for xh, nh in zip(x, n)]
    outs = []
    rhs = [jnp.concatenate([hd[2] * hd[3], kb * jnp.exp(hd[4])], axis=1) for kb, hd in zip(kbs, heads)]
    sol = [r + _bdot(nh, r) for r, nh in zip(rhs, n)]
    for s, qkh, (q, k, v, beta, gcol, grow, gtot) in zip(sol, qk, heads):
        outs.append((s[:, :B_DV], s[:, B_DV:], q * jnp.exp(gcol), k * jnp.exp(gtot - gcol), qkh, jnp.exp(gtot)))
    return outs


def _gdn_qkv_heads(qkv, h):
    q = qkv[:, h * B_DK:(h + 1) * B_DK]
    k = qkv[:, (B_HEADS + h) * B_DK:(B_HEADS + h + 1) * B_DK]
    v = qkv[:, (2 * B_HEADS + h) * B_DK:(2 * B_HEADS + h + 1) * B_DK]
    q = q * lax.rsqrt(jnp.sum(q * q, axis=-1, keepdims=True) + EPS) * (B_DK ** -0.5)
    k = k * lax.rsqrt(jnp.sum(k * k, axis=-1, keepdims=True) + EPS)
    return q, k, v


def _gdn_out(o, z, nw):
    return _rms(o, nw) * _silu(z)


def _pair_bd(x2):
    lo = lax.broadcasted_iota(jnp.int32, x2.shape, 1) < LANES
    return jnp.concatenate([jnp.where(lo, x2, 0.0), jnp.where(lo, 0.0, x2)], axis=0).astype(BF16)


def _quad_bd(x4):
    blk = lax.broadcasted_iota(jnp.int32, x4.shape, 1) >> (GDN_CHUNK.bit_length() - 1)
    return jnp.concatenate([jnp.where(blk == j, x4, 0.0) for j in range(PAIR // GDN_CHUNK)],
                           axis=0).astype(BF16)


def _pair_cols(x, l0, l1):
    lo = lax.broadcasted_iota(jnp.int32, (x.shape[0], PAIR), 1) < LANES
    return jnp.where(lo, x[:, l0:l0 + 1], x[:, l1:l1 + 1])


def _gdn_prompt_kernel(q_ref, k_ref, v_ref, ba_ref, z_ref, alog_ref, dtb_ref, nw_ref,
                       ob_ref, s_ref, sbd_ref, *, nb, ntb):
    t = pl.program_id(0)
    npair = B_HEADS // 2

    @pl.when(t == 0)
    def _():
        sbd_ref[...] = jnp.zeros(sbd_ref.shape, F32)

    alog = alog_ref[...]
    dtb = dtb_ref[...]
    nw = nw_ref[...]
    nchunk = LANES // GDN_CHUNK
    _, tri, strict, _ = _chunk_masks(GDN_CHUNK)
    tri2 = jnp.concatenate([tri, tri], axis=1)
    strict2 = jnp.concatenate([strict, strict], axis=1)

    items = []
    for b, tb in [(b, tb) for b in range(nb) for tb in range(ntb)]:
        rsl = slice(tb * LANES, (tb + 1) * LANES)
        beta_all, g_all = _gdn_gates(ba_ref[b, rsl, :], alog, dtb, None)
        gcol_all, grow_all, gtot_all = _gdn_cumsums(g_all, GDN_CHUNK)
        eg_all = jnp.exp(gcol_all)
        ed_all = jnp.exp(gtot_all - gcol_all)
        et_all = jnp.exp(gtot_all)
        for p in range(npair):
            h0, h1 = 2 * p, 2 * p + 1
            g0, g1 = B_HEADS + h0, B_HEADS + h1
            psl = slice(p * PAIR, (p + 1) * PAIR)
            q2 = q_ref[b, rsl, psl]
            k2 = k_ref[b, rsl, psl]
            v2 = v_ref[b, rsl, psl]
            beta2 = _pair_cols(beta_all, h0, h1)
            grow2 = jnp.concatenate([grow_all[g0:g0 + 1, :], grow_all[g1:g1 + 1, :]], axis=1)
            decay2 = jnp.exp(jnp.where(tri2, _pair_cols(gcol_all, g0, g1) - grow2, NEG))
            kb2 = k2 * beta2
            items.append(dict(
                b=b, tb=tb, p=p, rsl=rsl, decay2=decay2, kb2=kb2, v2b=v2 * beta2,
                kbe2=kb2 * _pair_cols(eg_all, g0, g1),
                lhs=jnp.concatenate([kb2, q2], axis=0).astype(BF16),
                kbd=_pair_bd(k2),
                qe2=(q2 * _pair_cols(eg_all, g0, g1)).astype(BF16),
                kd2=(k2 * _pair_cols(ed_all, g0, g1)).astype(BF16),
                et2=_pair_cols(et_all, g0, g1)))

    for it in items:
        kkqk = lax.dot_general(it["lhs"], it["kbd"], (((1,), (1,)), ((), ())),
                               preferred_element_type=F32)
        it["a2"] = jnp.where(strict2, kkqk[:LANES] * it["decay2"], 0.0)
        it["qk2"] = (kkqk[LANES:] * it["decay2"]).astype(BF16)
    qlane = lax.broadcasted_iota(jnp.int32, (GDN_CHUNK, PAIR), 1)
    qrow = lax.broadcasted_iota(jnp.int32, (GDN_CHUNK, PAIR), 0)
    qcol = qlane & (GDN_CHUNK - 1)
    top = (qlane & GDN_CHUNK) == 0
    for it in items:
        it["a64"] = it["a2"][:GDN_CHUNK] + it["a2"][GDN_CHUNK:]
        it["n"] = -jnp.where(_sibling_mask(qrow, qcol, 0), it["a64"], 0.0)
    for lvl in range(1, GDN_CHUNK.bit_length() - 1):
        sib = _sibling_mask(qrow, qcol, lvl)
        for it in items:
            it["al"] = jnp.where(sib, it["a64"], 0.0)
            it["x"] = it["al"] + jnp.dot(it["n"].astype(BF16), _quad_bd(it["al"]), preferred_element_type=F32)
        for it in items:
            x = it["x"]
            it["n"] = it["n"] - (x + jnp.dot(x.astype(BF16), _quad_bd(it["n"]), preferred_element_type=F32))
    for it in items:
        n64 = it["n"]
        it["n"] = jnp.concatenate([jnp.where(top, n64, 0.0), jnp.where(top, 0.0, n64)], axis=0)
    for it in items:
        us, ws = [], []
        for s in range(2):
            sl = slice(s * LANES, (s + 1) * LANES)
            rhs = jnp.concatenate([it["v2b"][:, sl], it["kbe2"][:, sl]], axis=1)
            sol = rhs + _bdot(it["n"][:, sl], rhs)
            us.append(sol[:, :B_DV])
            ws.append(sol[:, B_DV:])
        it["u2"] = jnp.concatenate(us, axis=1)
        it["w2"] = jnp.concatenate(ws, axis=1).astype(BF16)

    ri = lax.broadcasted_iota(jnp.int32, (PAIR, PAIR), 0) < LANES
    ci = lax.broadcasted_iota(jnp.int32, (PAIR, PAIR), 1) < LANES
    bdmask = ri == ci
    nchain = nb * npair
    state = [sbd_ref[i] for i in range(nchain)]
    outs = {}
    for tb in range(ntb):
        cur = [(it["b"] * npair + it["p"], i, it) for i, it in enumerate(items) if it["tb"] == tb]
        for c in range(nchunk):
            rows = slice(c * GDN_CHUNK, (c + 1) * GDN_CHUNK)
            rs = [jnp.dot(jnp.concatenate([it["w2"][rows], it["qe2"][rows]], axis=0), state[ch].astype(BF16),
                          preferred_element_type=F32) for ch, _, it in cur]
            for (ch, i, it), r in zip(cur, rs):
                s = state[ch]
                vn2 = it["u2"][rows] - r[:GDN_CHUNK]
                vt = _pair_bd(jnp.concatenate([vn2] * nchunk, axis=0))
                outs[(i, c)] = r[GDN_CHUNK:] + jnp.dot(it["qk2"][rows], vt, preferred_element_type=F32)
                upd = lax.dot_general(it["kd2"][rows], vn2.astype(BF16), (((0,), (0,)), ((), ())),
                                      preferred_element_type=F32)
                state[ch] = s * it["et2"][c * GDN_CHUNK:c * GDN_CHUNK + 1, :] + jnp.where(bdmask, upd, 0.0)
    for ch in range(nchain):
        sbd_ref[ch] = state[ch]

    @pl.when(t == pl.num_programs(0) - 1)
    def _():
        for ch in range(nchain):
            b, p = divmod(ch, npair)
            s_ref[b, 2 * p] = state[ch][:LANES, :LANES]
            s_ref[b, 2 * p + 1] = state[ch][LANES:, LANES:]

    for i, it in enumerate(items):
        o2 = jnp.concatenate([outs[(i, c)] for c in range(nchunk)], axis=0)
        for s in range(2):
            h = 2 * it["p"] + s
            sl = slice(h * B_DV, (h + 1) * B_DV)
            o = o2[:, s * LANES:(s + 1) * LANES]
            ob_ref[it["b"], it["rsl"], sl] = _gdn_out(o, z_ref[it["b"], it["rsl"], sl], nw).astype(BF16)


GDN_TOKEN_BLOCKS = 2


def _gdn_prompt(q, k, v, ba, z, alog, dtb, nw):
    batch, seq, _ = q.shape
    tm = GDN_TOKEN_BLOCKS * LANES
    tok = lambda n: pl.BlockSpec((batch, tm, n), lambda t: (0, t, 0))
    state = pl.BlockSpec((batch, B_HEADS, B_DK, B_DV), lambda t: (0, 0, 0, 0))
    return pl.pallas_call(
        functools.partial(_gdn_prompt_kernel, nb=batch, ntb=GDN_TOKEN_BLOCKS),
        grid=(seq // tm,),
        in_specs=[tok(512), tok(512), tok(512), tok(LANES), tok(512),
                  _const_spec((1, LANES)), _const_spec((1, LANES)), _const_spec((1, B_DV))],
        out_specs=[tok(512), state],
        out_shape=[jax.ShapeDtypeStruct((batch, seq, 512), BF16),
                   jax.ShapeDtypeStruct((batch, B_HEADS, B_DK, B_DV), F32)],
        scratch_shapes=[pltpu.VMEM((batch * (B_HEADS // 2), PAIR, PAIR), F32)],
        compiler_params=_cparams(("arbitrary",)),
        name="gdn_prompt",
    )(q, k, v, ba, z, alog, dtb, nw)


def _gdn_decode_kernel(raw_ref, hist_ref, ba_ref, z_ref, rec_ref, cw_ref, alog_ref, dtb_ref, nw_ref,
                       ob_ref, s_ref, buf_ref, u_s, w_s, qe_s, kd_s, qk_s, eg_s, o_s, *, rows, valid_len):
    buf_ref[:, 0:DEC_PAD, :] = hist_ref[...]
    buf_ref[:, DEC_PAD:2 * DEC_PAD, :] = raw_ref[...]
    conv = None
    for i in range(CONV_W):
        off = DEC_PAD - (CONV_W - 1) + i
        term = buf_ref[:, off:off + DEC_PAD, :] * cw_ref[i:i + 1, :]
        conv = term if conv is None else conv + term
    qkv = _silu(conv.reshape(rows * DEC_PAD, B_CONV_CH))
    tok = lax.broadcasted_iota(jnp.int32, (LANES, LANES), 0) & (DEC_PAD - 1)
    beta_all, g_all = _gdn_gates(ba_ref[...], alog_ref[...], dtb_ref[...], tok < valid_len)
    gcol_all, grow_all, gtot_all = _gdn_cumsums(g_all, DEC_PAD)
    heads = []
    for h in range(B_HEADS):
        gl = B_HEADS + h
        heads.append(_gdn_qkv_heads(qkv, h) + (beta_all[:, h:h + 1], gcol_all[:, gl:gl + 1],
                                               grow_all[gl:gl + 1, :], gtot_all[:, gl:gl + 1]))
    for h, (u, w, qe, kd, qk, egt) in enumerate(_gdn_phase_a(heads, DEC_PAD)):
        u_s[h] = u
        w_s[h] = w
        qe_s[h] = qe
        kd_s[h] = kd
        qk_s[h] = qk
        eg_s[h] = jnp.broadcast_to(egt, (LANES, LANES))

    def load(r):
        r0 = pl.multiple_of(r * DEC_PAD, DEC_PAD)
        rr = pl.ds(r0, DEC_PAD)
        return [(rec_ref[r, h], w_s[h, rr, :], qe_s[h, rr, :], u_s[h, rr, :], qk_s[h, rr, :],
                 kd_s[h, rr, :], eg_s[h, pl.ds(r0, 1), :]) for h in range(B_HEADS)]

    def read_state(heads):
        return [(_bdot(jnp.concatenate([w, qe], axis=0), s), s, u, qk, kd, eg)
                for s, w, qe, u, qk, kd, eg in heads]

    def update(heads):
        outs = []
        for res, s, u, qk, kd, eg in heads:
            v_new = u - res[:DEC_PAD]
            vt = jnp.concatenate([v_new] * (LANES // DEC_PAD), axis=0)
            outs.append((res[DEC_PAD:] + _bdot(qk, vt), s * eg + _bdot_tn(kd, v_new)))
        return outs

    def store(r, outs):
        rr = pl.ds(pl.multiple_of(r * DEC_PAD, DEC_PAD), DEC_PAD)
        for h, (o, s_new) in enumerate(outs):
            o_s[h, rr, :] = o
            s_ref[r, h] = s_new

    _rows_loop(rows, load, [read_state, update], store)
    nw = nw_ref[...]
    for h in range(B_HEADS):
        sl = slice(h * B_DV, (h + 1) * B_DV)
        ob_ref[:, sl] = _gdn_out(o_s[h], z_ref[:, sl], nw).astype(BF16)


def _gdn_decode(raw, histp, ba, z, rec, cw, alog, dtb, nw, rows, valid_len):
    nb = raw.shape[0]
    flat = rows * DEC_PAD
    assert flat == LANES
    sq = lambda: pltpu.VMEM((B_HEADS, LANES, LANES), F32)
    return pl.pallas_call(
        functools.partial(_gdn_decode_kernel, rows=rows, valid_len=valid_len),
        grid=(nb // rows,),
        in_specs=[pl.BlockSpec((rows, DEC_PAD, B_CONV_CH), lambda i: (i, 0, 0)),
                  pl.BlockSpec((rows, DEC_PAD, B_CONV_CH), lambda i: (i, 0, 0)),
                  pl.BlockSpec((flat, LANES), lambda i: (i, 0)),
                  pl.BlockSpec((flat, 512), lambda i: (i, 0)),
                  pl.BlockSpec((rows, B_HEADS, B_DK, B_DV), lambda i: (i, 0, 0, 0)),
                  _const_spec((CONV_W, B_CONV_CH)), _const_spec((1, LANES)), _const_spec((1, LANES)),
                  _const_spec((1, B_DV))],
        out_specs=[pl.BlockSpec((flat, 512), lambda i: (i, 0)),
                   pl.BlockSpec((rows, B_HEADS, B_DK, B_DV), lambda i: (i, 0, 0, 0))],
        out_shape=[jax.ShapeDtypeStruct((nb * DEC_PAD, 512), BF16),
                   jax.ShapeDtypeStruct(rec.shape, F32)],
        scratch_shapes=[pltpu.VMEM((rows, 2 * DEC_PAD, B_CONV_CH), F32)] + [sq() for _ in range(7)],
        compiler_params=_cparams(("arbitrary",)),
        name="gdn_decode",
    )(raw, histp, ba, z, rec, cw, alog, dtb, nw)


def _memkv_kernel(m_ref, g_ref, w_ref, k_ref, v_ref):
    h = _rms(m_ref[...], g_ref[...]).astype(BF16)
    n = C_HEADS * C_HD
    k_ref[...] = jnp.dot(h, w_ref[:, :n], preferred_element_type=F32)
    v_ref[...] = jnp.dot(h, w_ref[:, n:], preferred_element_type=F32)


def _memkv(mem, gain, w):
    t = mem.shape[0]
    tm = 512
    n = C_HEADS * C_HD
    return pl.pallas_call(
        _memkv_kernel,
        grid=(t // tm,),
        in_specs=[pl.BlockSpec((tm, D_MODEL), lambda i: (i, 0)), _const_spec((1, D_MODEL)),
                  _const_spec((D_MODEL, 2 * n))],
        out_specs=[pl.BlockSpec((tm, n), lambda i: (i, 0))] * 2,
        out_shape=[jax.ShapeDtypeStruct((t, n), F32)] * 2,
        compiler_params=_cparams(("arbitrary",)),
        name="memkv",
    )(mem, gain, w)


def _softmax_rows(logits):
    m = jnp.max(logits, axis=-1, keepdims=True)
    e = jnp.exp(logits - m)
    return e, 1.0 / jnp.sum(e, axis=-1, keepdims=True)


def _memattn_prompt_kernel(q_ref, k_ref, v_ref, o_ref):
    hs = lambda h: slice(h * C_HD, (h + 1) * C_HD)
    scores = lambda h: _bdot_nt(q_ref[:, hs(h)], k_ref[:, hs(h)])
    logits = scores(0)
    for h in range(C_HEADS):
        nxt = scores(h + 1) if h + 1 < C_HEADS else None
        e, inv = _softmax_rows(logits * (C_HD ** -0.5))
        o_ref[:, hs(h)] = (_bdot(e, v_ref[:, hs(h)]) * inv).astype(BF16)
        logits = nxt


def _memattn_prompt(qc, mk, mv, batch, seq, tm):
    nq = seq // tm
    n = C_HEADS * C_HD
    cur = pl.BlockSpec((tm, n), lambda b, i: (b * nq + i, 0))
    mem = pl.BlockSpec((N_MEM, n), lambda b, i: (b, 0))
    return pl.pallas_call(
        _memattn_prompt_kernel,
        grid=(batch, nq),
        in_specs=[cur, mem, mem],
        out_specs=cur,
        out_shape=jax.ShapeDtypeStruct(qc.shape, BF16),
        compiler_params=_cparams(("arbitrary", "arbitrary")),
        name="memattn_prompt",
    )(qc, mk, mv)


def _memattn_decode_kernel(q_ref, k_ref, v_ref, o_ref, *, rows):
    nk = N_MEM * C_HEADS
    col = lax.broadcasted_iota(jnp.int32, (C_HEADS * DEC_PAD, nk), 1)
    row = lax.broadcasted_iota(jnp.int32, (C_HEADS * DEC_PAD, nk), 0)
    own = (col & (C_HEADS - 1)) == (row >> (DEC_PAD.bit_length() - 1))

    def load(r):
        return q_ref[r], k_ref[r], v_ref[r]

    def scores(args):
        q, k, v = args
        q = q.astype(F32)
        lhs = jnp.concatenate([q[:, h * C_HD:(h + 1) * C_HD] for h in range(C_HEADS)], axis=0)
        return _bdot_nt(lhs, k), v

    def finish(args):
        logits, v = args
        e, inv = _softmax_rows(jnp.where(own, logits * (C_HD ** -0.5), NEG))
        pv = _bdot(e, v) * inv
        return jnp.concatenate([pv[h * DEC_PAD:(h + 1) * DEC_PAD] for h in range(C_HEADS)],
                               axis=1).astype(BF16)

    def store(r, o):
        o_ref[r] = o

    _rows_loop(rows, load, [scores, finish], store)


def _memattn_decode(qc, ck, cv, rows):
    nb = qc.shape[0]
    n = C_HEADS * C_HD
    blk = pl.BlockSpec((rows, DEC_PAD, n), lambda i: (i, 0, 0))
    cache = pl.BlockSpec((rows, N_MEM * C_HEADS, C_HD), lambda i: (i, 0, 0))
    return pl.pallas_call(
        functools.partial(_memattn_decode_kernel, rows=rows),
        grid=(nb // rows,),
        in_specs=[blk, cache, cache],
        out_specs=blk,
        out_shape=jax.ShapeDtypeStruct(qc.shape, BF16),
        compiler_params=_cparams(("arbitrary",)),
        name="memattn_decode",
    )(qc, ck, cv)


def _merge_kernel(x_ref, oa_ref, ob_ref, oc_ref, gpre_ref, wga_ref, wgb_ref, wgc_ref, wb_ref, wo_ref,
                  gpost_ref, gfpre_ref, x1_ref, h2_ref):
    x = x_ref[...]
    ups = [jnp.dot(o_ref[...], wb_ref[n], preferred_element_type=F32)
           for n, o_ref in enumerate((oa_ref, ob_ref, oc_ref))]
    h = _rms(x, gpre_ref[...]).astype(BF16)
    mix = None
    for n, wg_ref in enumerate((wga_ref, wgb_ref, wgc_ref)):
        gate = jax.nn.sigmoid(jnp.dot(h, wg_ref[...], preferred_element_type=F32))
        mix = gate * ups[n] if mix is None else mix + gate * ups[n]
    x1 = x + _rms(_bdot(mix, wo_ref[...]), gpost_ref[...])
    x1_ref[...] = x1
    h2_ref[...] = _rms(x1, gfpre_ref[...]).astype(BF16)


FF_SPLIT = 6 * PAIR


def _ffn_kernel(x1_ref, h2_ref, wfi_ref, wfo_ref, gfpost_ref, y_ref):
    h2 = h2_ref[...]
    f = None
    halves = []
    for a, b in ((0, FF_SPLIT), (FF_SPLIT, D_FF)):
        gt = jnp.dot(h2, wfi_ref[:, a:b], preferred_element_type=F32)
        uf = jnp.dot(h2, wfi_ref[:, D_FF + a:D_FF + b], preferred_element_type=F32)
        halves.append((a, b, gt, uf))
    for a, b, gt, uf in halves:
        part = _bdot(_silu(gt) * uf, wfo_ref[a:b, :])
        f = part if f is None else f + part
    y_ref[...] = x1_ref[...] + _rms(f, gfpost_ref[...])


def _merge_weight_specs():
    vec = _const_spec((1, D_MODEL))
    return ([vec] + [pl.BlockSpec((D_MODEL, D_MODEL), functools.partial(lambda n, *_: (0, n), n + 1),
                                  pipeline_mode=pl.Buffered(1)) for n in range(N_BRANCH)]
            + [_const_spec((N_BRANCH, BRANCH_W, D_MODEL)), _const_spec((D_MODEL, D_MODEL)), vec, vec])


def _merge(x, oa, ob, oc, gpre, wtail, wb, wo, gpost, gfpre, tm):
    t = x.shape[0]
    row = lambda n: pl.BlockSpec((tm, n), lambda i: (i, 0))
    return pl.pallas_call(
        _merge_kernel,
        grid=(t // tm,),
        in_specs=[row(D_MODEL), row(512), row(512), row(512)] + _merge_weight_specs(),
        out_specs=[row(D_MODEL), row(D_MODEL)],
        out_shape=[jax.ShapeDtypeStruct(x.shape, F32), jax.ShapeDtypeStruct(x.shape, BF16)],
        compiler_params=_cparams(("arbitrary",)),
        name="merge",
    )(x, oa, ob, oc, gpre, wtail, wtail, wtail, wb, wo, gpost, gfpre)


def _ffn(x1, h2, wfi, wfo, gfpost, tm):
    t = x1.shape[0]
    vec = _const_spec((1, D_MODEL))
    frow = pl.BlockSpec((tm, D_MODEL), lambda i: (i, 0))
    return pl.pallas_call(
        _ffn_kernel,
        grid=(t // tm,),
        in_specs=[frow, frow, _const_spec((D_MODEL, 2 * D_FF)),
                  _const_spec((D_FF, D_MODEL)), vec],
        out_specs=frow,
        out_shape=jax.ShapeDtypeStruct(x1.shape, F32),
        compiler_params=_cparams(("arbitrary",)),
        name="ffn",
    )(x1, h2, wfi, wfo, gfpost)


def _rope_tables(pos):
    half = A_HD // 2
    inv = ROPE_THETA ** (-jnp.arange(half, dtype=F32) / half)
    ang = pos.astype(F32)[:, None] * inv[None, :]
    cos, sin = jnp.cos(ang), jnp.sin(ang)
    cos = jnp.concatenate([cos, cos], axis=-1)
    sin = jnp.concatenate([-sin, sin], axis=-1)
    return jnp.tile(cos, (1, LANES // A_HD)), jnp.tile(sin, (1, LANES // A_HD))


def _lane_row(vals, offset):
    return jnp.zeros((1, LANES), F32).at[0, offset:offset + vals.shape[0]].set(vals.astype(F32))


def kernel(x_prompt, x_sample, mem_prompt, state_win_k, state_win_v, state_conv, state_rec,
           cache_mem_k, cache_mem_v, ln_mix_pre, w_in, attn_sink, gdn_conv_w, gdn_a_log,
           gdn_dt_bias, gdn_norm_w, ln_mem, w_mem_kv, w_branch, w_out, ln_mix_post,
           ln_ffn_pre, w_ffn_in, w_ffn_out, ln_ffn_post):
    bp, lp, _ = x_prompt.shape
    bs, ls, _ = x_sample.shape

    sizes = [512, 128, 128, B_CONV_CH, B_HEADS, B_HEADS, 512, 512, N_BRANCH * D_MODEL]
    o = np.cumsum([0] + sizes)
    hperm = np.concatenate([np.r_[j * A_HD:(j + 1) * A_HD, (j + 4) * A_HD:(j + 5) * A_HD] for j in range(4)])
    ws = (w_in[:, hperm].astype(BF16), w_in[:, o[1]:o[4]].astype(BF16), w_in[:, o[6]:o[9]].astype(BF16),
          jnp.pad(w_in[:, o[4]:o[6]], ((0, 0), (0, LANES - 2 * B_HEADS))).astype(BF16))
    wb = jnp.concatenate([w_branch[0:1][:, hperm], w_branch[1:]], axis=0).astype(BF16)
    wo = w_out.astype(BF16)
    wfi = w_ffn_in.astype(BF16)
    wfo = w_ffn_out.astype(BF16)
    wmem = w_mem_kv.astype(BF16)
    sink = attn_sink.astype(F32)[np.array([0, 4, 1, 5, 2, 6, 3, 7])]
    vec = lambda g: g.astype(F32).reshape(1, -1)
    alog = _lane_row(gdn_a_log, B_HEADS)
    dtb = _lane_row(gdn_dt_bias, B_HEADS)
    cw = gdn_conv_w.astype(F32)
    nw = vec(gdn_norm_w)

    merge_w = (vec(ln_mix_pre), ws[2], wb, wo, vec(ln_mix_post), vec(ln_ffn_pre))
    ffn = lambda x1, h2: _ffn(x1, h2, wfi, wfo, vec(ln_ffn_post), 512)

    tp = bp * lp
    xp = x_prompt.reshape(tp, D_MODEL)
    cos_p, sin_p = _rope_tables(jnp.arange(lp, dtype=jnp.int32))
    qa, ka, va, z, qc, ba, qn, kn, vv, tail = _proj(xp, vec(ln_mix_pre), ws, cos_p, sin_p, 512, cw, lp)
    b3 = lambda a: a.reshape(bp, lp, a.shape[-1])
    ob, rec_p = _gdn_prompt(b3(qn), b3(kn), b3(vv), b3(ba), b3(z), alog, dtb, nw)
    mk, mv = _memkv(mem_prompt.reshape(bp * N_MEM, D_MODEL), vec(ln_mem), wmem)
    oa = _swa_prompt(sink, qa, ka, va, bp, lp, 512)
    oc = _memattn_prompt(qc, mk, mv, bp, lp, 512)
    x1, h2 = _merge(xp, oa, ob.reshape(tp, 512), oc, *merge_w, 256)
    y_p = ffn(x1, h2).reshape(bp, lp, D_MODEL)
    wk_p = ka.reshape(bp, lp, LANES)[:, -WINDOW:].reshape(bp, WINDOW, A_KV, A_HD)
    wv_p = va.reshape(bp, lp, LANES)[:, -WINDOW:].reshape(bp, WINDOW, A_KV, A_HD)
    conv_p = tail[:, -(CONV_W - 1):]
    mem_k_p = mk.reshape(bp, N_MEM, C_HEADS, C_HD)
    mem_v_p = mv.reshape(bp, N_MEM, C_HEADS, C_HD)

    ts = bs * DEC_PAD
    xs = jnp.pad(x_sample, ((0, 0), (0, DEC_PAD - ls), (0, 0))).reshape(ts, D_MODEL)
    cos_s, sin_s = _rope_tables(PAST_LEN + jnp.arange(DEC_PAD, dtype=jnp.int32))
    cos_s, sin_s = jnp.tile(cos_s, (bs, 1)), jnp.tile(sin_s, (bs, 1))
    qa, ka, va, z, qc, ba, qkv = _proj(xs, vec(ln_mix_pre), ws, cos_s, sin_s, 512)
    r3 = lambda a: a.reshape(bs, DEC_PAD, a.shape[-1])
    oa, wk_s, wv_s = _swa_decode(sink, r3(qa), r3(ka), r3(va), state_win_k.reshape(bs, WINDOW, LANES),
                                 state_win_v.reshape(bs, WINDOW, LANES), 16, ls)
    oa = oa.reshape(ts, 512)
    wk_s = wk_s.reshape(state_win_k.shape)
    wv_s = wv_s.reshape(state_win_v.shape)
    histp = jnp.pad(state_conv, ((0, 0), (DEC_PAD - (CONV_W - 1), 0), (0, 0)))
    ob, rec_s = _gdn_decode(r3(qkv), histp, ba, z, state_rec, cw, alog, dtb, nw,
                            LANES // DEC_PAD, ls)
    oc = _memattn_decode(r3(qc), cache_mem_k.reshape(bs, N_MEM * C_HEADS, C_HD),
                         cache_mem_v.reshape(bs, N_MEM * C_HEADS, C_HD), 8).reshape(ts, 512)
    real = lambda a: a.reshape(bs, DEC_PAD, 512)[:, :ls].reshape(bs * ls, 512)
    x1, h2 = _merge(x_sample.reshape(bs * ls, D_MODEL), real(oa), real(ob), real(oc), *merge_w, 256)
    y_s = ffn(x1, h2).reshape(bs, ls, D_MODEL)
    conv_s = r3(qkv)[:, ls - (CONV_W - 1):ls]

    return (y_p, y_s, wk_p, wv_p, conv_p, rec_p, mem_k_p, mem_v_p, wk_s, wv_s, conv_s, rec_s)
```

```python
import functools

import numpy as np
import jax
import jax.numpy as jnp
from jax import lax
from jax.experimental import pallas as pl
from jax.experimental.pallas import tpu as pltpu

F32 = jnp.float32
BF16 = jnp.bfloat16

D_MODEL = 1024
PAST_LEN = 16384
EPS = 1e-6
ROPE_THETA = 10000.0
N_MEM = 256
WINDOW = 128
A_HD = 64
A_HEADS = 8
A_KV = 2
A_SCALE = A_HD ** -0.5
B_HEADS = 4
B_DK = 128
B_DV = 128
CONV_W = 4
GDN_CHUNK = 64
B_CONV_CH = B_HEADS * (2 * B_DK + B_DV)
C_HEADS = 4
C_HD = 128
N_BRANCH = 3
BRANCH_W = 512
D_FF = 2816

LANES = 128
SUBLANES = 8
PAIR = 2 * LANES
VMEM_LIMIT = 56 * 1024 * 1024
NEG = -1e30
DEC_PAD = SUBLANES


def _cparams(sem, vmem=VMEM_LIMIT):
    return pltpu.CompilerParams(dimension_semantics=sem, vmem_limit_bytes=vmem)


def _const_spec(shape):
    nd = len(shape)
    return pl.BlockSpec(shape, lambda *_: (0,) * nd, pipeline_mode=pl.Buffered(1))


def _rms(x, g):
    ms = jnp.mean(x * x, axis=-1, keepdims=True)
    return x * lax.rsqrt(ms + EPS) * g


def _bdot(a, b):
    return jnp.dot(a.astype(BF16), b.astype(BF16), preferred_element_type=F32)


def _bdot_nt(a, b):
    return lax.dot_general(a.astype(BF16), b.astype(BF16), (((1,), (1,)), ((), ())),
                           preferred_element_type=F32)


def _bdot_tn(a, b):
    return lax.dot_general(a.astype(BF16), b.astype(BF16), (((0,), (0,)), ((), ())),
                           preferred_element_type=F32)


def _hdot(a, b):
    return jnp.dot(a, b, precision=lax.Precision.HIGHEST, preferred_element_type=F32)


def _silu(x):
    return x * jax.nn.sigmoid(x)


def _softplus(x):
    return jnp.maximum(x, 0.0) + jnp.log1p(jnp.exp(-jnp.abs(x)))


def _rope128(v, cos, sin):
    lane = lax.broadcasted_iota(jnp.int32, v.shape, 1)
    fwd = pltpu.roll(v, 32, 1)
    bwd = pltpu.roll(v, 96, 1)
    sw = jnp.where((lane & 32) == 0, bwd, fwd)
    return v * cos + sw * sin


def _l2n(x):
    return x * lax.rsqrt(jnp.sum(x * x, axis=-1, keepdims=True) + EPS)

def _proj_steps(x_ref, g_ref, wq_ref, wr_ref, wzc_ref, wba_ref, cos_ref, sin_ref,
                qa_ref, ka_ref, va_ref, z_ref, qc_ref, ba_ref):
    h = _rms(x_ref[...], g_ref[...]).astype(BF16)
    cos = cos_ref[...]
    sin = sin_ref[...]

    def mm(w_ref, a, b):
        return jnp.dot(h, w_ref[:, a:b], preferred_element_type=F32)

    def qa_half(c0):
        q = mm(wq_ref, c0 * LANES, (c0 + 2) * LANES)
        for c in range(2):
            qa_ref[:, (c0 + c) * LANES:(c0 + c + 1) * LANES] = (
                _rope128(q[:, c * LANES:(c + 1) * LANES], cos, sin) * A_SCALE).astype(BF16)

    def kv():
        kv2 = mm(wr_ref, 0, PAIR)
        ka_ref[...] = _rope128(kv2[:, :LANES], cos, sin)
        va_ref[...] = kv2[:, LANES:]

    def z_half(c0):
        z_ref[:, c0:c0 + PAIR] = mm(wzc_ref, c0, c0 + PAIR)

    def qc_half(c0):
        qc_ref[:, c0:c0 + PAIR] = mm(wzc_ref, 512 + c0, 512 + c0 + PAIR).astype(BF16)

    def ba():
        ba_ref[...] = mm(wba_ref, 0, LANES)

    def qkv(a, b):
        return mm(wr_ref, PAIR + a, PAIR + b)

    steps = [lambda: qa_half(0), lambda: qa_half(2), kv, lambda: z_half(0), lambda: z_half(PAIR),
             lambda: qc_half(0), lambda: qc_half(PAIR), ba]
    return qkv, steps


def _proj_raw_kernel(x_ref, g_ref, wq_ref, wr_ref, wzc_ref, wba_ref, cos_ref, sin_ref,
                     qa_ref, ka_ref, va_ref, z_ref, qc_ref, ba_ref, qkv_ref):
    qkv, steps = _proj_steps(x_ref, g_ref, wq_ref, wr_ref, wzc_ref, wba_ref, cos_ref, sin_ref,
                             qa_ref, ka_ref, va_ref, z_ref, qc_ref, ba_ref)
    qkv_ref[...] = qkv(0, B_CONV_CH)
    for step in steps:
        step()


def _proj_conv_kernel(x_ref, g_ref, wq_ref, wr_ref, wzc_ref, wba_ref, cos_ref, sin_ref, cw_ref,
                      qa_ref, ka_ref, va_ref, z_ref, qc_ref, ba_ref, qn_ref, kn_ref, vv_ref, tail_ref,
                      buf_ref, *, tm, tiles_per_seq):
    hist = SUBLANES
    first = lax.rem(pl.program_id(0), tiles_per_seq) == 0

    @pl.when(first)
    def _():
        buf_ref[0:hist, :] = jnp.zeros((hist, B_CONV_CH), F32)

    @pl.when(jnp.logical_not(first))
    def _():
        buf_ref[0:hist, :] = buf_ref[tm:tm + hist, :]

    qkv, steps = _proj_steps(x_ref, g_ref, wq_ref, wr_ref, wzc_ref, wba_ref, cos_ref, sin_ref,
                             qa_ref, ka_ref, va_ref, z_ref, qc_ref, ba_ref)
    nq = B_HEADS * B_DK

    def conv_group(c0):
        cs = slice(c0, c0 + PAIR)
        raw = qkv(c0, c0 + PAIR)
        buf_ref[hist:hist + tm, cs] = raw
        tail_ref[0, :, cs] = raw[tm - hist:, :]
        xb = buf_ref[:, cs]
        acc = xb * cw_ref[0:1, cs]
        for i in range(1, CONV_W):
            acc = pltpu.roll(acc, 1, 0) + xb * cw_ref[i:i + 1, cs]
        act = _silu(acc[hist:])
        if c0 >= 2 * nq:
            vv_ref[:, c0 - 2 * nq:c0 - 2 * nq + PAIR] = act
            return
        out_ref, base, scale = (qn_ref, 0, B_DK ** -0.5) if c0 < nq else (kn_ref, nq, 1.0)
        for s in range(2):
            o0 = c0 - base + s * B_DK
            out_ref[:, o0:o0 + B_DK] = _l2n(act[:, s * B_DK:(s + 1) * B_DK]) * scale

    groups = [functools.partial(conv_group, c0) for c0 in range(0, B_CONV_CH, PAIR)]
    while groups or steps:
        if groups:
            groups.pop(0)()
        if steps:
            steps.pop(0)()


_PROJ_OUTS = [(512, BF16), (128, F32), (128, F32), (512, F32), (512, BF16), (128, F32)]


def _proj(x, gain, ws, cos, sin, tm, cw=None, seq=None):
    wq, wr, wtail, wba = ws
    t = x.shape[0]
    ntab = cos.shape[0] // tm
    row = lambda n: pl.BlockSpec((tm, n), lambda i: (i, 0))
    tab = pl.BlockSpec((tm, LANES), lambda i: (i % ntab, 0))
    in_specs = [row(D_MODEL), _const_spec((1, D_MODEL)), _const_spec(wq.shape), _const_spec(wr.shape),
                _const_spec((D_MODEL, D_MODEL)), _const_spec(wba.shape), tab, tab]
    out_specs = [row(n) for n, _ in _PROJ_OUTS]
    out_shape = [jax.ShapeDtypeStruct((t, n), d) for n, d in _PROJ_OUTS]
    if cw is None:
        return pl.pallas_call(
            _proj_raw_kernel, grid=(t // tm,), in_specs=in_specs,
            out_specs=out_specs + [row(B_CONV_CH)],
            out_shape=out_shape + [jax.ShapeDtypeStruct((t, B_CONV_CH), F32)],
            compiler_params=_cparams(("arbitrary",)), name="proj",
        )(x, gain, wq, wr, wtail, wba, cos, sin)
    tiles = seq // tm
    return pl.pallas_call(
        functools.partial(_proj_conv_kernel, tm=tm, tiles_per_seq=tiles),
        grid=(t // tm,),
        in_specs=in_specs + [_const_spec((CONV_W, B_CONV_CH))],
        out_specs=out_specs + [row(512)] * 3
        + [pl.BlockSpec((1, SUBLANES, B_CONV_CH), lambda i: (i // tiles, 0, 0))],
        out_shape=out_shape + [jax.ShapeDtypeStruct((t, 512), F32)] * 3
        + [jax.ShapeDtypeStruct((t // seq, SUBLANES, B_CONV_CH), F32)],
        scratch_shapes=[pltpu.VMEM((tm + SUBLANES, B_CONV_CH), F32)],
        compiler_params=_cparams(("arbitrary",)), name="proj_conv",
    )(x, gain, wq, wr, wtail, wba, cos, sin, cw)


def _swa_scores(q, k16):
    tq = q.shape[0]
    lo = lax.broadcasted_iota(jnp.int32, (tq, LANES), 1) < A_HD
    blocks = []
    for j in range(4):
        c = q[:, j * LANES:(j + 1) * LANES].astype(F32)
        blocks.append(jnp.where(lo, c, 0.0))
        blocks.append(jnp.where(lo, 0.0, c))
    lhs = jnp.concatenate(blocks, axis=0).astype(BF16)
    return lax.dot_general(lhs, k16, (((1,), (1,)), ((), ())), preferred_element_type=F32)


def _swa_finish(logits, v16, mask, sink_ref):
    tq = logits.shape[0] // A_HEADS
    lo = lax.broadcasted_iota(jnp.int32, (tq, LANES), 1) < A_HD
    es, inv = [], []
    for s in range(8):
        l = jnp.where(mask, logits[s * tq:(s + 1) * tq], NEG)
        sk = sink_ref[s]
        m = jnp.maximum(jnp.max(l, axis=-1, keepdims=True), sk)
        e = jnp.exp(l - m)
        den = jnp.sum(e, axis=-1, keepdims=True) + jnp.exp(sk - m)
        es.append(e.astype(BF16))
        inv.append(1.0 / den)
    pv = jnp.dot(jnp.concatenate(es, axis=0), v16, preferred_element_type=F32)
    outs = []
    for j in range(4):
        a = pv[(2 * j) * tq:(2 * j + 1) * tq] * inv[2 * j]
        b = pv[(2 * j + 1) * tq:(2 * j + 2) * tq] * inv[2 * j + 1]
        outs.append(jnp.where(lo, a, b))
    return jnp.concatenate(outs, axis=1)


def _swa_prompt_kernel(sink_ref, q_ref, kc_ref, kp_ref, vc_ref, vp_ref, o_ref, *, nblk):
    i = pl.program_id(1)
    kcat = jnp.concatenate([kp_ref[...], kc_ref[...]], axis=0).astype(BF16)
    vcat = jnp.concatenate([vp_ref[...], vc_ref[...]], axis=0).astype(BF16)
    ii = lax.broadcasted_iota(jnp.int32, (WINDOW, 2 * WINDOW), 0)
    jj = lax.broadcasted_iota(jnp.int32, (WINDOW, 2 * WINDOW), 1)
    band = (jj > ii) & (jj <= ii + WINDOW)
    for jb in range(nblk):
        mask = band
        if jb == 0:
            mask = band & ((jj >= WINDOW) | (i > 0))
        logits = _swa_scores(q_ref[jb * WINDOW:(jb + 1) * WINDOW, :], kcat[jb * WINDOW:(jb + 2) * WINDOW])
        o = _swa_finish(logits, vcat[jb * WINDOW:(jb + 2) * WINDOW], mask, sink_ref)
        o_ref[jb * WINDOW:(jb + 1) * WINDOW, :] = o.astype(BF16)


def _swa_prompt(sink, qa, ka, va, batch, seq, tq):
    nq = seq // tq
    nblk = tq // WINDOW
    nw = seq // WINDOW
    cur = lambda n: pl.BlockSpec((tq, n), lambda b, i: (b * nq + i, 0))
    prev = pl.BlockSpec((WINDOW, LANES), lambda b, i: (jnp.maximum(b * nw + i * nblk - 1, 0), 0))
    return pl.pallas_call(
        functools.partial(_swa_prompt_kernel, nblk=nblk),
        grid=(batch, nq),
        in_specs=[pl.BlockSpec(memory_space=pltpu.SMEM), cur(512), cur(LANES), prev, cur(LANES), prev],
        out_specs=cur(512),
        out_shape=jax.ShapeDtypeStruct(qa.shape, BF16),
        compiler_params=_cparams(("arbitrary", "arbitrary")),
        name="swa_prompt",
    )(sink, qa, ka, ka, va, va)


ROW_UNROLL = 8


def _rows_loop(rows, load, stages, store):
    def body(g, carry):
        idx = [g * ROW_UNROLL + j for j in range(ROW_UNROLL)]
        vals = [load(r) for r in idx]
        for stage in stages:
            vals = [stage(v) for v in vals]
        for r, o in zip(idx, vals):
            store(r, o)
        return carry

    lax.fori_loop(0, rows // ROW_UNROLL, body, 0)


def _swa_decode_kernel(sink_ref, q_ref, kn_ref, vn_ref, wk_ref, wv_ref, o_ref, wko_ref, wvo_ref,
                       *, rows, new_len):
    ii = lax.broadcasted_iota(jnp.int32, (DEC_PAD, 2 * WINDOW), 0)
    jj = lax.broadcasted_iota(jnp.int32, (DEC_PAD, 2 * WINDOW), 1)
    mask = ((jj < WINDOW) & (jj > ii)) | ((jj >= WINDOW) & (jj - WINDOW <= ii))
    pad = jnp.zeros((WINDOW - DEC_PAD, LANES), F32)
    keep = lax.broadcasted_iota(jnp.int32, (WINDOW, LANES), 0) < WINDOW - new_len

    def slide(win, new):
        tail = jnp.concatenate([pad, new], axis=0)
        return jnp.where(keep, pltpu.roll(win, WINDOW - new_len, 0), pltpu.roll(tail, DEC_PAD - new_len, 0))

    def load(r):
        return q_ref[r], wk_ref[r], kn_ref[r], wv_ref[r], vn_ref[r]

    def scores(args):
        q, wk, kn, wv, vn = args
        kk = jnp.concatenate([wk, kn, pad], axis=0).astype(BF16)
        return _swa_scores(q, kk), wk, kn, wv, vn

    def finish(args):
        logits, wk, kn, wv, vn = args
        vv = jnp.concatenate([wv, vn, pad], axis=0).astype(BF16)
        return _swa_finish(logits, vv, mask, sink_ref).astype(BF16), slide(wk, kn), slide(wv, vn)

    def store(r, outs):
        o_ref[r], wko_ref[r], wvo_ref[r] = outs

    _rows_loop(rows, load, [scores, finish], store)


def _swa_decode(sink, qa, kn, vn, wk, wv, rows, new_len):
    nb = qa.shape[0]
    blk = lambda a, n: pl.BlockSpec((rows, a, n), lambda i: (i, 0, 0))
    return pl.pallas_call(
        functools.partial(_swa_decode_kernel, rows=rows, new_len=new_len),
        grid=(nb // rows,),
        in_specs=[pl.BlockSpec(memory_space=pltpu.SMEM), blk(DEC_PAD, 512), blk(DEC_PAD, LANES),
                  blk(DEC_PAD, LANES), blk(WINDOW, LANES), blk(WINDOW, LANES)],
        out_specs=[blk(DEC_PAD, 512), blk(WINDOW, LANES), blk(WINDOW, LANES)],
        out_shape=[jax.ShapeDtypeStruct(qa.shape, BF16), jax.ShapeDtypeStruct(wk.shape, F32),
                   jax.ShapeDtypeStruct(wv.shape, F32)],
        compiler_params=_cparams(("arbitrary",)),
        name="swa_decode",
    )(sink, qa, kn, vn, wk, wv)


def _gdn_gates(ba, alog, dtb, valid):
    beta = jax.nn.sigmoid(ba)
    g = -jnp.exp(alog) * _softplus(ba + dtb)
    if valid is not None:
        beta = jnp.where(valid, beta, 0.0)
        g = jnp.where(valid, g, 0.0)
    return beta, g


def _chunk_masks(chunk):
    sh = chunk.bit_length() - 1
    ri = lax.broadcasted_iota(jnp.int32, (LANES, LANES), 0)
    ci = lax.broadcasted_iota(jnp.int32, (LANES, LANES), 1)
    same = (ri >> sh) == (ci >> sh)
    return same, same & (ri >= ci), same & (ri > ci), ri == ci


def _gdn_cumsums(g_all, chunk):
    same, tri, _, _ = _chunk_masks(chunk)
    lower = jnp.where(tri, 1.0, 0.0)
    gcol = _hdot(lower, g_all)
    grow = _hdot(g_all.T, lower.T)
    gtot = _hdot(jnp.where(same, 1.0, 0.0), g_all)
    return gcol, grow, gtot


def _sibling_mask(ri, ci, lvl):
    rb = ri >> lvl
    return ((rb & 1) == 1) & ((ci >> lvl) == rb - 1)


def _gdn_phase_a(heads, chunk):
    _, tri, strict, _ = _chunk_masks(chunk)
    ri = lax.broadcasted_iota(jnp.int32, (LANES, LANES), 0)
    ci = lax.broadcasted_iota(jnp.int32, (LANES, LANES), 1)
    kbs = [k * beta for q, k, v, beta, gcol, grow, gtot in heads]
    kkqk = [_bdot_nt(jnp.concatenate([kb, hd[0]], axis=0), hd[1]) for kb, hd in zip(kbs, heads)]
    a, qk = [], []
    for r, (q, k, v, beta, gcol, grow, gtot) in zip(kkqk, heads):
        decay = jnp.exp(jnp.where(tri, gcol - grow, NEG))
        a.append(jnp.where(strict, r[:LANES] * decay, 0.0))
        qk.append(r[LANES:] * decay)
    n = [-jnp.where(_sibling_mask(ri, ci, 0), ah, 0.0) for ah in a]
    for lvl in range(1, chunk.bit_length() - 1):
        sib = _sibling_mask(ri, ci, lvl)
        al = [jnp.where(sib, ah, 0.0) for ah in a]
        x = [alh + _bdot(nh, alh) for alh, nh in zip(al, n)]
        n = [nh - (xh + _bdot(xh, nh)) for xh, nh in zip(x, n)]
    outs = []
    rhs = [jnp.concatenate([hd[2] * hd[3], kb * jnp.exp(hd[4])], axis=1) for kb, hd in zip(kbs, heads)]
    sol = [r + _bdot(nh, r) for r, nh in zip(rhs, n)]
    for s, qkh, (q, k, v, beta, gcol, grow, gtot) in zip(sol, qk, heads):
        outs.append((s[:, :B_DV], s[:, B_DV:], q * jnp.exp(gcol), k * jnp.exp(gtot - gcol), qkh, jnp.exp(gtot)))
    return outs


def _gdn_qkv_heads(qkv, h):
    q = qkv[:, h * B_DK:(h + 1) * B_DK]
    k = qkv[:, (B_HEADS + h) * B_DK:(B_HEADS + h + 1) * B_DK]
    v = qkv[:, (2 * B_HEADS + h) * B_DK:(2 * B_HEADS + h + 1) * B_DK]
    q = q * lax.rsqrt(jnp.sum(q * q, axis=-1, keepdims=True) + EPS) * (B_DK ** -0.5)
    k = k * lax.rsqrt(jnp.sum(k * k, axis=-1, keepdims=True) + EPS)
    return q, k, v


def _gdn_out(o, z, nw):
    return _rms(o, nw) * _silu(z)


def _pair_bd(x2):
    lo = lax.broadcasted_iota(jnp.int32, x2.shape, 1) < LANES
    return jnp.concatenate([jnp.where(lo, x2, 0.0), jnp.where(lo, 0.0, x2)], axis=0).astype(BF16)


def _quad_bd(x4):
    blk = lax.broadcasted_iota(jnp.int32, x4.shape, 1) >> (GDN_CHUNK.bit_length() - 1)
    return jnp.concatenate([jnp.where(blk == j, x4, 0.0) for j in range(PAIR // GDN_CHUNK)],
                           axis=0).astype(BF16)


def _pair_cols(x, l0, l1):
    lo = lax.broadcasted_iota(jnp.int32, (x.shape[0], PAIR), 1) < LANES
    return jnp.where(lo, x[:, l0:l0 + 1], x[:, l1:l1 + 1])


def _gdn_prompt_kernel(q_ref, k_ref, v_ref, ba_ref, z_ref, alog_ref, dtb_ref, nw_ref,
                       ob_ref, s_ref, sbd_ref, *, nb, ntb):
    t = pl.program_id(0)
    npair = B_HEADS // 2

    @pl.when(t == 0)
    def _():
        sbd_ref[...] = jnp.zeros(sbd_ref.shape, F32)

    alog = alog_ref[...]
    dtb = dtb_ref[...]
    nw = nw_ref[...]
    nchunk = LANES // GDN_CHUNK
    _, tri, strict, _ = _chunk_masks(GDN_CHUNK)
    tri2 = jnp.concatenate([tri, tri], axis=1)
    strict2 = jnp.concatenate([strict, strict], axis=1)

    items = []
    for b, tb in [(b, tb) for b in range(nb) for tb in range(ntb)]:
        rsl = slice(tb * LANES, (tb + 1) * LANES)
        beta_all, g_all = _gdn_gates(ba_ref[b, rsl, :], alog, dtb, None)
        gcol_all, grow_all, gtot_all = _gdn_cumsums(g_all, GDN_CHUNK)
        eg_all = jnp.exp(gcol_all)
        ed_all = jnp.exp(gtot_all - gcol_all)
        et_all = jnp.exp(gtot_all)
        for p in range(npair):
            h0, h1 = 2 * p, 2 * p + 1
            g0, g1 = B_HEADS + h0, B_HEADS + h1
            psl = slice(p * PAIR, (p + 1) * PAIR)
            q2 = q_ref[b, rsl, psl]
            k2 = k_ref[b, rsl, psl]
            v2 = v_ref[b, rsl, psl]
            beta2 = _pair_cols(beta_all, h0, h1)
            grow2 = jnp.concatenate([grow_all[g0:g0 + 1, :], grow_all[g1:g1 + 1, :]], axis=1)
            decay2 = jnp.exp(jnp.where(tri2, _pair_cols(gcol_all, g0, g1) - grow2, NEG))
            kb2 = k2 * beta2
            items.append(dict(
                b=b, tb=tb, p=p, rsl=rsl, decay2=decay2, kb2=kb2, v2b=v2 * beta2,
                kbe2=kb2 * _pair_cols(eg_all, g0, g1),
                lhs=jnp.concatenate([kb2, q2], axis=0).astype(BF16),
                kbd=_pair_bd(k2),
                qe2=(q2 * _pair_cols(eg_all, g0, g1)).astype(BF16),
                kd2=(k2 * _pair_cols(ed_all, g0, g1)).astype(BF16),
                et2=_pair_cols(et_all, g0, g1)))

    for it in items:
        kkqk = lax.dot_general(it["lhs"], it["kbd"], (((1,), (1,)), ((), ())),
                               preferred_element_type=F32)
        it["a2"] = jnp.where(strict2, kkqk[:LANES] * it["decay2"], 0.0)
        it["qk2"] = (kkqk[LANES:] * it["decay2"]).astype(BF16)
    qlane = lax.broadcasted_iota(jnp.int32, (GDN_CHUNK, PAIR), 1)
    qrow = lax.broadcasted_iota(jnp.int32, (GDN_CHUNK, PAIR), 0)
    qcol = qlane & (GDN_CHUNK - 1)
    top = (qlane & GDN_CHUNK) == 0
    for it in items:
        it["a64"] = it["a2"][:GDN_CHUNK] + it["a2"][GDN_CHUNK:]
        it["n"] = -jnp.where(_sibling_mask(qrow, qcol, 0), it["a64"], 0.0)
    for lvl in range(1, GDN_CHUNK.bit_length() - 1):
        sib = _sibling_mask(qrow, qcol, lvl)
        for it in items:
            it["al"] = jnp.where(sib, it["a64"], 0.0)
            it["x"] = it["al"] + jnp.dot(it["n"].astype(BF16), _quad_bd(it["al"]), preferred_element_type=F32)
        for it in items:
            x = it["x"]
            it["n"] = it["n"] - (x + jnp.dot(x.astype(BF16), _quad_bd(it["n"]), preferred_element_type=F32))
    for it in items:
        n64 = it["n"]
        it["n"] = jnp.concatenate([jnp.where(top, n64, 0.0), jnp.where(top, 0.0, n64)], axis=0)
    for it in items:
        us, ws = [], []
        for s in range(2):
            sl = slice(s * LANES, (s + 1) * LANES)
            rhs = jnp.concatenate([it["v2b"][:, sl], it["kbe2"][:, sl]], axis=1)
            sol = rhs + _bdot(it["n"][:, sl], rhs)
            us.append(sol[:, :B_DV])
            ws.append(sol[:, B_DV:])
        it["u2"] = jnp.concatenate(us, axis=1)
        it["w2"] = jnp.concatenate(ws, axis=1).astype(BF16)

    ri = lax.broadcasted_iota(jnp.int32, (PAIR, PAIR), 0) < LANES
    ci = lax.broadcasted_iota(jnp.int32, (PAIR, PAIR), 1) < LANES
    bdmask = ri == ci
    nchain = nb * npair
    state = [sbd_ref[i] for i in range(nchain)]
    outs = {}
    for tb in range(ntb):
        cur = [(it["b"] * npair + it["p"], i, it) for i, it in enumerate(items) if it["tb"] == tb]
        for c in range(nchunk):
            rows = slice(c * GDN_CHUNK, (c + 1) * GDN_CHUNK)
            rs = [jnp.dot(jnp.concatenate([it["w2"][rows], it["qe2"][rows]], axis=0), state[ch].astype(BF16),
                          preferred_element_type=F32) for ch, _, it in cur]
            for (ch, i, it), r in zip(cur, rs):
                s = state[ch]
                vn2 = it["u2"][rows] - r[:GDN_CHUNK]
                vt = _pair_bd(jnp.concatenate([vn2] * nchunk, axis=0))
                outs[(i, c)] = r[GDN_CHUNK:] + jnp.dot(it["qk2"][rows], vt, preferred_element_type=F32)
                upd = lax.dot_general(it["kd2"][rows], vn2.astype(BF16), (((0,), (0,)), ((), ())),
                                      preferred_element_type=F32)
                state[ch] = s * it["et2"][c * GDN_CHUNK:c * GDN_CHUNK + 1, :] + jnp.where(bdmask, upd, 0.0)
    for ch in range(nchain):
        sbd_ref[ch] = state[ch]

    @pl.when(t == pl.num_programs(0) - 1)
    def _():
        for ch in range(nchain):
            b, p = divmod(ch, npair)
            s_ref[b, 2 * p] = state[ch][:LANES, :LANES]
            s_ref[b, 2 * p + 1] = state[ch][LANES:, LANES:]

    for i, it in enumerate(items):
        o2 = jnp.concatenate([outs[(i, c)] for c in range(nchunk)], axis=0)
        for s in range(2):
            h = 2 * it["p"] + s
            sl = slice(h * B_DV, (h + 1) * B_DV)
            o = o2[:, s * LANES:(s + 1) * LANES]
            ob_ref[it["b"], it["rsl"], sl] = _gdn_out(o, z_ref[it["b"], it["rsl"], sl], nw).astype(BF16)


GDN_TOKEN_BLOCKS = 2


def _gdn_prompt(q, k, v, ba, z, alog, dtb, nw):
    batch, seq, _ = q.shape
    tm = GDN_TOKEN_BLOCKS * LANES
    tok = lambda n: pl.BlockSpec((batch, tm, n), lambda t: (0, t, 0))
    state = pl.BlockSpec((batch, B_HEADS, B_DK, B_DV), lambda t: (0, 0, 0, 0))
    return pl.pallas_call(
        functools.partial(_gdn_prompt_kernel, nb=batch, ntb=GDN_TOKEN_BLOCKS),
        grid=(seq // tm,),
        in_specs=[tok(512), tok(512), tok(512), tok(LANES), tok(512),
                  _const_spec((1, LANES)), _const_spec((1, LANES)), _const_spec((1, B_DV))],
        out_specs=[tok(512), state],
        out_shape=[jax.ShapeDtypeStruct((batch, seq, 512), BF16),
                   jax.ShapeDtypeStruct((batch, B_HEADS, B_DK, B_DV), F32)],
        scratch_shapes=[pltpu.VMEM((batch * (B_HEADS // 2), PAIR, PAIR), F32)],
        compiler_params=_cparams(("arbitrary",)),
        name="gdn_prompt",
    )(q, k, v, ba, z, alog, dtb, nw)


def _gdn_decode_kernel(raw_ref, hist_ref, ba_ref, z_ref, rec_ref, cw_ref, alog_ref, dtb_ref, nw_ref,
                       ob_ref, s_ref, buf_ref, u_s, w_s, qe_s, kd_s, qk_s, eg_s, o_s, *, rows, valid_len):
    buf_ref[:, 0:DEC_PAD, :] = hist_ref[...]
    buf_ref[:, DEC_PAD:2 * DEC_PAD, :] = raw_ref[...]
    conv = None
    for i in range(CONV_W):
        off = DEC_PAD - (CONV_W - 1) + i
        term = buf_ref[:, off:off + DEC_PAD, :] * cw_ref[i:i + 1, :]
        conv = term if conv is None else conv + term
    qkv = _silu(conv.reshape(rows * DEC_PAD, B_CONV_CH))
    tok = lax.broadcasted_iota(jnp.int32, (LANES, LANES), 0) & (DEC_PAD - 1)
    beta_all, g_all = _gdn_gates(ba_ref[...], alog_ref[...], dtb_ref[...], tok < valid_len)
    gcol_all, grow_all, gtot_all = _gdn_cumsums(g_all, DEC_PAD)
    heads = []
    for h in range(B_HEADS):
        gl = B_HEADS + h
        heads.append(_gdn_qkv_heads(qkv, h) + (beta_all[:, h:h + 1], gcol_all[:, gl:gl + 1],
                                               grow_all[gl:gl + 1, :], gtot_all[:, gl:gl + 1]))
    for h, (u, w, qe, kd, qk, egt) in enumerate(_gdn_phase_a(heads, DEC_PAD)):
        u_s[h] = u
        w_s[h] = w
        qe_s[h] = qe
        kd_s[h] = kd
        qk_s[h] = qk
        eg_s[h] = jnp.broadcast_to(egt, (LANES, LANES))

    def load(r):
        r0 = pl.multiple_of(r * DEC_PAD, DEC_PAD)
        rr = pl.ds(r0, DEC_PAD)
        return [(rec_ref[r, h], w_s[h, rr, :], qe_s[h, rr, :], u_s[h, rr, :], qk_s[h, rr, :],
                 kd_s[h, rr, :], eg_s[h, pl.ds(r0, 1), :]) for h in range(B_HEADS)]

    def read_state(heads):
        return [(_bdot(jnp.concatenate([w, qe], axis=0), s), s, u, qk, kd, eg)
                for s, w, qe, u, qk, kd, eg in heads]

    def update(heads):
        outs = []
        for res, s, u, qk, kd, eg in heads:
            v_new = u - res[:DEC_PAD]
            vt = jnp.concatenate([v_new] * (LANES // DEC_PAD), axis=0)
            outs.append((res[DEC_PAD:] + _bdot(qk, vt), s * eg + _bdot_tn(kd, v_new)))
        return outs

    def store(r, outs):
        rr = pl.ds(pl.multiple_of(r * DEC_PAD, DEC_PAD), DEC_PAD)
        for h, (o, s_new) in enumerate(outs):
            o_s[h, rr, :] = o
            s_ref[r, h] = s_new

    _rows_loop(rows, load, [read_state, update], store)
    nw = nw_ref[...]
    for h in range(B_HEADS):
        sl = slice(h * B_DV, (h + 1) * B_DV)
        ob_ref[:, sl] = _gdn_out(o_s[h], z_ref[:, sl], nw).astype(BF16)


def _gdn_decode(raw, histp, ba, z, rec, cw, alog, dtb, nw, rows, valid_len):
    nb = raw.shape[0]
    flat = rows * DEC_PAD
    assert flat == LANES
    sq = lambda: pltpu.VMEM((B_HEADS, LANES, LANES), F32)
    return pl.pallas_call(
        functools.partial(_gdn_decode_kernel, rows=rows, valid_len=valid_len),
        grid=(nb // rows,),
        in_specs=[pl.BlockSpec((rows, DEC_PAD, B_CONV_CH), lambda i: (i, 0, 0)),
                  pl.BlockSpec((rows, DEC_PAD, B_CONV_CH), lambda i: (i, 0, 0)),
                  pl.BlockSpec((flat, LANES), lambda i: (i, 0)),
                  pl.BlockSpec((flat, 512), lambda i: (i, 0)),
                  pl.BlockSpec((rows, B_HEADS, B_DK, B_DV), lambda i: (i, 0, 0, 0)),
                  _const_spec((CONV_W, B_CONV_CH)), _const_spec((1, LANES)), _const_spec((1, LANES)),
                  _const_spec((1, B_DV))],
        out_specs=[pl.BlockSpec((flat, 512), lambda i: (i, 0)),
                   pl.BlockSpec((rows, B_HEADS, B_DK, B_DV), lambda i: (i, 0, 0, 0))],
        out_shape=[jax.ShapeDtypeStruct((nb * DEC_PAD, 512), BF16),
                   jax.ShapeDtypeStruct(rec.shape, F32)],
        scratch_shapes=[pltpu.VMEM((rows, 2 * DEC_PAD, B_CONV_CH), F32)] + [sq() for _ in range(7)],
        compiler_params=_cparams(("arbitrary",)),
        name="gdn_decode",
    )(raw, histp, ba, z, rec, cw, alog, dtb, nw)


def _memkv_kernel(m_ref, g_ref, w_ref, k_ref, v_ref):
    h = _rms(m_ref[...], g_ref[...]).astype(BF16)
    n = C_HEADS * C_HD
    k_ref[...] = jnp.dot(h, w_ref[:, :n], preferred_element_type=F32)
    v_ref[...] = jnp.dot(h, w_ref[:, n:], preferred_element_type=F32)


def _memkv(mem, gain, w):
    t = mem.shape[0]
    tm = 512
    n = C_HEADS * C_HD
    return pl.pallas_call(
        _memkv_kernel,
        grid=(t // tm,),
        in_specs=[pl.BlockSpec((tm, D_MODEL), lambda i: (i, 0)), _const_spec((1, D_MODEL)),
                  _const_spec((D_MODEL, 2 * n))],
        out_specs=[pl.BlockSpec((tm, n), lambda i: (i, 0))] * 2,
        out_shape=[jax.ShapeDtypeStruct((t, n), F32)] * 2,
        compiler_params=_cparams(("arbitrary",)),
        name="memkv",
    )(mem, gain, w)


def _softmax_rows(logits):
    m = jnp.max(logits, axis=-1, keepdims=True)
    e = jnp.exp(logits - m)
    return e, 1.0 / jnp.sum(e, axis=-1, keepdims=True)


def _memattn_prompt_kernel(q_ref, k_ref, v_ref, o_ref):
    hs = lambda h: slice(h * C_HD, (h + 1) * C_HD)
    scores = lambda h: _bdot_nt(q_ref[:, hs(h)], k_ref[:, hs(h)])
    logits = scores(0)
    for h in range(C_HEADS):
        nxt = scores(h + 1) if h + 1 < C_HEADS else None
        e, inv = _softmax_rows(logits * (C_HD ** -0.5))
        o_ref[:, hs(h)] = (_bdot(e, v_ref[:, hs(h)]) * inv).astype(BF16)
        logits = nxt


def _memattn_prompt(qc, mk, mv, batch, seq, tm):
    nq = seq // tm
    n = C_HEADS * C_HD
    cur = pl.BlockSpec((tm, n), lambda b, i: (b * nq + i, 0))
    mem = pl.BlockSpec((N_MEM, n), lambda b, i: (b, 0))
    return pl.pallas_call(
        _memattn_prompt_kernel,
        grid=(batch, nq),
        in_specs=[cur, mem, mem],
        out_specs=cur,
        out_shape=jax.ShapeDtypeStruct(qc.shape, BF16),
        compiler_params=_cparams(("arbitrary", "arbitrary")),
        name="memattn_prompt",
    )(qc, mk, mv)


def _memattn_decode_kernel(q_ref, k_ref, v_ref, o_ref, *, rows):
    nk = N_MEM * C_HEADS
    col = lax.broadcasted_iota(jnp.int32, (C_HEADS * DEC_PAD, nk), 1)
    row = lax.broadcasted_iota(jnp.int32, (C_HEADS * DEC_PAD, nk), 0)
    own = (col & (C_HEADS - 1)) == (row >> (DEC_PAD.bit_length() - 1))

    def load(r):
        return q_ref[r], k_ref[r], v_ref[r]

    def scores(args):
        q, k, v = args
        q = q.astype(F32)
        lhs = jnp.concatenate([q[:, h * C_HD:(h + 1) * C_HD] for h in range(C_HEADS)], axis=0)
        return _bdot_nt(lhs, k), v

    def finish(args):
        logits, v = args
        e, inv = _softmax_rows(jnp.where(own, logits * (C_HD ** -0.5), NEG))
        pv = _bdot(e, v) * inv
        return jnp.concatenate([pv[h * DEC_PAD:(h + 1) * DEC_PAD] for h in range(C_HEADS)],
                               axis=1).astype(BF16)

    def store(r, o):
        o_ref[r] = o

    _rows_loop(rows, load, [scores, finish], store)


def _memattn_decode(qc, ck, cv, rows):
    nb = qc.shape[0]
    n = C_HEADS * C_HD
    blk = pl.BlockSpec((rows, DEC_PAD, n), lambda i: (i, 0, 0))
    cache = pl.BlockSpec((rows, N_MEM * C_HEADS, C_HD), lambda i: (i, 0, 0))
    return pl.pallas_call(
        functools.partial(_memattn_decode_kernel, rows=rows),
        grid=(nb // rows,),
        in_specs=[blk, cache, cache],
        out_specs=blk,
        out_shape=jax.ShapeDtypeStruct(qc.shape, BF16),
        compiler_params=_cparams(("arbitrary",)),
        name="memattn_decode",
    )(qc, ck, cv)


def _merge_kernel(x_ref, oa_ref, ob_ref, oc_ref, gpre_ref, wga_ref, wgb_ref, wgc_ref, wb_ref, wo_ref,
                  gpost_ref, gfpre_ref, x1_ref, h2_ref):
    x = x_ref[...]
    ups = [jnp.dot(o_ref[...], wb_ref[n], preferred_element_type=F32)
           for n, o_ref in enumerate((oa_ref, ob_ref, oc_ref))]
    h = _rms(x, gpre_ref[...]).astype(BF16)
    mix = None
    for n, wg_ref in enumerate((wga_ref, wgb_ref, wgc_ref)):
        gate = jax.nn.sigmoid(jnp.dot(h, wg_ref[...], preferred_element_type=F32))
        mix = gate * ups[n] if mix is None else mix + gate * ups[n]
    x1 = x + _rms(_bdot(mix, wo_ref[...]), gpost_ref[...])
    x1_ref[...] = x1
    h2_ref[...] = _rms(x1, gfpre_ref[...]).astype(BF16)


FF_SPLIT = 6 * PAIR


def _ffn_kernel(x1_ref, h2_ref, wfi_ref, wfo_ref, gfpost_ref, y_ref):
    h2 = h2_ref[...]
    f = None
    halves = []
    for a, b in ((0, FF_SPLIT), (FF_SPLIT, D_FF)):
        gt = jnp.dot(h2, wfi_ref[:, a:b], preferred_element_type=F32)
        uf = jnp.dot(h2, wfi_ref[:, D_FF + a:D_FF + b], preferred_element_type=F32)
        halves.append((a, b, gt, uf))
    for a, b, gt, uf in halves:
        part = _bdot(_silu(gt) * uf, wfo_ref[a:b, :])
        f = part if f is None else f + part
    y_ref[...] = x1_ref[...] + _rms(f, gfpost_ref[...])


def _merge_weight_specs():
    vec = _const_spec((1, D_MODEL))
    return ([vec] + [pl.BlockSpec((D_MODEL, D_MODEL), functools.partial(lambda n, *_: (0, n), n + 1),
                                  pipeline_mode=pl.Buffered(1)) for n in range(N_BRANCH)]
            + [_const_spec((N_BRANCH, BRANCH_W, D_MODEL)), _const_spec((D_MODEL, D_MODEL)), vec, vec])


def _merge(x, oa, ob, oc, gpre, wtail, wb, wo, gpost, gfpre, tm):
    t = x.shape[0]
    row = lambda n: pl.BlockSpec((tm, n), lambda i: (i, 0))
    return pl.pallas_call(
        _merge_kernel,
        grid=(t // tm,),
        in_specs=[row(D_MODEL), row(512), row(512), row(512)] + _merge_weight_specs(),
        out_specs=[row(D_MODEL), row(D_MODEL)],
        out_shape=[jax.ShapeDtypeStruct(x.shape, F32), jax.ShapeDtypeStruct(x.shape, BF16)],
        compiler_params=_cparams(("arbitrary",)),
        name="merge",
    )(x, oa, ob, oc, gpre, wtail, wtail, wtail, wb, wo, gpost, gfpre)


def _ffn(x1, h2, wfi, wfo, gfpost, tm):
    t = x1.shape[0]
    vec = _const_spec((1, D_MODEL))
    frow = pl.BlockSpec((tm, D_MODEL), lambda i: (i, 0))
    return pl.pallas_call(
        _ffn_kernel,
        grid=(t // tm,),
        in_specs=[frow, frow, _const_spec((D_MODEL, 2 * D_FF)),
                  _const_spec((D_FF, D_MODEL)), vec],
        out_specs=frow,
        out_shape=jax.ShapeDtypeStruct(x1.shape, F32),
        compiler_params=_cparams(("arbitrary",)),
        name="ffn",
    )(x1, h2, wfi, wfo, gfpost)


def _rope_tables(pos):
    half = A_HD // 2
    inv = ROPE_THETA ** (-jnp.arange(half, dtype=F32) / half)
    ang = pos.astype(F32)[:, None] * inv[None, :]
    cos, sin = jnp.cos(ang), jnp.sin(ang)
    cos = jnp.concatenate([cos, cos], axis=-1)
    sin = jnp.concatenate([-sin, sin], axis=-1)
    return jnp.tile(cos, (1, LANES // A_HD)), jnp.tile(sin, (1, LANES // A_HD))


def _lane_row(vals, offset):
    return jnp.zeros((1, LANES), F32).at[0, offset:offset + vals.shape[0]].set(vals.astype(F32))


def kernel(x_prompt, x_sample, mem_prompt, state_win_k, state_win_v, state_conv, state_rec,
           cache_mem_k, cache_mem_v, ln_mix_pre, w_in, attn_sink, gdn_conv_w, gdn_a_log,
           gdn_dt_bias, gdn_norm_w, ln_mem, w_mem_kv, w_branch, w_out, ln_mix_post,
           ln_ffn_pre, w_ffn_in, w_ffn_out, ln_ffn_post):
    bp, lp, _ = x_prompt.shape
    bs, ls, _ = x_sample.shape

    sizes = [512, 128, 128, B_CONV_CH, B_HEADS, B_HEADS, 512, 512, N_BRANCH * D_MODEL]
    o = np.cumsum([0] + sizes)
    hperm = np.concatenate([np.r_[j * A_HD:(j + 1) * A_HD, (j + 4) * A_HD:(j + 5) * A_HD] for j in range(4)])
    ws = (w_in[:, hperm].astype(BF16), w_in[:, o[1]:o[4]].astype(BF16), w_in[:, o[6]:o[9]].astype(BF16),
          jnp.pad(w_in[:, o[4]:o[6]], ((0, 0), (0, LANES - 2 * B_HEADS))).astype(BF16))
    wb = jnp.concatenate([w_branch[0:1][:, hperm], w_branch[1:]], axis=0).astype(BF16)
    wo = w_out.astype(BF16)
    wfi = w_ffn_in.astype(BF16)
    wfo = w_ffn_out.astype(BF16)
    wmem = w_mem_kv.astype(BF16)
    sink = attn_sink.astype(F32)[np.array([0, 4, 1, 5, 2, 6, 3, 7])]
    vec = lambda g: g.astype(F32).reshape(1, -1)
    alog = _lane_row(gdn_a_log, B_HEADS)
    dtb = _lane_row(gdn_dt_bias, B_HEADS)
    cw = gdn_conv_w.astype(F32)
    nw = vec(gdn_norm_w)

    merge_w = (vec(ln_mix_pre), ws[2], wb, wo, vec(ln_mix_post), vec(ln_ffn_pre))
    ffn = lambda x1, h2: _ffn(x1, h2, wfi, wfo, vec(ln_ffn_post), 512)

    tp = bp * lp
    xp = x_prompt.reshape(tp, D_MODEL)
    cos_p, sin_p = _rope_tables(jnp.arange(lp, dtype=jnp.int32))
    qa, ka, va, z, qc, ba, qn, kn, vv, tail = _proj(xp, vec(ln_mix_pre), ws, cos_p, sin_p, 512, cw, lp)
    b3 = lambda a: a.reshape(bp, lp, a.shape[-1])
    ob, rec_p = _gdn_prompt(b3(qn), b3(kn), b3(vv), b3(ba), b3(z), alog, dtb, nw)
    mk, mv = _memkv(mem_prompt.reshape(bp * N_MEM, D_MODEL), vec(ln_mem), wmem)
    oa = _swa_prompt(sink, qa, ka, va, bp, lp, 2048)
    oc = _memattn_prompt(qc, mk, mv, bp, lp, 2048)
    x1, h2 = _merge(xp, oa, ob.reshape(tp, 512), oc, *merge_w, 256)
    y_p = ffn(x1, h2).reshape(bp, lp, D_MODEL)
    wk_p = ka.reshape(bp, lp, LANES)[:, -WINDOW:].reshape(bp, WINDOW, A_KV, A_HD)
    wv_p = va.reshape(bp, lp, LANES)[:, -WINDOW:].reshape(bp, WINDOW, A_KV, A_HD)
    conv_p = tail[:, -(CONV_W - 1):]
    mem_k_p = mk.reshape(bp, N_MEM, C_HEADS, C_HD)
    mem_v_p = mv.reshape(bp, N_MEM, C_HEADS, C_HD)

    ts = bs * DEC_PAD
    xs = jnp.pad(x_sample, ((0, 0), (0, DEC_PAD - ls), (0, 0))).reshape(ts, D_MODEL)
    cos_s, sin_s = _rope_tables(PAST_LEN + jnp.arange(DEC_PAD, dtype=jnp.int32))
    cos_s, sin_s = jnp.tile(cos_s, (bs, 1)), jnp.tile(sin_s, (bs, 1))
    qa, ka, va, z, qc, ba, qkv = _proj(xs, vec(ln_mix_pre), ws, cos_s, sin_s, 512)
    r3 = lambda a: a.reshape(bs, DEC_PAD, a.shape[-1])
    oa, wk_s, wv_s = _swa_decode(sink, r3(qa), r3(ka), r3(va), state_win_k.reshape(bs, WINDOW, LANES),
                                 state_win_v.reshape(bs, WINDOW, LANES), 16, ls)
    oa = oa.reshape(ts, 512)
    wk_s = wk_s.reshape(state_win_k.shape)
    wv_s = wv_s.reshape(state_win_v.shape)
    histp = jnp.pad(state_conv, ((0, 0), (DEC_PAD - (CONV_W - 1), 0), (0, 0)))
    ob, rec_s = _gdn_decode(r3(qkv), histp, ba, z, state_rec, cw, alog, dtb, nw,
                            LANES // DEC_PAD, ls)
    oc = _memattn_decode(r3(qc), cache_mem_k.reshape(bs, N_MEM * C_HEADS, C_HD),
                         cache_mem_v.reshape(bs, N_MEM * C_HEADS, C_HD), 8).reshape(ts, 512)
    real = lambda a: a.reshape(bs, DEC_PAD, 512)[:, :ls].reshape(bs * ls, 512)
    x1, h2 = _merge(x_sample.reshape(bs * ls, D_MODEL), real(oa), real(ob), real(oc), *merge_w, 256)
    y_s = ffn(x1, h2).reshape(bs, ls, D_MODEL)
    conv_s = r3(qkv)[:, ls - (CONV_W - 1):ls]

    return (y_p, y_s, wk_p, wv_p, conv_p, rec_p, mem_k_p, mem_v_p, wk_s, wv_s, conv_s, rec_s)
```

```python
import functools

import numpy as np
import jax
import jax.numpy as jnp
from jax import lax
from jax.experimental import pallas as pl
from jax.experimental.pallas import tpu as pltpu

F32 = jnp.float32
BF16 = jnp.bfloat16

D_MODEL = 1024
PAST_LEN = 16384
EPS = 1e-6
ROPE_THETA = 10000.0
N_MEM = 256
WINDOW = 128
A_HD = 64
A_HEADS = 8
A_KV = 2
A_SCALE = A_HD ** -0.5
B_HEADS = 4
B_DK = 128
B_DV = 128
CONV_W = 4
GDN_CHUNK = 64
B_CONV_CH = B_HEADS * (2 * B_DK + B_DV)
C_HEADS = 4
C_HD = 128
N_BRANCH = 3
BRANCH_W = 512
D_FF = 2816

LANES = 128
SUBLANES = 8
PAIR = 2 * LANES
VMEM_LIMIT = 56 * 1024 * 1024
NEG = -1e30
DEC_PAD = SUBLANES


def _cparams(sem, vmem=VMEM_LIMIT):
    return pltpu.CompilerParams(dimension_semantics=sem, vmem_limit_bytes=vmem)


def _const_spec(shape):
    nd = len(shape)
    return pl.BlockSpec(shape, lambda *_: (0,) * nd, pipeline_mode=pl.Buffered(1))


def _rms(x, g):
    ms = jnp.mean(x * x, axis=-1, keepdims=True)
    return x * lax.rsqrt(ms + EPS) * g


def _bdot(a, b):
    return jnp.dot(a.astype(BF16), b.astype(BF16), preferred_element_type=F32)


def _bdot_nt(a, b):
    return lax.dot_general(a.astype(BF16), b.astype(BF16), (((1,), (1,)), ((), ())),
                           preferred_element_type=F32)


def _bdot_tn(a, b):
    return lax.dot_general(a.astype(BF16), b.astype(BF16), (((0,), (0,)), ((), ())),
                           preferred_element_type=F32)


def _silu(x):
    return x * jax.nn.sigmoid(x)


def _softplus(x):
    return jnp.maximum(x, 0.0) + jnp.log1p(jnp.exp(-jnp.abs(x)))


def _rope128(v, cos, sin):
    lane = lax.broadcasted_iota(jnp.int32, v.shape, 1)
    fwd = pltpu.roll(v, 32, 1)
    bwd = pltpu.roll(v, 96, 1)
    sw = jnp.where((lane & 32) == 0, bwd, fwd)
    return v * cos + sw * sin


def _l2n(x):
    return x * lax.rsqrt(jnp.sum(x * x, axis=-1, keepdims=True) + EPS)

def _proj_steps(x_ref, g_ref, wq_ref, wr_ref, wzc_ref, wba_ref, cos_ref, sin_ref,
                qa_ref, ka_ref, va_ref, z_ref, qc_ref, ba_ref):
    h = _rms(x_ref[...], g_ref[...]).astype(BF16)
    cos = cos_ref[...]
    sin = sin_ref[...]

    def mm(w_ref, a, b):
        return jnp.dot(h, w_ref[:, a:b], preferred_element_type=F32)

    def qa_half(c0):
        q = mm(wq_ref, c0 * LANES, (c0 + 2) * LANES)
        for c in range(2):
            qa_ref[:, (c0 + c) * LANES:(c0 + c + 1) * LANES] = (
                _rope128(q[:, c * LANES:(c + 1) * LANES], cos, sin) * A_SCALE).astype(BF16)

    def kv():
        kv2 = mm(wr_ref, 0, PAIR)
        ka_ref[...] = _rope128(kv2[:, :LANES], cos, sin)
        va_ref[...] = kv2[:, LANES:]

    def z_half(c0):
        z_ref[:, c0:c0 + PAIR] = mm(wzc_ref, c0, c0 + PAIR)

    def qc_half(c0):
        qc_ref[:, c0:c0 + PAIR] = mm(wzc_ref, 512 + c0, 512 + c0 + PAIR).astype(BF16)

    def ba():
        ba_ref[...] = mm(wba_ref, 0, LANES)

    def qkv(a, b):
        return mm(wr_ref, PAIR + a, PAIR + b)

    steps = [lambda: qa_half(0), lambda: qa_half(2), kv, lambda: z_half(0), lambda: z_half(PAIR),
             lambda: qc_half(0), lambda: qc_half(PAIR), ba]
    return qkv, steps


def _proj_raw_kernel(x_ref, g_ref, wq_ref, wr_ref, wzc_ref, wba_ref, cos_ref, sin_ref,
                     qa_ref, ka_ref, va_ref, z_ref, qc_ref, ba_ref, qkv_ref):
    qkv, steps = _proj_steps(x_ref, g_ref, wq_ref, wr_ref, wzc_ref, wba_ref, cos_ref, sin_ref,
                             qa_ref, ka_ref, va_ref, z_ref, qc_ref, ba_ref)
    qkv_ref[...] = qkv(0, B_CONV_CH)
    for step in steps:
        step()


def _proj_conv_kernel(x_ref, g_ref, wq_ref, wr_ref, wzc_ref, wba_ref, cos_ref, sin_ref, cw_ref,
                      qa_ref, ka_ref, va_ref, z_ref, qc_ref, ba_ref, qn_ref, kn_ref, vv_ref, tail_ref,
                      buf_ref, *, tm, tiles_per_seq):
    hist = SUBLANES
    first = lax.rem(pl.program_id(0), tiles_per_seq) == 0

    @pl.when(first)
    def _():
        buf_ref[0:hist, :] = jnp.zeros((hist, B_CONV_CH), F32)

    @pl.when(jnp.logical_not(first))
    def _():
        buf_ref[0:hist, :] = buf_ref[tm:tm + hist, :]

    qkv, steps = _proj_steps(x_ref, g_ref, wq_ref, wr_ref, wzc_ref, wba_ref, cos_ref, sin_ref,
                             qa_ref, ka_ref, va_ref, z_ref, qc_ref, ba_ref)
    nq = B_HEADS * B_DK

    def conv_group(c0):
        cs = slice(c0, c0 + PAIR)
        raw = qkv(c0, c0 + PAIR)
        buf_ref[hist:hist + tm, cs] = raw
        tail_ref[0, :, cs] = raw[tm - hist:, :]
        xb = buf_ref[:, cs]
        acc = xb * cw_ref[0:1, cs]
        for i in range(1, CONV_W):
            acc = pltpu.roll(acc, 1, 0) + xb * cw_ref[i:i + 1, cs]
        act = _silu(acc[hist:])
        if c0 >= 2 * nq:
            vv_ref[:, c0 - 2 * nq:c0 - 2 * nq + PAIR] = act
            return
        out_ref, base, scale = (qn_ref, 0, B_DK ** -0.5) if c0 < nq else (kn_ref, nq, 1.0)
        for s in range(2):
            o0 = c0 - base + s * B_DK
            out_ref[:, o0:o0 + B_DK] = _l2n(act[:, s * B_DK:(s + 1) * B_DK]) * scale

    groups = [functools.partial(conv_group, c0) for c0 in range(0, B_CONV_CH, PAIR)]
    while groups or steps:
        if groups:
            groups.pop(0)()
        if steps:
            steps.pop(0)()


_PROJ_OUTS = [(512, BF16), (128, F32), (128, F32), (512, F32), (512, BF16), (128, F32)]


def _proj(x, gain, ws, cos, sin, tm, cw=None, seq=None):
    wq, wr, wtail, wba = ws
    t = x.shape[0]
    ntab = cos.shape[0] // tm
    row = lambda n: pl.BlockSpec((tm, n), lambda i: (i, 0))
    tab = pl.BlockSpec((tm, LANES), lambda i: (i % ntab, 0))
    in_specs = [row(D_MODEL), _const_spec((1, D_MODEL)), _const_spec(wq.shape), _const_spec(wr.shape),
                _const_spec((D_MODEL, D_MODEL)), _const_spec(wba.shape), tab, tab]
    out_specs = [row(n) for n, _ in _PROJ_OUTS]
    out_shape = [jax.ShapeDtypeStruct((t, n), d) for n, d in _PROJ_OUTS]
    if cw is None:
        return pl.pallas_call(
            _proj_raw_kernel, grid=(t // tm,), in_specs=in_specs,
            out_specs=out_specs + [row(B_CONV_CH)],
            out_shape=out_shape + [jax.ShapeDtypeStruct((t, B_CONV_CH), F32)],
            compiler_params=_cparams(("arbitrary",)), name="proj",
        )(x, gain, wq, wr, wtail, wba, cos, sin)
    tiles = seq // tm
    return pl.pallas_call(
        functools.partial(_proj_conv_kernel, tm=tm, tiles_per_seq=tiles),
        grid=(t // tm,),
        in_specs=in_specs + [_const_spec((CONV_W, B_CONV_CH))],
        out_specs=out_specs + [row(512)] * 3
        + [pl.BlockSpec((1, SUBLANES, B_CONV_CH), lambda i: (i // tiles, 0, 0))],
        out_shape=out_shape + [jax.ShapeDtypeStruct((t, 512), F32)] * 3
        + [jax.ShapeDtypeStruct((t // seq, SUBLANES, B_CONV_CH), F32)],
        scratch_shapes=[pltpu.VMEM((tm + SUBLANES, B_CONV_CH), F32)],
        compiler_params=_cparams(("arbitrary",)), name="proj_conv",
    )(x, gain, wq, wr, wtail, wba, cos, sin, cw)


def _swa_scores(q, k16):
    tq = q.shape[0]
    lo = lax.broadcasted_iota(jnp.int32, (tq, LANES), 1) < A_HD
    blocks = []
    for j in range(4):
        c = q[:, j * LANES:(j + 1) * LANES].astype(F32)
        blocks.append(jnp.where(lo, c, 0.0))
        blocks.append(jnp.where(lo, 0.0, c))
    lhs = jnp.concatenate(blocks, axis=0).astype(BF16)
    return lax.dot_general(lhs, k16, (((1,), (1,)), ((), ())), preferred_element_type=F32)


def _swa_finish(logits, v16, mask, sink_ref):
    tq = logits.shape[0] // A_HEADS
    lo = lax.broadcasted_iota(jnp.int32, (tq, LANES), 1) < A_HD
    es, inv = [], []
    for s in range(8):
        l = jnp.where(mask, logits[s * tq:(s + 1) * tq], NEG)
        sk = sink_ref[s]
        m = jnp.maximum(jnp.max(l, axis=-1, keepdims=True), sk)
        e = jnp.exp(l - m)
        den = jnp.sum(e, axis=-1, keepdims=True) + jnp.exp(sk - m)
        es.append(e.astype(BF16))
        inv.append(1.0 / den)
    pv = jnp.dot(jnp.concatenate(es, axis=0), v16, preferred_element_type=F32)
    outs = []
    for j in range(4):
        a = pv[(2 * j) * tq:(2 * j + 1) * tq] * inv[2 * j]
        b = pv[(2 * j + 1) * tq:(2 * j + 2) * tq] * inv[2 * j + 1]
        outs.append(jnp.where(lo, a, b))
    return jnp.concatenate(outs, axis=1)


def _swa_prompt_kernel(sink_ref, q_ref, kc_ref, kp_ref, vc_ref, vp_ref, o_ref, *, nblk):
    i = pl.program_id(1)
    kcat = jnp.concatenate([kp_ref[...], kc_ref[...]], axis=0).astype(BF16)
    vcat = jnp.concatenate([vp_ref[...], vc_ref[...]], axis=0).astype(BF16)
    ii = lax.broadcasted_iota(jnp.int32, (WINDOW, 2 * WINDOW), 0)
    jj = lax.broadcasted_iota(jnp.int32, (WINDOW, 2 * WINDOW), 1)
    band = (jj > ii) & (jj <= ii + WINDOW)
    for jb in range(nblk):
        mask = band
        if jb == 0:
            mask = band & ((jj >= WINDOW) | (i > 0))
        logits = _swa_scores(q_ref[jb * WINDOW:(jb + 1) * WINDOW, :], kcat[jb * WINDOW:(jb + 2) * WINDOW])
        o = _swa_finish(logits, vcat[jb * WINDOW:(jb + 2) * WINDOW], mask, sink_ref)
        o_ref[jb * WINDOW:(jb + 1) * WINDOW, :] = o.astype(BF16)


def _swa_prompt(sink, qa, ka, va, batch, seq, tq):
    nq = seq // tq
    nblk = tq // WINDOW
    nw = seq // WINDOW
    cur = lambda n: pl.BlockSpec((tq, n), lambda b, i: (b * nq + i, 0))
    prev = pl.BlockSpec((WINDOW, LANES), lambda b, i: (jnp.maximum(b * nw + i * nblk - 1, 0), 0))
    return pl.pallas_call(
        functools.partial(_swa_prompt_kernel, nblk=nblk),
        grid=(batch, nq),
        in_specs=[pl.BlockSpec(memory_space=pltpu.SMEM), cur(512), cur(LANES), prev, cur(LANES), prev],
        out_specs=cur(512),
        out_shape=jax.ShapeDtypeStruct(qa.shape, BF16),
        compiler_params=_cparams(("arbitrary", "arbitrary")),
        name="swa_prompt",
    )(sink, qa, ka, ka, va, va)


ROW_UNROLL = 8


def _rows_loop(rows, load, stages, store):
    def body(g, carry):
        idx = [g * ROW_UNROLL + j for j in range(ROW_UNROLL)]
        vals = [load(r) for r in idx]
        for stage in stages:
            vals = [stage(v) for v in vals]
        for r, o in zip(idx, vals):
            store(r, o)
        return carry

    lax.fori_loop(0, rows // ROW_UNROLL, body, 0)


def _swa_decode_kernel(sink_ref, q_ref, kn_ref, vn_ref, wk_ref, wv_ref, o_ref, wko_ref, wvo_ref,
                       *, rows, new_len):
    ii = lax.broadcasted_iota(jnp.int32, (DEC_PAD, 2 * WINDOW), 0)
    jj = lax.broadcasted_iota(jnp.int32, (DEC_PAD, 2 * WINDOW), 1)
    mask = ((jj < WINDOW) & (jj > ii)) | ((jj >= WINDOW) & (jj - WINDOW <= ii))
    pad = jnp.zeros((WINDOW - DEC_PAD, LANES), F32)
    keep = lax.broadcasted_iota(jnp.int32, (WINDOW, LANES), 0) < WINDOW - new_len

    def slide(win, new):
        tail = jnp.concatenate([pad, new], axis=0)
        return jnp.where(keep, pltpu.roll(win, WINDOW - new_len, 0), pltpu.roll(tail, DEC_PAD - new_len, 0))

    def load(r):
        return q_ref[r], wk_ref[r], kn_ref[r], wv_ref[r], vn_ref[r]

    def scores(args):
        q, wk, kn, wv, vn = args
        kk = jnp.concatenate([wk, kn, pad], axis=0).astype(BF16)
        return _swa_scores(q, kk), wk, kn, wv, vn

    def finish(args):
        logits, wk, kn, wv, vn = args
        vv = jnp.concatenate([wv, vn, pad], axis=0).astype(BF16)
        return _swa_finish(logits, vv, mask, sink_ref).astype(BF16), slide(wk, kn), slide(wv, vn)

    def store(r, outs):
        o_ref[r], wko_ref[r], wvo_ref[r] = outs

    _rows_loop(rows, load, [scores, finish], store)


def _swa_decode(sink, qa, kn, vn, wk, wv, rows, new_len):
    nb = qa.shape[0]
    blk = lambda a, n: pl.BlockSpec((rows, a, n), lambda i: (i, 0, 0))
    return pl.pallas_call(
        functools.partial(_swa_decode_kernel, rows=rows, new_len=new_len),
        grid=(nb // rows,),
        in_specs=[pl.BlockSpec(memory_space=pltpu.SMEM), blk(DEC_PAD, 512), blk(DEC_PAD, LANES),
                  blk(DEC_PAD, LANES), blk(WINDOW, LANES), blk(WINDOW, LANES)],
        out_specs=[blk(DEC_PAD, 512), blk(WINDOW, LANES), blk(WINDOW, LANES)],
        out_shape=[jax.ShapeDtypeStruct(qa.shape, BF16), jax.ShapeDtypeStruct(wk.shape, F32),
                   jax.ShapeDtypeStruct(wv.shape, F32)],
        compiler_params=_cparams(("arbitrary",)),
        name="swa_decode",
    )(sink, qa, kn, vn, wk, wv)


def _gdn_gates(ba, alog, dtb, valid):
    beta = jax.nn.sigmoid(ba)
    g = -jnp.exp(alog) * _softplus(ba + dtb)
    if valid is not None:
        beta = jnp.where(valid, beta, 0.0)
        g = jnp.where(valid, g, 0.0)
    return beta, g


def _chunk_masks(chunk):
    sh = chunk.bit_length() - 1
    ri = lax.broadcasted_iota(jnp.int32, (LANES, LANES), 0)
    ci = lax.broadcasted_iota(jnp.int32, (LANES, LANES), 1)
    same = (ri >> sh) == (ci >> sh)
    return same, same & (ri >= ci), same & (ri > ci), ri == ci


def _gdn_cumsums(g_all, chunk):
    n = g_all.shape[0]
    pos = lax.broadcasted_iota(jnp.int32, g_all.shape, 0) & (chunk - 1)
    gcol = g_all
    step = 1
    while step < chunk:
        gcol = gcol + jnp.where(pos >= step, pltpu.roll(gcol, step, 0), 0.0)
        step *= 2
    gtot = jnp.where(pos == chunk - 1, gcol, 0.0)
    step = 1
    while step < chunk:
        gtot = gtot + jnp.where(pos + step < chunk, pltpu.roll(gtot, n - step, 0), 0.0)
        step *= 2
    return gcol, gcol.T, gtot


def _sibling_mask(ri, ci, lvl):
    rb = ri >> lvl
    return ((rb & 1) == 1) & ((ci >> lvl) == rb - 1)


def _gdn_phase_a(heads, chunk):
    _, tri, strict, _ = _chunk_masks(chunk)
    ri = lax.broadcasted_iota(jnp.int32, (LANES, LANES), 0)
    ci = lax.broadcasted_iota(jnp.int32, (LANES, LANES), 1)
    kbs = [k * beta for q, k, v, beta, gcol, grow, gtot in heads]
    kkqk = [_bdot_nt(jnp.concatenate([kb, hd[0]], axis=0), hd[1]) for kb, hd in zip(kbs, heads)]
    a, qk = [], []
    for r, (q, k, v, beta, gcol, grow, gtot) in zip(kkqk, heads):
        decay = jnp.exp(jnp.where(tri, gcol - grow, NEG))
        a.append(jnp.where(strict, r[:LANES] * decay, 0.0))
        qk.append(r[LANES:] * decay)
    n = [-jnp.where(_sibling_mask(ri, ci, 0), ah, 0.0) for ah in a]
    for lvl in range(1, chunk.bit_length() - 1):
        sib = _sibling_mask(ri, ci, lvl)
        al = [jnp.where(sib, ah, 0.0) for ah in a]
        x = [alh + _bdot(nh, alh) for alh, nh in zip(al, n)]
        n = [nh - (xh + _bdot(xh, nh)) for xh, nh in zip(x, n)]
    outs = []
    rhs = [jnp.concatenate([hd[2] * hd[3], kb * jnp.exp(hd[4])], axis=1) for kb, hd in zip(kbs, heads)]
    sol = [r + _bdot(nh, r) for r, nh in zip(rhs, n)]
    for s, qkh, (q, k, v, beta, gcol, grow, gtot) in zip(sol, qk, heads):
        outs.append((s[:, :B_DV], s[:, B_DV:], q * jnp.exp(gcol), k * jnp.exp(gtot - gcol), qkh, jnp.exp(gtot)))
    return outs


def _gdn_qkv_heads(qkv, h):
    q = qkv[:, h * B_DK:(h + 1) * B_DK]
    k = qkv[:, (B_HEADS + h) * B_DK:(B_HEADS + h + 1) * B_DK]
    v = qkv[:, (2 * B_HEADS + h) * B_DK:(2 * B_HEADS + h + 1) * B_DK]
    q = q * lax.rsqrt(jnp.sum(q * q, axis=-1, keepdims=True) + EPS) * (B_DK ** -0.5)
    k = k * lax.rsqrt(jnp.sum(k * k, axis=-1, keepdims=True) + EPS)
    return q, k, v


def _gdn_out(o, z, nw):
    return _rms(o, nw) * _silu(z)


def _pair_bd(x2):
    lo = lax.broadcasted_iota(jnp.int32, x2.shape, 1) < LANES
    return jnp.concatenate([jnp.where(lo, x2, 0.0), jnp.where(lo, 0.0, x2)], axis=0).astype(BF16)


def _quad_bd(x4):
    blk = lax.broadcasted_iota(jnp.int32, x4.shape, 1) >> (GDN_CHUNK.bit_length() - 1)
    return jnp.concatenate([jnp.where(blk == j, x4, 0.0) for j in range(PAIR // GDN_CHUNK)],
                           axis=0).astype(BF16)


def _pair_cols(x, l0, l1):
    lo = lax.broadcasted_iota(jnp.int32, (x.shape[0], PAIR), 1) < LANES
    return jnp.where(lo, x[:, l0:l0 + 1], x[:, l1:l1 + 1])


def _gdn_prompt_kernel(q_ref, k_ref, v_ref, ba_ref, z_ref, alog_ref, dtb_ref, nw_ref,
                       ob_ref, s_ref, sbd_ref, *, nb, ntb):
    t = pl.program_id(0)
    npair = B_HEADS // 2

    @pl.when(t == 0)
    def _():
        sbd_ref[...] = jnp.zeros(sbd_ref.shape, F32)

    alog = alog_ref[...]
    dtb = dtb_ref[...]
    nw = nw_ref[...]
    nchunk = LANES // GDN_CHUNK
    _, tri, strict, _ = _chunk_masks(GDN_CHUNK)
    tri2 = jnp.concatenate([tri, tri], axis=1)
    strict2 = jnp.concatenate([strict, strict], axis=1)

    items = []
    for b, tb in [(b, tb) for b in range(nb) for tb in range(ntb)]:
        rsl = slice(tb * LANES, (tb + 1) * LANES)
        beta_all, g_all = _gdn_gates(ba_ref[b, rsl, :], alog, dtb, None)
        gcol_all, grow_all, gtot_all = _gdn_cumsums(g_all, GDN_CHUNK)
        eg_all = jnp.exp(gcol_all)
        ed_all = jnp.exp(gtot_all - gcol_all)
        et_all = jnp.exp(gtot_all)
        for p in range(npair):
            h0, h1 = 2 * p, 2 * p + 1
            g0, g1 = B_HEADS + h0, B_HEADS + h1
            psl = slice(p * PAIR, (p + 1) * PAIR)
            q2 = q_ref[b, rsl, psl]
            k2 = k_ref[b, rsl, psl]
            v2 = v_ref[b, rsl, psl]
            beta2 = _pair_cols(beta_all, h0, h1)
            grow2 = jnp.concatenate([grow_all[g0:g0 + 1, :], grow_all[g1:g1 + 1, :]], axis=1)
            decay2 = jnp.exp(jnp.where(tri2, _pair_cols(gcol_all, g0, g1) - grow2, NEG))
            kb2 = k2 * beta2
            items.append(dict(
                b=b, tb=tb, p=p, rsl=rsl, decay2=decay2, kb2=kb2, v2b=v2 * beta2,
                kbe2=kb2 * _pair_cols(eg_all, g0, g1),
                lhs=jnp.concatenate([kb2, q2], axis=0).astype(BF16),
                kbd=_pair_bd(k2),
                qe2=(q2 * _pair_cols(eg_all, g0, g1)).astype(BF16),
                kd2=(k2 * _pair_cols(ed_all, g0, g1)).astype(BF16),
                et2=_pair_cols(et_all, g0, g1)))

    for it in items:
        kkqk = lax.dot_general(it["lhs"], it["kbd"], (((1,), (1,)), ((), ())),
                               preferred_element_type=F32)
        it["a2"] = jnp.where(strict2, kkqk[:LANES] * it["decay2"], 0.0)
        it["qk2"] = (kkqk[LANES:] * it["decay2"]).astype(BF16)
    qlane = lax.broadcasted_iota(jnp.int32, (GDN_CHUNK, PAIR), 1)
    qrow = lax.broadcasted_iota(jnp.int32, (GDN_CHUNK, PAIR), 0)
    qcol = qlane & (GDN_CHUNK - 1)
    top = (qlane & GDN_CHUNK) == 0
    for it in items:
        it["a64"] = it["a2"][:GDN_CHUNK] + it["a2"][GDN_CHUNK:]
        it["n"] = -jnp.where(_sibling_mask(qrow, qcol, 0), it["a64"], 0.0)
    for lvl in range(1, GDN_CHUNK.bit_length() - 1):
        sib = _sibling_mask(qrow, qcol, lvl)
        for it in items:
            it["al"] = jnp.where(sib, it["a64"], 0.0)
            it["x"] = it["al"] + jnp.dot(it["n"].astype(BF16), _quad_bd(it["al"]), preferred_element_type=F32)
        for it in items:
            x = it["x"]
            it["n"] = it["n"] - (x + jnp.dot(x.astype(BF16), _quad_bd(it["n"]), preferred_element_type=F32))
    for it in items:
        n64 = it["n"]
        it["n"] = jnp.concatenate([jnp.where(top, n64, 0.0), jnp.where(top, 0.0, n64)], axis=0)
    for it in items:
        us, ws = [], []
        for s in range(2):
            sl = slice(s * LANES, (s + 1) * LANES)
            rhs = jnp.concatenate([it["v2b"][:, sl], it["kbe2"][:, sl]], axis=1)
            sol = rhs + _bdot(it["n"][:, sl], rhs)
            us.append(sol[:, :B_DV])
            ws.append(sol[:, B_DV:])
        it["u2"] = jnp.concatenate(us, axis=1)
        it["w2"] = jnp.concatenate(ws, axis=1).astype(BF16)

    ri = lax.broadcasted_iota(jnp.int32, (PAIR, PAIR), 0) < LANES
    ci = lax.broadcasted_iota(jnp.int32, (PAIR, PAIR), 1) < LANES
    bdmask = ri == ci
    nchain = nb * npair
    state = [sbd_ref[i] for i in range(nchain)]
    outs = {}
    for tb in range(ntb):
        cur = [(it["b"] * npair + it["p"], i, it) for i, it in enumerate(items) if it["tb"] == tb]
        for c in range(nchunk):
            rows = slice(c * GDN_CHUNK, (c + 1) * GDN_CHUNK)
            rs = [jnp.dot(jnp.concatenate([it["w2"][rows], it["qe2"][rows]], axis=0), state[ch].astype(BF16),
                          preferred_element_type=F32) for ch, _, it in cur]
            for (ch, i, it), r in zip(cur, rs):
                s = state[ch]
                vn2 = it["u2"][rows] - r[:GDN_CHUNK]
                vt = _pair_bd(jnp.concatenate([vn2] * nchunk, axis=0))
                outs[(i, c)] = r[GDN_CHUNK:] + jnp.dot(it["qk2"][rows], vt, preferred_element_type=F32)
                upd = lax.dot_general(it["kd2"][rows], vn2.astype(BF16), (((0,), (0,)), ((), ())),
                                      preferred_element_type=F32)
                state[ch] = s * it["et2"][c * GDN_CHUNK:c * GDN_CHUNK + 1, :] + jnp.where(bdmask, upd, 0.0)
    for ch in range(nchain):
        sbd_ref[ch] = state[ch]

    @pl.when(t == pl.num_programs(0) - 1)
    def _():
        for ch in range(nchain):
            b, p = divmod(ch, npair)
            s_ref[b, 2 * p] = state[ch][:LANES, :LANES]
            s_ref[b, 2 * p + 1] = state[ch][LANES:, LANES:]

    for i, it in enumerate(items):
        o2 = jnp.concatenate([outs[(i, c)] for c in range(nchunk)], axis=0)
        for s in range(2):
            h = 2 * it["p"] + s
            sl = slice(h * B_DV, (h + 1) * B_DV)
            o = o2[:, s * LANES:(s + 1) * LANES]
            ob_ref[it["b"], it["rsl"], sl] = _gdn_out(o, z_ref[it["b"], it["rsl"], sl], nw).astype(BF16)


GDN_TOKEN_BLOCKS = 2


def _gdn_prompt(q, k, v, ba, z, alog, dtb, nw):
    batch, seq, _ = q.shape
    tm = GDN_TOKEN_BLOCKS * LANES
    tok = lambda n: pl.BlockSpec((batch, tm, n), lambda t: (0, t, 0))
    state = pl.BlockSpec((batch, B_HEADS, B_DK, B_DV), lambda t: (0, 0, 0, 0))
    return pl.pallas_call(
        functools.partial(_gdn_prompt_kernel, nb=batch, ntb=GDN_TOKEN_BLOCKS),
        grid=(seq // tm,),
        in_specs=[tok(512), tok(512), tok(512), tok(LANES), tok(512),
                  _const_spec((1, LANES)), _const_spec((1, LANES)), _const_spec((1, B_DV))],
        out_specs=[tok(512), state],
        out_shape=[jax.ShapeDtypeStruct((batch, seq, 512), BF16),
                   jax.ShapeDtypeStruct((batch, B_HEADS, B_DK, B_DV), F32)],
        scratch_shapes=[pltpu.VMEM((batch * (B_HEADS // 2), PAIR, PAIR), F32)],
        compiler_params=_cparams(("arbitrary",)),
        name="gdn_prompt",
    )(q, k, v, ba, z, alog, dtb, nw)


def _gdn_decode_kernel(raw_ref, hist_ref, ba_ref, z_ref, rec_ref, cw_ref, alog_ref, dtb_ref, nw_ref,
                       ob_ref, s_ref, buf_ref, u_s, w_s, qe_s, kd_s, qk_s, eg_s, o_s, *, rows, valid_len):
    buf_ref[:, 0:DEC_PAD, :] = hist_ref[...]
    buf_ref[:, DEC_PAD:2 * DEC_PAD, :] = raw_ref[...]
    conv = None
    for i in range(CONV_W):
        off = DEC_PAD - (CONV_W - 1) + i
        term = buf_ref[:, off:off + DEC_PAD, :] * cw_ref[i:i + 1, :]
        conv = term if conv is None else conv + term
    qkv = _silu(conv.reshape(rows * DEC_PAD, B_CONV_CH))
    tok = lax.broadcasted_iota(jnp.int32, (LANES, LANES), 0) & (DEC_PAD - 1)
    beta_all, g_all = _gdn_gates(ba_ref[...], alog_ref[...], dtb_ref[...], tok < valid_len)
    gcol_all, grow_all, gtot_all = _gdn_cumsums(g_all, DEC_PAD)
    heads = []
    for h in range(B_HEADS):
        gl = B_HEADS + h
        heads.append(_gdn_qkv_heads(qkv, h) + (beta_all[:, h:h + 1], gcol_all[:, gl:gl + 1],
                                               grow_all[gl:gl + 1, :], gtot_all[:, gl:gl + 1]))
    for h, (u, w, qe, kd, qk, egt) in enumerate(_gdn_phase_a(heads, DEC_PAD)):
        u_s[h] = u
        w_s[h] = w
        qe_s[h] = qe
        kd_s[h] = kd
        qk_s[h] = qk
        eg_s[h] = jnp.broadcast_to(egt, (LANES, LANES))

    def load(r):
        r0 = pl.multiple_of(r * DEC_PAD, DEC_PAD)
        rr = pl.ds(r0, DEC_PAD)
        return [(rec_ref[r, h], w_s[h, rr, :], qe_s[h, rr, :], u_s[h, rr, :], qk_s[h, rr, :],
                 kd_s[h, rr, :], eg_s[h, pl.ds(r0, 1), :]) for h in range(B_HEADS)]

    def read_state(heads):
        return [(_bdot(jnp.concatenate([w, qe], axis=0), s), s, u, qk, kd, eg)
                for s, w, qe, u, qk, kd, eg in heads]

    def update(heads):
        outs = []
        for res, s, u, qk, kd, eg in heads:
            v_new = u - res[:DEC_PAD]
            vt = jnp.concatenate([v_new] * (LANES // DEC_PAD), axis=0)
            outs.append((res[DEC_PAD:] + _bdot(qk, vt), s * eg + _bdot_tn(kd, v_new)))
        return outs

    def store(r, outs):
        rr = pl.ds(pl.multiple_of(r * DEC_PAD, DEC_PAD), DEC_PAD)
        for h, (o, s_new) in enumerate(outs):
            o_s[h, rr, :] = o
            s_ref[r, h] = s_new

    _rows_loop(rows, load, [read_state, update], store)
    nw = nw_ref[...]
    for h in range(B_HEADS):
        sl = slice(h * B_DV, (h + 1) * B_DV)
        ob_ref[:, sl] = _gdn_out(o_s[h], z_ref[:, sl], nw).astype(BF16)


def _gdn_decode(raw, histp, ba, z, rec, cw, alog, dtb, nw, rows, valid_len):
    nb = raw.shape[0]
    flat = rows * DEC_PAD
    assert flat == LANES
    sq = lambda: pltpu.VMEM((B_HEADS, LANES, LANES), F32)
    return pl.pallas_call(
        functools.partial(_gdn_decode_kernel, rows=rows, valid_len=valid_len),
        grid=(nb // rows,),
        in_specs=[pl.BlockSpec((rows, DEC_PAD, B_CONV_CH), lambda i: (i, 0, 0)),
                  pl.BlockSpec((rows, DEC_PAD, B_CONV_CH), lambda i: (i, 0, 0)),
                  pl.BlockSpec((flat, LANES), lambda i: (i, 0)),
                  pl.BlockSpec((flat, 512), lambda i: (i, 0)),
                  pl.BlockSpec((rows, B_HEADS, B_DK, B_DV), lambda i: (i, 0, 0, 0)),
                  _const_spec((CONV_W, B_CONV_CH)), _const_spec((1, LANES)), _const_spec((1, LANES)),
                  _const_spec((1, B_DV))],
        out_specs=[pl.BlockSpec((flat, 512), lambda i: (i, 0)),
                   pl.BlockSpec((rows, B_HEADS, B_DK, B_DV), lambda i: (i, 0, 0, 0))],
        out_shape=[jax.ShapeDtypeStruct((nb * DEC_PAD, 512), BF16),
                   jax.ShapeDtypeStruct(rec.shape, F32)],
        scratch_shapes=[pltpu.VMEM((rows, 2 * DEC_PAD, B_CONV_CH), F32)] + [sq() for _ in range(7)],
        compiler_params=_cparams(("arbitrary",)),
        name="gdn_decode",
    )(raw, histp, ba, z, rec, cw, alog, dtb, nw)


def _memkv_kernel(m_ref, g_ref, w_ref, k_ref, v_ref):
    h = _rms(m_ref[...], g_ref[...]).astype(BF16)
    n = C_HEADS * C_HD
    k_ref[...] = jnp.dot(h, w_ref[:, :n], preferred_element_type=F32)
    v_ref[...] = jnp.dot(h, w_ref[:, n:], preferred_element_type=F32)


def _memkv(mem, gain, w):
    t = mem.shape[0]
    tm = 512
    n = C_HEADS * C_HD
    return pl.pallas_call(
        _memkv_kernel,
        grid=(t // tm,),
        in_specs=[pl.BlockSpec((tm, D_MODEL), lambda i: (i, 0)), _const_spec((1, D_MODEL)),
                  _const_spec((D_MODEL, 2 * n))],
        out_specs=[pl.BlockSpec((tm, n), lambda i: (i, 0))] * 2,
        out_shape=[jax.ShapeDtypeStruct((t, n), F32)] * 2,
        compiler_params=_cparams(("arbitrary",)),
        name="memkv",
    )(mem, gain, w)


def _softmax_rows(logits):
    m = jnp.max(logits, axis=-1, keepdims=True)
    e = jnp.exp(logits - m)
    return e, 1.0 / jnp.sum(e, axis=-1, keepdims=True)


def _memattn_prompt_kernel(q_ref, k_ref, v_ref, o_ref):
    hs = lambda h: slice(h * C_HD, (h + 1) * C_HD)
    scores = lambda h: _bdot_nt(q_ref[:, hs(h)], k_ref[:, hs(h)])
    logits = scores(0)
    for h in range(C_HEADS):
        nxt = scores(h + 1) if h + 1 < C_HEADS else None
        e, inv = _softmax_rows(logits * (C_HD ** -0.5))
        o_ref[:, hs(h)] = (_bdot(e, v_ref[:, hs(h)]) * inv).astype(BF16)
        logits = nxt


def _memattn_prompt(qc, mk, mv, batch, seq, tm):
    nq = seq // tm
    n = C_HEADS * C_HD
    cur = pl.BlockSpec((tm, n), lambda b, i: (b * nq + i, 0))
    mem = pl.BlockSpec((N_MEM, n), lambda b, i: (b, 0))
    return pl.pallas_call(
        _memattn_prompt_kernel,
        grid=(batch, nq),
        in_specs=[cur, mem, mem],
        out_specs=cur,
        out_shape=jax.ShapeDtypeStruct(qc.shape, BF16),
        compiler_params=_cparams(("arbitrary", "arbitrary")),
        name="memattn_prompt",
    )(qc, mk, mv)


def _memattn_decode_kernel(q_ref, k_ref, v_ref, o_ref, *, rows):
    nk = N_MEM * C_HEADS
    col = lax.broadcasted_iota(jnp.int32, (C_HEADS * DEC_PAD, nk), 1)
    row = lax.broadcasted_iota(jnp.int32, (C_HEADS * DEC_PAD, nk), 0)
    own = (col & (C_HEADS - 1)) == (row >> (DEC_PAD.bit_length() - 1))

    def load(r):
        return q_ref[r], k_ref[r], v_ref[r]

    def scores(args):
        q, k, v = args
        q = q.astype(F32)
        lhs = jnp.concatenate([q[:, h * C_HD:(h + 1) * C_HD] for h in range(C_HEADS)], axis=0)
        return _bdot_nt(lhs, k), v

    def finish(args):
        logits, v = args
        e, inv = _softmax_rows(jnp.where(own, logits * (C_HD ** -0.5), NEG))
        pv = _bdot(e, v) * inv
        return jnp.concatenate([pv[h * DEC_PAD:(h + 1) * DEC_PAD] for h in range(C_HEADS)],
                               axis=1).astype(BF16)

    def store(r, o):
        o_ref[r] = o

    _rows_loop(rows, load, [scores, finish], store)


def _memattn_decode(qc, ck, cv, rows):
    nb = qc.shape[0]
    n = C_HEADS * C_HD
    blk = pl.BlockSpec((rows, DEC_PAD, n), lambda i: (i, 0, 0))
    cache = pl.BlockSpec((rows, N_MEM * C_HEADS, C_HD), lambda i: (i, 0, 0))
    return pl.pallas_call(
        functools.partial(_memattn_decode_kernel, rows=rows),
        grid=(nb // rows,),
        in_specs=[blk, cache, cache],
        out_specs=blk,
        out_shape=jax.ShapeDtypeStruct(qc.shape, BF16),
        compiler_params=_cparams(("arbitrary",)),
        name="memattn_decode",
    )(qc, ck, cv)


def _merge_kernel(x_ref, oa_ref, ob_ref, oc_ref, gpre_ref, wga_ref, wgb_ref, wgc_ref, wb_ref, wo_ref,
                  gpost_ref, gfpre_ref, x1_ref, h2_ref):
    x = x_ref[...]
    ups = [jnp.dot(o_ref[...], wb_ref[n], preferred_element_type=F32)
           for n, o_ref in enumerate((oa_ref, ob_ref, oc_ref))]
    h = _rms(x, gpre_ref[...]).astype(BF16)
    mix = None
    for n, wg_ref in enumerate((wga_ref, wgb_ref, wgc_ref)):
        gate = jax.nn.sigmoid(jnp.dot(h, wg_ref[...], preferred_element_type=F32))
        mix = gate * ups[n] if mix is None else mix + gate * ups[n]
    x1 = x + _rms(_bdot(mix, wo_ref[...]), gpost_ref[...])
    x1_ref[...] = x1
    h2_ref[...] = _rms(x1, gfpre_ref[...]).astype(BF16)


FF_SPLIT = 6 * PAIR


def _ffn_kernel(x1_ref, h2_ref, wfi_ref, wfo_ref, gfpost_ref, y_ref):
    h2 = h2_ref[...]
    f = None
    halves = []
    for a, b in ((0, FF_SPLIT), (FF_SPLIT, D_FF)):
        gt = jnp.dot(h2, wfi_ref[:, a:b], preferred_element_type=F32)
        uf = jnp.dot(h2, wfi_ref[:, D_FF + a:D_FF + b], preferred_element_type=F32)
        halves.append((a, b, gt, uf))
    for a, b, gt, uf in halves:
        part = _bdot(_silu(gt) * uf, wfo_ref[a:b, :])
        f = part if f is None else f + part
    y_ref[...] = x1_ref[...] + _rms(f, gfpost_ref[...])


def _merge_weight_specs():
    vec = _const_spec((1, D_MODEL))
    return ([vec] + [pl.BlockSpec((D_MODEL, D_MODEL), functools.partial(lambda n, *_: (0, n), n + 1),
                                  pipeline_mode=pl.Buffered(1)) for n in range(N_BRANCH)]
            + [_const_spec((N_BRANCH, BRANCH_W, D_MODEL)), _const_spec((D_MODEL, D_MODEL)), vec, vec])


def _merge(x, oa, ob, oc, gpre, wtail, wb, wo, gpost, gfpre, tm):
    t = x.shape[0]
    row = lambda n: pl.BlockSpec((tm, n), lambda i: (i, 0))
    return pl.pallas_call(
        _merge_kernel,
        grid=(t // tm,),
        in_specs=[row(D_MODEL), row(512), row(512), row(512)] + _merge_weight_specs(),
        out_specs=[row(D_MODEL), row(D_MODEL)],
        out_shape=[jax.ShapeDtypeStruct(x.shape, F32), jax.ShapeDtypeStruct(x.shape, BF16)],
        compiler_params=_cparams(("arbitrary",)),
        name="merge",
    )(x, oa, ob, oc, gpre, wtail, wtail, wtail, wb, wo, gpost, gfpre)


def _ffn(x1, h2, wfi, wfo, gfpost, tm):
    t = x1.shape[0]
    vec = _const_spec((1, D_MODEL))
    frow = pl.BlockSpec((tm, D_MODEL), lambda i: (i, 0))
    return pl.pallas_call(
        _ffn_kernel,
        grid=(t // tm,),
        in_specs=[frow, frow, _const_spec((D_MODEL, 2 * D_FF)),
                  _const_spec((D_FF, D_MODEL)), vec],
        out_specs=frow,
        out_shape=jax.ShapeDtypeStruct(x1.shape, F32),
        compiler_params=_cparams(("arbitrary",)),
        name="ffn",
    )(x1, h2, wfi, wfo, gfpost)


def _rope_tables(pos):
    half = A_HD // 2
    inv = ROPE_THETA ** (-jnp.arange(half, dtype=F32) / half)
    ang = pos.astype(F32)[:, None] * inv[None, :]
    cos, sin = jnp.cos(ang), jnp.sin(ang)
    cos = jnp.concatenate([cos, cos], axis=-1)
    sin = jnp.concatenate([-sin, sin], axis=-1)
    return jnp.tile(cos, (1, LANES // A_HD)), jnp.tile(sin, (1, LANES // A_HD))


def _lane_row(vals, offset):
    return jnp.zeros((1, LANES), F32).at[0, offset:offset + vals.shape[0]].set(vals.astype(F32))


def kernel(x_prompt, x_sample, mem_prompt, state_win_k, state_win_v, state_conv, state_rec,
           cache_mem_k, cache_mem_v, ln_mix_pre, w_in, attn_sink, gdn_conv_w, gdn_a_log,
           gdn_dt_bias, gdn_norm_w, ln_mem, w_mem_kv, w_branch, w_out, ln_mix_post,
           ln_ffn_pre, w_ffn_in, w_ffn_out, ln_ffn_post):
    bp, lp, _ = x_prompt.shape
    bs, ls, _ = x_sample.shape

    sizes = [512, 128, 128, B_CONV_CH, B_HEADS, B_HEADS, 512, 512, N_BRANCH * D_MODEL]
    o = np.cumsum([0] + sizes)
    hperm = np.concatenate([np.r_[j * A_HD:(j + 1) * A_HD, (j + 4) * A_HD:(j + 5) * A_HD] for j in range(4)])
    ws = (w_in[:, hperm].astype(BF16), w_in[:, o[1]:o[4]].astype(BF16), w_in[:, o[6]:o[9]].astype(BF16),
          jnp.pad(w_in[:, o[4]:o[6]], ((0, 0), (0, LANES - 2 * B_HEADS))).astype(BF16))
    wb = jnp.concatenate([w_branch[0:1][:, hperm], w_branch[1:]], axis=0).astype(BF16)
    wo = w_out.astype(BF16)
    wfi = w_ffn_in.astype(BF16)
    wfo = w_ffn_out.astype(BF16)
    wmem = w_mem_kv.astype(BF16)
    sink = attn_sink.astype(F32)[np.array([0, 4, 1, 5, 2, 6, 3, 7])]
    vec = lambda g: g.astype(F32).reshape(1, -1)
    alog = _lane_row(gdn_a_log, B_HEADS)
    dtb = _lane_row(gdn_dt_bias, B_HEADS)
    cw = gdn_conv_w.astype(F32)
    nw = vec(gdn_norm_w)

    merge_w = (vec(ln_mix_pre), ws[2], wb, wo, vec(ln_mix_post), vec(ln_ffn_pre))
    ffn = lambda x1, h2: _ffn(x1, h2, wfi, wfo, vec(ln_ffn_post), 512)

    tp = bp * lp
    xp = x_prompt.reshape(tp, D_MODEL)
    cos_p, sin_p = _rope_tables(jnp.arange(lp, dtype=jnp.int32))
    qa, ka, va, z, qc, ba, qn, kn, vv, tail = _proj(xp, vec(ln_mix_pre), ws, cos_p, sin_p, 512, cw, lp)
    b3 = lambda a: a.reshape(bp, lp, a.shape[-1])
    ob, rec_p = _gdn_prompt(b3(qn), b3(kn), b3(vv), b3(ba), b3(z), alog, dtb, nw)
    mk, mv = _memkv(mem_prompt.reshape(bp * N_MEM, D_MODEL), vec(ln_mem), wmem)
    oa = _swa_prompt(sink, qa, ka, va, bp, lp, 2048)
    oc = _memattn_prompt(qc, mk, mv, bp, lp, 2048)
    x1, h2 = _merge(xp, oa, ob.reshape(tp, 512), oc, *merge_w, 256)
    y_p = ffn(x1, h2).reshape(bp, lp, D_MODEL)
    wk_p = ka.reshape(bp, lp, LANES)[:, -WINDOW:].reshape(bp, WINDOW, A_KV, A_HD)
    wv_p = va.reshape(bp, lp, LANES)[:, -WINDOW:].reshape(bp, WINDOW, A_KV, A_HD)
    conv_p = tail[:, -(CONV_W - 1):]
    mem_k_p = mk.reshape(bp, N_MEM, C_HEADS, C_HD)
    mem_v_p = mv.reshape(bp, N_MEM, C_HEADS, C_HD)

    ts = bs * DEC_PAD
    xs = jnp.pad(x_sample, ((0, 0), (0, DEC_PAD - ls), (0, 0))).reshape(ts, D_MODEL)
    cos_s, sin_s = _rope_tables(PAST_LEN + jnp.arange(DEC_PAD, dtype=jnp.int32))
    cos_s, sin_s = jnp.tile(cos_s, (bs, 1)), jnp.tile(sin_s, (bs, 1))
    qa, ka, va, z, qc, ba, qkv = _proj(xs, vec(ln_mix_pre), ws, cos_s, sin_s, 512)
    r3 = lambda a: a.reshape(bs, DEC_PAD, a.shape[-1])
    oa, wk_s, wv_s = _swa_decode(sink, r3(qa), r3(ka), r3(va), state_win_k.reshape(bs, WINDOW, LANES),
                                 state_win_v.reshape(bs, WINDOW, LANES), 16, ls)
    oa = oa.reshape(ts, 512)
    wk_s = wk_s.reshape(state_win_k.shape)
    wv_s = wv_s.reshape(state_win_v.shape)
    histp = jnp.pad(state_conv, ((0, 0), (DEC_PAD - (CONV_W - 1), 0), (0, 0)))
    ob, rec_s = _gdn_decode(r3(qkv), histp, ba, z, state_rec, cw, alog, dtb, nw,
                            LANES // DEC_PAD, ls)
    oc = _memattn_decode(r3(qc), cache_mem_k.reshape(bs, N_MEM * C_HEADS, C_HD),
                         cache_mem_v.reshape(bs, N_MEM * C_HEADS, C_HD), 8).reshape(ts, 512)
    real = lambda a: a.reshape(bs, DEC_PAD, 512)[:, :ls].reshape(bs * ls, 512)
    x1, h2 = _merge(x_sample.reshape(bs * ls, D_MODEL), real(oa), real(ob), real(oc), *merge_w, 256)
    y_s = ffn(x1, h2).reshape(bs, ls, D_MODEL)
    conv_s = r3(qkv)[:, ls - (CONV_W - 1):ls]

    return (y_p, y_s, wk_p, wv_p, conv_p, rec_p, mem_k_p, mem_v_p, wk_s, wv_s, conv_s, rec_s)
```

```python
import functools

import numpy as np
import jax
import jax.numpy as jnp
from jax import lax
from jax.experimental import pallas as pl
from jax.experimental.pallas import tpu as pltpu

F32 = jnp.float32
BF16 = jnp.bfloat16

D_MODEL = 1024
PAST_LEN = 16384
EPS = 1e-6
ROPE_THETA = 10000.0
N_MEM = 256
WINDOW = 128
A_HD = 64
A_HEADS = 8
A_KV = 2
A_SCALE = A_HD ** -0.5
B_HEADS = 4
B_DK = 128
B_DV = 128
CONV_W = 4
GDN_CHUNK = 64
B_CONV_CH = B_HEADS * (2 * B_DK + B_DV)
C_HEADS = 4
C_HD = 128
N_BRANCH = 3
BRANCH_W = 512
D_FF = 2816

LANES = 128
SUBLANES = 8
PAIR = 2 * LANES
GATE_LANES = 2 * B_HEADS
VMEM_LIMIT = 56 * 1024 * 1024
NEG = -1e30
DEC_PAD = SUBLANES


def _cparams(sem, vmem=VMEM_LIMIT):
    return pltpu.CompilerParams(dimension_semantics=sem, vmem_limit_bytes=vmem)


def _const_spec(shape):
    nd = len(shape)
    return pl.BlockSpec(shape, lambda *_: (0,) * nd, pipeline_mode=pl.Buffered(1))


def _rms(x, g):
    ms = jnp.mean(x * x, axis=-1, keepdims=True)
    return x * lax.rsqrt(ms + EPS) * g


def _bdot(a, b):
    return jnp.dot(a.astype(BF16), b.astype(BF16), preferred_element_type=F32)


def _bdot_nt(a, b):
    return lax.dot_general(a.astype(BF16), b.astype(BF16), (((1,), (1,)), ((), ())),
                           preferred_element_type=F32)


def _bdot_tn(a, b):
    return lax.dot_general(a.astype(BF16), b.astype(BF16), (((0,), (0,)), ((), ())),
                           preferred_element_type=F32)


def _silu(x):
    return x * jax.nn.sigmoid(x)


def _softplus(x):
    return jnp.maximum(x, 0.0) + jnp.log1p(jnp.exp(-jnp.abs(x)))


def _rope128(v, cos, sin):
    lane = lax.broadcasted_iota(jnp.int32, v.shape, 1)
    fwd = pltpu.roll(v, 32, 1)
    bwd = pltpu.roll(v, 96, 1)
    sw = jnp.where((lane & 32) == 0, bwd, fwd)
    return v * cos + sw * sin


def _l2n(x):
    return x * lax.rsqrt(jnp.sum(x * x, axis=-1, keepdims=True) + EPS)

def _proj_steps(x_ref, g_ref, wq_ref, wr_ref, wzc_ref, wba_ref, cos_ref, sin_ref,
                qa_ref, ka_ref, va_ref, z_ref, qc_ref, ba_ref):
    h = _rms(x_ref[...], g_ref[...]).astype(BF16)
    cos = cos_ref[...]
    sin = sin_ref[...]

    def mm(w_ref, a, b):
        return jnp.dot(h, w_ref[:, a:b], preferred_element_type=F32)

    def qa_half(c0):
        q = mm(wq_ref, c0 * LANES, (c0 + 2) * LANES)
        for c in range(2):
            qa_ref[:, (c0 + c) * LANES:(c0 + c + 1) * LANES] = (
                _rope128(q[:, c * LANES:(c + 1) * LANES], cos, sin) * A_SCALE).astype(BF16)

    def kv():
        kv2 = mm(wr_ref, 0, PAIR)
        ka_ref[...] = _rope128(kv2[:, :LANES], cos, sin)
        va_ref[...] = kv2[:, LANES:]

    def z_half(c0):
        z_ref[:, c0:c0 + PAIR] = mm(wzc_ref, c0, c0 + PAIR)

    def qc_half(c0):
        qc_ref[:, c0:c0 + PAIR] = mm(wzc_ref, 512 + c0, 512 + c0 + PAIR).astype(BF16)

    def ba():
        ba_ref[...] = mm(wba_ref, 0, LANES)

    def qkv(a, b):
        return mm(wr_ref, PAIR + a, PAIR + b)

    steps = [lambda: qa_half(0), lambda: qa_half(2), kv, lambda: z_half(0), lambda: z_half(PAIR),
             lambda: qc_half(0), lambda: qc_half(PAIR), ba]
    return qkv, steps


def _proj_raw_kernel(x_ref, g_ref, wq_ref, wr_ref, wzc_ref, wba_ref, cos_ref, sin_ref,
                     qa_ref, ka_ref, va_ref, z_ref, qc_ref, ba_ref, qkv_ref):
    qkv, steps = _proj_steps(x_ref, g_ref, wq_ref, wr_ref, wzc_ref, wba_ref, cos_ref, sin_ref,
                             qa_ref, ka_ref, va_ref, z_ref, qc_ref, ba_ref)
    qkv_ref[...] = qkv(0, B_CONV_CH)
    for step in steps:
        step()


def _proj_conv_kernel(x_ref, g_ref, wq_ref, wr_ref, wzc_ref, wba_ref, cos_ref, sin_ref, cw_ref,
                      qa_ref, ka_ref, va_ref, z_ref, qc_ref, ba_ref, qn_ref, kn_ref, vv_ref, tail_ref,
                      buf_ref, *, tm, tiles_per_seq):
    hist = SUBLANES
    first = lax.rem(pl.program_id(0), tiles_per_seq) == 0

    @pl.when(first)
    def _():
        buf_ref[0:hist, :] = jnp.zeros((hist, B_CONV_CH), F32)

    @pl.when(jnp.logical_not(first))
    def _():
        buf_ref[0:hist, :] = buf_ref[tm:tm + hist, :]

    qkv, steps = _proj_steps(x_ref, g_ref, wq_ref, wr_ref, wzc_ref, wba_ref, cos_ref, sin_ref,
                             qa_ref, ka_ref, va_ref, z_ref, qc_ref, ba_ref)
    nq = B_HEADS * B_DK

    def conv_group(c0):
        cs = slice(c0, c0 + PAIR)
        raw = qkv(c0, c0 + PAIR)
        buf_ref[hist:hist + tm, cs] = raw
        tail_ref[0, :, cs] = raw[tm - hist:, :]
        xb = buf_ref[:, cs]
        acc = xb * cw_ref[0:1, cs]
        for i in range(1, CONV_W):
            acc = pltpu.roll(acc, 1, 0) + xb * cw_ref[i:i + 1, cs]
        act = _silu(acc[hist:])
        if c0 >= 2 * nq:
            vv_ref[:, c0 - 2 * nq:c0 - 2 * nq + PAIR] = act
            return
        out_ref, base, scale = (qn_ref, 0, B_DK ** -0.5) if c0 < nq else (kn_ref, nq, 1.0)
        for s in range(2):
            o0 = c0 - base + s * B_DK
            out_ref[:, o0:o0 + B_DK] = _l2n(act[:, s * B_DK:(s + 1) * B_DK]) * scale

    groups = [functools.partial(conv_group, c0) for c0 in range(0, B_CONV_CH, PAIR)]
    while groups or steps:
        if groups:
            groups.pop(0)()
        if steps:
            steps.pop(0)()


_PROJ_OUTS = [(512, BF16), (128, F32), (128, F32), (512, F32), (512, BF16), (128, F32)]


def _proj(x, gain, ws, cos, sin, tm, cw=None, seq=None):
    wq, wr, wtail, wba = ws
    t = x.shape[0]
    ntab = cos.shape[0] // tm
    row = lambda n: pl.BlockSpec((tm, n), lambda i: (i, 0))
    tab = pl.BlockSpec((tm, LANES), lambda i: (i % ntab, 0))
    in_specs = [row(D_MODEL), _const_spec((1, D_MODEL)), _const_spec(wq.shape), _const_spec(wr.shape),
                _const_spec((D_MODEL, D_MODEL)), _const_spec(wba.shape), tab, tab]
    out_specs = [row(n) for n, _ in _PROJ_OUTS]
    out_shape = [jax.ShapeDtypeStruct((t, n), d) for n, d in _PROJ_OUTS]
    if cw is None:
        return pl.pallas_call(
            _proj_raw_kernel, grid=(t // tm,), in_specs=in_specs,
            out_specs=out_specs + [row(B_CONV_CH)],
            out_shape=out_shape + [jax.ShapeDtypeStruct((t, B_CONV_CH), F32)],
            compiler_params=_cparams(("arbitrary",)), name="proj",
        )(x, gain, wq, wr, wtail, wba, cos, sin)
    tiles = seq // tm
    return pl.pallas_call(
        functools.partial(_proj_conv_kernel, tm=tm, tiles_per_seq=tiles),
        grid=(t // tm,),
        in_specs=in_specs + [_const_spec((CONV_W, B_CONV_CH))],
        out_specs=out_specs + [row(512)] * 3
        + [pl.BlockSpec((1, SUBLANES, B_CONV_CH), lambda i: (i // tiles, 0, 0))],
        out_shape=out_shape + [jax.ShapeDtypeStruct((t, 512), F32)] * 3
        + [jax.ShapeDtypeStruct((t // seq, SUBLANES, B_CONV_CH), F32)],
        scratch_shapes=[pltpu.VMEM((tm + SUBLANES, B_CONV_CH), F32)],
        compiler_params=_cparams(("arbitrary",)), name="proj_conv",
    )(x, gain, wq, wr, wtail, wba, cos, sin, cw)


def _swa_scores(q, k16):
    tq = q.shape[0]
    lo = lax.broadcasted_iota(jnp.int32, (tq, LANES), 1) < A_HD
    blocks = []
    for j in range(4):
        c = q[:, j * LANES:(j + 1) * LANES].astype(F32)
        blocks.append(jnp.where(lo, c, 0.0))
        blocks.append(jnp.where(lo, 0.0, c))
    lhs = jnp.concatenate(blocks, axis=0).astype(BF16)
    return lax.dot_general(lhs, k16, (((1,), (1,)), ((), ())), preferred_element_type=F32)


def _swa_finish(logits, v16, mask, sink_ref):
    tq = logits.shape[0] // A_HEADS
    lo = lax.broadcasted_iota(jnp.int32, (tq, LANES), 1) < A_HD
    es, inv = [], []
    for s in range(8):
        l = jnp.where(mask, logits[s * tq:(s + 1) * tq], NEG)
        sk = sink_ref[s]
        m = jnp.maximum(jnp.max(l, axis=-1, keepdims=True), sk)
        e = jnp.exp(l - m)
        den = jnp.sum(e, axis=-1, keepdims=True) + jnp.exp(sk - m)
        es.append(e.astype(BF16))
        inv.append(1.0 / den)
    pv = jnp.dot(jnp.concatenate(es, axis=0), v16, preferred_element_type=F32)
    outs = []
    for j in range(4):
        a = pv[(2 * j) * tq:(2 * j + 1) * tq] * inv[2 * j]
        b = pv[(2 * j + 1) * tq:(2 * j + 2) * tq] * inv[2 * j + 1]
        outs.append(jnp.where(lo, a, b))
    return jnp.concatenate(outs, axis=1)


def _swa_prompt_kernel(sink_ref, q_ref, kc_ref, kp_ref, vc_ref, vp_ref, o_ref, *, nblk):
    i = pl.program_id(1)
    kcat = jnp.concatenate([kp_ref[...], kc_ref[...]], axis=0).astype(BF16)
    vcat = jnp.concatenate([vp_ref[...], vc_ref[...]], axis=0).astype(BF16)
    ii = lax.broadcasted_iota(jnp.int32, (WINDOW, 2 * WINDOW), 0)
    jj = lax.broadcasted_iota(jnp.int32, (WINDOW, 2 * WINDOW), 1)
    band = (jj > ii) & (jj <= ii + WINDOW)
    for jb in range(nblk):
        mask = band
        if jb == 0:
            mask = band & ((jj >= WINDOW) | (i > 0))
        logits = _swa_scores(q_ref[jb * WINDOW:(jb + 1) * WINDOW, :], kcat[jb * WINDOW:(jb + 2) * WINDOW])
        o = _swa_finish(logits, vcat[jb * WINDOW:(jb + 2) * WINDOW], mask, sink_ref)
        o_ref[jb * WINDOW:(jb + 1) * WINDOW, :] = o.astype(BF16)


def _swa_prompt(sink, qa, ka, va, batch, seq, tq):
    nq = seq // tq
    nblk = tq // WINDOW
    nw = seq // WINDOW
    cur = lambda n: pl.BlockSpec((tq, n), lambda b, i: (b * nq + i, 0))
    prev = pl.BlockSpec((WINDOW, LANES), lambda b, i: (jnp.maximum(b * nw + i * nblk - 1, 0), 0))
    return pl.pallas_call(
        functools.partial(_swa_prompt_kernel, nblk=nblk),
        grid=(batch, nq),
        in_specs=[pl.BlockSpec(memory_space=pltpu.SMEM), cur(512), cur(LANES), prev, cur(LANES), prev],
        out_specs=cur(512),
        out_shape=jax.ShapeDtypeStruct(qa.shape, BF16),
        compiler_params=_cparams(("arbitrary", "arbitrary")),
        name="swa_prompt",
    )(sink, qa, ka, ka, va, va)


ROW_UNROLL = 8


def _rows_loop(rows, load, stages, store):
    def body(g, carry):
        idx = [g * ROW_UNROLL + j for j in range(ROW_UNROLL)]
        vals = [load(r) for r in idx]
        for stage in stages:
            vals = [stage(v) for v in vals]
        for r, o in zip(idx, vals):
            store(r, o)
        return carry

    lax.fori_loop(0, rows // ROW_UNROLL, body, 0)


def _swa_decode_kernel(sink_ref, q_ref, kn_ref, vn_ref, wk_ref, wv_ref, o_ref, wko_ref, wvo_ref,
                       *, rows, new_len):
    ii = lax.broadcasted_iota(jnp.int32, (DEC_PAD, 2 * WINDOW), 0)
    jj = lax.broadcasted_iota(jnp.int32, (DEC_PAD, 2 * WINDOW), 1)
    mask = ((jj < WINDOW) & (jj > ii)) | ((jj >= WINDOW) & (jj - WINDOW <= ii))
    pad = jnp.zeros((WINDOW - DEC_PAD, LANES), F32)
    keep = lax.broadcasted_iota(jnp.int32, (WINDOW, LANES), 0) < WINDOW - new_len

    def slide(win, new):
        tail = jnp.concatenate([pad, new], axis=0)
        return jnp.where(keep, pltpu.roll(win, WINDOW - new_len, 0), pltpu.roll(tail, DEC_PAD - new_len, 0))

    def load(r):
        return q_ref[r], wk_ref[r], kn_ref[r], wv_ref[r], vn_ref[r]

    def scores(args):
        q, wk, kn, wv, vn = args
        kk = jnp.concatenate([wk, kn, pad], axis=0).astype(BF16)
        return _swa_scores(q, kk), wk, kn, wv, vn

    def finish(args):
        logits, wk, kn, wv, vn = args
        vv = jnp.concatenate([wv, vn, pad], axis=0).astype(BF16)
        return _swa_finish(logits, vv, mask, sink_ref).astype(BF16), slide(wk, kn), slide(wv, vn)

    def store(r, outs):
        o_ref[r], wko_ref[r], wvo_ref[r] = outs

    _rows_loop(rows, load, [scores, finish], store)


def _swa_decode(sink, qa, kn, vn, wk, wv, rows, new_len):
    nb = qa.shape[0]
    blk = lambda a, n: pl.BlockSpec((rows, a, n), lambda i: (i, 0, 0))
    return pl.pallas_call(
        functools.partial(_swa_decode_kernel, rows=rows, new_len=new_len),
        grid=(nb // rows,),
        in_specs=[pl.BlockSpec(memory_space=pltpu.SMEM), blk(DEC_PAD, 512), blk(DEC_PAD, LANES),
                  blk(DEC_PAD, LANES), blk(WINDOW, LANES), blk(WINDOW, LANES)],
        out_specs=[blk(DEC_PAD, 512), blk(WINDOW, LANES), blk(WINDOW, LANES)],
        out_shape=[jax.ShapeDtypeStruct(qa.shape, BF16), jax.ShapeDtypeStruct(wk.shape, F32),
                   jax.ShapeDtypeStruct(wv.shape, F32)],
        compiler_params=_cparams(("arbitrary",)),
        name="swa_decode",
    )(sink, qa, kn, vn, wk, wv)


def _gdn_gates(ba, alog, dtb, valid):
    beta = jax.nn.sigmoid(ba)
    g = -jnp.exp(alog) * _softplus(ba + dtb)
    if valid is not None:
        beta = jnp.where(valid, beta, 0.0)
        g = jnp.where(valid, g, 0.0)
    return beta, g


def _chunk_masks(chunk):
    sh = chunk.bit_length() - 1
    ri = lax.broadcasted_iota(jnp.int32, (LANES, LANES), 0)
    ci = lax.broadcasted_iota(jnp.int32, (LANES, LANES), 1)
    same = (ri >> sh) == (ci >> sh)
    return same, same & (ri >= ci), same & (ri > ci), ri == ci


def _gdn_cumsums(g_all, chunk):
    n = g_all.shape[0]
    pos = lax.broadcasted_iota(jnp.int32, g_all.shape, 0) & (chunk - 1)
    gcol = g_all
    step = 1
    while step < chunk:
        gcol = gcol + jnp.where(pos >= step, pltpu.roll(gcol, step, 0), 0.0)
        step *= 2
    gtot = jnp.where(pos == chunk - 1, gcol, 0.0)
    step = 1
    while step < chunk:
        gtot = gtot + jnp.where(pos + step < chunk, pltpu.roll(gtot, n - step, 0), 0.0)
        step *= 2
    return gcol, gcol.T, gtot


def _sibling_mask(ri, ci, lvl):
    rb = ri >> lvl
    return ((rb & 1) == 1) & ((ci >> lvl) == rb - 1)


def _gdn_phase_a(heads, chunk):
    _, tri, strict, _ = _chunk_masks(chunk)
    ri = lax.broadcasted_iota(jnp.int32, (LANES, LANES), 0)
    ci = lax.broadcasted_iota(jnp.int32, (LANES, LANES), 1)
    kbs = [k * beta for q, k, v, beta, gcol, grow, gtot in heads]
    kkqk = [_bdot_nt(jnp.concatenate([kb, hd[0]], axis=0), hd[1]) for kb, hd in zip(kbs, heads)]
    a, qk = [], []
    for r, (q, k, v, beta, gcol, grow, gtot) in zip(kkqk, heads):
        decay = jnp.exp(jnp.where(tri, gcol - grow, NEG))
        a.append(jnp.where(strict, r[:LANES] * decay, 0.0))
        qk.append(r[LANES:] * decay)
    n = [-jnp.where(_sibling_mask(ri, ci, 0), ah, 0.0) for ah in a]
    for lvl in range(1, chunk.bit_length() - 1):
        sib = _sibling_mask(ri, ci, lvl)
        al = [jnp.where(sib, ah, 0.0) for ah in a]
        x = [alh + _bdot(nh, alh) for alh, nh in zip(al, n)]
        n = [nh - (xh + _bdot(xh, nh)) for xh, nh in zip(x, n)]
    outs = []
    rhs = [jnp.concatenate([hd[2] * hd[3], kb * jnp.exp(hd[4])], axis=1) for kb, hd in zip(kbs, heads)]
    sol = [r + _bdot(nh, r) for r, nh in zip(rhs, n)]
    for s, qkh, (q, k, v, beta, gcol, grow, gtot) in zip(sol, qk, heads):
        outs.append((s[:, :B_DV], s[:, B_DV:], q * jnp.exp(gcol), k * jnp.exp(gtot - gcol), qkh, jnp.exp(gtot)))
    return outs


def _gdn_qkv_heads(qkv, h):
    q = qkv[:, h * B_DK:(h + 1) * B_DK]
    k = qkv[:, (B_HEADS + h) * B_DK:(B_HEADS + h + 1) * B_DK]
    v = qkv[:, (2 * B_HEADS + h) * B_DK:(2 * B_HEADS + h + 1) * B_DK]
    q = q * lax.rsqrt(jnp.sum(q * q, axis=-1, keepdims=True) + EPS) * (B_DK ** -0.5)
    k = k * lax.rsqrt(jnp.sum(k * k, axis=-1, keepdims=True) + EPS)
    return q, k, v


def _gdn_out(o, z, nw):
    return _rms(o, nw) * _silu(z)


def _pair_bd(x2):
    lo = lax.broadcasted_iota(jnp.int32, x2.shape, 1) < LANES
    return jnp.concatenate([jnp.where(lo, x2, 0.0), jnp.where(lo, 0.0, x2)], axis=0).astype(BF16)


def _quad_bd(x4):
    blk = lax.broadcasted_iota(jnp.int32, x4.shape, 1) >> (GDN_CHUNK.bit_length() - 1)
    return jnp.concatenate([jnp.where(blk == j, x4, 0.0) for j in range(PAIR // GDN_CHUNK)],
                           axis=0).astype(BF16)


def _pair_cols(x, l0, l1):
    lo = lax.broadcasted_iota(jnp.int32, (x.shape[0], PAIR), 1) < LANES
    return jnp.where(lo, x[:, l0:l0 + 1], x[:, l1:l1 + 1])


def _gdn_prompt_kernel(q_ref, k_ref, v_ref, ba_ref, z_ref, alog_ref, dtb_ref, nw_ref,
                       ob_ref, s_ref, *, nb, ntb):
    t = pl.program_id(0)
    npair = B_HEADS // 2

    @pl.when(t == 0)
    def _():
        s_ref[...] = jnp.zeros(s_ref.shape, F32)

    alog = alog_ref[...]
    dtb = dtb_ref[...]
    nw = nw_ref[...]
    nchunk = LANES // GDN_CHUNK
    _, tri, strict, _ = _chunk_masks(GDN_CHUNK)
    tri2 = jnp.concatenate([tri, tri], axis=1)
    strict2 = jnp.concatenate([strict, strict], axis=1)

    items = []
    blocks = [(b, tb) for b in range(nb) for tb in range(ntb)]
    assert len(blocks) * GATE_LANES <= LANES
    packed = None
    for j, (b, tb) in enumerate(blocks):
        blk = ba_ref[b, tb * LANES:(tb + 1) * LANES, :]
        packed = blk if j == 0 else packed + pltpu.roll(blk, GATE_LANES * j, 1)
    beta_all, g_all = _gdn_gates(packed, alog, dtb, None)
    gcol_all, grow_all, gtot_all = _gdn_cumsums(g_all, GDN_CHUNK)
    eg_all = jnp.exp(gcol_all)
    ed_all = jnp.exp(gtot_all - gcol_all)
    et_all = jnp.exp(gtot_all)
    for j, (b, tb) in enumerate(blocks):
        rsl = slice(tb * LANES, (tb + 1) * LANES)
        for p in range(npair):
            h0, h1 = GATE_LANES * j + 2 * p, GATE_LANES * j + 2 * p + 1
            g0, g1 = B_HEADS + h0, B_HEADS + h1
            psl = slice(p * PAIR, (p + 1) * PAIR)
            q2 = q_ref[b, rsl, psl]
            k2 = k_ref[b, rsl, psl]
            v2 = v_ref[b, rsl, psl]
            beta2 = _pair_cols(beta_all, h0, h1)
            grow2 = jnp.concatenate([grow_all[g0:g0 + 1, :], grow_all[g1:g1 + 1, :]], axis=1)
            decay2 = jnp.exp(jnp.where(tri2, _pair_cols(gcol_all, g0, g1) - grow2, NEG))
            kb2 = k2 * beta2
            items.append(dict(
                b=b, tb=tb, p=p, rsl=rsl, decay2=decay2, kb2=kb2, v2b=v2 * beta2,
                kbe2=kb2 * _pair_cols(eg_all, g0, g1),
                lhs=jnp.concatenate([kb2, q2], axis=0).astype(BF16),
                kbd=_pair_bd(k2),
                qe2=(q2 * _pair_cols(eg_all, g0, g1)).astype(BF16),
                kd2=(k2 * _pair_cols(ed_all, g0, g1)).astype(BF16),
                et2=_pair_cols(et_all, g0, g1)))

    for it in items:
        kkqk = lax.dot_general(it["lhs"], it["kbd"], (((1,), (1,)), ((), ())),
                               preferred_element_type=F32)
        it["a2"] = jnp.where(strict2, kkqk[:LANES] * it["decay2"], 0.0)
        it["qk2"] = (kkqk[LANES:] * it["decay2"]).astype(BF16)
    qlane = lax.broadcasted_iota(jnp.int32, (GDN_CHUNK, PAIR), 1)
    qrow = lax.broadcasted_iota(jnp.int32, (GDN_CHUNK, PAIR), 0)
    qcol = qlane & (GDN_CHUNK - 1)
    top = (qlane & GDN_CHUNK) == 0
    for it in items:
        it["a64"] = it["a2"][:GDN_CHUNK] + it["a2"][GDN_CHUNK:]
        it["n"] = -jnp.where(_sibling_mask(qrow, qcol, 0), it["a64"], 0.0)
    for lvl in range(1, GDN_CHUNK.bit_length() - 1):
        sib = _sibling_mask(qrow, qcol, lvl)
        for it in items:
            it["al"] = jnp.where(sib, it["a64"], 0.0)
            it["x"] = it["al"] + jnp.dot(it["n"].astype(BF16), _quad_bd(it["al"]), preferred_element_type=F32)
        for it in items:
            x = it["x"]
            it["n"] = it["n"] - (x + jnp.dot(x.astype(BF16), _quad_bd(it["n"]), preferred_element_type=F32))
    for it in items:
        n64 = it["n"]
        it["n"] = jnp.concatenate([jnp.where(top, n64, 0.0), jnp.where(top, 0.0, n64)], axis=0)
    for it in items:
        us, ws = [], []
        for s in range(2):
            sl = slice(s * LANES, (s + 1) * LANES)
            rhs = jnp.concatenate([it["v2b"][:, sl], it["kbe2"][:, sl]], axis=1)
            sol = rhs + _bdot(it["n"][:, sl], rhs)
            us.append(sol[:, :B_DV])
            ws.append(sol[:, B_DV:])
        it["u2"] = jnp.concatenate(us, axis=1)
        it["w2"] = jnp.concatenate(ws, axis=1).astype(BF16)

    nchain = nb * npair
    state = [[s_ref[ch // npair, 2 * (ch % npair) + s] for s in range(2)] for ch in range(nchain)]
    zeros16 = jnp.zeros((LANES, LANES), BF16)
    outs = {}
    for tb in range(ntb):
        cur = [(it["b"] * npair + it["p"], i, it) for i, it in enumerate(items) if it["tb"] == tb]
        for c in range(nchunk):
            rows = slice(c * GDN_CHUNK, (c + 1) * GDN_CHUNK)
            rs = []
            for ch, _, it in cur:
                sa, sb = (s.astype(BF16) for s in state[ch])
                sbd = jnp.concatenate([jnp.concatenate([sa, zeros16], axis=1),
                                       jnp.concatenate([zeros16, sb], axis=1)], axis=0)
                rs.append(jnp.dot(jnp.concatenate([it["w2"][rows], it["qe2"][rows]], axis=0), sbd,
                                  preferred_element_type=F32))
            for (ch, i, it), r in zip(cur, rs):
                vn2 = it["u2"][rows] - r[:GDN_CHUNK]
                vnb = vn2.astype(BF16)
                vt = _pair_bd(jnp.concatenate([vn2] * nchunk, axis=0))
                outs[(i, c)] = r[GDN_CHUNK:] + jnp.dot(it["qk2"][rows], vt, preferred_element_type=F32)
                for s in range(2):
                    hl = slice(s * LANES, (s + 1) * LANES)
                    upd = lax.dot_general(it["kd2"][rows, hl], vnb[:, hl], (((0,), (0,)), ((), ())),
                                          preferred_element_type=F32)
                    state[ch][s] = state[ch][s] * it["et2"][c * GDN_CHUNK:c * GDN_CHUNK + 1, hl] + upd
    for ch in range(nchain):
        for s in range(2):
            s_ref[ch // npair, 2 * (ch % npair) + s] = state[ch][s]

    for i, it in enumerate(items):
        o2 = jnp.concatenate([outs[(i, c)] for c in range(nchunk)], axis=0)
        for s in range(2):
            h = 2 * it["p"] + s
            sl = slice(h * B_DV, (h + 1) * B_DV)
            o = o2[:, s * LANES:(s + 1) * LANES]
            ob_ref[it["b"], it["rsl"], sl] = _gdn_out(o, z_ref[it["b"], it["rsl"], sl], nw).astype(BF16)


GDN_TOKEN_BLOCKS = 2


def _gdn_prompt(q, k, v, ba, z, alog, dtb, nw):
    batch, seq, _ = q.shape
    tm = GDN_TOKEN_BLOCKS * LANES
    tok = lambda n: pl.BlockSpec((batch, tm, n), lambda t: (0, t, 0))
    state = pl.BlockSpec((batch, B_HEADS, B_DK, B_DV), lambda t: (0, 0, 0, 0))
    return pl.pallas_call(
        functools.partial(_gdn_prompt_kernel, nb=batch, ntb=GDN_TOKEN_BLOCKS),
        grid=(seq // tm,),
        in_specs=[tok(512), tok(512), tok(512), tok(LANES), tok(512),
                  _const_spec((1, LANES)), _const_spec((1, LANES)), _const_spec((1, B_DV))],
        out_specs=[tok(512), state],
        out_shape=[jax.ShapeDtypeStruct((batch, seq, 512), BF16),
                   jax.ShapeDtypeStruct((batch, B_HEADS, B_DK, B_DV), F32)],
        compiler_params=_cparams(("arbitrary",)),
        name="gdn_prompt",
    )(q, k, v, ba, z, alog, dtb, nw)


def _gdn_decode_kernel(raw_ref, hist_ref, ba_ref, z_ref, rec_ref, cw_ref, alog_ref, dtb_ref, nw_ref,
                       ob_ref, s_ref, buf_ref, u_s, w_s, qe_s, kd_s, qk_s, eg_s, o_s, *, rows, valid_len):
    buf_ref[:, 0:DEC_PAD, :] = hist_ref[...]
    buf_ref[:, DEC_PAD:2 * DEC_PAD, :] = raw_ref[...]
    conv = None
    for i in range(CONV_W):
        off = DEC_PAD - (CONV_W - 1) + i
        term = buf_ref[:, off:off + DEC_PAD, :] * cw_ref[i:i + 1, :]
        conv = term if conv is None else conv + term
    qkv = _silu(conv.reshape(rows * DEC_PAD, B_CONV_CH))
    tok = lax.broadcasted_iota(jnp.int32, (LANES, LANES), 0) & (DEC_PAD - 1)
    beta_all, g_all = _gdn_gates(ba_ref[...], alog_ref[...], dtb_ref[...], tok < valid_len)
    gcol_all, grow_all, gtot_all = _gdn_cumsums(g_all, DEC_PAD)
    heads = []
    for h in range(B_HEADS):
        gl = B_HEADS + h
        heads.append(_gdn_qkv_heads(qkv, h) + (beta_all[:, h:h + 1], gcol_all[:, gl:gl + 1],
                                               grow_all[gl:gl + 1, :], gtot_all[:, gl:gl + 1]))
    for h, (u, w, qe, kd, qk, egt) in enumerate(_gdn_phase_a(heads, DEC_PAD)):
        u_s[h] = u
        w_s[h] = w
        qe_s[h] = qe
        kd_s[h] = kd
        qk_s[h] = qk
        eg_s[h] = jnp.broadcast_to(egt, (LANES, LANES))

    def load(r):
        r0 = pl.multiple_of(r * DEC_PAD, DEC_PAD)
        rr = pl.ds(r0, DEC_PAD)
        return [(rec_ref[r, h], w_s[h, rr, :], qe_s[h, rr, :], u_s[h, rr, :], qk_s[h, rr, :],
                 kd_s[h, rr, :], eg_s[h, pl.ds(r0, 1), :]) for h in range(B_HEADS)]

    def read_state(heads):
        return [(_bdot(jnp.concatenate([w, qe], axis=0), s), s, u, qk, kd, eg)
                for s, w, qe, u, qk, kd, eg in heads]

    def update(heads):
        outs = []
        for res, s, u, qk, kd, eg in heads:
            v_new = u - res[:DEC_PAD]
            vt = jnp.concatenate([v_new] * (LANES // DEC_PAD), axis=0)
            outs.append((res[DEC_PAD:] + _bdot(qk, vt), s * eg + _bdot_tn(kd, v_new)))
        return outs

    def store(r, outs):
        rr = pl.ds(pl.multiple_of(r * DEC_PAD, DEC_PAD), DEC_PAD)
        for h, (o, s_new) in enumerate(outs):
            o_s[h, rr, :] = o
            s_ref[r, h] = s_new

    _rows_loop(rows, load, [read_state, update], store)
    nw = nw_ref[...]
    for h in range(B_HEADS):
        sl = slice(h * B_DV, (h + 1) * B_DV)
        ob_ref[:, sl] = _gdn_out(o_s[h], z_ref[:, sl], nw).astype(BF16)


def _gdn_decode(raw, histp, ba, z, rec, cw, alog, dtb, nw, rows, valid_len):
    nb = raw.shape[0]
    flat = rows * DEC_PAD
    assert flat == LANES
    sq = lambda: pltpu.VMEM((B_HEADS, LANES, LANES), F32)
    return pl.pallas_call(
        functools.partial(_gdn_decode_kernel, rows=rows, valid_len=valid_len),
        grid=(nb // rows,),
        in_specs=[pl.BlockSpec((rows, DEC_PAD, B_CONV_CH), lambda i: (i, 0, 0)),
                  pl.BlockSpec((rows, DEC_PAD, B_CONV_CH), lambda i: (i, 0, 0)),
                  pl.BlockSpec((flat, LANES), lambda i: (i, 0)),
                  pl.BlockSpec((flat, 512), lambda i: (i, 0)),
                  pl.BlockSpec((rows, B_HEADS, B_DK, B_DV), lambda i: (i, 0, 0, 0)),
                  _const_spec((CONV_W, B_CONV_CH)), _const_spec((1, LANES)), _const_spec((1, LANES)),
                  _const_spec((1, B_DV))],
        out_specs=[pl.BlockSpec((flat, 512), lambda i: (i, 0)),
                   pl.BlockSpec((rows, B_HEADS, B_DK, B_DV), lambda i: (i, 0, 0, 0))],
        out_shape=[jax.ShapeDtypeStruct((nb * DEC_PAD, 512), BF16),
                   jax.ShapeDtypeStruct(rec.shape, F32)],
        scratch_shapes=[pltpu.VMEM((rows, 2 * DEC_PAD, B_CONV_CH), F32)] + [sq() for _ in range(7)],
        compiler_params=_cparams(("arbitrary",)),
        name="gdn_decode",
    )(raw, histp, ba, z, rec, cw, alog, dtb, nw)


def _memkv_kernel(m_ref, g_ref, w_ref, k_ref, v_ref):
    h = _rms(m_ref[...], g_ref[...]).astype(BF16)
    n = C_HEADS * C_HD
    k_ref[...] = jnp.dot(h, w_ref[:, :n], preferred_element_type=F32)
    v_ref[...] = jnp.dot(h, w_ref[:, n:], preferred_element_type=F32)


def _memkv(mem, gain, w):
    t = mem.shape[0]
    tm = 512
    n = C_HEADS * C_HD
    return pl.pallas_call(
        _memkv_kernel,
        grid=(t // tm,),
        in_specs=[pl.BlockSpec((tm, D_MODEL), lambda i: (i, 0)), _const_spec((1, D_MODEL)),
                  _const_spec((D_MODEL, 2 * n))],
        out_specs=[pl.BlockSpec((tm, n), lambda i: (i, 0))] * 2,
        out_shape=[jax.ShapeDtypeStruct((t, n), F32)] * 2,
        compiler_params=_cparams(("arbitrary",)),
        name="memkv",
    )(mem, gain, w)


def _softmax_rows(logits):
    m = jnp.max(logits, axis=-1, keepdims=True)
    e = jnp.exp(logits - m)
    return e, 1.0 / jnp.sum(e, axis=-1, keepdims=True)


def _memattn_prompt_kernel(q_ref, k_ref, v_ref, o_ref):
    hs = lambda h: slice(h * C_HD, (h + 1) * C_HD)
    scores = lambda h: _bdot_nt(q_ref[:, hs(h)], k_ref[:, hs(h)])
    logits = scores(0)
    for h in range(C_HEADS):
        nxt = scores(h + 1) if h + 1 < C_HEADS else None
        e, inv = _softmax_rows(logits * (C_HD ** -0.5))
        o_ref[:, hs(h)] = (_bdot(e, v_ref[:, hs(h)]) * inv).astype(BF16)
        logits = nxt


def _memattn_prompt(qc, mk, mv, batch, seq, tm):
    nq = seq // tm
    n = C_HEADS * C_HD
    cur = pl.BlockSpec((tm, n), lambda b, i: (b * nq + i, 0))
    mem = pl.BlockSpec((N_MEM, n), lambda b, i: (b, 0))
    return pl.pallas_call(
        _memattn_prompt_kernel,
        grid=(batch, nq),
        in_specs=[cur, mem, mem],
        out_specs=cur,
        out_shape=jax.ShapeDtypeStruct(qc.shape, BF16),
        compiler_params=_cparams(("arbitrary", "arbitrary")),
        name="memattn_prompt",
    )(qc, mk, mv)


def _memattn_decode_kernel(q_ref, k_ref, v_ref, o_ref, *, rows):
    nk = N_MEM * C_HEADS
    col = lax.broadcasted_iota(jnp.int32, (C_HEADS * DEC_PAD, nk), 1)
    row = lax.broadcasted_iota(jnp.int32, (C_HEADS * DEC_PAD, nk), 0)
    own = (col & (C_HEADS - 1)) == (row >> (DEC_PAD.bit_length() - 1))

    def load(r):
        return q_ref[r], k_ref[r], v_ref[r]

    def scores(args):
        q, k, v = args
        q = q.astype(F32)
        lhs = jnp.concatenate([q[:, h * C_HD:(h + 1) * C_HD] for h in range(C_HEADS)], axis=0)
        return _bdot_nt(lhs, k), v

    def finish(args):
        logits, v = args
        e, inv = _softmax_rows(jnp.where(own, logits * (C_HD ** -0.5), NEG))
        pv = _bdot(e, v) * inv
        return jnp.concatenate([pv[h * DEC_PAD:(h + 1) * DEC_PAD] for h in range(C_HEADS)],
                               axis=1).astype(BF16)

    def store(r, o):
        o_ref[r] = o

    _rows_loop(rows, load, [scores, finish], store)


def _memattn_decode(qc, ck, cv, rows):
    nb = qc.shape[0]
    n = C_HEADS * C_HD
    blk = pl.BlockSpec((rows, DEC_PAD, n), lambda i: (i, 0, 0))
    cache = pl.BlockSpec((rows, N_MEM * C_HEADS, C_HD), lambda i: (i, 0, 0))
    return pl.pallas_call(
        functools.partial(_memattn_decode_kernel, rows=rows),
        grid=(nb // rows,),
        in_specs=[blk, cache, cache],
        out_specs=blk,
        out_shape=jax.ShapeDtypeStruct(qc.shape, BF16),
        compiler_params=_cparams(("arbitrary",)),
        name="memattn_decode",
    )(qc, ck, cv)


def _merge_kernel(x_ref, oa_ref, ob_ref, oc_ref, gpre_ref, wga_ref, wgb_ref, wgc_ref, wb_ref, wo_ref,
                  gpost_ref, gfpre_ref, x1_ref, h2_ref):
    x = x_ref[...]
    ups = [jnp.dot(o_ref[...], wb_ref[n], preferred_element_type=F32)
           for n, o_ref in enumerate((oa_ref, ob_ref, oc_ref))]
    h = _rms(x, gpre_ref[...]).astype(BF16)
    mix = None
    for n, wg_ref in enumerate((wga_ref, wgb_ref, wgc_ref)):
        gate = jax.nn.sigmoid(jnp.dot(h, wg_ref[...], preferred_element_type=F32))
        mix = gate * ups[n] if mix is None else mix + gate * ups[n]
    x1 = x + _rms(_bdot(mix, wo_ref[...]), gpost_ref[...])
    x1_ref[...] = x1
    h2_ref[...] = _rms(x1, gfpre_ref[...]).astype(BF16)


FF_SPLIT = 6 * PAIR


def _ffn_kernel(x1_ref, h2_ref, wfi_ref, wfo_ref, gfpost_ref, y_ref):
    h2 = h2_ref[...]
    f = None
    halves = []
    for a, b in ((0, FF_SPLIT), (FF_SPLIT, D_FF)):
        gt = jnp.dot(h2, wfi_ref[:, a:b], preferred_element_type=F32)
        uf = jnp.dot(h2, wfi_ref[:, D_FF + a:D_FF + b], preferred_element_type=F32)
        halves.append((a, b, gt, uf))
    for a, b, gt, uf in halves:
        part = _bdot(_silu(gt) * uf, wfo_ref[a:b, :])
        f = part if f is None else f + part
    y_ref[...] = x1_ref[...] + _rms(f, gfpost_ref[...])


def _merge_weight_specs():
    vec = _const_spec((1, D_MODEL))
    return ([vec] + [pl.BlockSpec((D_MODEL, D_MODEL), functools.partial(lambda n, *_: (0, n), n + 1),
                                  pipeline_mode=pl.Buffered(1)) for n in range(N_BRANCH)]
            + [_const_spec((N_BRANCH, BRANCH_W, D_MODEL)), _const_spec((D_MODEL, D_MODEL)), vec, vec])


def _merge(x, oa, ob, oc, gpre, wtail, wb, wo, gpost, gfpre, tm):
    t = x.shape[0]
    row = lambda n: pl.BlockSpec((tm, n), lambda i: (i, 0))
    return pl.pallas_call(
        _merge_kernel,
        grid=(t // tm,),
        in_specs=[row(D_MODEL), row(512), row(512), row(512)] + _merge_weight_specs(),
        out_specs=[row(D_MODEL), row(D_MODEL)],
        out_shape=[jax.ShapeDtypeStruct(x.shape, F32), jax.ShapeDtypeStruct(x.shape, BF16)],
        compiler_params=_cparams(("arbitrary",)),
        name="merge",
    )(x, oa, ob, oc, gpre, wtail, wtail, wtail, wb, wo, gpost, gfpre)


def _ffn(x1, h2, wfi, wfo, gfpost, tm):
    t = x1.shape[0]
    vec = _const_spec((1, D_MODEL))
    frow = pl.BlockSpec((tm, D_MODEL), lambda i: (i, 0))
    return pl.pallas_call(
        _ffn_kernel,
        grid=(t // tm,),
        in_specs=[frow, frow, _const_spec((D_MODEL, 2 * D_FF)),
                  _const_spec((D_FF, D_MODEL)), vec],
        out_specs=frow,
        out_shape=jax.ShapeDtypeStruct(x1.shape, F32),
        compiler_params=_cparams(("arbitrary",)),
        name="ffn",
    )(x1, h2, wfi, wfo, gfpost)


def _rope_tables(pos):
    half = A_HD // 2
    inv = ROPE_THETA ** (-jnp.arange(half, dtype=F32) / half)
    ang = pos.astype(F32)[:, None] * inv[None, :]
    cos, sin = jnp.cos(ang), jnp.sin(ang)
    cos = jnp.concatenate([cos, cos], axis=-1)
    sin = jnp.concatenate([-sin, sin], axis=-1)
    return jnp.tile(cos, (1, LANES // A_HD)), jnp.tile(sin, (1, LANES // A_HD))


def _lane_row(vals, offset):
    grp = jnp.zeros((GATE_LANES,), F32).at[offset:offset + vals.shape[0]].set(vals.astype(F32))
    return jnp.tile(grp, LANES // GATE_LANES).reshape(1, LANES)


def kernel(x_prompt, x_sample, mem_prompt, state_win_k, state_win_v, state_conv, state_rec,
           cache_mem_k, cache_mem_v, ln_mix_pre, w_in, attn_sink, gdn_conv_w, gdn_a_log,
           gdn_dt_bias, gdn_norm_w, ln_mem, w_mem_kv, w_branch, w_out, ln_mix_post,
           ln_ffn_pre, w_ffn_in, w_ffn_out, ln_ffn_post):
    bp, lp, _ = x_prompt.shape
    bs, ls, _ = x_sample.shape

    sizes = [512, 128, 128, B_CONV_CH, B_HEADS, B_HEADS, 512, 512, N_BRANCH * D_MODEL]
    o = np.cumsum([0] + sizes)
    hperm = np.concatenate([np.r_[j * A_HD:(j + 1) * A_HD, (j + 4) * A_HD:(j + 5) * A_HD] for j in range(4)])
    ws = (w_in[:, hperm].astype(BF16), w_in[:, o[1]:o[4]].astype(BF16), w_in[:, o[6]:o[9]].astype(BF16),
          jnp.pad(w_in[:, o[4]:o[6]], ((0, 0), (0, LANES - 2 * B_HEADS))).astype(BF16))
    wb = jnp.concatenate([w_branch[0:1][:, hperm], w_branch[1:]], axis=0).astype(BF16)
    wo = w_out.astype(BF16)
    wfi = w_ffn_in.astype(BF16)
    wfo = w_ffn_out.astype(BF16)
    wmem = w_mem_kv.astype(BF16)
    sink = attn_sink.astype(F32)[np.array([0, 4, 1, 5, 2, 6, 3, 7])]
    vec = lambda g: g.astype(F32).reshape(1, -1)
    alog = _lane_row(gdn_a_log, B_HEADS)
    dtb = _lane_row(gdn_dt_bias, B_HEADS)
    cw = gdn_conv_w.astype(F32)
    nw = vec(gdn_norm_w)

    merge_w = (vec(ln_mix_pre), ws[2], wb, wo, vec(ln_mix_post), vec(ln_ffn_pre))
    ffn = lambda x1, h2: _ffn(x1, h2, wfi, wfo, vec(ln_ffn_post), 512)

    tp = bp * lp
    xp = x_prompt.reshape(tp, D_MODEL)
    cos_p, sin_p = _rope_tables(jnp.arange(lp, dtype=jnp.int32))
    qa, ka, va, z, qc, ba, qn, kn, vv, tail = _proj(xp, vec(ln_mix_pre), ws, cos_p, sin_p, 512, cw, lp)
    b3 = lambda a: a.reshape(bp, lp, a.shape[-1])
    ob, rec_p = _gdn_prompt(b3(qn), b3(kn), b3(vv), b3(ba), b3(z), alog, dtb, nw)
    mk, mv = _memkv(mem_prompt.reshape(bp * N_MEM, D_MODEL), vec(ln_mem), wmem)
    oa = _swa_prompt(sink, qa, ka, va, bp, lp, 2048)
    oc = _memattn_prompt(qc, mk, mv, bp, lp, 2048)
    x1, h2 = _merge(xp, oa, ob.reshape(tp, 512), oc, *merge_w, 256)
    y_p = ffn(x1, h2).reshape(bp, lp, D_MODEL)
    wk_p = ka.reshape(bp, lp, LANES)[:, -WINDOW:].reshape(bp, WINDOW, A_KV, A_HD)
    wv_p = va.reshape(bp, lp, LANES)[:, -WINDOW:].reshape(bp, WINDOW, A_KV, A_HD)
    conv_p = tail[:, -(CONV_W - 1):]
    mem_k_p = mk.reshape(bp, N_MEM, C_HEADS, C_HD)
    mem_v_p = mv.reshape(bp, N_MEM, C_HEADS, C_HD)

    ts = bs * DEC_PAD
    xs = jnp.pad(x_sample, ((0, 0), (0, DEC_PAD - ls), (0, 0))).reshape(ts, D_MODEL)
    cos_s, sin_s = _rope_tables(PAST_LEN + jnp.arange(DEC_PAD, dtype=jnp.int32))
    cos_s, sin_s = jnp.tile(cos_s, (bs, 1)), jnp.tile(sin_s, (bs, 1))
    qa, ka, va, z, qc, ba, qkv = _proj(xs, vec(ln_mix_pre), ws, cos_s, sin_s, 512)
    r3 = lambda a: a.reshape(bs, DEC_PAD, a.shape[-1])
    oa, wk_s, wv_s = _swa_decode(sink, r3(qa), r3(ka), r3(va), state_win_k.reshape(bs, WINDOW, LANES),
                                 state_win_v.reshape(bs, WINDOW, LANES), 16, ls)
    oa = oa.reshape(ts, 512)
    wk_s = wk_s.reshape(state_win_k.shape)
    wv_s = wv_s.reshape(state_win_v.shape)
    histp = jnp.pad(state_conv, ((0, 0), (DEC_PAD - (CONV_W - 1), 0), (0, 0)))
    ob, rec_s = _gdn_decode(r3(qkv), histp, ba, z, state_rec, cw, alog, dtb, nw,
                            LANES // DEC_PAD, ls)
    oc = _memattn_decode(r3(qc), cache_mem_k.reshape(bs, N_MEM * C_HEADS, C_HD),
                         cache_mem_v.reshape(bs, N_MEM * C_HEADS, C_HD), 8).reshape(ts, 512)
    real = lambda a: a.reshape(bs, DEC_PAD, 512)[:, :ls].reshape(bs * ls, 512)
    x1, h2 = _merge(x_sample.reshape(bs * ls, D_MODEL), real(oa), real(ob), real(oc), *merge_w, 256)
    y_s = ffn(x1, h2).reshape(bs, ls, D_MODEL)
    conv_s = r3(qkv)[:, ls - (CONV_W - 1):ls]

    return (y_p, y_s, wk_p, wv_p, conv_p, rec_p, mem_k_p, mem_v_p, wk_s, wv_s, conv_s, rec_s)
```

```python
import functools

import numpy as np
import jax
import jax.numpy as jnp
from jax import lax
from jax.experimental import pallas as pl
from jax.experimental.pallas import tpu as pltpu

F32 = jnp.float32
BF16 = jnp.bfloat16

D_MODEL = 1024
PAST_LEN = 16384
EPS = 1e-6
ROPE_THETA = 10000.0
N_MEM = 256
WINDOW = 128
A_HD = 64
A_HEADS = 8
A_KV = 2
A_SCALE = A_HD ** -0.5
B_HEADS = 4
B_DK = 128
B_DV = 128
CONV_W = 4
GDN_CHUNK = 64
B_CONV_CH = B_HEADS * (2 * B_DK + B_DV)
C_HEADS = 4
C_HD = 128
N_BRANCH = 3
BRANCH_W = 512
D_FF = 2816

LANES = 128
SUBLANES = 8
PAIR = 2 * LANES
GATE_LANES = 2 * B_HEADS
VMEM_LIMIT = 56 * 1024 * 1024
NEG = -1e30
DEC_PAD = SUBLANES


def _cparams(sem, vmem=VMEM_LIMIT):
    return pltpu.CompilerParams(dimension_semantics=sem, vmem_limit_bytes=vmem)


def _const_spec(shape):
    nd = len(shape)
    return pl.BlockSpec(shape, lambda *_: (0,) * nd, pipeline_mode=pl.Buffered(1))


def _rms(x, g):
    ms = jnp.mean(x * x, axis=-1, keepdims=True)
    return x * lax.rsqrt(ms + EPS) * g


def _bdot(a, b):
    return jnp.dot(a.astype(BF16), b.astype(BF16), preferred_element_type=F32)


def _bdot_nt(a, b):
    return lax.dot_general(a.astype(BF16), b.astype(BF16), (((1,), (1,)), ((), ())),
                           preferred_element_type=F32)


def _bdot_tn(a, b):
    return lax.dot_general(a.astype(BF16), b.astype(BF16), (((0,), (0,)), ((), ())),
                           preferred_element_type=F32)


def _silu(x):
    return x * jax.nn.sigmoid(x)


def _softplus(x):
    return jnp.maximum(x, 0.0) + jnp.log1p(jnp.exp(-jnp.abs(x)))


def _rope128(v, cos, sin):
    lane = lax.broadcasted_iota(jnp.int32, v.shape, 1)
    fwd = pltpu.roll(v, 32, 1)
    bwd = pltpu.roll(v, 96, 1)
    sw = jnp.where((lane & 32) == 0, bwd, fwd)
    return v * cos + sw * sin


def _l2n(x):
    return x * lax.rsqrt(jnp.sum(x * x, axis=-1, keepdims=True) + EPS)

def _proj_steps(x_ref, g_ref, wq_ref, wr_ref, wzc_ref, wba_ref, cos_ref, sin_ref,
                qa_ref, ka_ref, va_ref, z_ref, qc_ref, ba_ref):
    h = _rms(x_ref[...], g_ref[...]).astype(BF16)
    cos = cos_ref[...]
    sin = sin_ref[...]

    def mm(w_ref, a, b):
        return jnp.dot(h, w_ref[:, a:b], preferred_element_type=F32)

    def qa_half(c0):
        q = mm(wq_ref, c0 * LANES, (c0 + 2) * LANES)
        for c in range(2):
            qa_ref[:, (c0 + c) * LANES:(c0 + c + 1) * LANES] = (
                _rope128(q[:, c * LANES:(c + 1) * LANES], cos, sin) * A_SCALE).astype(BF16)

    def kv():
        kv2 = mm(wr_ref, 0, PAIR)
        ka_ref[...] = _rope128(kv2[:, :LANES], cos, sin)
        va_ref[...] = kv2[:, LANES:]

    def z_half(c0):
        z_ref[:, c0:c0 + PAIR] = mm(wzc_ref, c0, c0 + PAIR)

    def qc_half(c0):
        qc_ref[:, c0:c0 + PAIR] = mm(wzc_ref, 512 + c0, 512 + c0 + PAIR).astype(BF16)

    def ba():
        ba_ref[...] = mm(wba_ref, 0, LANES)

    def qkv(a, b):
        return mm(wr_ref, PAIR + a, PAIR + b)

    steps = [lambda: qa_half(0), lambda: qa_half(2), kv, lambda: z_half(0), lambda: z_half(PAIR),
             lambda: qc_half(0), lambda: qc_half(PAIR), ba]
    return qkv, steps


def _proj_raw_kernel(x_ref, g_ref, wq_ref, wr_ref, wzc_ref, wba_ref, cos_ref, sin_ref,
                     qa_ref, ka_ref, va_ref, z_ref, qc_ref, ba_ref, qkv_ref):
    qkv, steps = _proj_steps(x_ref, g_ref, wq_ref, wr_ref, wzc_ref, wba_ref, cos_ref, sin_ref,
                             qa_ref, ka_ref, va_ref, z_ref, qc_ref, ba_ref)
    qkv_ref[...] = qkv(0, B_CONV_CH)
    for step in steps:
        step()


def _proj_conv_kernel(x_ref, g_ref, wq_ref, wr_ref, wzc_ref, wba_ref, cos_ref, sin_ref, cw_ref,
                      qa_ref, ka_ref, va_ref, z_ref, qc_ref, ba_ref, qn_ref, kn_ref, vv_ref, tail_ref,
                      buf_ref, *, tm, tiles_per_seq):
    hist = SUBLANES
    first = lax.rem(pl.program_id(0), tiles_per_seq) == 0

    @pl.when(first)
    def _():
        buf_ref[0:hist, :] = jnp.zeros((hist, B_CONV_CH), F32)

    @pl.when(jnp.logical_not(first))
    def _():
        buf_ref[0:hist, :] = buf_ref[tm:tm + hist, :]

    qkv, steps = _proj_steps(x_ref, g_ref, wq_ref, wr_ref, wzc_ref, wba_ref, cos_ref, sin_ref,
                             qa_ref, ka_ref, va_ref, z_ref, qc_ref, ba_ref)
    nq = B_HEADS * B_DK

    def conv_group(c0):
        cs = slice(c0, c0 + PAIR)
        raw = qkv(c0, c0 + PAIR)
        buf_ref[hist:hist + tm, cs] = raw
        tail_ref[0, :, cs] = raw[tm - hist:, :]
        xb = buf_ref[:, cs]
        acc = xb * cw_ref[0:1, cs]
        for i in range(1, CONV_W):
            acc = pltpu.roll(acc, 1, 0) + xb * cw_ref[i:i + 1, cs]
        act = _silu(acc[hist:])
        if c0 >= 2 * nq:
            vv_ref[:, c0 - 2 * nq:c0 - 2 * nq + PAIR] = act
            return
        out_ref, base, scale = (qn_ref, 0, B_DK ** -0.5) if c0 < nq else (kn_ref, nq, 1.0)
        for s in range(2):
            o0 = c0 - base + s * B_DK
            out_ref[:, o0:o0 + B_DK] = _l2n(act[:, s * B_DK:(s + 1) * B_DK]) * scale

    groups = [functools.partial(conv_group, c0) for c0 in range(0, B_CONV_CH, PAIR)]
    while groups or steps:
        if groups:
            groups.pop(0)()
        if steps:
            steps.pop(0)()


_PROJ_OUTS = [(512, BF16), (128, F32), (128, F32), (512, F32), (512, BF16), (128, F32)]


def _proj(x, gain, ws, cos, sin, tm, cw=None, seq=None):
    wq, wr, wtail, wba = ws
    t = x.shape[0]
    ntab = cos.shape[0] // tm
    row = lambda n: pl.BlockSpec((tm, n), lambda i: (i, 0))
    tab = pl.BlockSpec((tm, LANES), lambda i: (i % ntab, 0))
    in_specs = [row(D_MODEL), _const_spec((1, D_MODEL)), _const_spec(wq.shape), _const_spec(wr.shape),
                _const_spec((D_MODEL, D_MODEL)), _const_spec(wba.shape), tab, tab]
    out_specs = [row(n) for n, _ in _PROJ_OUTS]
    out_shape = [jax.ShapeDtypeStruct((t, n), d) for n, d in _PROJ_OUTS]
    if cw is None:
        return pl.pallas_call(
            _proj_raw_kernel, grid=(t // tm,), in_specs=in_specs,
            out_specs=out_specs + [row(B_CONV_CH)],
            out_shape=out_shape + [jax.ShapeDtypeStruct((t, B_CONV_CH), F32)],
            compiler_params=_cparams(("arbitrary",)), name="proj",
        )(x, gain, wq, wr, wtail, wba, cos, sin)
    tiles = seq // tm
    return pl.pallas_call(
        functools.partial(_proj_conv_kernel, tm=tm, tiles_per_seq=tiles),
        grid=(t // tm,),
        in_specs=in_specs + [_const_spec((CONV_W, B_CONV_CH))],
        out_specs=out_specs + [row(512)] * 3
        + [pl.BlockSpec((1, SUBLANES, B_CONV_CH), lambda i: (i // tiles, 0, 0))],
        out_shape=out_shape + [jax.ShapeDtypeStruct((t, 512), F32)] * 3
        + [jax.ShapeDtypeStruct((t // seq, SUBLANES, B_CONV_CH), F32)],
        scratch_shapes=[pltpu.VMEM((tm + SUBLANES, B_CONV_CH), F32)],
        compiler_params=_cparams(("arbitrary",)), name="proj_conv",
    )(x, gain, wq, wr, wtail, wba, cos, sin, cw)


def _swa_scores(q, k16):
    tq = q.shape[0]
    lo = lax.broadcasted_iota(jnp.int32, (tq, LANES), 1) < A_HD
    blocks = []
    for j in range(4):
        c = q[:, j * LANES:(j + 1) * LANES].astype(F32)
        blocks.append(jnp.where(lo, c, 0.0))
        blocks.append(jnp.where(lo, 0.0, c))
    lhs = jnp.concatenate(blocks, axis=0).astype(BF16)
    return lax.dot_general(lhs, k16, (((1,), (1,)), ((), ())), preferred_element_type=F32)


def _swa_finish(logits, v16, mask, sink_ref):
    tq = logits.shape[0] // A_HEADS
    lo = lax.broadcasted_iota(jnp.int32, (tq, LANES), 1) < A_HD
    es, inv = [], []
    for s in range(8):
        l = jnp.where(mask, logits[s * tq:(s + 1) * tq], NEG)
        sk = sink_ref[s]
        m = jnp.maximum(jnp.max(l, axis=-1, keepdims=True), sk)
        e = jnp.exp(l - m)
        den = jnp.sum(e, axis=-1, keepdims=True) + jnp.exp(sk - m)
        es.append(e.astype(BF16))
        inv.append(1.0 / den)
    pv = jnp.dot(jnp.concatenate(es, axis=0), v16, preferred_element_type=F32)
    outs = []
    for j in range(4):
        a = pv[(2 * j) * tq:(2 * j + 1) * tq] * inv[2 * j]
        b = pv[(2 * j + 1) * tq:(2 * j + 2) * tq] * inv[2 * j + 1]
        outs.append(jnp.where(lo, a, b))
    return jnp.concatenate(outs, axis=1)


def _swa_prompt_kernel(sink_ref, q_ref, kc_ref, kp_ref, vc_ref, vp_ref, o_ref, *, nblk):
    i = pl.program_id(1)
    kcat = jnp.concatenate([kp_ref[...], kc_ref[...]], axis=0).astype(BF16)
    vcat = jnp.concatenate([vp_ref[...], vc_ref[...]], axis=0).astype(BF16)
    ii = lax.broadcasted_iota(jnp.int32, (WINDOW, 2 * WINDOW), 0)
    jj = lax.broadcasted_iota(jnp.int32, (WINDOW, 2 * WINDOW), 1)
    band = (jj > ii) & (jj <= ii + WINDOW)
    for jb in range(nblk):
        mask = band
        if jb == 0:
            mask = band & ((jj >= WINDOW) | (i > 0))
        logits = _swa_scores(q_ref[jb * WINDOW:(jb + 1) * WINDOW, :], kcat[jb * WINDOW:(jb + 2) * WINDOW])
        o = _swa_finish(logits, vcat[jb * WINDOW:(jb + 2) * WINDOW], mask, sink_ref)
        o_ref[jb * WINDOW:(jb + 1) * WINDOW, :] = o.astype(BF16)


def _swa_prompt(sink, qa, ka, va, batch, seq, tq):
    nq = seq // tq
    nblk = tq // WINDOW
    nw = seq // WINDOW
    cur = lambda n: pl.BlockSpec((tq, n), lambda b, i: (b * nq + i, 0))
    prev = pl.BlockSpec((WINDOW, LANES), lambda b, i: (jnp.maximum(b * nw + i * nblk - 1, 0), 0))
    return pl.pallas_call(
        functools.partial(_swa_prompt_kernel, nblk=nblk),
        grid=(batch, nq),
        in_specs=[pl.BlockSpec(memory_space=pltpu.SMEM), cur(512), cur(LANES), prev, cur(LANES), prev],
        out_specs=cur(512),
        out_shape=jax.ShapeDtypeStruct(qa.shape, BF16),
        compiler_params=_cparams(("arbitrary", "arbitrary")),
        name="swa_prompt",
    )(sink, qa, ka, ka, va, va)


ROW_UNROLL = 8


def _rows_loop(rows, load, stages, store):
    def body(g, carry):
        idx = [g * ROW_UNROLL + j for j in range(ROW_UNROLL)]
        vals = [load(r) for r in idx]
        for stage in stages:
            vals = [stage(v) for v in vals]
        for r, o in zip(idx, vals):
            store(r, o)
        return carry

    lax.fori_loop(0, rows // ROW_UNROLL, body, 0)


def _swa_decode_kernel(sink_ref, q_ref, kn_ref, vn_ref, wk_ref, wv_ref, o_ref, wko_ref, wvo_ref,
                       *, rows, new_len):
    ii = lax.broadcasted_iota(jnp.int32, (DEC_PAD, 2 * WINDOW), 0)
    jj = lax.broadcasted_iota(jnp.int32, (DEC_PAD, 2 * WINDOW), 1)
    mask = ((jj < WINDOW) & (jj > ii)) | ((jj >= WINDOW) & (jj - WINDOW <= ii))
    pad = jnp.zeros((WINDOW - DEC_PAD, LANES), F32)
    keep = lax.broadcasted_iota(jnp.int32, (WINDOW, LANES), 0) < WINDOW - new_len

    def slide(win, new):
        tail = jnp.concatenate([pad, new], axis=0)
        return jnp.where(keep, pltpu.roll(win, WINDOW - new_len, 0), pltpu.roll(tail, DEC_PAD - new_len, 0))

    def load(r):
        return q_ref[r], wk_ref[r], kn_ref[r], wv_ref[r], vn_ref[r]

    def scores(args):
        q, wk, kn, wv, vn = args
        kk = jnp.concatenate([wk, kn, pad], axis=0).astype(BF16)
        return _swa_scores(q, kk), wk, kn, wv, vn

    def finish(args):
        logits, wk, kn, wv, vn = args
        vv = jnp.concatenate([wv, vn, pad], axis=0).astype(BF16)
        return _swa_finish(logits, vv, mask, sink_ref).astype(BF16), slide(wk, kn), slide(wv, vn)

    def store(r, outs):
        o_ref[r], wko_ref[r], wvo_ref[r] = outs

    _rows_loop(rows, load, [scores, finish], store)


def _swa_decode(sink, qa, kn, vn, wk, wv, rows, new_len):
    nb = qa.shape[0]
    blk = lambda a, n: pl.BlockSpec((rows, a, n), lambda i: (i, 0, 0))
    return pl.pallas_call(
        functools.partial(_swa_decode_kernel, rows=rows, new_len=new_len),
        grid=(nb // rows,),
        in_specs=[pl.BlockSpec(memory_space=pltpu.SMEM), blk(DEC_PAD, 512), blk(DEC_PAD, LANES),
                  blk(DEC_PAD, LANES), blk(WINDOW, LANES), blk(WINDOW, LANES)],
        out_specs=[blk(DEC_PAD, 512), blk(WINDOW, LANES), blk(WINDOW, LANES)],
        out_shape=[jax.ShapeDtypeStruct(qa.shape, BF16), jax.ShapeDtypeStruct(wk.shape, F32),
                   jax.ShapeDtypeStruct(wv.shape, F32)],
        compiler_params=_cparams(("arbitrary",)),
        name="swa_decode",
    )(sink, qa, kn, vn, wk, wv)


def _gdn_gates(ba, alog, dtb, valid):
    beta = jax.nn.sigmoid(ba)
    g = -jnp.exp(alog) * _softplus(ba + dtb)
    if valid is not None:
        beta = jnp.where(valid, beta, 0.0)
        g = jnp.where(valid, g, 0.0)
    return beta, g


def _chunk_masks(chunk):
    sh = chunk.bit_length() - 1
    ri = lax.broadcasted_iota(jnp.int32, (LANES, LANES), 0)
    ci = lax.broadcasted_iota(jnp.int32, (LANES, LANES), 1)
    same = (ri >> sh) == (ci >> sh)
    return same, same & (ri >= ci), same & (ri > ci), ri == ci


def _gdn_cumsums(g_all, chunk):
    n = g_all.shape[0]
    pos = lax.broadcasted_iota(jnp.int32, g_all.shape, 0) & (chunk - 1)
    gcol = g_all
    step = 1
    while step < chunk:
        gcol = gcol + jnp.where(pos >= step, pltpu.roll(gcol, step, 0), 0.0)
        step *= 2
    gtot = jnp.where(pos == chunk - 1, gcol, 0.0)
    step = 1
    while step < chunk:
        gtot = gtot + jnp.where(pos + step < chunk, pltpu.roll(gtot, n - step, 0), 0.0)
        step *= 2
    return gcol, gcol.T, gtot


def _sibling_mask(ri, ci, lvl):
    rb = ri >> lvl
    return ((rb & 1) == 1) & ((ci >> lvl) == rb - 1)


def _gdn_phase_a(heads, chunk):
    _, tri, strict, _ = _chunk_masks(chunk)
    ri = lax.broadcasted_iota(jnp.int32, (LANES, LANES), 0)
    ci = lax.broadcasted_iota(jnp.int32, (LANES, LANES), 1)
    kbs = [k * beta for q, k, v, beta, gcol, grow, gtot in heads]
    kkqk = [_bdot_nt(jnp.concatenate([kb, hd[0]], axis=0), hd[1]) for kb, hd in zip(kbs, heads)]
    a, qk = [], []
    for r, (q, k, v, beta, gcol, grow, gtot) in zip(kkqk, heads):
        decay = jnp.exp(jnp.where(tri, gcol - grow, NEG))
        a.append(jnp.where(strict, r[:LANES] * decay, 0.0))
        qk.append(r[LANES:] * decay)
    n = [-jnp.where(_sibling_mask(ri, ci, 0), ah, 0.0) for ah in a]
    for lvl in range(1, chunk.bit_length() - 1):
        sib = _sibling_mask(ri, ci, lvl)
        al = [jnp.where(sib, ah, 0.0) for ah in a]
        x = [alh + _bdot(nh, alh) for alh, nh in zip(al, n)]
        n = [nh - (xh + _bdot(xh, nh)) for xh, nh in zip(x, n)]
    outs = []
    rhs = [jnp.concatenate([hd[2] * hd[3], kb * jnp.exp(hd[4])], axis=1) for kb, hd in zip(kbs, heads)]
    sol = [r + _bdot(nh, r) for r, nh in zip(rhs, n)]
    for s, qkh, (q, k, v, beta, gcol, grow, gtot) in zip(sol, qk, heads):
        outs.append((s[:, :B_DV], s[:, B_DV:], q * jnp.exp(gcol), k * jnp.exp(gtot - gcol), qkh, jnp.exp(gtot)))
    return outs


def _gdn_qkv_heads(qkv, h):
    q = qkv[:, h * B_DK:(h + 1) * B_DK]
    k = qkv[:, (B_HEADS + h) * B_DK:(B_HEADS + h + 1) * B_DK]
    v = qkv[:, (2 * B_HEADS + h) * B_DK:(2 * B_HEADS + h + 1) * B_DK]
    q = q * lax.rsqrt(jnp.sum(q * q, axis=-1, keepdims=True) + EPS) * (B_DK ** -0.5)
    k = k * lax.rsqrt(jnp.sum(k * k, axis=-1, keepdims=True) + EPS)
    return q, k, v


def _gdn_out(o, z, nw):
    return _rms(o, nw) * _silu(z)


def _pair_bd(x2):
    lo = lax.broadcasted_iota(jnp.int32, x2.shape, 1) < LANES
    return jnp.concatenate([jnp.where(lo, x2, 0.0), jnp.where(lo, 0.0, x2)], axis=0).astype(BF16)


def _quad_bd(x4):
    blk = lax.broadcasted_iota(jnp.int32, x4.shape, 1) >> (GDN_CHUNK.bit_length() - 1)
    return jnp.concatenate([jnp.where(blk == j, x4, 0.0) for j in range(PAIR // GDN_CHUNK)],
                           axis=0).astype(BF16)


def _pair_cols(x, l0, l1):
    lo = lax.broadcasted_iota(jnp.int32, (x.shape[0], PAIR), 1) < LANES
    return jnp.where(lo, x[:, l0:l0 + 1], x[:, l1:l1 + 1])


def _gdn_prompt_kernel(q_ref, k_ref, v_ref, ba_ref, z_ref, alog_ref, dtb_ref, nw_ref,
                       ob_ref, s_ref, *, nb, ntb):
    t = pl.program_id(0)
    npair = B_HEADS // 2

    @pl.when(t == 0)
    def _():
        s_ref[...] = jnp.zeros(s_ref.shape, F32)

    alog = alog_ref[...]
    dtb = dtb_ref[...]
    nw = nw_ref[...]
    nchunk = LANES // GDN_CHUNK
    _, tri, strict, _ = _chunk_masks(GDN_CHUNK)
    tri2 = jnp.concatenate([tri, tri], axis=1)
    strict2 = jnp.concatenate([strict, strict], axis=1)

    items = []
    blocks = [(b, tb) for b in range(nb) for tb in range(ntb)]
    assert len(blocks) * GATE_LANES <= LANES
    packed = None
    for j, (b, tb) in enumerate(blocks):
        blk = ba_ref[b, tb * LANES:(tb + 1) * LANES, :]
        packed = blk if j == 0 else packed + pltpu.roll(blk, GATE_LANES * j, 1)
    beta_all, g_all = _gdn_gates(packed, alog, dtb, None)
    gcol_all, grow_all, gtot_all = _gdn_cumsums(g_all, GDN_CHUNK)
    eg_all = jnp.exp(gcol_all)
    ed_all = jnp.exp(gtot_all - gcol_all)
    et_all = jnp.exp(gtot_all)
    for j, (b, tb) in enumerate(blocks):
        rsl = slice(tb * LANES, (tb + 1) * LANES)
        for p in range(npair):
            h0, h1 = GATE_LANES * j + 2 * p, GATE_LANES * j + 2 * p + 1
            g0, g1 = B_HEADS + h0, B_HEADS + h1
            psl = slice(p * PAIR, (p + 1) * PAIR)
            q2 = q_ref[b, rsl, psl]
            k2 = k_ref[b, rsl, psl]
            v2 = v_ref[b, rsl, psl]
            beta2 = _pair_cols(beta_all, h0, h1)
            grow2 = jnp.concatenate([grow_all[g0:g0 + 1, :], grow_all[g1:g1 + 1, :]], axis=1)
            decay2 = jnp.exp(jnp.where(tri2, _pair_cols(gcol_all, g0, g1) - grow2, NEG))
            kb2 = k2 * beta2
            items.append(dict(
                b=b, tb=tb, p=p, rsl=rsl, decay2=decay2, kb2=kb2, v2b=v2 * beta2,
                kbe2=kb2 * _pair_cols(eg_all, g0, g1),
                lhs=jnp.concatenate([kb2, q2], axis=0).astype(BF16),
                kbd=_pair_bd(k2),
                qe2=(q2 * _pair_cols(eg_all, g0, g1)).astype(BF16),
                kd2=(k2 * _pair_cols(ed_all, g0, g1)).astype(BF16),
                et2=_pair_cols(et_all, g0, g1)))

    for it in items:
        kkqk = lax.dot_general(it["lhs"], it["kbd"], (((1,), (1,)), ((), ())),
                               preferred_element_type=F32)
        it["a2"] = jnp.where(strict2, kkqk[:LANES] * it["decay2"], 0.0)
        it["qk2"] = (kkqk[LANES:] * it["decay2"]).astype(BF16)
    qlane = lax.broadcasted_iota(jnp.int32, (GDN_CHUNK, PAIR), 1)
    qrow = lax.broadcasted_iota(jnp.int32, (GDN_CHUNK, PAIR), 0)
    qcol = qlane & (GDN_CHUNK - 1)
    top = (qlane & GDN_CHUNK) == 0
    for it in items:
        it["a64"] = it["a2"][:GDN_CHUNK] + it["a2"][GDN_CHUNK:]
        it["n"] = -jnp.where(_sibling_mask(qrow, qcol, 0), it["a64"], 0.0)
    for lvl in range(1, GDN_CHUNK.bit_length() - 1):
        sib = _sibling_mask(qrow, qcol, lvl)
        for it in items:
            it["al"] = jnp.where(sib, it["a64"], 0.0)
            it["x"] = it["al"] + jnp.dot(it["n"].astype(BF16), _quad_bd(it["al"]), preferred_element_type=F32)
        for it in items:
            x = it["x"]
            it["n"] = it["n"] - (x + jnp.dot(x.astype(BF16), _quad_bd(it["n"]), preferred_element_type=F32))
    for it in items:
        n64 = it["n"]
        it["n"] = jnp.concatenate([jnp.where(top, n64, 0.0), jnp.where(top, 0.0, n64)], axis=0)
    for it in items:
        us, ws = [], []
        for s in range(2):
            sl = slice(s * LANES, (s + 1) * LANES)
            rhs = jnp.concatenate([it["v2b"][:, sl], it["kbe2"][:, sl]], axis=1)
            sol = rhs + _bdot(it["n"][:, sl], rhs)
            us.append(sol[:, :B_DV])
            ws.append(sol[:, B_DV:])
        it["u2"] = jnp.concatenate(us, axis=1)
        it["w2"] = jnp.concatenate(ws, axis=1).astype(BF16)

    nchain = nb * npair
    state = [[s_ref[ch // npair, 2 * (ch % npair) + s] for s in range(2)] for ch in range(nchain)]
    zeros16 = jnp.zeros((LANES, LANES), BF16)
    outs = {}
    for tb in range(ntb):
        cur = [(it["b"] * npair + it["p"], i, it) for i, it in enumerate(items) if it["tb"] == tb]
        for c in range(nchunk):
            rows = slice(c * GDN_CHUNK, (c + 1) * GDN_CHUNK)
            rs = []
            for ch, _, it in cur:
                sa, sb = (s.astype(BF16) for s in state[ch])
                sbd = jnp.concatenate([jnp.concatenate([sa, zeros16], axis=1),
                                       jnp.concatenate([zeros16, sb], axis=1)], axis=0)
                rs.append(jnp.dot(jnp.concatenate([it["w2"][rows], it["qe2"][rows]], axis=0), sbd,
                                  preferred_element_type=F32))
            for (ch, i, it), r in zip(cur, rs):
                vn2 = it["u2"][rows] - r[:GDN_CHUNK]
                vnb = vn2.astype(BF16)
                vt = _pair_bd(jnp.concatenate([vn2] * nchunk, axis=0))
                outs[(i, c)] = r[GDN_CHUNK:] + jnp.dot(it["qk2"][rows], vt, preferred_element_type=F32)
                for s in range(2):
                    hl = slice(s * LANES, (s + 1) * LANES)
                    upd = lax.dot_general(it["kd2"][rows, hl], vnb[:, hl], (((0,), (0,)), ((), ())),
                                          preferred_element_type=F32)
                    state[ch][s] = state[ch][s] * it["et2"][c * GDN_CHUNK:c * GDN_CHUNK + 1, hl] + upd
    for ch in range(nchain):
        for s in range(2):
            s_ref[ch // npair, 2 * (ch % npair) + s] = state[ch][s]

    for i, it in enumerate(items):
        o2 = jnp.concatenate([outs[(i, c)] for c in range(nchunk)], axis=0)
        for s in range(2):
            h = 2 * it["p"] + s
            sl = slice(h * B_DV, (h + 1) * B_DV)
            o = o2[:, s * LANES:(s + 1) * LANES]
            ob_ref[it["b"], it["rsl"], sl] = _gdn_out(o, z_ref[it["b"], it["rsl"], sl], nw).astype(BF16)


GDN_TOKEN_BLOCKS = 2


def _gdn_prompt(q, k, v, ba, z, alog, dtb, nw):
    batch, seq, _ = q.shape
    tm = GDN_TOKEN_BLOCKS * LANES
    tok = lambda n: pl.BlockSpec((batch, tm, n), lambda t: (0, t, 0))
    state = pl.BlockSpec((batch, B_HEADS, B_DK, B_DV), lambda t: (0, 0, 0, 0))
    return pl.pallas_call(
        functools.partial(_gdn_prompt_kernel, nb=batch, ntb=GDN_TOKEN_BLOCKS),
        grid=(seq // tm,),
        in_specs=[tok(512), tok(512), tok(512), tok(LANES), tok(512),
                  _const_spec((1, LANES)), _const_spec((1, LANES)), _const_spec((1, B_DV))],
        out_specs=[tok(512), state],
        out_shape=[jax.ShapeDtypeStruct((batch, seq, 512), BF16),
                   jax.ShapeDtypeStruct((batch, B_HEADS, B_DK, B_DV), F32)],
        compiler_params=_cparams(("arbitrary",)),
        name="gdn_prompt",
    )(q, k, v, ba, z, alog, dtb, nw)


def _gdn_decode_kernel(raw_ref, hist_ref, ba_ref, z_ref, rec_ref, cw_ref, alog_ref, dtb_ref, nw_ref,
                       ob_ref, s_ref, buf_ref, u_s, w_s, qe_s, kd_s, qk_s, eg_s, o_s, *, rows, valid_len):
    buf_ref[:, 0:DEC_PAD, :] = hist_ref[...]
    buf_ref[:, DEC_PAD:2 * DEC_PAD, :] = raw_ref[...]
    conv = None
    for i in range(CONV_W):
        off = DEC_PAD - (CONV_W - 1) + i
        term = buf_ref[:, off:off + DEC_PAD, :] * cw_ref[i:i + 1, :]
        conv = term if conv is None else conv + term
    qkv = _silu(conv.reshape(rows * DEC_PAD, B_CONV_CH))
    tok = lax.broadcasted_iota(jnp.int32, (LANES, LANES), 0) & (DEC_PAD - 1)
    beta_all, g_all = _gdn_gates(ba_ref[...], alog_ref[...], dtb_ref[...], tok < valid_len)
    gcol_all, grow_all, gtot_all = _gdn_cumsums(g_all, DEC_PAD)
    heads = []
    for h in range(B_HEADS):
        gl = B_HEADS + h
        heads.append(_gdn_qkv_heads(qkv, h) + (beta_all[:, h:h + 1], gcol_all[:, gl:gl + 1],
                                               grow_all[gl:gl + 1, :], gtot_all[:, gl:gl + 1]))
    for h, (u, w, qe, kd, qk, egt) in enumerate(_gdn_phase_a(heads, DEC_PAD)):
        u_s[h] = u
        w_s[h] = w
        qe_s[h] = qe
        kd_s[h] = kd
        qk_s[h] = qk
        eg_s[h] = jnp.broadcast_to(egt, (LANES, LANES))

    def load(r):
        r0 = pl.multiple_of(r * DEC_PAD, DEC_PAD)
        rr = pl.ds(r0, DEC_PAD)
        return [(rec_ref[r, h], w_s[h, rr, :], qe_s[h, rr, :], u_s[h, rr, :], qk_s[h, rr, :],
                 kd_s[h, rr, :], eg_s[h, pl.ds(r0, 1), :]) for h in range(B_HEADS)]

    def read_state(heads):
        return [(_bdot(jnp.concatenate([w, qe], axis=0), s), s, u, qk, kd, eg)
                for s, w, qe, u, qk, kd, eg in heads]

    def update(heads):
        outs = []
        for res, s, u, qk, kd, eg in heads:
            v_new = u - res[:DEC_PAD]
            vt = jnp.concatenate([v_new] * (LANES // DEC_PAD), axis=0)
            outs.append((res[DEC_PAD:] + _bdot(qk, vt), s * eg + _bdot_tn(kd, v_new)))
        return outs

    def store(r, outs):
        rr = pl.ds(pl.multiple_of(r * DEC_PAD, DEC_PAD), DEC_PAD)
        for h, (o, s_new) in enumerate(outs):
            o_s[h, rr, :] = o
            s_ref[r, h] = s_new

    _rows_loop(rows, load, [read_state, update], store)
    nw = nw_ref[...]
    for h in range(B_HEADS):
        sl = slice(h * B_DV, (h + 1) * B_DV)
        ob_ref[:, sl] = _gdn_out(o_s[h], z_ref[:, sl], nw).astype(BF16)


def _gdn_decode(raw, histp, ba, z, rec, cw, alog, dtb, nw, rows, valid_len):
    nb = raw.shape[0]
    flat = rows * DEC_PAD
    assert flat == LANES
    sq = lambda: pltpu.VMEM((B_HEADS, LANES, LANES), F32)
    return pl.pallas_call(
        functools.partial(_gdn_decode_kernel, rows=rows, valid_len=valid_len),
        grid=(nb // rows,),
        in_specs=[pl.BlockSpec((rows, DEC_PAD, B_CONV_CH), lambda i: (i, 0, 0)),
                  pl.BlockSpec((rows, DEC_PAD, B_CONV_CH), lambda i: (i, 0, 0)),
                  pl.BlockSpec((flat, LANES), lambda i: (i, 0)),
                  pl.BlockSpec((flat, 512), lambda i: (i, 0)),
                  pl.BlockSpec((rows, B_HEADS, B_DK, B_DV), lambda i: (i, 0, 0, 0)),
                  _const_spec((CONV_W, B_CONV_CH)), _const_spec((1, LANES)), _const_spec((1, LANES)),
                  _const_spec((1, B_DV))],
        out_specs=[pl.BlockSpec((flat, 512), lambda i: (i, 0)),
                   pl.BlockSpec((rows, B_HEADS, B_DK, B_DV), lambda i: (i, 0, 0, 0))],
        out_shape=[jax.ShapeDtypeStruct((nb * DEC_PAD, 512), BF16),
                   jax.ShapeDtypeStruct(rec.shape, F32)],
        scratch_shapes=[pltpu.VMEM((rows, 2 * DEC_PAD, B_CONV_CH), F32)] + [sq() for _ in range(7)],
        compiler_params=_cparams(("arbitrary",)),
        name="gdn_decode",
    )(raw, histp, ba, z, rec, cw, alog, dtb, nw)


def _memkv_kernel(m_ref, g_ref, w_ref, k_ref, v_ref):
    h = _rms(m_ref[...], g_ref[...]).astype(BF16)
    n = C_HEADS * C_HD
    k_ref[...] = jnp.dot(h, w_ref[:, :n], preferred_element_type=F32)
    v_ref[...] = jnp.dot(h, w_ref[:, n:], preferred_element_type=F32)


def _memkv(mem, gain, w):
    t = mem.shape[0]
    tm = 512
    n = C_HEADS * C_HD
    return pl.pallas_call(
        _memkv_kernel,
        grid=(t // tm,),
        in_specs=[pl.BlockSpec((tm, D_MODEL), lambda i: (i, 0)), _const_spec((1, D_MODEL)),
                  _const_spec((D_MODEL, 2 * n))],
        out_specs=[pl.BlockSpec((tm, n), lambda i: (i, 0))] * 2,
        out_shape=[jax.ShapeDtypeStruct((t, n), F32)] * 2,
        compiler_params=_cparams(("arbitrary",)),
        name="memkv",
    )(mem, gain, w)


def _softmax_rows(logits):
    m = jnp.max(logits, axis=-1, keepdims=True)
    e = jnp.exp(logits - m)
    return e, 1.0 / jnp.sum(e, axis=-1, keepdims=True)


def _memattn_prompt_kernel(q_ref, k_ref, v_ref, o_ref):
    hs = lambda h: slice(h * C_HD, (h + 1) * C_HD)
    scores = lambda h: _bdot_nt(q_ref[:, hs(h)], k_ref[:, hs(h)])
    logits = scores(0)
    for h in range(C_HEADS):
        nxt = scores(h + 1) if h + 1 < C_HEADS else None
        e, inv = _softmax_rows(logits * (C_HD ** -0.5))
        o_ref[:, hs(h)] = (_bdot(e, v_ref[:, hs(h)]) * inv).astype(BF16)
        logits = nxt


def _memattn_prompt(qc, mk, mv, batch, seq, tm):
    nq = seq // tm
    n = C_HEADS * C_HD
    cur = pl.BlockSpec((tm, n), lambda b, i: (b * nq + i, 0))
    mem = pl.BlockSpec((N_MEM, n), lambda b, i: (b, 0))
    return pl.pallas_call(
        _memattn_prompt_kernel,
        grid=(batch, nq),
        in_specs=[cur, mem, mem],
        out_specs=cur,
        out_shape=jax.ShapeDtypeStruct(qc.shape, BF16),
        compiler_params=_cparams(("arbitrary", "arbitrary")),
        name="memattn_prompt",
    )(qc, mk, mv)


def _memattn_decode_kernel(q_ref, k_ref, v_ref, o_ref, *, rows):
    nk = N_MEM * C_HEADS
    col = lax.broadcasted_iota(jnp.int32, (C_HEADS * DEC_PAD, nk), 1)
    row = lax.broadcasted_iota(jnp.int32, (C_HEADS * DEC_PAD, nk), 0)
    own = (col & (C_HEADS - 1)) == (row >> (DEC_PAD.bit_length() - 1))

    def load(r):
        return q_ref[r], k_ref[r], v_ref[r]

    def scores(args):
        q, k, v = args
        q = q.astype(F32)
        lhs = jnp.concatenate([q[:, h * C_HD:(h + 1) * C_HD] for h in range(C_HEADS)], axis=0)
        return _bdot_nt(lhs, k), v

    def finish(args):
        logits, v = args
        e, inv = _softmax_rows(jnp.where(own, logits * (C_HD ** -0.5), NEG))
        pv = _bdot(e, v) * inv
        return jnp.concatenate([pv[h * DEC_PAD:(h + 1) * DEC_PAD] for h in range(C_HEADS)],
                               axis=1).astype(BF16)

    def store(r, o):
        o_ref[r] = o

    _rows_loop(rows, load, [scores, finish], store)


def _memattn_decode(qc, ck, cv, rows):
    nb = qc.shape[0]
    n = C_HEADS * C_HD
    blk = pl.BlockSpec((rows, DEC_PAD, n), lambda i: (i, 0, 0))
    cache = pl.BlockSpec((rows, N_MEM * C_HEADS, C_HD), lambda i: (i, 0, 0))
    return pl.pallas_call(
        functools.partial(_memattn_decode_kernel, rows=rows),
        grid=(nb // rows,),
        in_specs=[blk, cache, cache],
        out_specs=blk,
        out_shape=jax.ShapeDtypeStruct(qc.shape, BF16),
        compiler_params=_cparams(("arbitrary",)),
        name="memattn_decode",
    )(qc, ck, cv)


def _merge_kernel(x_ref, oa_ref, ob_ref, oc_ref, gpre_ref, wga_ref, wgb_ref, wgc_ref, wb_ref, wo_ref,
                  gpost_ref, gfpre_ref, x1_ref, h2_ref):
    x = x_ref[...]
    ups = [jnp.dot(o_ref[...], wb_ref[n], preferred_element_type=F32)
           for n, o_ref in enumerate((oa_ref, ob_ref, oc_ref))]
    h = _rms(x, gpre_ref[...]).astype(BF16)
    mix = None
    for n, wg_ref in enumerate((wga_ref, wgb_ref, wgc_ref)):
        gate = jax.nn.sigmoid(jnp.dot(h, wg_ref[...], preferred_element_type=F32))
        mix = gate * ups[n] if mix is None else mix + gate * ups[n]
    x1 = x + _rms(_bdot(mix, wo_ref[...]), gpost_ref[...])
    x1_ref[...] = x1
    h2_ref[...] = _rms(x1, gfpre_ref[...]).astype(BF16)


FF_SPLIT = 6 * PAIR


def _ffn_kernel(x1_ref, h2_ref, wfi_ref, wfo_ref, gfpost_ref, y_ref):
    h2 = h2_ref[...]
    f = None
    halves = []
    for a, b in ((0, FF_SPLIT), (FF_SPLIT, D_FF)):
        gt = jnp.dot(h2, wfi_ref[:, a:b], preferred_element_type=F32)
        uf = jnp.dot(h2, wfi_ref[:, D_FF + a:D_FF + b], preferred_element_type=F32)
        halves.append((a, b, gt, uf))
    for a, b, gt, uf in halves:
        part = _bdot(_silu(gt) * uf, wfo_ref[a:b, :])
        f = part if f is None else f + part
    y_ref[...] = x1_ref[...] + _rms(f, gfpost_ref[...])


def _merge_weight_specs():
    vec = _const_spec((1, D_MODEL))
    return ([vec] + [pl.BlockSpec((D_MODEL, D_MODEL), functools.partial(lambda n, *_: (0, n), n + 1),
                                  pipeline_mode=pl.Buffered(1)) for n in range(N_BRANCH)]
            + [_const_spec((N_BRANCH, BRANCH_W, D_MODEL)), _const_spec((D_MODEL, D_MODEL)), vec, vec])


def _merge(x, oa, ob, oc, gpre, wtail, wb, wo, gpost, gfpre, tm):
    t = x.shape[0]
    row = lambda n: pl.BlockSpec((tm, n), lambda i: (i, 0))
    return pl.pallas_call(
        _merge_kernel,
        grid=(t // tm,),
        in_specs=[row(D_MODEL), row(512), row(512), row(512)] + _merge_weight_specs(),
        out_specs=[row(D_MODEL), row(D_MODEL)],
        out_shape=[jax.ShapeDtypeStruct(x.shape, F32), jax.ShapeDtypeStruct(x.shape, BF16)],
        compiler_params=_cparams(("arbitrary",)),
        name="merge",
    )(x, oa, ob, oc, gpre, wtail, wtail, wtail, wb, wo, gpost, gfpre)


def _ffn(x1, h2, wfi, wfo, gfpost, tm):
    t = x1.shape[0]
    vec = _const_spec((1, D_MODEL))
    frow = pl.BlockSpec((tm, D_MODEL), lambda i: (i, 0))
    return pl.pallas_call(
        _ffn_kernel,
        grid=(t // tm,),
        in_specs=[frow, frow, _const_spec((D_MODEL, 2 * D_FF)),
                  _const_spec((D_FF, D_MODEL)), vec],
        out_specs=frow,
        out_shape=jax.ShapeDtypeStruct(x1.shape, F32),
        compiler_params=_cparams(("arbitrary",)),
        name="ffn",
    )(x1, h2, wfi, wfo, gfpost)


def _rope_tables(pos):
    half = A_HD // 2
    inv = ROPE_THETA ** (-jnp.arange(half, dtype=F32) / half)
    ang = pos.astype(F32)[:, None] * inv[None, :]
    cos, sin = jnp.cos(ang), jnp.sin(ang)
    cos = jnp.concatenate([cos, cos], axis=-1)
    sin = jnp.concatenate([-sin, sin], axis=-1)
    return jnp.tile(cos, (1, LANES // A_HD)), jnp.tile(sin, (1, LANES // A_HD))


def _lane_row(vals, offset):
    grp = jnp.zeros((GATE_LANES,), F32).at[offset:offset + vals.shape[0]].set(vals.astype(F32))
    return jnp.tile(grp, LANES // GATE_LANES).reshape(1, LANES)


def kernel(x_prompt, x_sample, mem_prompt, state_win_k, state_win_v, state_conv, state_rec,
           cache_mem_k, cache_mem_v, ln_mix_pre, w_in, attn_sink, gdn_conv_w, gdn_a_log,
           gdn_dt_bias, gdn_norm_w, ln_mem, w_mem_kv, w_branch, w_out, ln_mix_post,
           ln_ffn_pre, w_ffn_in, w_ffn_out, ln_ffn_post):
    bp, lp, _ = x_prompt.shape
    bs, ls, _ = x_sample.shape

    sizes = [512, 128, 128, B_CONV_CH, B_HEADS, B_HEADS, 512, 512, N_BRANCH * D_MODEL]
    o = np.cumsum([0] + sizes)
    hperm = np.concatenate([np.r_[j * A_HD:(j + 1) * A_HD, (j + 4) * A_HD:(j + 5) * A_HD] for j in range(4)])
    ws = (w_in[:, hperm].astype(BF16), w_in[:, o[1]:o[4]].astype(BF16), w_in[:, o[6]:o[9]].astype(BF16),
          jnp.pad(w_in[:, o[4]:o[6]], ((0, 0), (0, LANES - 2 * B_HEADS))).astype(BF16))
    wb = jnp.concatenate([w_branch[0:1][:, hperm], w_branch[1:]], axis=0).astype(BF16)
    wo = w_out.astype(BF16)
    wfi = w_ffn_in.astype(BF16)
    wfo = w_ffn_out.astype(BF16)
    wmem = w_mem_kv.astype(BF16)
    sink = attn_sink.astype(F32)[np.array([0, 4, 1, 5, 2, 6, 3, 7])]
    vec = lambda g: g.astype(F32).reshape(1, -1)
    alog = _lane_row(gdn_a_log, B_HEADS)
    dtb = _lane_row(gdn_dt_bias, B_HEADS)
    cw = gdn_conv_w.astype(F32)
    nw = vec(gdn_norm_w)

    merge_w = (vec(ln_mix_pre), ws[2], wb, wo, vec(ln_mix_post), vec(ln_ffn_pre))
    ffn = lambda x1, h2: _ffn(x1, h2, wfi, wfo, vec(ln_ffn_post), min(1024, x1.shape[0]))

    tp = bp * lp
    xp = x_prompt.reshape(tp, D_MODEL)
    cos_p, sin_p = _rope_tables(jnp.arange(lp, dtype=jnp.int32))
    qa, ka, va, z, qc, ba, qn, kn, vv, tail = _proj(xp, vec(ln_mix_pre), ws, cos_p, sin_p, 512, cw, lp)
    b3 = lambda a: a.reshape(bp, lp, a.shape[-1])
    ob, rec_p = _gdn_prompt(b3(qn), b3(kn), b3(vv), b3(ba), b3(z), alog, dtb, nw)
    mk, mv = _memkv(mem_prompt.reshape(bp * N_MEM, D_MODEL), vec(ln_mem), wmem)
    oa = _swa_prompt(sink, qa, ka, va, bp, lp, 2048)
    oc = _memattn_prompt(qc, mk, mv, bp, lp, 2048)
    x1, h2 = _merge(xp, oa, ob.reshape(tp, 512), oc, *merge_w, 256)
    y_p = ffn(x1, h2).reshape(bp, lp, D_MODEL)
    wk_p = ka.reshape(bp, lp, LANES)[:, -WINDOW:].reshape(bp, WINDOW, A_KV, A_HD)
    wv_p = va.reshape(bp, lp, LANES)[:, -WINDOW:].reshape(bp, WINDOW, A_KV, A_HD)
    conv_p = tail[:, -(CONV_W - 1):]
    mem_k_p = mk.reshape(bp, N_MEM, C_HEADS, C_HD)
    mem_v_p = mv.reshape(bp, N_MEM, C_HEADS, C_HD)

    ts = bs * DEC_PAD
    xs = jnp.pad(x_sample, ((0, 0), (0, DEC_PAD - ls), (0, 0))).reshape(ts, D_MODEL)
    cos_s, sin_s = _rope_tables(PAST_LEN + jnp.arange(DEC_PAD, dtype=jnp.int32))
    cos_s, sin_s = jnp.tile(cos_s, (bs, 1)), jnp.tile(sin_s, (bs, 1))
    qa, ka, va, z, qc, ba, qkv = _proj(xs, vec(ln_mix_pre), ws, cos_s, sin_s, 512)
    r3 = lambda a: a.reshape(bs, DEC_PAD, a.shape[-1])
    oa, wk_s, wv_s = _swa_decode(sink, r3(qa), r3(ka), r3(va), state_win_k.reshape(bs, WINDOW, LANES),
                                 state_win_v.reshape(bs, WINDOW, LANES), 16, ls)
    oa = oa.reshape(ts, 512)
    wk_s = wk_s.reshape(state_win_k.shape)
    wv_s = wv_s.reshape(state_win_v.shape)
    histp = jnp.pad(state_conv, ((0, 0), (DEC_PAD - (CONV_W - 1), 0), (0, 0)))
    ob, rec_s = _gdn_decode(r3(qkv), histp, ba, z, state_rec, cw, alog, dtb, nw,
                            LANES // DEC_PAD, ls)
    oc = _memattn_decode(r3(qc), cache_mem_k.reshape(bs, N_MEM * C_HEADS, C_HD),
                         cache_mem_v.reshape(bs, N_MEM * C_HEADS, C_HD), 8).reshape(ts, 512)
    real = lambda a: a.reshape(bs, DEC_PAD, 512)[:, :ls].reshape(bs * ls, 512)
    x1, h2 = _merge(x_sample.reshape(bs * ls, D_MODEL), real(oa), real(ob), real(oc), *merge_w, 256)
    y_s = ffn(x1, h2).reshape(bs, ls, D_MODEL)
    conv_s = r3(qkv)[:, ls - (CONV_W - 1):ls]

    return (y_p, y_s, wk_p, wv_p, conv_p, rec_p, mem_k_p, mem_v_p, wk_s, wv_s, conv_s, rec_s)
```

```python
import functools

import numpy as np
import jax
import jax.numpy as jnp
from jax import lax
from jax.experimental import pallas as pl
from jax.experimental.pallas import tpu as pltpu

F32 = jnp.float32
BF16 = jnp.bfloat16

D_MODEL = 1024
PAST_LEN = 16384
EPS = 1e-6
ROPE_THETA = 10000.0
N_MEM = 256
WINDOW = 128
A_HD = 64
A_HEADS = 8
A_KV = 2
A_SCALE = A_HD ** -0.5
B_HEADS = 4
B_DK = 128
B_DV = 128
CONV_W = 4
GDN_CHUNK = 64
B_CONV_CH = B_HEADS * (2 * B_DK + B_DV)
C_HEADS = 4
C_HD = 128
N_BRANCH = 3
BRANCH_W = 512
D_FF = 2816

LANES = 128
SUBLANES = 8
PAIR = 2 * LANES
GATE_LANES = 2 * B_HEADS
VMEM_LIMIT = 56 * 1024 * 1024
NEG = -1e30
DEC_PAD = SUBLANES


def _cparams(sem, vmem=VMEM_LIMIT):
    return pltpu.CompilerParams(dimension_semantics=sem, vmem_limit_bytes=vmem)


def _const_spec(shape):
    nd = len(shape)
    return pl.BlockSpec(shape, lambda *_: (0,) * nd, pipeline_mode=pl.Buffered(1))


def _rms(x, g):
    ms = jnp.mean(x * x, axis=-1, keepdims=True)
    return x * lax.rsqrt(ms + EPS) * g


def _bdot(a, b):
    return jnp.dot(a.astype(BF16), b.astype(BF16), preferred_element_type=F32)


def _bdot_nt(a, b):
    return lax.dot_general(a.astype(BF16), b.astype(BF16), (((1,), (1,)), ((), ())),
                           preferred_element_type=F32)


def _bdot_tn(a, b):
    return lax.dot_general(a.astype(BF16), b.astype(BF16), (((0,), (0,)), ((), ())),
                           preferred_element_type=F32)


def _silu(x):
    return x * jax.nn.sigmoid(x)


def _softplus(x):
    return jnp.maximum(x, 0.0) + jnp.log1p(jnp.exp(-jnp.abs(x)))


def _rope128(v, cos, sin):
    lane = lax.broadcasted_iota(jnp.int32, v.shape, 1)
    fwd = pltpu.roll(v, 32, 1)
    bwd = pltpu.roll(v, 96, 1)
    sw = jnp.where((lane & 32) == 0, bwd, fwd)
    return v * cos + sw * sin


def _l2n(x):
    return x * lax.rsqrt(jnp.sum(x * x, axis=-1, keepdims=True) + EPS)

def _proj_steps(x_ref, g_ref, wq_ref, wr_ref, wzc_ref, wba_ref, cos_ref, sin_ref,
                qa_ref, ka_ref, va_ref, z_ref, qc_ref, ba_ref):
    h = _rms(x_ref[...], g_ref[...]).astype(BF16)
    cos = cos_ref[...]
    sin = sin_ref[...]

    def mm(w_ref, a, b):
        return jnp.dot(h, w_ref[:, a:b], preferred_element_type=F32)

    def qa_half(c0):
        q = mm(wq_ref, c0 * LANES, (c0 + 2) * LANES)
        for c in range(2):
            qa_ref[:, (c0 + c) * LANES:(c0 + c + 1) * LANES] = (
                _rope128(q[:, c * LANES:(c + 1) * LANES], cos, sin) * A_SCALE).astype(BF16)

    def kv():
        kv2 = mm(wr_ref, 0, PAIR)
        ka_ref[...] = _rope128(kv2[:, :LANES], cos, sin)
        va_ref[...] = kv2[:, LANES:]

    def z_half(c0):
        z_ref[:, c0:c0 + PAIR] = mm(wzc_ref, c0, c0 + PAIR)

    def qc_half(c0):
        qc_ref[:, c0:c0 + PAIR] = mm(wzc_ref, 512 + c0, 512 + c0 + PAIR).astype(BF16)

    def ba():
        ba_ref[...] = mm(wba_ref, 0, LANES)

    def qkv(a, b):
        return mm(wr_ref, PAIR + a, PAIR + b)

    steps = [lambda: qa_half(0), lambda: qa_half(2), kv, lambda: z_half(0), lambda: z_half(PAIR),
             lambda: qc_half(0), lambda: qc_half(PAIR), ba]
    return qkv, steps


def _proj_raw_kernel(x_ref, g_ref, wq_ref, wr_ref, wzc_ref, wba_ref, cos_ref, sin_ref,
                     qa_ref, ka_ref, va_ref, z_ref, qc_ref, ba_ref, qkv_ref):
    qkv, steps = _proj_steps(x_ref, g_ref, wq_ref, wr_ref, wzc_ref, wba_ref, cos_ref, sin_ref,
                             qa_ref, ka_ref, va_ref, z_ref, qc_ref, ba_ref)
    qkv_ref[...] = qkv(0, B_CONV_CH)
    for step in steps:
        step()


def _proj_conv_kernel(x_ref, g_ref, wq_ref, wr_ref, wzc_ref, wba_ref, cos_ref, sin_ref, cw_ref,
                      qa_ref, ka_ref, va_ref, z_ref, qc_ref, ba_ref, qn_ref, kn_ref, vv_ref, tail_ref,
                      buf_ref, *, tm, tiles_per_seq):
    hist = SUBLANES
    first = lax.rem(pl.program_id(0), tiles_per_seq) == 0

    @pl.when(first)
    def _():
        buf_ref[0:hist, :] = jnp.zeros((hist, B_CONV_CH), F32)

    @pl.when(jnp.logical_not(first))
    def _():
        buf_ref[0:hist, :] = buf_ref[tm:tm + hist, :]

    qkv, steps = _proj_steps(x_ref, g_ref, wq_ref, wr_ref, wzc_ref, wba_ref, cos_ref, sin_ref,
                             qa_ref, ka_ref, va_ref, z_ref, qc_ref, ba_ref)
    nq = B_HEADS * B_DK

    def conv_group(c0):
        cs = slice(c0, c0 + PAIR)
        raw = qkv(c0, c0 + PAIR)
        buf_ref[hist:hist + tm, cs] = raw
        tail_ref[0, :, cs] = raw[tm - hist:, :]
        xb = buf_ref[:, cs]
        acc = xb * cw_ref[0:1, cs]
        for i in range(1, CONV_W):
            acc = pltpu.roll(acc, 1, 0) + xb * cw_ref[i:i + 1, cs]
        act = _silu(acc[hist:])
        if c0 >= 2 * nq:
            vv_ref[:, c0 - 2 * nq:c0 - 2 * nq + PAIR] = act
            return
        out_ref, base, scale = (qn_ref, 0, B_DK ** -0.5) if c0 < nq else (kn_ref, nq, 1.0)
        for s in range(2):
            o0 = c0 - base + s * B_DK
            out_ref[:, o0:o0 + B_DK] = _l2n(act[:, s * B_DK:(s + 1) * B_DK]) * scale

    groups = [functools.partial(conv_group, c0) for c0 in range(0, B_CONV_CH, PAIR)]
    while groups or steps:
        if groups:
            groups.pop(0)()
        if steps:
            steps.pop(0)()


_PROJ_OUTS = [(512, BF16), (128, F32), (128, F32), (512, F32), (512, BF16), (128, F32)]


def _proj(x, gain, ws, cos, sin, tm, cw=None, seq=None):
    wq, wr, wtail, wba = ws
    t = x.shape[0]
    ntab = cos.shape[0] // tm
    row = lambda n: pl.BlockSpec((tm, n), lambda i: (i, 0))
    tab = pl.BlockSpec((tm, LANES), lambda i: (i % ntab, 0))
    in_specs = [row(D_MODEL), _const_spec((1, D_MODEL)), _const_spec(wq.shape), _const_spec(wr.shape),
                _const_spec((D_MODEL, D_MODEL)), _const_spec(wba.shape), tab, tab]
    out_specs = [row(n) for n, _ in _PROJ_OUTS]
    out_shape = [jax.ShapeDtypeStruct((t, n), d) for n, d in _PROJ_OUTS]
    if cw is None:
        return pl.pallas_call(
            _proj_raw_kernel, grid=(t // tm,), in_specs=in_specs,
            out_specs=out_specs + [row(B_CONV_CH)],
            out_shape=out_shape + [jax.ShapeDtypeStruct((t, B_CONV_CH), F32)],
            compiler_params=_cparams(("arbitrary",)), name="proj",
        )(x, gain, wq, wr, wtail, wba, cos, sin)
    tiles = seq // tm
    return pl.pallas_call(
        functools.partial(_proj_conv_kernel, tm=tm, tiles_per_seq=tiles),
        grid=(t // tm,),
        in_specs=in_specs + [_const_spec((CONV_W, B_CONV_CH))],
        out_specs=out_specs + [row(512)] * 3
        + [pl.BlockSpec((1, SUBLANES, B_CONV_CH), lambda i: (i // tiles, 0, 0))],
        out_shape=out_shape + [jax.ShapeDtypeStruct((t, 512), F32)] * 3
        + [jax.ShapeDtypeStruct((t // seq, SUBLANES, B_CONV_CH), F32)],
        scratch_shapes=[pltpu.VMEM((tm + SUBLANES, B_CONV_CH), F32)],
        compiler_params=_cparams(("arbitrary",)), name="proj_conv",
    )(x, gain, wq, wr, wtail, wba, cos, sin, cw)


def _swa_scores(q, k16):
    tq = q.shape[0]
    lo = lax.broadcasted_iota(jnp.int32, (tq, LANES), 1) < A_HD
    blocks = []
    for j in range(4):
        c = q[:, j * LANES:(j + 1) * LANES].astype(F32)
        blocks.append(jnp.where(lo, c, 0.0))
        blocks.append(jnp.where(lo, 0.0, c))
    lhs = jnp.concatenate(blocks, axis=0).astype(BF16)
    return lax.dot_general(lhs, k16, (((1,), (1,)), ((), ())), preferred_element_type=F32)


def _swa_finish(logits, v16, mask, sink_ref):
    tq = logits.shape[0] // A_HEADS
    lo = lax.broadcasted_iota(jnp.int32, (tq, LANES), 1) < A_HD
    es, inv = [], []
    for s in range(8):
        l = jnp.where(mask, logits[s * tq:(s + 1) * tq], NEG)
        sk = sink_ref[s]
        m = jnp.maximum(jnp.max(l, axis=-1, keepdims=True), sk)
        e = jnp.exp(l - m)
        den = jnp.sum(e, axis=-1, keepdims=True) + jnp.exp(sk - m)
        es.append(e.astype(BF16))
        inv.append(1.0 / den)
    pv = jnp.dot(jnp.concatenate(es, axis=0), v16, preferred_element_type=F32)
    outs = []
    for j in range(4):
        a = pv[(2 * j) * tq:(2 * j + 1) * tq] * inv[2 * j]
        b = pv[(2 * j + 1) * tq:(2 * j + 2) * tq] * inv[2 * j + 1]
        outs.append(jnp.where(lo, a, b))
    return jnp.concatenate(outs, axis=1)


def _swa_prompt_kernel(sink_ref, q_ref, kc_ref, kp_ref, vc_ref, vp_ref, o_ref, *, nblk):
    i = pl.program_id(1)
    kcat = jnp.concatenate([kp_ref[...], kc_ref[...]], axis=0).astype(BF16)
    vcat = jnp.concatenate([vp_ref[...], vc_ref[...]], axis=0).astype(BF16)
    ii = lax.broadcasted_iota(jnp.int32, (WINDOW, 2 * WINDOW), 0)
    jj = lax.broadcasted_iota(jnp.int32, (WINDOW, 2 * WINDOW), 1)
    band = (jj > ii) & (jj <= ii + WINDOW)
    for jb in range(nblk):
        mask = band
        if jb == 0:
            mask = band & ((jj >= WINDOW) | (i > 0))
        logits = _swa_scores(q_ref[jb * WINDOW:(jb + 1) * WINDOW, :], kcat[jb * WINDOW:(jb + 2) * WINDOW])
        o = _swa_finish(logits, vcat[jb * WINDOW:(jb + 2) * WINDOW], mask, sink_ref)
        o_ref[jb * WINDOW:(jb + 1) * WINDOW, :] = o.astype(BF16)


def _swa_prompt(sink, qa, ka, va, batch, seq, tq):
    nq = seq // tq
    nblk = tq // WINDOW
    nw = seq // WINDOW
    cur = lambda n: pl.BlockSpec((tq, n), lambda b, i: (b * nq + i, 0))
    prev = pl.BlockSpec((WINDOW, LANES), lambda b, i: (jnp.maximum(b * nw + i * nblk - 1, 0), 0))
    return pl.pallas_call(
        functools.partial(_swa_prompt_kernel, nblk=nblk),
        grid=(batch, nq),
        in_specs=[pl.BlockSpec(memory_space=pltpu.SMEM), cur(512), cur(LANES), prev, cur(LANES), prev],
        out_specs=cur(512),
        out_shape=jax.ShapeDtypeStruct(qa.shape, BF16),
        compiler_params=_cparams(("arbitrary", "arbitrary")),
        name="swa_prompt",
    )(sink, qa, ka, ka, va, va)


ROW_UNROLL = 8


def _rows_loop(rows, load, stages, store):
    def body(g, carry):
        idx = [g * ROW_UNROLL + j for j in range(ROW_UNROLL)]
        vals = [load(r) for r in idx]
        for stage in stages:
            vals = [stage(v) for v in vals]
        for r, o in zip(idx, vals):
            store(r, o)
        return carry

    lax.fori_loop(0, rows // ROW_UNROLL, body, 0)


def _swa_decode_kernel(sink_ref, q_ref, kn_ref, vn_ref, wk_ref, wv_ref, o_ref, wko_ref, wvo_ref,
                       *, rows, new_len):
    ii = lax.broadcasted_iota(jnp.int32, (DEC_PAD, 2 * WINDOW), 0)
    jj = lax.broadcasted_iota(jnp.int32, (DEC_PAD, 2 * WINDOW), 1)
    mask = ((jj < WINDOW) & (jj > ii)) | ((jj >= WINDOW) & (jj - WINDOW <= ii))
    pad = jnp.zeros((WINDOW - DEC_PAD, LANES), F32)
    keep = lax.broadcasted_iota(jnp.int32, (WINDOW, LANES), 0) < WINDOW - new_len

    def slide(win, new):
        tail = jnp.concatenate([pad, new], axis=0)
        return jnp.where(keep, pltpu.roll(win, WINDOW - new_len, 0), pltpu.roll(tail, DEC_PAD - new_len, 0))

    def load(r):
        return q_ref[r], wk_ref[r], kn_ref[r], wv_ref[r], vn_ref[r]

    def scores(args):
        q, wk, kn, wv, vn = args
        kk = jnp.concatenate([wk, kn, pad], axis=0).astype(BF16)
        return _swa_scores(q, kk), wk, kn, wv, vn

    def finish(args):
        logits, wk, kn, wv, vn = args
        vv = jnp.concatenate([wv, vn, pad], axis=0).astype(BF16)
        return _swa_finish(logits, vv, mask, sink_ref).astype(BF16), slide(wk, kn), slide(wv, vn)

    def store(r, outs):
        o_ref[r], wko_ref[r], wvo_ref[r] = outs

    _rows_loop(rows, load, [scores, finish], store)


def _swa_decode(sink, qa, kn, vn, wk, wv, rows, new_len):
    nb = qa.shape[0]
    blk = lambda a, n: pl.BlockSpec((rows, a, n), lambda i: (i, 0, 0))
    return pl.pallas_call(
        functools.partial(_swa_decode_kernel, rows=rows, new_len=new_len),
        grid=(nb // rows,),
        in_specs=[pl.BlockSpec(memory_space=pltpu.SMEM), blk(DEC_PAD, 512), blk(DEC_PAD, LANES),
                  blk(DEC_PAD, LANES), blk(WINDOW, LANES), blk(WINDOW, LANES)],
        out_specs=[blk(DEC_PAD, 512), blk(WINDOW, LANES), blk(WINDOW, LANES)],
        out_shape=[jax.ShapeDtypeStruct(qa.shape, BF16), jax.ShapeDtypeStruct(wk.shape, F32),
                   jax.ShapeDtypeStruct(wv.shape, F32)],
        compiler_params=_cparams(("arbitrary",)),
        name="swa_decode",
    )(sink, qa, kn, vn, wk, wv)


def _gdn_gates(ba, alog, dtb, valid):
    beta = jax.nn.sigmoid(ba)
    g = -jnp.exp(alog) * _softplus(ba + dtb)
    if valid is not None:
        beta = jnp.where(valid, beta, 0.0)
        g = jnp.where(valid, g, 0.0)
    return beta, g


def _chunk_masks(chunk):
    sh = chunk.bit_length() - 1
    ri = lax.broadcasted_iota(jnp.int32, (LANES, LANES), 0)
    ci = lax.broadcasted_iota(jnp.int32, (LANES, LANES), 1)
    same = (ri >> sh) == (ci >> sh)
    return same, same & (ri >= ci), same & (ri > ci), ri == ci


def _gdn_cumsums(g_all, chunk):
    n = g_all.shape[0]
    pos = lax.broadcasted_iota(jnp.int32, g_all.shape, 0) & (chunk - 1)
    gcol = g_all
    step = 1
    while step < chunk:
        gcol = gcol + jnp.where(pos >= step, pltpu.roll(gcol, step, 0), 0.0)
        step *= 2
    gtot = jnp.where(pos == chunk - 1, gcol, 0.0)
    step = 1
    while step < chunk:
        gtot = gtot + jnp.where(pos + step < chunk, pltpu.roll(gtot, n - step, 0), 0.0)
        step *= 2
    return gcol, gcol.T, gtot


def _sibling_mask(ri, ci, lvl):
    rb = ri >> lvl
    return ((rb & 1) == 1) & ((ci >> lvl) == rb - 1)


def _gdn_phase_a(heads, chunk):
    _, tri, strict, _ = _chunk_masks(chunk)
    ri = lax.broadcasted_iota(jnp.int32, (LANES, LANES), 0)
    ci = lax.broadcasted_iota(jnp.int32, (LANES, LANES), 1)
    kbs = [k * beta for q, k, v, beta, gcol, grow, gtot in heads]
    kkqk = [_bdot_nt(jnp.concatenate([kb, hd[0]], axis=0), hd[1]) for kb, hd in zip(kbs, heads)]
    a, qk = [], []
    for r, (q, k, v, beta, gcol, grow, gtot) in zip(kkqk, heads):
        decay = jnp.exp(jnp.where(tri, gcol - grow, NEG))
        a.append(jnp.where(strict, r[:LANES] * decay, 0.0))
        qk.append(r[LANES:] * decay)
    n = [-jnp.where(_sibling_mask(ri, ci, 0), ah, 0.0) for ah in a]
    for lvl in range(1, chunk.bit_length() - 1):
        sib = _sibling_mask(ri, ci, lvl)
        al = [jnp.where(sib, ah, 0.0) for ah in a]
        x = [alh + _bdot(nh, alh) for alh, nh in zip(al, n)]
        n = [nh - (xh + _bdot(xh, nh)) for xh, nh in zip(x, n)]
    outs = []
    rhs = [jnp.concatenate([hd[2] * hd[3], kb * jnp.exp(hd[4])], axis=1) for kb, hd in zip(kbs, heads)]
    sol = [r + _bdot(nh, r) for r, nh in zip(rhs, n)]
    for s, qkh, (q, k, v, beta, gcol, grow, gtot) in zip(sol, qk, heads):
        outs.append((s[:, :B_DV], s[:, B_DV:], q * jnp.exp(gcol), k * jnp.exp(gtot - gcol), qkh, jnp.exp(gtot)))
    return outs


def _gdn_qkv_heads(qkv, h):
    q = qkv[:, h * B_DK:(h + 1) * B_DK]
    k = qkv[:, (B_HEADS + h) * B_DK:(B_HEADS + h + 1) * B_DK]
    v = qkv[:, (2 * B_HEADS + h) * B_DK:(2 * B_HEADS + h + 1) * B_DK]
    q = q * lax.rsqrt(jnp.sum(q * q, axis=-1, keepdims=True) + EPS) * (B_DK ** -0.5)
    k = k * lax.rsqrt(jnp.sum(k * k, axis=-1, keepdims=True) + EPS)
    return q, k, v


def _gdn_out(o, z, nw):
    return _rms(o, nw) * _silu(z)


def _pair_bd(x2):
    lo = lax.broadcasted_iota(jnp.int32, x2.shape, 1) < LANES
    return jnp.concatenate([jnp.where(lo, x2, 0.0), jnp.where(lo, 0.0, x2)], axis=0).astype(BF16)


def _quad_bd(x4):
    blk = lax.broadcasted_iota(jnp.int32, x4.shape, 1) >> (GDN_CHUNK.bit_length() - 1)
    return jnp.concatenate([jnp.where(blk == j, x4, 0.0) for j in range(PAIR // GDN_CHUNK)],
                           axis=0).astype(BF16)


def _pair_cols(x, l0, l1):
    lo = lax.broadcasted_iota(jnp.int32, (x.shape[0], PAIR), 1) < LANES
    return jnp.where(lo, x[:, l0:l0 + 1], x[:, l1:l1 + 1])


def _gdn_prompt_kernel(q_ref, k_ref, v_ref, ba_ref, z_ref, alog_ref, dtb_ref, nw_ref,
                       ob_ref, s_ref, *, nb, ntb):
    t = pl.program_id(0)
    npair = B_HEADS // 2

    @pl.when(t == 0)
    def _():
        s_ref[...] = jnp.zeros(s_ref.shape, F32)

    alog = alog_ref[...]
    dtb = dtb_ref[...]
    nw = nw_ref[...]
    nchunk = LANES // GDN_CHUNK
    _, tri, strict, _ = _chunk_masks(GDN_CHUNK)
    tri2 = jnp.concatenate([tri, tri], axis=1)
    strict2 = jnp.concatenate([strict, strict], axis=1)

    items = []
    blocks = [(b, tb) for b in range(nb) for tb in range(ntb)]
    assert len(blocks) * GATE_LANES <= LANES
    packed = None
    for j, (b, tb) in enumerate(blocks):
        blk = ba_ref[b, tb * LANES:(tb + 1) * LANES, :]
        packed = blk if j == 0 else packed + pltpu.roll(blk, GATE_LANES * j, 1)
    beta_all, g_all = _gdn_gates(packed, alog, dtb, None)
    gcol_all, grow_all, gtot_all = _gdn_cumsums(g_all, GDN_CHUNK)
    eg_all = jnp.exp(gcol_all)
    ed_all = jnp.exp(gtot_all - gcol_all)
    et_all = jnp.exp(gtot_all)
    for j, (b, tb) in enumerate(blocks):
        rsl = slice(tb * LANES, (tb + 1) * LANES)
        for p in range(npair):
            h0, h1 = GATE_LANES * j + 2 * p, GATE_LANES * j + 2 * p + 1
            g0, g1 = B_HEADS + h0, B_HEADS + h1
            psl = slice(p * PAIR, (p + 1) * PAIR)
            q2 = q_ref[b, rsl, psl]
            k2 = k_ref[b, rsl, psl]
            v2 = v_ref[b, rsl, psl]
            beta2 = _pair_cols(beta_all, h0, h1)
            grow2 = jnp.concatenate([grow_all[g0:g0 + 1, :], grow_all[g1:g1 + 1, :]], axis=1)
            decay2 = jnp.exp(jnp.where(tri2, _pair_cols(gcol_all, g0, g1) - grow2, NEG))
            kb2 = k2 * beta2
            items.append(dict(
                b=b, tb=tb, p=p, rsl=rsl, decay2=decay2, kb2=kb2, v2b=v2 * beta2,
                kbe2=kb2 * _pair_cols(eg_all, g0, g1),
                lhs=jnp.concatenate([kb2, q2], axis=0).astype(BF16),
                kbd=_pair_bd(k2),
                qe2=(q2 * _pair_cols(eg_all, g0, g1)).astype(BF16),
                kd2=(k2 * _pair_cols(ed_all, g0, g1)).astype(BF16),
                et2=_pair_cols(et_all, g0, g1)))

    for it in items:
        kkqk = lax.dot_general(it["lhs"], it["kbd"], (((1,), (1,)), ((), ())),
                               preferred_element_type=F32)
        it["a2"] = jnp.where(strict2, kkqk[:LANES] * it["decay2"], 0.0)
        it["qk2"] = (kkqk[LANES:] * it["decay2"]).astype(BF16)
    qlane = lax.broadcasted_iota(jnp.int32, (GDN_CHUNK, PAIR), 1)
    qrow = lax.broadcasted_iota(jnp.int32, (GDN_CHUNK, PAIR), 0)
    qcol = qlane & (GDN_CHUNK - 1)
    top = (qlane & GDN_CHUNK) == 0
    for it in items:
        it["a64"] = it["a2"][:GDN_CHUNK] + it["a2"][GDN_CHUNK:]
        it["n"] = -jnp.where(_sibling_mask(qrow, qcol, 0), it["a64"], 0.0)
    for lvl in range(1, GDN_CHUNK.bit_length() - 1):
        sib = _sibling_mask(qrow, qcol, lvl)
        for it in items:
            it["al"] = jnp.where(sib, it["a64"], 0.0)
            it["x"] = it["al"] + jnp.dot(it["n"].astype(BF16), _quad_bd(it["al"]), preferred_element_type=F32)
        for it in items:
            x = it["x"]
            it["n"] = it["n"] - (x + jnp.dot(x.astype(BF16), _quad_bd(it["n"]), preferred_element_type=F32))
    for it in items:
        n64 = it["n"]
        it["n"] = jnp.concatenate([jnp.where(top, n64, 0.0), jnp.where(top, 0.0, n64)], axis=0)
    for it in items:
        us, ws = [], []
        for s in range(2):
            sl = slice(s * LANES, (s + 1) * LANES)
            rhs = jnp.concatenate([it["v2b"][:, sl], it["kbe2"][:, sl]], axis=1)
            sol = rhs + _bdot(it["n"][:, sl], rhs)
            us.append(sol[:, :B_DV])
            ws.append(sol[:, B_DV:])
        it["u2"] = jnp.concatenate(us, axis=1)
        it["w2"] = jnp.concatenate(ws, axis=1).astype(BF16)

    nchain = nb * npair
    state = [[s_ref[ch // npair, 2 * (ch % npair) + s] for s in range(2)] for ch in range(nchain)]
    zeros16 = jnp.zeros((LANES, LANES), BF16)
    outs = {}
    for tb in range(ntb):
        cur = [(it["b"] * npair + it["p"], i, it) for i, it in enumerate(items) if it["tb"] == tb]
        for c in range(nchunk):
            rows = slice(c * GDN_CHUNK, (c + 1) * GDN_CHUNK)
            rs = []
            for ch, _, it in cur:
                sa, sb = (s.astype(BF16) for s in state[ch])
                sbd = jnp.concatenate([jnp.concatenate([sa, zeros16], axis=1),
                                       jnp.concatenate([zeros16, sb], axis=1)], axis=0)
                rs.append(jnp.dot(jnp.concatenate([it["w2"][rows], it["qe2"][rows]], axis=0), sbd,
                                  preferred_element_type=F32))
            for (ch, i, it), r in zip(cur, rs):
                vn2 = it["u2"][rows] - r[:GDN_CHUNK]
                vnb = vn2.astype(BF16)
                vt = _pair_bd(jnp.concatenate([vn2] * nchunk, axis=0))
                outs[(i, c)] = r[GDN_CHUNK:] + jnp.dot(it["qk2"][rows], vt, preferred_element_type=F32)
                for s in range(2):
                    hl = slice(s * LANES, (s + 1) * LANES)
                    upd = lax.dot_general(it["kd2"][rows, hl], vnb[:, hl], (((0,), (0,)), ((), ())),
                                          preferred_element_type=F32)
                    state[ch][s] = state[ch][s] * it["et2"][c * GDN_CHUNK:c * GDN_CHUNK + 1, hl] + upd
    for ch in range(nchain):
        for s in range(2):
            s_ref[ch // npair, 2 * (ch % npair) + s] = state[ch][s]

    for i, it in enumerate(items):
        o2 = jnp.concatenate([outs[(i, c)] for c in range(nchunk)], axis=0)
        for s in range(2):
            h = 2 * it["p"] + s
            sl = slice(h * B_DV, (h + 1) * B_DV)
            o = o2[:, s * LANES:(s + 1) * LANES]
            ob_ref[it["b"], it["rsl"], sl] = _gdn_out(o, z_ref[it["b"], it["rsl"], sl], nw).astype(BF16)


GDN_TOKEN_BLOCKS = 2


def _gdn_prompt(q, k, v, ba, z, alog, dtb, nw):
    batch, seq, _ = q.shape
    tm = GDN_TOKEN_BLOCKS * LANES
    tok = lambda n: pl.BlockSpec((batch, tm, n), lambda t: (0, t, 0))
    state = pl.BlockSpec((batch, B_HEADS, B_DK, B_DV), lambda t: (0, 0, 0, 0))
    return pl.pallas_call(
        functools.partial(_gdn_prompt_kernel, nb=batch, ntb=GDN_TOKEN_BLOCKS),
        grid=(seq // tm,),
        in_specs=[tok(512), tok(512), tok(512), tok(LANES), tok(512),
                  _const_spec((1, LANES)), _const_spec((1, LANES)), _const_spec((1, B_DV))],
        out_specs=[tok(512), state],
        out_shape=[jax.ShapeDtypeStruct((batch, seq, 512), BF16),
                   jax.ShapeDtypeStruct((batch, B_HEADS, B_DK, B_DV), F32)],
        compiler_params=_cparams(("arbitrary",)),
        name="gdn_prompt",
    )(q, k, v, ba, z, alog, dtb, nw)


def _gdn_decode_kernel(raw_ref, hist_ref, ba_ref, z_ref, rec_ref, cw_ref, alog_ref, dtb_ref, nw_ref,
                       ob_ref, s_ref, buf_ref, u_s, w_s, qe_s, kd_s, qk_s, eg_s, o_s, *, rows, valid_len):
    buf_ref[:, 0:DEC_PAD, :] = hist_ref[...]
    buf_ref[:, DEC_PAD:2 * DEC_PAD, :] = raw_ref[...]
    conv = None
    for i in range(CONV_W):
        off = DEC_PAD - (CONV_W - 1) + i
        term = buf_ref[:, off:off + DEC_PAD, :] * cw_ref[i:i + 1, :]
        conv = term if conv is None else conv + term
    qkv = _silu(conv.reshape(rows * DEC_PAD, B_CONV_CH))
    tok = lax.broadcasted_iota(jnp.int32, (LANES, LANES), 0) & (DEC_PAD - 1)
    beta_all, g_all = _gdn_gates(ba_ref[...], alog_ref[...], dtb_ref[...], tok < valid_len)
    gcol_all, grow_all, gtot_all = _gdn_cumsums(g_all, DEC_PAD)
    heads = []
    for h in range(B_HEADS):
        gl = B_HEADS + h
        heads.append(_gdn_qkv_heads(qkv, h) + (beta_all[:, h:h + 1], gcol_all[:, gl:gl + 1],
                                               grow_all[gl:gl + 1, :], gtot_all[:, gl:gl + 1]))
    for h, (u, w, qe, kd, qk, egt) in enumerate(_gdn_phase_a(heads, DEC_PAD)):
        u_s[h] = u
        w_s[h] = w
        qe_s[h] = qe
        kd_s[h] = kd
        qk_s[h] = qk
        eg_s[h] = jnp.broadcast_to(egt, (LANES, LANES))

    def load(r):
        r0 = pl.multiple_of(r * DEC_PAD, DEC_PAD)
        rr = pl.ds(r0, DEC_PAD)
        return [(rec_ref[r, h], w_s[h, rr, :], qe_s[h, rr, :], u_s[h, rr, :], qk_s[h, rr, :],
                 kd_s[h, rr, :], eg_s[h, pl.ds(r0, 1), :]) for h in range(B_HEADS)]

    def read_state(heads):
        return [(_bdot(jnp.concatenate([w, qe], axis=0), s), s, u, qk, kd, eg)
                for s, w, qe, u, qk, kd, eg in heads]

    def update(heads):
        outs = []
        for res, s, u, qk, kd, eg in heads:
            v_new = u - res[:DEC_PAD]
            vt = jnp.concatenate([v_new] * (LANES // DEC_PAD), axis=0)
            outs.append((res[DEC_PAD:] + _bdot(qk, vt), s * eg + _bdot_tn(kd, v_new)))
        return outs

    def store(r, outs):
        rr = pl.ds(pl.multiple_of(r * DEC_PAD, DEC_PAD), DEC_PAD)
        for h, (o, s_new) in enumerate(outs):
            o_s[h, rr, :] = o
            s_ref[r, h] = s_new

    _rows_loop(rows, load, [read_state, update], store)
    nw = nw_ref[...]
    for h in range(B_HEADS):
        sl = slice(h * B_DV, (h + 1) * B_DV)
        ob_ref[:, sl] = _gdn_out(o_s[h], z_ref[:, sl], nw).astype(BF16)


def _gdn_decode(raw, histp, ba, z, rec, cw, alog, dtb, nw, rows, valid_len):
    nb = raw.shape[0]
    flat = rows * DEC_PAD
    assert flat == LANES
    sq = lambda: pltpu.VMEM((B_HEADS, LANES, LANES), F32)
    return pl.pallas_call(
        functools.partial(_gdn_decode_kernel, rows=rows, valid_len=valid_len),
        grid=(nb // rows,),
        in_specs=[pl.BlockSpec((rows, DEC_PAD, B_CONV_CH), lambda i: (i, 0, 0)),
                  pl.BlockSpec((rows, DEC_PAD, B_CONV_CH), lambda i: (i, 0, 0)),
                  pl.BlockSpec((flat, LANES), lambda i: (i, 0)),
                  pl.BlockSpec((flat, 512), lambda i: (i, 0)),
                  pl.BlockSpec((rows, B_HEADS, B_DK, B_DV), lambda i: (i, 0, 0, 0)),
                  _const_spec((CONV_W, B_CONV_CH)), _const_spec((1, LANES)), _const_spec((1, LANES)),
                  _const_spec((1, B_DV))],
        out_specs=[pl.BlockSpec((flat, 512), lambda i: (i, 0)),
                   pl.BlockSpec((rows, B_HEADS, B_DK, B_DV), lambda i: (i, 0, 0, 0))],
        out_shape=[jax.ShapeDtypeStruct((nb * DEC_PAD, 512), BF16),
                   jax.ShapeDtypeStruct(rec.shape, F32)],
        scratch_shapes=[pltpu.VMEM((rows, 2 * DEC_PAD, B_CONV_CH), F32)] + [sq() for _ in range(7)],
        compiler_params=_cparams(("arbitrary",)),
        name="gdn_decode",
    )(raw, histp, ba, z, rec, cw, alog, dtb, nw)


def _memkv_kernel(m_ref, g_ref, w_ref, k_ref, v_ref):
    h = _rms(m_ref[...], g_ref[...]).astype(BF16)
    n = C_HEADS * C_HD
    k_ref[...] = jnp.dot(h, w_ref[:, :n], preferred_element_type=F32)
    v_ref[...] = jnp.dot(h, w_ref[:, n:], preferred_element_type=F32)


def _memkv(mem, gain, w):
    t = mem.shape[0]
    tm = 512
    n = C_HEADS * C_HD
    return pl.pallas_call(
        _memkv_kernel,
        grid=(t // tm,),
        in_specs=[pl.BlockSpec((tm, D_MODEL), lambda i: (i, 0)), _const_spec((1, D_MODEL)),
                  _const_spec((D_MODEL, 2 * n))],
        out_specs=[pl.BlockSpec((tm, n), lambda i: (i, 0))] * 2,
        out_shape=[jax.ShapeDtypeStruct((t, n), F32)] * 2,
        compiler_params=_cparams(("arbitrary",)),
        name="memkv",
    )(mem, gain, w)


def _softmax_rows(logits):
    m = jnp.max(logits, axis=-1, keepdims=True)
    e = jnp.exp(logits - m)
    return e, 1.0 / jnp.sum(e, axis=-1, keepdims=True)


def _memattn_prompt_kernel(q_ref, k_ref, v_ref, o_ref):
    hs = lambda h: slice(h * C_HD, (h + 1) * C_HD)
    scores = lambda h: _bdot_nt(q_ref[:, hs(h)], k_ref[:, hs(h)])
    logits = scores(0)
    for h in range(C_HEADS):
        nxt = scores(h + 1) if h + 1 < C_HEADS else None
        e, inv = _softmax_rows(logits * (C_HD ** -0.5))
        o_ref[:, hs(h)] = (_bdot(e, v_ref[:, hs(h)]) * inv).astype(BF16)
        logits = nxt


def _memattn_prompt(qc, mk, mv, batch, seq, tm):
    nq = seq // tm
    n = C_HEADS * C_HD
    cur = pl.BlockSpec((tm, n), lambda b, i: (b * nq + i, 0))
    mem = pl.BlockSpec((N_MEM, n), lambda b, i: (b, 0))
    return pl.pallas_call(
        _memattn_prompt_kernel,
        grid=(batch, nq),
        in_specs=[cur, mem, mem],
        out_specs=cur,
        out_shape=jax.ShapeDtypeStruct(qc.shape, BF16),
        compiler_params=_cparams(("arbitrary", "arbitrary")),
        name="memattn_prompt",
    )(qc, mk, mv)


def _memattn_decode_kernel(q_ref, k_ref, v_ref, o_ref, *, rows):
    nk = N_MEM * C_HEADS
    col = lax.broadcasted_iota(jnp.int32, (C_HEADS * DEC_PAD, nk), 1)
    row = lax.broadcasted_iota(jnp.int32, (C_HEADS * DEC_PAD, nk), 0)
    own = (col & (C_HEADS - 1)) == (row >> (DEC_PAD.bit_length() - 1))

    def load(r):
        return q_ref[r], k_ref[r], v_ref[r]

    def scores(args):
        q, k, v = args
        q = q.astype(F32)
        lhs = jnp.concatenate([q[:, h * C_HD:(h + 1) * C_HD] for h in range(C_HEADS)], axis=0)
        return _bdot_nt(lhs, k), v

    def finish(args):
        logits, v = args
        e, inv = _softmax_rows(jnp.where(own, logits * (C_HD ** -0.5), NEG))
        pv = _bdot(e, v) * inv
        return jnp.concatenate([pv[h * DEC_PAD:(h + 1) * DEC_PAD] for h in range(C_HEADS)],
                               axis=1).astype(BF16)

    def store(r, o):
        o_ref[r] = o

    _rows_loop(rows, load, [scores, finish], store)


def _memattn_decode(qc, ck, cv, rows):
    nb = qc.shape[0]
    n = C_HEADS * C_HD
    blk = pl.BlockSpec((rows, DEC_PAD, n), lambda i: (i, 0, 0))
    cache = pl.BlockSpec((rows, N_MEM * C_HEADS, C_HD), lambda i: (i, 0, 0))
    return pl.pallas_call(
        functools.partial(_memattn_decode_kernel, rows=rows),
        grid=(nb // rows,),
        in_specs=[blk, cache, cache],
        out_specs=blk,
        out_shape=jax.ShapeDtypeStruct(qc.shape, BF16),
        compiler_params=_cparams(("arbitrary",)),
        name="memattn_decode",
    )(qc, ck, cv)


def _merge_kernel(x_ref, oa_ref, ob_ref, oc_ref, gpre_ref, wga_ref, wgb_ref, wgc_ref, wb_ref, wo_ref,
                  gpost_ref, gfpre_ref, x1_ref, h2_ref):
    x = x_ref[...]
    ups = [jnp.dot(o_ref[...], wb_ref[n], preferred_element_type=F32)
           for n, o_ref in enumerate((oa_ref, ob_ref, oc_ref))]
    h = _rms(x, gpre_ref[...]).astype(BF16)
    mix = None
    for n, wg_ref in enumerate((wga_ref, wgb_ref, wgc_ref)):
        gate = jax.nn.sigmoid(jnp.dot(h, wg_ref[...], preferred_element_type=F32))
        mix = gate * ups[n] if mix is None else mix + gate * ups[n]
    x1 = x + _rms(_bdot(mix, wo_ref[...]), gpost_ref[...])
    x1_ref[...] = x1
    h2_ref[...] = _rms(x1, gfpre_ref[...]).astype(BF16)


FF_SPLIT = 6 * PAIR


def _ffn_kernel(x1_ref, h2_ref, wfi_ref, wfo_ref, gfpost_ref, y_ref):
    h2 = h2_ref[...]
    f = None
    halves = []
    for a, b in ((0, FF_SPLIT), (FF_SPLIT, D_FF)):
        gt = jnp.dot(h2, wfi_ref[:, a:b], preferred_element_type=F32)
        uf = jnp.dot(h2, wfi_ref[:, D_FF + a:D_FF + b], preferred_element_type=F32)
        halves.append((a, b, gt, uf))
    for a, b, gt, uf in halves:
        part = _bdot(_silu(gt) * uf, wfo_ref[a:b, :])
        f = part if f is None else f + part
    y_ref[...] = x1_ref[...] + _rms(f, gfpost_ref[...])


def _merge_weight_specs():
    vec = _const_spec((1, D_MODEL))
    return ([vec] + [pl.BlockSpec((D_MODEL, D_MODEL), functools.partial(lambda n, *_: (0, n), n + 1),
                                  pipeline_mode=pl.Buffered(1)) for n in range(N_BRANCH)]
            + [_const_spec((N_BRANCH, BRANCH_W, D_MODEL)), _const_spec((D_MODEL, D_MODEL)), vec, vec])


def _merge(x, oa, ob, oc, gpre, wtail, wb, wo, gpost, gfpre, tm):
    t = x.shape[0]
    row = lambda n: pl.BlockSpec((tm, n), lambda i: (i, 0))
    return pl.pallas_call(
        _merge_kernel,
        grid=(t // tm,),
        in_specs=[row(D_MODEL), row(512), row(512), row(512)] + _merge_weight_specs(),
        out_specs=[row(D_MODEL), row(D_MODEL)],
        out_shape=[jax.ShapeDtypeStruct(x.shape, F32), jax.ShapeDtypeStruct(x.shape, BF16)],
        compiler_params=_cparams(("arbitrary",)),
        name="merge",
    )(x, oa, ob, oc, gpre, wtail, wtail, wtail, wb, wo, gpost, gfpre)


def _ffn(x1, h2, wfi, wfo, gfpost, tm):
    t = x1.shape[0]
    vec = _const_spec((1, D_MODEL))
    frow = pl.BlockSpec((tm, D_MODEL), lambda i: (i, 0))
    return pl.pallas_call(
        _ffn_kernel,
        grid=(t // tm,),
        in_specs=[frow, frow, _const_spec((D_MODEL, 2 * D_FF)),
                  _const_spec((D_FF, D_MODEL)), vec],
        out_specs=frow,
        out_shape=jax.ShapeDtypeStruct(x1.shape, F32),
        compiler_params=_cparams(("arbitrary",)),
        name="ffn",
    )(x1, h2, wfi, wfo, gfpost)


def _rope_tables(pos):
    half = A_HD // 2
    inv = ROPE_THETA ** (-jnp.arange(half, dtype=F32) / half)
    ang = pos.astype(F32)[:, None] * inv[None, :]
    cos, sin = jnp.cos(ang), jnp.sin(ang)
    cos = jnp.concatenate([cos, cos], axis=-1)
    sin = jnp.concatenate([-sin, sin], axis=-1)
    return jnp.tile(cos, (1, LANES // A_HD)), jnp.tile(sin, (1, LANES // A_HD))


def _lane_row(vals, offset):
    grp = jnp.zeros((GATE_LANES,), F32).at[offset:offset + vals.shape[0]].set(vals.astype(F32))
    return jnp.tile(grp, LANES // GATE_LANES).reshape(1, LANES)


def kernel(x_prompt, x_sample, mem_prompt, state_win_k, state_win_v, state_conv, state_rec,
           cache_mem_k, cache_mem_v, ln_mix_pre, w_in, attn_sink, gdn_conv_w, gdn_a_log,
           gdn_dt_bias, gdn_norm_w, ln_mem, w_mem_kv, w_branch, w_out, ln_mix_post,
           ln_ffn_pre, w_ffn_in, w_ffn_out, ln_ffn_post):
    bp, lp, _ = x_prompt.shape
    bs, ls, _ = x_sample.shape

    sizes = [512, 128, 128, B_CONV_CH, B_HEADS, B_HEADS, 512, 512, N_BRANCH * D_MODEL]
    o = np.cumsum([0] + sizes)
    hperm = np.concatenate([np.r_[j * A_HD:(j + 1) * A_HD, (j + 4) * A_HD:(j + 5) * A_HD] for j in range(4)])
    ws = (w_in[:, hperm].astype(BF16), w_in[:, o[1]:o[4]].astype(BF16), w_in[:, o[6]:o[9]].astype(BF16),
          jnp.pad(w_in[:, o[4]:o[6]], ((0, 0), (0, LANES - 2 * B_HEADS))).astype(BF16))
    wb = jnp.concatenate([w_branch[0:1][:, hperm], w_branch[1:]], axis=0).astype(BF16)
    wo = w_out.astype(BF16)
    wfi = w_ffn_in.astype(BF16)
    wfo = w_ffn_out.astype(BF16)
    wmem = w_mem_kv.astype(BF16)
    sink = attn_sink.astype(F32)[np.array([0, 4, 1, 5, 2, 6, 3, 7])]
    vec = lambda g: g.astype(F32).reshape(1, -1)
    alog = _lane_row(gdn_a_log, B_HEADS)
    dtb = _lane_row(gdn_dt_bias, B_HEADS)
    cw = gdn_conv_w.astype(F32)
    nw = vec(gdn_norm_w)

    merge_w = (vec(ln_mix_pre), ws[2], wb, wo, vec(ln_mix_post), vec(ln_ffn_pre))
    ffn = lambda x1, h2: _ffn(x1, h2, wfi, wfo, vec(ln_ffn_post), min(1024, x1.shape[0]))

    tp = bp * lp
    xp = x_prompt.reshape(tp, D_MODEL)
    cos_p, sin_p = _rope_tables(jnp.arange(lp, dtype=jnp.int32))
    qa, ka, va, z, qc, ba, qn, kn, vv, tail = _proj(xp, vec(ln_mix_pre), ws, cos_p, sin_p, 1024, cw, lp)
    b3 = lambda a: a.reshape(bp, lp, a.shape[-1])
    ob, rec_p = _gdn_prompt(b3(qn), b3(kn), b3(vv), b3(ba), b3(z), alog, dtb, nw)
    mk, mv = _memkv(mem_prompt.reshape(bp * N_MEM, D_MODEL), vec(ln_mem), wmem)
    oa = _swa_prompt(sink, qa, ka, va, bp, lp, 2048)
    oc = _memattn_prompt(qc, mk, mv, bp, lp, 2048)
    x1, h2 = _merge(xp, oa, ob.reshape(tp, 512), oc, *merge_w, 256)
    y_p = ffn(x1, h2).reshape(bp, lp, D_MODEL)
    wk_p = ka.reshape(bp, lp, LANES)[:, -WINDOW:].reshape(bp, WINDOW, A_KV, A_HD)
    wv_p = va.reshape(bp, lp, LANES)[:, -WINDOW:].reshape(bp, WINDOW, A_KV, A_HD)
    conv_p = tail[:, -(CONV_W - 1):]
    mem_k_p = mk.reshape(bp, N_MEM, C_HEADS, C_HD)
    mem_v_p = mv.reshape(bp, N_MEM, C_HEADS, C_HD)

    ts = bs * DEC_PAD
    xs = jnp.pad(x_sample, ((0, 0), (0, DEC_PAD - ls), (0, 0))).reshape(ts, D_MODEL)
    cos_s, sin_s = _rope_tables(PAST_LEN + jnp.arange(DEC_PAD, dtype=jnp.int32))
    cos_s, sin_s = jnp.tile(cos_s, (bs, 1)), jnp.tile(sin_s, (bs, 1))
    qa, ka, va, z, qc, ba, qkv = _proj(xs, vec(ln_mix_pre), ws, cos_s, sin_s, 512)
    r3 = lambda a: a.reshape(bs, DEC_PAD, a.shape[-1])
    oa, wk_s, wv_s = _swa_decode(sink, r3(qa), r3(ka), r3(va), state_win_k.reshape(bs, WINDOW, LANES),
                                 state_win_v.reshape(bs, WINDOW, LANES), 16, ls)
    oa = oa.reshape(ts, 512)
    wk_s = wk_s.reshape(state_win_k.shape)
    wv_s = wv_s.reshape(state_win_v.shape)
    histp = jnp.pad(state_conv, ((0, 0), (DEC_PAD - (CONV_W - 1), 0), (0, 0)))
    ob, rec_s = _gdn_decode(r3(qkv), histp, ba, z, state_rec, cw, alog, dtb, nw,
                            LANES // DEC_PAD, ls)
    oc = _memattn_decode(r3(qc), cache_mem_k.reshape(bs, N_MEM * C_HEADS, C_HD),
                         cache_mem_v.reshape(bs, N_MEM * C_HEADS, C_HD), 8).reshape(ts, 512)
    real = lambda a: a.reshape(bs, DEC_PAD, 512)[:, :ls].reshape(bs * ls, 512)
    x1, h2 = _merge(x_sample.reshape(bs * ls, D_MODEL), real(oa), real(ob), real(oc), *merge_w, 256)
    y_s = ffn(x1, h2).reshape(bs, ls, D_MODEL)
    conv_s = r3(qkv)[:, ls - (CONV_W - 1):ls]

    return (y_p, y_s, wk_p, wv_p, conv_p, rec_p, mem_k_p, mem_v_p, wk_s, wv_s, conv_s, rec_s)
```

```python
import functools

import numpy as np
import jax
import jax.numpy as jnp
from jax import lax
from jax.experimental import pallas as pl
from jax.experimental.pallas import tpu as pltpu

F32 = jnp.float32
BF16 = jnp.bfloat16

D_MODEL = 1024
PAST_LEN = 16384
EPS = 1e-6
ROPE_THETA = 10000.0
N_MEM = 256
WINDOW = 128
A_HD = 64
A_HEADS = 8
A_KV = 2
A_SCALE = A_HD ** -0.5
B_HEADS = 4
B_DK = 128
B_DV = 128
CONV_W = 4
GDN_CHUNK = 64
B_CONV_CH = B_HEADS * (2 * B_DK + B_DV)
C_HEADS = 4
C_HD = 128
N_BRANCH = 3
BRANCH_W = 512
D_FF = 2816

LANES = 128
SUBLANES = 8
PAIR = 2 * LANES
GATE_LANES = 2 * B_HEADS
VMEM_LIMIT = 56 * 1024 * 1024
NEG = -1e30
DEC_PAD = SUBLANES


def _cparams(sem, vmem=VMEM_LIMIT):
    return pltpu.CompilerParams(dimension_semantics=sem, vmem_limit_bytes=vmem)


def _const_spec(shape):
    nd = len(shape)
    return pl.BlockSpec(shape, lambda *_: (0,) * nd, pipeline_mode=pl.Buffered(1))


def _rms(x, g):
    ms = jnp.mean(x * x, axis=-1, keepdims=True)
    return x * lax.rsqrt(ms + EPS) * g


def _bdot(a, b):
    return jnp.dot(a.astype(BF16), b.astype(BF16), preferred_element_type=F32)


def _bdot_nt(a, b):
    return lax.dot_general(a.astype(BF16), b.astype(BF16), (((1,), (1,)), ((), ())),
                           preferred_element_type=F32)


def _bdot_tn(a, b):
    return lax.dot_general(a.astype(BF16), b.astype(BF16), (((0,), (0,)), ((), ())),
                           preferred_element_type=F32)


def _silu(x):
    return x * jax.nn.sigmoid(x)


def _softplus(x):
    return jnp.maximum(x, 0.0) + jnp.log1p(jnp.exp(-jnp.abs(x)))


def _rope128(v, cos, sin):
    lane = lax.broadcasted_iota(jnp.int32, v.shape, 1)
    fwd = pltpu.roll(v, 32, 1)
    bwd = pltpu.roll(v, 96, 1)
    sw = jnp.where((lane & 32) == 0, bwd, fwd)
    return v * cos + sw * sin


def _l2n(x):
    return x * lax.rsqrt(jnp.sum(x * x, axis=-1, keepdims=True) + EPS)

def _proj_steps(x_ref, g_ref, wq_ref, wr_ref, wzc_ref, wba_ref, cos_ref, sin_ref,
                qa_ref, ka_ref, va_ref, z_ref, qc_ref, ba_ref):
    h = _rms(x_ref[...], g_ref[...]).astype(BF16)
    cos = cos_ref[...]
    sin = sin_ref[...]

    def mm(w_ref, a, b):
        return jnp.dot(h, w_ref[:, a:b], preferred_element_type=F32)

    def qa_half(c0):
        q = mm(wq_ref, c0 * LANES, (c0 + 2) * LANES)
        for c in range(2):
            qa_ref[:, (c0 + c) * LANES:(c0 + c + 1) * LANES] = (
                _rope128(q[:, c * LANES:(c + 1) * LANES], cos, sin) * A_SCALE).astype(BF16)

    def kv():
        kv2 = mm(wr_ref, 0, PAIR)
        ka_ref[...] = _rope128(kv2[:, :LANES], cos, sin)
        va_ref[...] = kv2[:, LANES:]

    def z_half(c0):
        z_ref[:, c0:c0 + PAIR] = mm(wzc_ref, c0, c0 + PAIR)

    def qc_half(c0):
        qc_ref[:, c0:c0 + PAIR] = mm(wzc_ref, 512 + c0, 512 + c0 + PAIR).astype(BF16)

    def ba():
        ba_ref[...] = mm(wba_ref, 0, LANES)

    def qkv(a, b):
        return mm(wr_ref, PAIR + a, PAIR + b)

    steps = [lambda: qa_half(0), lambda: qa_half(2), kv, lambda: z_half(0), lambda: z_half(PAIR),
             lambda: qc_half(0), lambda: qc_half(PAIR), ba]
    return qkv, steps


def _proj_raw_kernel(x_ref, g_ref, wq_ref, wr_ref, wzc_ref, wba_ref, cos_ref, sin_ref,
                     qa_ref, ka_ref, va_ref, z_ref, qc_ref, ba_ref, qkv_ref):
    qkv, steps = _proj_steps(x_ref, g_ref, wq_ref, wr_ref, wzc_ref, wba_ref, cos_ref, sin_ref,
                             qa_ref, ka_ref, va_ref, z_ref, qc_ref, ba_ref)
    qkv_ref[...] = qkv(0, B_CONV_CH)
    for step in steps:
        step()


def _proj_conv_kernel(x_ref, g_ref, wq_ref, wr_ref, wzc_ref, wba_ref, cos_ref, sin_ref, cw_ref,
                      qa_ref, ka_ref, va_ref, z_ref, qc_ref, ba_ref, qn_ref, kn_ref, vv_ref, tail_ref,
                      buf_ref, *, tm, tiles_per_seq):
    hist = SUBLANES
    first = lax.rem(pl.program_id(0), tiles_per_seq) == 0

    @pl.when(first)
    def _():
        buf_ref[0:hist, :] = jnp.zeros((hist, B_CONV_CH), F32)

    @pl.when(jnp.logical_not(first))
    def _():
        buf_ref[0:hist, :] = buf_ref[tm:tm + hist, :]

    qkv, steps = _proj_steps(x_ref, g_ref, wq_ref, wr_ref, wzc_ref, wba_ref, cos_ref, sin_ref,
                             qa_ref, ka_ref, va_ref, z_ref, qc_ref, ba_ref)
    nq = B_HEADS * B_DK

    def conv_group(c0):
        cs = slice(c0, c0 + PAIR)
        raw = qkv(c0, c0 + PAIR)
        buf_ref[hist:hist + tm, cs] = raw
        tail_ref[0, :, cs] = raw[tm - hist:, :]
        xb = buf_ref[:, cs]
        acc = xb * cw_ref[0:1, cs]
        for i in range(1, CONV_W):
            acc = pltpu.roll(acc, 1, 0) + xb * cw_ref[i:i + 1, cs]
        act = _silu(acc[hist:])
        if c0 >= 2 * nq:
            vv_ref[:, c0 - 2 * nq:c0 - 2 * nq + PAIR] = act
            return
        out_ref, base, scale = (qn_ref, 0, B_DK ** -0.5) if c0 < nq else (kn_ref, nq, 1.0)
        for s in range(2):
            o0 = c0 - base + s * B_DK
            out_ref[:, o0:o0 + B_DK] = _l2n(act[:, s * B_DK:(s + 1) * B_DK]) * scale

    groups = [functools.partial(conv_group, c0) for c0 in range(0, B_CONV_CH, PAIR)]
    while groups or steps:
        if groups:
            groups.pop(0)()
        if steps:
            steps.pop(0)()


_PROJ_OUTS = [(512, BF16), (128, F32), (128, F32), (512, F32), (512, BF16), (128, F32)]


def _proj(x, gain, ws, cos, sin, tm, cw=None, seq=None):
    wq, wr, wtail, wba = ws
    t = x.shape[0]
    ntab = cos.shape[0] // tm
    row = lambda n: pl.BlockSpec((tm, n), lambda i: (i, 0))
    tab = pl.BlockSpec((tm, LANES), lambda i: (i % ntab, 0))
    in_specs = [row(D_MODEL), _const_spec((1, D_MODEL)), _const_spec(wq.shape), _const_spec(wr.shape),
                _const_spec((D_MODEL, D_MODEL)), _const_spec(wba.shape), tab, tab]
    out_specs = [row(n) for n, _ in _PROJ_OUTS]
    out_shape = [jax.ShapeDtypeStruct((t, n), d) for n, d in _PROJ_OUTS]
    if cw is None:
        return pl.pallas_call(
            _proj_raw_kernel, grid=(t // tm,), in_specs=in_specs,
            out_specs=out_specs + [row(B_CONV_CH)],
            out_shape=out_shape + [jax.ShapeDtypeStruct((t, B_CONV_CH), F32)],
            compiler_params=_cparams(("arbitrary",)), name="proj",
        )(x, gain, wq, wr, wtail, wba, cos, sin)
    tiles = seq // tm
    return pl.pallas_call(
        functools.partial(_proj_conv_kernel, tm=tm, tiles_per_seq=tiles),
        grid=(t // tm,),
        in_specs=in_specs + [_const_spec((CONV_W, B_CONV_CH))],
        out_specs=out_specs + [row(512)] * 3
        + [pl.BlockSpec((1, SUBLANES, B_CONV_CH), lambda i: (i // tiles, 0, 0))],
        out_shape=out_shape + [jax.ShapeDtypeStruct((t, 512), F32)] * 3
        + [jax.ShapeDtypeStruct((t // seq, SUBLANES, B_CONV_CH), F32)],
        scratch_shapes=[pltpu.VMEM((tm + SUBLANES, B_CONV_CH), F32)],
        compiler_params=_cparams(("arbitrary",)), name="proj_conv",
    )(x, gain, wq, wr, wtail, wba, cos, sin, cw)


def _swa_scores(q, k16):
    tq = q.shape[0]
    lo = lax.broadcasted_iota(jnp.int32, (tq, LANES), 1) < A_HD
    blocks = []
    for j in range(4):
        c = q[:, j * LANES:(j + 1) * LANES].astype(F32)
        blocks.append(jnp.where(lo, c, 0.0))
        blocks.append(jnp.where(lo, 0.0, c))
    lhs = jnp.concatenate(blocks, axis=0).astype(BF16)
    return lax.dot_general(lhs, k16, (((1,), (1,)), ((), ())), preferred_element_type=F32)


def _swa_finish(logits, v16, mask, sink_ref):
    tq = logits.shape[0] // A_HEADS
    lo = lax.broadcasted_iota(jnp.int32, (tq, LANES), 1) < A_HD
    es, inv = [], []
    for s in range(8):
        l = jnp.where(mask, logits[s * tq:(s + 1) * tq], NEG)
        sk = sink_ref[s]
        m = jnp.maximum(jnp.max(l, axis=-1, keepdims=True), sk)
        e = jnp.exp(l - m)
        den = jnp.sum(e, axis=-1, keepdims=True) + jnp.exp(sk - m)
        es.append(e.astype(BF16))
        inv.append(1.0 / den)
    pv = jnp.dot(jnp.concatenate(es, axis=0), v16, preferred_element_type=F32)
    outs = []
    for j in range(4):
        a = pv[(2 * j) * tq:(2 * j + 1) * tq] * inv[2 * j]
        b = pv[(2 * j + 1) * tq:(2 * j + 2) * tq] * inv[2 * j + 1]
        outs.append(jnp.where(lo, a, b))
    return jnp.concatenate(outs, axis=1)


def _swa_prompt_kernel(sink_ref, q_ref, kc_ref, kp_ref, vc_ref, vp_ref, o_ref, *, nblk):
    i = pl.program_id(1)
    kcat = jnp.concatenate([kp_ref[...], kc_ref[...]], axis=0).astype(BF16)
    vcat = jnp.concatenate([vp_ref[...], vc_ref[...]], axis=0).astype(BF16)
    ii = lax.broadcasted_iota(jnp.int32, (WINDOW, 2 * WINDOW), 0)
    jj = lax.broadcasted_iota(jnp.int32, (WINDOW, 2 * WINDOW), 1)
    band = (jj > ii) & (jj <= ii + WINDOW)
    for jb in range(nblk):
        mask = band
        if jb == 0:
            mask = band & ((jj >= WINDOW) | (i > 0))
        logits = _swa_scores(q_ref[jb * WINDOW:(jb + 1) * WINDOW, :], kcat[jb * WINDOW:(jb + 2) * WINDOW])
        o = _swa_finish(logits, vcat[jb * WINDOW:(jb + 2) * WINDOW], mask, sink_ref)
        o_ref[jb * WINDOW:(jb + 1) * WINDOW, :] = o.astype(BF16)


def _swa_prompt(sink, qa, ka, va, batch, seq, tq):
    nq = seq // tq
    nblk = tq // WINDOW
    nw = seq // WINDOW
    cur = lambda n: pl.BlockSpec((tq, n), lambda b, i: (b * nq + i, 0))
    prev = pl.BlockSpec((WINDOW, LANES), lambda b, i: (jnp.maximum(b * nw + i * nblk - 1, 0), 0))
    return pl.pallas_call(
        functools.partial(_swa_prompt_kernel, nblk=nblk),
        grid=(batch, nq),
        in_specs=[pl.BlockSpec(memory_space=pltpu.SMEM), cur(512), cur(LANES), prev, cur(LANES), prev],
        out_specs=cur(512),
        out_shape=jax.ShapeDtypeStruct(qa.shape, BF16),
        compiler_params=_cparams(("arbitrary", "arbitrary")),
        name="swa_prompt",
    )(sink, qa, ka, ka, va, va)


ROW_UNROLL = 8


def _rows_loop(rows, load, stages, store):
    def body(g, carry):
        idx = [g * ROW_UNROLL + j for j in range(ROW_UNROLL)]
        vals = [load(r) for r in idx]
        for stage in stages:
            vals = [stage(v) for v in vals]
        for r, o in zip(idx, vals):
            store(r, o)
        return carry

    lax.fori_loop(0, rows // ROW_UNROLL, body, 0)


def _swa_decode_kernel(sink_ref, q_ref, kn_ref, vn_ref, wk_ref, wv_ref, o_ref, wko_ref, wvo_ref,
                       *, rows, new_len):
    ii = lax.broadcasted_iota(jnp.int32, (DEC_PAD, 2 * WINDOW), 0)
    jj = lax.broadcasted_iota(jnp.int32, (DEC_PAD, 2 * WINDOW), 1)
    mask = ((jj < WINDOW) & (jj > ii)) | ((jj >= WINDOW) & (jj - WINDOW <= ii))
    pad = jnp.zeros((WINDOW - DEC_PAD, LANES), F32)
    keep = lax.broadcasted_iota(jnp.int32, (WINDOW, LANES), 0) < WINDOW - new_len

    def slide(win, new):
        tail = jnp.concatenate([pad, new], axis=0)
        return jnp.where(keep, pltpu.roll(win, WINDOW - new_len, 0), pltpu.roll(tail, DEC_PAD - new_len, 0))

    def load(r):
        return q_ref[r], wk_ref[r], kn_ref[r], wv_ref[r], vn_ref[r]

    def scores(args):
        q, wk, kn, wv, vn = args
        kk = jnp.concatenate([wk, kn, pad], axis=0).astype(BF16)
        return _swa_scores(q, kk), wk, kn, wv, vn

    def finish(args):
        logits, wk, kn, wv, vn = args
        vv = jnp.concatenate([wv, vn, pad], axis=0).astype(BF16)
        return _swa_finish(logits, vv, mask, sink_ref).astype(BF16), slide(wk, kn), slide(wv, vn)

    def store(r, outs):
        o_ref[r], wko_ref[r], wvo_ref[r] = outs

    _rows_loop(rows, load, [scores, finish], store)


def _swa_decode(sink, qa, kn, vn, wk, wv, rows, new_len):
    nb = qa.shape[0]
    blk = lambda a, n: pl.BlockSpec((rows, a, n), lambda i: (i, 0, 0))
    return pl.pallas_call(
        functools.partial(_swa_decode_kernel, rows=rows, new_len=new_len),
        grid=(nb // rows,),
        in_specs=[pl.BlockSpec(memory_space=pltpu.SMEM), blk(DEC_PAD, 512), blk(DEC_PAD, LANES),
                  blk(DEC_PAD, LANES), blk(WINDOW, LANES), blk(WINDOW, LANES)],
        out_specs=[blk(DEC_PAD, 512), blk(WINDOW, LANES), blk(WINDOW, LANES)],
        out_shape=[jax.ShapeDtypeStruct(qa.shape, BF16), jax.ShapeDtypeStruct(wk.shape, F32),
                   jax.ShapeDtypeStruct(wv.shape, F32)],
        compiler_params=_cparams(("arbitrary",)),
        name="swa_decode",
    )(sink, qa, kn, vn, wk, wv)


def _gdn_gates(ba, alog, dtb, valid):
    beta = jax.nn.sigmoid(ba)
    g = -jnp.exp(alog) * _softplus(ba + dtb)
    if valid is not None:
        beta = jnp.where(valid, beta, 0.0)
        g = jnp.where(valid, g, 0.0)
    return beta, g


def _chunk_masks(chunk):
    sh = chunk.bit_length() - 1
    ri = lax.broadcasted_iota(jnp.int32, (LANES, LANES), 0)
    ci = lax.broadcasted_iota(jnp.int32, (LANES, LANES), 1)
    same = (ri >> sh) == (ci >> sh)
    return same, same & (ri >= ci), same & (ri > ci), ri == ci


def _gdn_cumsums(g_all, chunk):
    n = g_all.shape[0]
    pos = lax.broadcasted_iota(jnp.int32, g_all.shape, 0) & (chunk - 1)
    gcol = g_all
    step = 1
    while step < chunk:
        gcol = gcol + jnp.where(pos >= step, pltpu.roll(gcol, step, 0), 0.0)
        step *= 2
    gtot = jnp.where(pos == chunk - 1, gcol, 0.0)
    step = 1
    while step < chunk:
        gtot = gtot + jnp.where(pos + step < chunk, pltpu.roll(gtot, n - step, 0), 0.0)
        step *= 2
    return gcol, gcol.T, gtot


def _sibling_mask(ri, ci, lvl):
    rb = ri >> lvl
    return ((rb & 1) == 1) & ((ci >> lvl) == rb - 1)


def _gdn_phase_a(heads, chunk):
    _, tri, strict, _ = _chunk_masks(chunk)
    ri = lax.broadcasted_iota(jnp.int32, (LANES, LANES), 0)
    ci = lax.broadcasted_iota(jnp.int32, (LANES, LANES), 1)
    kbs = [k * beta for q, k, v, beta, gcol, grow, gtot in heads]
    kkqk = [_bdot_nt(jnp.concatenate([kb, hd[0]], axis=0), hd[1]) for kb, hd in zip(kbs, heads)]
    a, qk = [], []
    for r, (q, k, v, beta, gcol, grow, gtot) in zip(kkqk, heads):
        decay = jnp.exp(jnp.where(tri, gcol - grow, NEG))
        a.append(jnp.where(strict, r[:LANES] * decay, 0.0))
        qk.append(r[LANES:] * decay)
    n = [-jnp.where(_sibling_mask(ri, ci, 0), ah, 0.0) for ah in a]
    for lvl in range(1, chunk.bit_length() - 1):
        sib = _sibling_mask(ri, ci, lvl)
        al = [jnp.where(sib, ah, 0.0) for ah in a]
        x = [alh + _bdot(nh, alh) for alh, nh in zip(al, n)]
        n = [nh - (xh + _bdot(xh, nh)) for xh, nh in zip(x, n)]
    outs = []
    rhs = [jnp.concatenate([hd[2] * hd[3], kb * jnp.exp(hd[4])], axis=1) for kb, hd in zip(kbs, heads)]
    sol = [r + _bdot(nh, r) for r, nh in zip(rhs, n)]
    for s, qkh, (q, k, v, beta, gcol, grow, gtot) in zip(sol, qk, heads):
        outs.append((s[:, :B_DV], s[:, B_DV:], q * jnp.exp(gcol), k * jnp.exp(gtot - gcol), qkh, jnp.exp(gtot)))
    return outs


def _gdn_qkv_heads(qkv, h):
    q = qkv[:, h * B_DK:(h + 1) * B_DK]
    k = qkv[:, (B_HEADS + h) * B_DK:(B_HEADS + h + 1) * B_DK]
    v = qkv[:, (2 * B_HEADS + h) * B_DK:(2 * B_HEADS + h + 1) * B_DK]
    q = q * lax.rsqrt(jnp.sum(q * q, axis=-1, keepdims=True) + EPS) * (B_DK ** -0.5)
    k = k * lax.rsqrt(jnp.sum(k * k, axis=-1, keepdims=True) + EPS)
    return q, k, v


def _gdn_out(o, z, nw):
    return _rms(o, nw) * _silu(z)


def _pair_bd(x2):
    lo = lax.broadcasted_iota(jnp.int32, x2.shape, 1) < LANES
    return jnp.concatenate([jnp.where(lo, x2, 0.0), jnp.where(lo, 0.0, x2)], axis=0).astype(BF16)


def _quad_bd(x4):
    blk = lax.broadcasted_iota(jnp.int32, x4.shape, 1) >> (GDN_CHUNK.bit_length() - 1)
    return jnp.concatenate([jnp.where(blk == j, x4, 0.0) for j in range(PAIR // GDN_CHUNK)],
                           axis=0).astype(BF16)


def _pair_cols(x, l0, l1):
    lo = lax.broadcasted_iota(jnp.int32, (x.shape[0], PAIR), 1) < LANES
    return jnp.where(lo, x[:, l0:l0 + 1], x[:, l1:l1 + 1])


def _gdn_prompt_kernel(q_ref, k_ref, v_ref, ba_ref, z_ref, alog_ref, dtb_ref, nw_ref,
                       ob_ref, s_ref, *, nb, ntb):
    t = pl.program_id(0)
    npair = B_HEADS // 2

    @pl.when(t == 0)
    def _():
        s_ref[...] = jnp.zeros(s_ref.shape, F32)

    alog = alog_ref[...]
    dtb = dtb_ref[...]
    nw = nw_ref[...]
    nchunk = LANES // GDN_CHUNK
    _, tri, strict, _ = _chunk_masks(GDN_CHUNK)
    tri2 = jnp.concatenate([tri, tri], axis=1)
    strict2 = jnp.concatenate([strict, strict], axis=1)

    items = []
    blocks = [(b, tb) for b in range(nb) for tb in range(ntb)]
    assert len(blocks) * GATE_LANES <= LANES
    packed = None
    for j, (b, tb) in enumerate(blocks):
        blk = ba_ref[b, tb * LANES:(tb + 1) * LANES, :]
        packed = blk if j == 0 else packed + pltpu.roll(blk, GATE_LANES * j, 1)
    beta_all, g_all = _gdn_gates(packed, alog, dtb, None)
    gcol_all, grow_all, gtot_all = _gdn_cumsums(g_all, GDN_CHUNK)
    eg_all = jnp.exp(gcol_all)
    ed_all = jnp.exp(gtot_all - gcol_all)
    et_all = jnp.exp(gtot_all)
    for j, (b, tb) in enumerate(blocks):
        rsl = slice(tb * LANES, (tb + 1) * LANES)
        for p in range(npair):
            h0, h1 = GATE_LANES * j + 2 * p, GATE_LANES * j + 2 * p + 1
            g0, g1 = B_HEADS + h0, B_HEADS + h1
            psl = slice(p * PAIR, (p + 1) * PAIR)
            q2 = q_ref[b, rsl, psl]
            k2 = k_ref[b, rsl, psl]
            v2 = v_ref[b, rsl, psl]
            beta2 = _pair_cols(beta_all, h0, h1)
            grow2 = jnp.concatenate([grow_all[g0:g0 + 1, :], grow_all[g1:g1 + 1, :]], axis=1)
            decay2 = jnp.exp(jnp.where(tri2, _pair_cols(gcol_all, g0, g1) - grow2, NEG))
            kb2 = k2 * beta2
            items.append(dict(
                b=b, tb=tb, p=p, rsl=rsl, decay2=decay2, kb2=kb2, v2b=v2 * beta2,
                kbe2=kb2 * _pair_cols(eg_all, g0, g1),
                lhs=jnp.concatenate([kb2, q2], axis=0).astype(BF16),
                kbd=_pair_bd(k2),
                qe2=(q2 * _pair_cols(eg_all, g0, g1)).astype(BF16),
                kd2=(k2 * _pair_cols(ed_all, g0, g1)).astype(BF16),
                et2=_pair_cols(et_all, g0, g1)))

    for it in items:
        kkqk = lax.dot_general(it["lhs"], it["kbd"], (((1,), (1,)), ((), ())),
                               preferred_element_type=F32)
        it["a2"] = jnp.where(strict2, kkqk[:LANES] * it["decay2"], 0.0)
        it["qk2"] = (kkqk[LANES:] * it["decay2"]).astype(BF16)
    qlane = lax.broadcasted_iota(jnp.int32, (GDN_CHUNK, PAIR), 1)
    qrow = lax.broadcasted_iota(jnp.int32, (GDN_CHUNK, PAIR), 0)
    qcol = qlane & (GDN_CHUNK - 1)
    top = (qlane & GDN_CHUNK) == 0
    for it in items:
        it["a64"] = it["a2"][:GDN_CHUNK] + it["a2"][GDN_CHUNK:]
        it["n"] = -jnp.where(_sibling_mask(qrow, qcol, 0), it["a64"], 0.0)
    for lvl in range(1, GDN_CHUNK.bit_length() - 1):
        sib = _sibling_mask(qrow, qcol, lvl)
        for it in items:
            it["al"] = jnp.where(sib, it["a64"], 0.0)
            it["x"] = it["al"] + jnp.dot(it["n"].astype(BF16), _quad_bd(it["al"]), preferred_element_type=F32)
        for it in items:
            x = it["x"]
            it["n"] = it["n"] - (x + jnp.dot(x.astype(BF16), _quad_bd(it["n"]), preferred_element_type=F32))
    for it in items:
        n64 = it["n"]
        it["n"] = jnp.concatenate([jnp.where(top, n64, 0.0), jnp.where(top, 0.0, n64)], axis=0)
    for it in items:
        us, ws = [], []
        for s in range(2):
            sl = slice(s * LANES, (s + 1) * LANES)
            rhs = jnp.concatenate([it["v2b"][:, sl], it["kbe2"][:, sl]], axis=1)
            sol = rhs + _bdot(it["n"][:, sl], rhs)
            us.append(sol[:, :B_DV])
            ws.append(sol[:, B_DV:])
        it["u2"] = jnp.concatenate(us, axis=1)
        it["w2"] = jnp.concatenate(ws, axis=1).astype(BF16)

    nchain = nb * npair
    state = [[s_ref[ch // npair, 2 * (ch % npair) + s] for s in range(2)] for ch in range(nchain)]
    zeros16 = jnp.zeros((LANES, LANES), BF16)
    outs = {}
    for tb in range(ntb):
        cur = [(it["b"] * npair + it["p"], i, it) for i, it in enumerate(items) if it["tb"] == tb]
        for c in range(nchunk):
            rows = slice(c * GDN_CHUNK, (c + 1) * GDN_CHUNK)
            rs = []
            for ch, _, it in cur:
                sa, sb = (s.astype(BF16) for s in state[ch])
                sbd = jnp.concatenate([jnp.concatenate([sa, zeros16], axis=1),
                                       jnp.concatenate([zeros16, sb], axis=1)], axis=0)
                rs.append(jnp.dot(jnp.concatenate([it["w2"][rows], it["qe2"][rows]], axis=0), sbd,
                                  preferred_element_type=F32))
            for (ch, i, it), r in zip(cur, rs):
                vn2 = it["u2"][rows] - r[:GDN_CHUNK]
                vnb = vn2.astype(BF16)
                vt = _pair_bd(jnp.concatenate([vn2] * nchunk, axis=0))
                outs[(i, c)] = r[GDN_CHUNK:] + jnp.dot(it["qk2"][rows], vt, preferred_element_type=F32)
                for s in range(2):
                    hl = slice(s * LANES, (s + 1) * LANES)
                    upd = lax.dot_general(it["kd2"][rows, hl], vnb[:, hl], (((0,), (0,)), ((), ())),
                                          preferred_element_type=F32)
                    state[ch][s] = state[ch][s] * it["et2"][c * GDN_CHUNK:c * GDN_CHUNK + 1, hl] + upd
    for ch in range(nchain):
        for s in range(2):
            s_ref[ch // npair, 2 * (ch % npair) + s] = state[ch][s]

    for i, it in enumerate(items):
        o2 = jnp.concatenate([outs[(i, c)] for c in range(nchunk)], axis=0)
        for s in range(2):
            h = 2 * it["p"] + s
            sl = slice(h * B_DV, (h + 1) * B_DV)
            o = o2[:, s * LANES:(s + 1) * LANES]
            ob_ref[it["b"], it["rsl"], sl] = _gdn_out(o, z_ref[it["b"], it["rsl"], sl], nw).astype(BF16)


GDN_TOKEN_BLOCKS = 2


def _gdn_prompt(q, k, v, ba, z, alog, dtb, nw):
    batch, seq, _ = q.shape
    tm = GDN_TOKEN_BLOCKS * LANES
    tok = lambda n: pl.BlockSpec((batch, tm, n), lambda t: (0, t, 0))
    state = pl.BlockSpec((batch, B_HEADS, B_DK, B_DV), lambda t: (0, 0, 0, 0))
    return pl.pallas_call(
        functools.partial(_gdn_prompt_kernel, nb=batch, ntb=GDN_TOKEN_BLOCKS),
        grid=(seq // tm,),
        in_specs=[tok(512), tok(512), tok(512), tok(LANES), tok(512),
                  _const_spec((1, LANES)), _const_spec((1, LANES)), _const_spec((1, B_DV))],
        out_specs=[tok(512), state],
        out_shape=[jax.ShapeDtypeStruct((batch, seq, 512), BF16),
                   jax.ShapeDtypeStruct((batch, B_HEADS, B_DK, B_DV), F32)],
        compiler_params=_cparams(("arbitrary",)),
        name="gdn_prompt",
    )(q, k, v, ba, z, alog, dtb, nw)


def _gdn_decode_kernel(raw_ref, hist_ref, ba_ref, z_ref, rec_ref, cw_ref, alog_ref, dtb_ref, nw_ref,
                       ob_ref, s_ref, buf_ref, u_s, w_s, qe_s, kd_s, qk_s, eg_s, o_s, *, rows, valid_len):
    buf_ref[:, 0:DEC_PAD, :] = hist_ref[...]
    buf_ref[:, DEC_PAD:2 * DEC_PAD, :] = raw_ref[...]
    conv = None
    for i in range(CONV_W):
        off = DEC_PAD - (CONV_W - 1) + i
        term = buf_ref[:, off:off + DEC_PAD, :] * cw_ref[i:i + 1, :]
        conv = term if conv is None else conv + term
    qkv = _silu(conv.reshape(rows * DEC_PAD, B_CONV_CH))
    tok = lax.broadcasted_iota(jnp.int32, (LANES, LANES), 0) & (DEC_PAD - 1)
    beta_all, g_all = _gdn_gates(ba_ref[...], alog_ref[...], dtb_ref[...], tok < valid_len)
    gcol_all, grow_all, gtot_all = _gdn_cumsums(g_all, DEC_PAD)
    heads = []
    for h in range(B_HEADS):
        gl = B_HEADS + h
        heads.append(_gdn_qkv_heads(qkv, h) + (beta_all[:, h:h + 1], gcol_all[:, gl:gl + 1],
                                               grow_all[gl:gl + 1, :], gtot_all[:, gl:gl + 1]))
    for h, (u, w, qe, kd, qk, egt) in enumerate(_gdn_phase_a(heads, DEC_PAD)):
        u_s[h] = u
        w_s[h] = w
        qe_s[h] = qe
        kd_s[h] = kd
        qk_s[h] = qk
        eg_s[h] = jnp.broadcast_to(egt, (LANES, LANES))

    def load(r):
        r0 = pl.multiple_of(r * DEC_PAD, DEC_PAD)
        rr = pl.ds(r0, DEC_PAD)
        return [(rec_ref[r, h], w_s[h, rr, :], qe_s[h, rr, :], u_s[h, rr, :], qk_s[h, rr, :],
                 kd_s[h, rr, :], eg_s[h, pl.ds(r0, 1), :]) for h in range(B_HEADS)]

    def read_state(heads):
        return [(_bdot(jnp.concatenate([w, qe], axis=0), s), s, u, qk, kd, eg)
                for s, w, qe, u, qk, kd, eg in heads]

    def update(heads):
        outs = []
        for res, s, u, qk, kd, eg in heads:
            v_new = u - res[:DEC_PAD]
            vt = jnp.concatenate([v_new] * (LANES // DEC_PAD), axis=0)
            outs.append((res[DEC_PAD:] + _bdot(qk, vt), s * eg + _bdot_tn(kd, v_new)))
        return outs

    def store(r, outs):
        rr = pl.ds(pl.multiple_of(r * DEC_PAD, DEC_PAD), DEC_PAD)
        for h, (o, s_new) in enumerate(outs):
            o_s[h, rr, :] = o
            s_ref[r, h] = s_new

    _rows_loop(rows, load, [read_state, update], store)
    nw = nw_ref[...]
    for h in range(B_HEADS):
        sl = slice(h * B_DV, (h + 1) * B_DV)
        ob_ref[:, sl] = _gdn_out(o_s[h], z_ref[:, sl], nw).astype(BF16)


def _gdn_decode(raw, histp, ba, z, rec, cw, alog, dtb, nw, rows, valid_len):
    nb = raw.shape[0]
    flat = rows * DEC_PAD
    assert flat == LANES
    sq = lambda: pltpu.VMEM((B_HEADS, LANES, LANES), F32)
    return pl.pallas_call(
        functools.partial(_gdn_decode_kernel, rows=rows, valid_len=valid_len),
        grid=(nb // rows,),
        in_specs=[pl.BlockSpec((rows, DEC_PAD, B_CONV_CH), lambda i: (i, 0, 0)),
                  pl.BlockSpec((rows, DEC_PAD, B_CONV_CH), lambda i: (i, 0, 0)),
                  pl.BlockSpec((flat, LANES), lambda i: (i, 0)),
                  pl.BlockSpec((flat, 512), lambda i: (i, 0)),
                  pl.BlockSpec((rows, B_HEADS, B_DK, B_DV), lambda i: (i, 0, 0, 0)),
                  _const_spec((CONV_W, B_CONV_CH)), _const_spec((1, LANES)), _const_spec((1, LANES)),
                  _const_spec((1, B_DV))],
        out_specs=[pl.BlockSpec((flat, 512), lambda i: (i, 0)),
                   pl.BlockSpec((rows, B_HEADS, B_DK, B_DV), lambda i: (i, 0, 0, 0))],
        out_shape=[jax.ShapeDtypeStruct((nb * DEC_PAD, 512), BF16),
                   jax.ShapeDtypeStruct(rec.shape, F32)],
        scratch_shapes=[pltpu.VMEM((rows, 2 * DEC_PAD, B_CONV_CH), F32)] + [sq() for _ in range(7)],
        compiler_params=_cparams(("arbitrary",)),
        name="gdn_decode",
    )(raw, histp, ba, z, rec, cw, alog, dtb, nw)


def _memkv_kernel(m_ref, g_ref, w_ref, k_ref, v_ref):
    h = _rms(m_ref[...], g_ref[...]).astype(BF16)
    n = C_HEADS * C_HD
    k_ref[...] = jnp.dot(h, w_ref[:, :n], preferred_element_type=F32)
    v_ref[...] = jnp.dot(h, w_ref[:, n:], preferred_element_type=F32)


def _memkv(mem, gain, w):
    t = mem.shape[0]
    tm = 512
    n = C_HEADS * C_HD
    return pl.pallas_call(
        _memkv_kernel,
        grid=(t // tm,),
        in_specs=[pl.BlockSpec((tm, D_MODEL), lambda i: (i, 0)), _const_spec((1, D_MODEL)),
                  _const_spec((D_MODEL, 2 * n))],
        out_specs=[pl.BlockSpec((tm, n), lambda i: (i, 0))] * 2,
        out_shape=[jax.ShapeDtypeStruct((t, n), F32)] * 2,
        compiler_params=_cparams(("arbitrary",)),
        name="memkv",
    )(mem, gain, w)


def _softmax_rows(logits):
    m = jnp.max(logits, axis=-1, keepdims=True)
    e = jnp.exp(logits - m)
    return e, 1.0 / jnp.sum(e, axis=-1, keepdims=True)


def _memattn_prompt_kernel(q_ref, k_ref, v_ref, o_ref):
    hs = lambda h: slice(h * C_HD, (h + 1) * C_HD)
    scores = lambda h: _bdot_nt(q_ref[:, hs(h)], k_ref[:, hs(h)])
    logits = scores(0)
    for h in range(C_HEADS):
        nxt = scores(h + 1) if h + 1 < C_HEADS else None
        e, inv = _softmax_rows(logits * (C_HD ** -0.5))
        o_ref[:, hs(h)] = (_bdot(e, v_ref[:, hs(h)]) * inv).astype(BF16)
        logits = nxt


def _memattn_prompt(qc, mk, mv, batch, seq, tm):
    nq = seq // tm
    n = C_HEADS * C_HD
    cur = pl.BlockSpec((tm, n), lambda b, i: (b * nq + i, 0))
    mem = pl.BlockSpec((N_MEM, n), lambda b, i: (b, 0))
    return pl.pallas_call(
        _memattn_prompt_kernel,
        grid=(batch, nq),
        in_specs=[cur, mem, mem],
        out_specs=cur,
        out_shape=jax.ShapeDtypeStruct(qc.shape, BF16),
        compiler_params=_cparams(("arbitrary", "arbitrary")),
        name="memattn_prompt",
    )(qc, mk, mv)


def _memattn_decode_kernel(q_ref, k_ref, v_ref, o_ref, *, rows):
    nk = N_MEM * C_HEADS
    col = lax.broadcasted_iota(jnp.int32, (C_HEADS * DEC_PAD, nk), 1)
    row = lax.broadcasted_iota(jnp.int32, (C_HEADS * DEC_PAD, nk), 0)
    own = (col & (C_HEADS - 1)) == (row >> (DEC_PAD.bit_length() - 1))

    def load(r):
        return q_ref[r], k_ref[r], v_ref[r]

    def scores(args):
        q, k, v = args
        q = q.astype(F32)
        lhs = jnp.concatenate([q[:, h * C_HD:(h + 1) * C_HD] for h in range(C_HEADS)], axis=0)
        return _bdot_nt(lhs, k), v

    def finish(args):
        logits, v = args
        e, inv = _softmax_rows(jnp.where(own, logits * (C_HD ** -0.5), NEG))
        pv = _bdot(e, v) * inv
        return jnp.concatenate([pv[h * DEC_PAD:(h + 1) * DEC_PAD] for h in range(C_HEADS)],
                               axis=1).astype(BF16)

    def store(r, o):
        o_ref[r] = o

    _rows_loop(rows, load, [scores, finish], store)


def _memattn_decode(qc, ck, cv, rows):
    nb = qc.shape[0]
    n = C_HEADS * C_HD
    blk = pl.BlockSpec((rows, DEC_PAD, n), lambda i: (i, 0, 0))
    cache = pl.BlockSpec((rows, N_MEM * C_HEADS, C_HD), lambda i: (i, 0, 0))
    return pl.pallas_call(
        functools.partial(_memattn_decode_kernel, rows=rows),
        grid=(nb // rows,),
        in_specs=[blk, cache, cache],
        out_specs=blk,
        out_shape=jax.ShapeDtypeStruct(qc.shape, BF16),
        compiler_params=_cparams(("arbitrary",)),
        name="memattn_decode",
    )(qc, ck, cv)


def _merge_kernel(x_ref, oa_ref, ob_ref, oc_ref, gpre_ref, wga_ref, wgb_ref, wgc_ref, wb_ref, wo_ref,
                  gpost_ref, gfpre_ref, x1_ref, h2_ref):
    x = x_ref[...]
    ups = [jnp.dot(o_ref[...], wb_ref[n], preferred_element_type=F32)
           for n, o_ref in enumerate((oa_ref, ob_ref, oc_ref))]
    h = _rms(x, gpre_ref[...]).astype(BF16)
    mix = None
    for n, wg_ref in enumerate((wga_ref, wgb_ref, wgc_ref)):
        gate = jax.nn.sigmoid(jnp.dot(h, wg_ref[...], preferred_element_type=F32))
        mix = gate * ups[n] if mix is None else mix + gate * ups[n]
    x1 = x + _rms(_bdot(mix, wo_ref[...]), gpost_ref[...])
    x1_ref[...] = x1
    h2_ref[...] = _rms(x1, gfpre_ref[...]).astype(BF16)


FF_SPLIT = 6 * PAIR


def _ffn_kernel(x1_ref, h2_ref, wfi_ref, wfo_ref, gfpost_ref, y_ref):
    h2 = h2_ref[...]
    f = None
    halves = []
    for a, b in ((0, FF_SPLIT), (FF_SPLIT, D_FF)):
        gt = jnp.dot(h2, wfi_ref[:, a:b], preferred_element_type=F32)
        uf = jnp.dot(h2, wfi_ref[:, D_FF + a:D_FF + b], preferred_element_type=F32)
        halves.append((a, b, gt, uf))
    for a, b, gt, uf in halves:
        part = _bdot(_silu(gt) * uf, wfo_ref[a:b, :])
        f = part if f is None else f + part
    y_ref[...] = x1_ref[...] + _rms(f, gfpost_ref[...])


def _merge_weight_specs():
    vec = _const_spec((1, D_MODEL))
    return ([vec] + [pl.BlockSpec((D_MODEL, D_MODEL), functools.partial(lambda n, *_: (0, n), n + 1),
                                  pipeline_mode=pl.Buffered(1)) for n in range(N_BRANCH)]
            + [_const_spec((N_BRANCH, BRANCH_W, D_MODEL)), _const_spec((D_MODEL, D_MODEL)), vec, vec])


def _merge(x, oa, ob, oc, gpre, wtail, wb, wo, gpost, gfpre, tm):
    t = x.shape[0]
    row = lambda n: pl.BlockSpec((tm, n), lambda i: (i, 0))
    return pl.pallas_call(
        _merge_kernel,
        grid=(t // tm,),
        in_specs=[row(D_MODEL), row(512), row(512), row(512)] + _merge_weight_specs(),
        out_specs=[row(D_MODEL), row(D_MODEL)],
        out_shape=[jax.ShapeDtypeStruct(x.shape, F32), jax.ShapeDtypeStruct(x.shape, BF16)],
        compiler_params=_cparams(("arbitrary",)),
        name="merge",
    )(x, oa, ob, oc, gpre, wtail, wtail, wtail, wb, wo, gpost, gfpre)


def _ffn(x1, h2, wfi, wfo, gfpost, tm):
    t = x1.shape[0]
    vec = _const_spec((1, D_MODEL))
    frow = pl.BlockSpec((tm, D_MODEL), lambda i: (i, 0))
    return pl.pallas_call(
        _ffn_kernel,
        grid=(t // tm,),
        in_specs=[frow, frow, _const_spec((D_MODEL, 2 * D_FF)),
                  _const_spec((D_FF, D_MODEL)), vec],
        out_specs=frow,
        out_shape=jax.ShapeDtypeStruct(x1.shape, F32),
        compiler_params=_cparams(("arbitrary",)),
        name="ffn",
    )(x1, h2, wfi, wfo, gfpost)


def _rope_tables(pos):
    half = A_HD // 2
    inv = ROPE_THETA ** (-jnp.arange(half, dtype=F32) / half)
    ang = pos.astype(F32)[:, None] * inv[None, :]
    cos, sin = jnp.cos(ang), jnp.sin(ang)
    cos = jnp.concatenate([cos, cos], axis=-1)
    sin = jnp.concatenate([-sin, sin], axis=-1)
    return jnp.tile(cos, (1, LANES // A_HD)), jnp.tile(sin, (1, LANES // A_HD))


def _lane_row(vals, offset):
    grp = jnp.zeros((GATE_LANES,), F32).at[offset:offset + vals.shape[0]].set(vals.astype(F32))
    return jnp.tile(grp, LANES // GATE_LANES).reshape(1, LANES)


def kernel(x_prompt, x_sample, mem_prompt, state_win_k, state_win_v, state_conv, state_rec,
           cache_mem_k, cache_mem_v, ln_mix_pre, w_in, attn_sink, gdn_conv_w, gdn_a_log,
           gdn_dt_bias, gdn_norm_w, ln_mem, w_mem_kv, w_branch, w_out, ln_mix_post,
           ln_ffn_pre, w_ffn_in, w_ffn_out, ln_ffn_post):
    bp, lp, _ = x_prompt.shape
    bs, ls, _ = x_sample.shape

    sizes = [512, 128, 128, B_CONV_CH, B_HEADS, B_HEADS, 512, 512, N_BRANCH * D_MODEL]
    o = np.cumsum([0] + sizes)
    hperm = np.concatenate([np.r_[j * A_HD:(j + 1) * A_HD, (j + 4) * A_HD:(j + 5) * A_HD] for j in range(4)])
    ws = (w_in[:, hperm].astype(BF16), w_in[:, o[1]:o[4]].astype(BF16), w_in[:, o[6]:o[9]].astype(BF16),
          jnp.pad(w_in[:, o[4]:o[6]], ((0, 0), (0, LANES - 2 * B_HEADS))).astype(BF16))
    wb = jnp.concatenate([w_branch[0:1][:, hperm], w_branch[1:]], axis=0).astype(BF16)
    wo = w_out.astype(BF16)
    wfi = w_ffn_in.astype(BF16)
    wfo = w_ffn_out.astype(BF16)
    wmem = w_mem_kv.astype(BF16)
    sink = attn_sink.astype(F32)[np.array([0, 4, 1, 5, 2, 6, 3, 7])]
    vec = lambda g: g.astype(F32).reshape(1, -1)
    alog = _lane_row(gdn_a_log, B_HEADS)
    dtb = _lane_row(gdn_dt_bias, B_HEADS)
    cw = gdn_conv_w.astype(F32)
    nw = vec(gdn_norm_w)

    merge_w = (vec(ln_mix_pre), ws[2], wb, wo, vec(ln_mix_post), vec(ln_ffn_pre))
    ffn = lambda x1, h2: _ffn(x1, h2, wfi, wfo, vec(ln_ffn_post), min(1024, x1.shape[0]))

    tp = bp * lp
    xp = x_prompt.reshape(tp, D_MODEL)
    cos_p, sin_p = _rope_tables(jnp.arange(lp, dtype=jnp.int32))
    qa, ka, va, z, qc, ba, qn, kn, vv, tail = _proj(xp, vec(ln_mix_pre), ws, cos_p, sin_p, 1024, cw, lp)
    b3 = lambda a: a.reshape(bp, lp, a.shape[-1])
    ob, rec_p = _gdn_prompt(b3(qn), b3(kn), b3(vv), b3(ba), b3(z), alog, dtb, nw)
    mk, mv = _memkv(mem_prompt.reshape(bp * N_MEM, D_MODEL), vec(ln_mem), wmem)
    oa = _swa_prompt(sink, qa, ka, va, bp, lp, 2048)
    oc = _memattn_prompt(qc, mk, mv, bp, lp, 2048)
    x1, h2 = _merge(xp, oa, ob.reshape(tp, 512), oc, *merge_w, 1024)
    y_p = ffn(x1, h2).reshape(bp, lp, D_MODEL)
    wk_p = ka.reshape(bp, lp, LANES)[:, -WINDOW:].reshape(bp, WINDOW, A_KV, A_HD)
    wv_p = va.reshape(bp, lp, LANES)[:, -WINDOW:].reshape(bp, WINDOW, A_KV, A_HD)
    conv_p = tail[:, -(CONV_W - 1):]
    mem_k_p = mk.reshape(bp, N_MEM, C_HEADS, C_HD)
    mem_v_p = mv.reshape(bp, N_MEM, C_HEADS, C_HD)

    ts = bs * DEC_PAD
    xs = jnp.pad(x_sample, ((0, 0), (0, DEC_PAD - ls), (0, 0))).reshape(ts, D_MODEL)
    cos_s, sin_s = _rope_tables(PAST_LEN + jnp.arange(DEC_PAD, dtype=jnp.int32))
    cos_s, sin_s = jnp.tile(cos_s, (bs, 1)), jnp.tile(sin_s, (bs, 1))
    qa, ka, va, z, qc, ba, qkv = _proj(xs, vec(ln_mix_pre), ws, cos_s, sin_s, 512)
    r3 = lambda a: a.reshape(bs, DEC_PAD, a.shape[-1])
    oa, wk_s, wv_s = _swa_decode(sink, r3(qa), r3(ka), r3(va), state_win_k.reshape(bs, WINDOW, LANES),
                                 state_win_v.reshape(bs, WINDOW, LANES), 16, ls)
    oa = oa.reshape(ts, 512)
    wk_s = wk_s.reshape(state_win_k.shape)
    wv_s = wv_s.reshape(state_win_v.shape)
    histp = jnp.pad(state_conv, ((0, 0), (DEC_PAD - (CONV_W - 1), 0), (0, 0)))
    ob, rec_s = _gdn_decode(r3(qkv), histp, ba, z, state_rec, cw, alog, dtb, nw,
                            LANES // DEC_PAD, ls)
    oc = _memattn_decode(r3(qc), cache_mem_k.reshape(bs, N_MEM * C_HEADS, C_HD),
                         cache_mem_v.reshape(bs, N_MEM * C_HEADS, C_HD), 8).reshape(ts, 512)
    real = lambda a: a.reshape(bs, DEC_PAD, 512)[:, :ls].reshape(bs * ls, 512)
    x1, h2 = _merge(x_sample.reshape(bs * ls, D_MODEL), real(oa), real(ob), real(oc), *merge_w, 256)
    y_s = ffn(x1, h2).reshape(bs, ls, D_MODEL)
    conv_s = r3(qkv)[:, ls - (CONV_W - 1):ls]

    return (y_p, y_s, wk_p, wv_p, conv_p, rec_p, mem_k_p, mem_v_p, wk_s, wv_s, conv_s, rec_s)
```

```python
import functools

import numpy as np
import jax
import jax.numpy as jnp
from jax import lax
from jax.experimental import pallas as pl
from jax.experimental.pallas import tpu as pltpu

F32 = jnp.float32
BF16 = jnp.bfloat16

D_MODEL = 1024
PAST_LEN = 16384
EPS = 1e-6
ROPE_THETA = 10000.0
N_MEM = 256
WINDOW = 128
A_HD = 64
A_HEADS = 8
A_KV = 2
A_SCALE = A_HD ** -0.5
B_HEADS = 4
B_DK = 128
B_DV = 128
CONV_W = 4
GDN_CHUNK = 64
B_CONV_CH = B_HEADS * (2 * B_DK + B_DV)
C_HEADS = 4
C_HD = 128
N_BRANCH = 3
BRANCH_W = 512
D_FF = 2816

LANES = 128
SUBLANES = 8
PAIR = 2 * LANES
GATE_LANES = 2 * B_HEADS
VMEM_LIMIT = 56 * 1024 * 1024
NEG = -1e30
DEC_PAD = SUBLANES


def _cparams(sem, vmem=VMEM_LIMIT):
    return pltpu.CompilerParams(dimension_semantics=sem, vmem_limit_bytes=vmem)


def _const_spec(shape):
    nd = len(shape)
    return pl.BlockSpec(shape, lambda *_: (0,) * nd, pipeline_mode=pl.Buffered(1))


def _rms(x, g):
    ms = jnp.mean(x * x, axis=-1, keepdims=True)
    return x * lax.rsqrt(ms + EPS) * g


def _bdot(a, b):
    return jnp.dot(a.astype(BF16), b.astype(BF16), preferred_element_type=F32)


def _bdot_nt(a, b):
    return lax.dot_general(a.astype(BF16), b.astype(BF16), (((1,), (1,)), ((), ())),
                           preferred_element_type=F32)


def _bdot_tn(a, b):
    return lax.dot_general(a.astype(BF16), b.astype(BF16), (((0,), (0,)), ((), ())),
                           preferred_element_type=F32)


def _silu(x):
    return x * jax.nn.sigmoid(x)


def _softplus(x):
    return jnp.maximum(x, 0.0) + jnp.log1p(jnp.exp(-jnp.abs(x)))


def _rope128(v, cos, sin):
    lane = lax.broadcasted_iota(jnp.int32, v.shape, 1)
    fwd = pltpu.roll(v, 32, 1)
    bwd = pltpu.roll(v, 96, 1)
    sw = jnp.where((lane & 32) == 0, bwd, fwd)
    return v * cos + sw * sin


def _l2n(x):
    return x * lax.rsqrt(jnp.sum(x * x, axis=-1, keepdims=True) + EPS)

def _proj_steps(x_ref, g_ref, wq_ref, wr_ref, wzc_ref, wba_ref, cos_ref, sin_ref,
                qa_ref, ka_ref, va_ref, z_ref, qc_ref, ba_ref):
    h = _rms(x_ref[...], g_ref[...]).astype(BF16)
    cos = cos_ref[...]
    sin = sin_ref[...]

    def mm(w_ref, a, b):
        return jnp.dot(h, w_ref[:, a:b], preferred_element_type=F32)

    def qa_half(c0):
        q = mm(wq_ref, c0 * LANES, (c0 + 2) * LANES)
        for c in range(2):
            qa_ref[:, (c0 + c) * LANES:(c0 + c + 1) * LANES] = (
                _rope128(q[:, c * LANES:(c + 1) * LANES], cos, sin) * A_SCALE).astype(BF16)

    def kv():
        kv2 = mm(wr_ref, 0, PAIR)
        ka_ref[...] = _rope128(kv2[:, :LANES], cos, sin)
        va_ref[...] = kv2[:, LANES:]

    def z_half(c0):
        z_ref[:, c0:c0 + PAIR] = mm(wzc_ref, c0, c0 + PAIR)

    def qc_half(c0):
        qc_ref[:, c0:c0 + PAIR] = mm(wzc_ref, 512 + c0, 512 + c0 + PAIR).astype(BF16)

    def ba():
        ba_ref[...] = mm(wba_ref, 0, LANES)

    def qkv(a, b):
        return mm(wr_ref, PAIR + a, PAIR + b)

    steps = [lambda: qa_half(0), lambda: qa_half(2), kv, lambda: z_half(0), lambda: z_half(PAIR),
             lambda: qc_half(0), lambda: qc_half(PAIR), ba]
    return qkv, steps


def _proj_raw_kernel(x_ref, g_ref, wq_ref, wr_ref, wzc_ref, wba_ref, cos_ref, sin_ref,
                     qa_ref, ka_ref, va_ref, z_ref, qc_ref, ba_ref, qkv_ref):
    qkv, steps = _proj_steps(x_ref, g_ref, wq_ref, wr_ref, wzc_ref, wba_ref, cos_ref, sin_ref,
                             qa_ref, ka_ref, va_ref, z_ref, qc_ref, ba_ref)
    qkv_ref[...] = qkv(0, B_CONV_CH)
    for step in steps:
        step()


def _proj_conv_kernel(x_ref, g_ref, wq_ref, wr_ref, wzc_ref, wba_ref, cos_ref, sin_ref, cw_ref,
                      qa_ref, ka_ref, va_ref, z_ref, qc_ref, ba_ref, qn_ref, kn_ref, vv_ref, tail_ref,
                      buf_ref, *, tm, tiles_per_seq):
    hist = SUBLANES
    first = lax.rem(pl.program_id(0), tiles_per_seq) == 0

    @pl.when(first)
    def _():
        buf_ref[0:hist, :] = jnp.zeros((hist, B_CONV_CH), F32)

    @pl.when(jnp.logical_not(first))
    def _():
        buf_ref[0:hist, :] = buf_ref[tm:tm + hist, :]

    qkv, steps = _proj_steps(x_ref, g_ref, wq_ref, wr_ref, wzc_ref, wba_ref, cos_ref, sin_ref,
                             qa_ref, ka_ref, va_ref, z_ref, qc_ref, ba_ref)
    nq = B_HEADS * B_DK

    def conv_group(c0):
        cs = slice(c0, c0 + PAIR)
        raw = qkv(c0, c0 + PAIR)
        buf_ref[hist:hist + tm, cs] = raw
        tail_ref[0, :, cs] = raw[tm - hist:, :]
        xb = buf_ref[:, cs]
        acc = xb * cw_ref[0:1, cs]
        for i in range(1, CONV_W):
            acc = pltpu.roll(acc, 1, 0) + xb * cw_ref[i:i + 1, cs]
        act = _silu(acc[hist:])
        if c0 >= 2 * nq:
            vv_ref[:, c0 - 2 * nq:c0 - 2 * nq + PAIR] = act
            return
        out_ref, base, scale = (qn_ref, 0, B_DK ** -0.5) if c0 < nq else (kn_ref, nq, 1.0)
        for s in range(2):
            o0 = c0 - base + s * B_DK
            out_ref[:, o0:o0 + B_DK] = _l2n(act[:, s * B_DK:(s + 1) * B_DK]) * scale

    groups = [functools.partial(conv_group, c0) for c0 in range(0, B_CONV_CH, PAIR)]
    while groups or steps:
        if groups:
            groups.pop(0)()
        if steps:
            steps.pop(0)()


_PROJ_OUTS = [(512, BF16), (128, F32), (128, F32), (512, F32), (512, BF16), (128, F32)]


def _proj(x, gain, ws, cos, sin, tm, cw=None, seq=None):
    wq, wr, wtail, wba = ws
    t = x.shape[0]
    ntab = cos.shape[0] // tm
    row = lambda n: pl.BlockSpec((tm, n), lambda i: (i, 0))
    tab = pl.BlockSpec((tm, LANES), lambda i: (i % ntab, 0))
    in_specs = [row(D_MODEL), _const_spec((1, D_MODEL)), _const_spec(wq.shape), _const_spec(wr.shape),
                _const_spec((D_MODEL, D_MODEL)), _const_spec(wba.shape), tab, tab]
    out_specs = [row(n) for n, _ in _PROJ_OUTS]
    out_shape = [jax.ShapeDtypeStruct((t, n), d) for n, d in _PROJ_OUTS]
    if cw is None:
        return pl.pallas_call(
            _proj_raw_kernel, grid=(t // tm,), in_specs=in_specs,
            out_specs=out_specs + [row(B_CONV_CH)],
            out_shape=out_shape + [jax.ShapeDtypeStruct((t, B_CONV_CH), F32)],
            compiler_params=_cparams(("arbitrary",)), name="proj",
        )(x, gain, wq, wr, wtail, wba, cos, sin)
    tiles = seq // tm
    return pl.pallas_call(
        functools.partial(_proj_conv_kernel, tm=tm, tiles_per_seq=tiles),
        grid=(t // tm,),
        in_specs=in_specs + [_const_spec((CONV_W, B_CONV_CH))],
        out_specs=out_specs + [row(512)] * 3
        + [pl.BlockSpec((1, SUBLANES, B_CONV_CH), lambda i: (i // tiles, 0, 0))],
        out_shape=out_shape + [jax.ShapeDtypeStruct((t, 512), F32)] * 3
        + [jax.ShapeDtypeStruct((t // seq, SUBLANES, B_CONV_CH), F32)],
        scratch_shapes=[pltpu.VMEM((tm + SUBLANES, B_CONV_CH), F32)],
        compiler_params=_cparams(("arbitrary",)), name="proj_conv",
    )(x, gain, wq, wr, wtail, wba, cos, sin, cw)


def _swa_scores(q, k16):
    tq = q.shape[0]
    lo = lax.broadcasted_iota(jnp.int32, (tq, LANES), 1) < A_HD
    blocks = []
    for j in range(4):
        c = q[:, j * LANES:(j + 1) * LANES].astype(F32)
        blocks.append(jnp.where(lo, c, 0.0))
        blocks.append(jnp.where(lo, 0.0, c))
    lhs = jnp.concatenate(blocks, axis=0).astype(BF16)
    return lax.dot_general(lhs, k16, (((1,), (1,)), ((), ())), preferred_element_type=F32)


def _swa_finish(logits, v16, mask, sink_ref):
    tq = logits.shape[0] // A_HEADS
    lo = lax.broadcasted_iota(jnp.int32, (tq, LANES), 1) < A_HD
    es, inv = [], []
    for s in range(8):
        l = jnp.where(mask, logits[s * tq:(s + 1) * tq], NEG)
        sk = sink_ref[s]
        m = jnp.maximum(jnp.max(l, axis=-1, keepdims=True), sk)
        e = jnp.exp(l - m)
        den = jnp.sum(e, axis=-1, keepdims=True) + jnp.exp(sk - m)
        es.append(e.astype(BF16))
        inv.append(1.0 / den)
    pv = jnp.dot(jnp.concatenate(es, axis=0), v16, preferred_element_type=F32)
    outs = []
    for j in range(4):
        a = pv[(2 * j) * tq:(2 * j + 1) * tq] * inv[2 * j]
        b = pv[(2 * j + 1) * tq:(2 * j + 2) * tq] * inv[2 * j + 1]
        outs.append(jnp.where(lo, a, b))
    return jnp.concatenate(outs, axis=1)


def _swa_prompt_kernel(sink_ref, q_ref, kc_ref, kp_ref, vc_ref, vp_ref, o_ref, *, nblk):
    i = pl.program_id(1)
    kcat = jnp.concatenate([kp_ref[...], kc_ref[...]], axis=0).astype(BF16)
    vcat = jnp.concatenate([vp_ref[...], vc_ref[...]], axis=0).astype(BF16)
    ii = lax.broadcasted_iota(jnp.int32, (WINDOW, 2 * WINDOW), 0)
    jj = lax.broadcasted_iota(jnp.int32, (WINDOW, 2 * WINDOW), 1)
    band = (jj > ii) & (jj <= ii + WINDOW)
    for jb in range(nblk):
        mask = band
        if jb == 0:
            mask = band & ((jj >= WINDOW) | (i > 0))
        logits = _swa_scores(q_ref[jb * WINDOW:(jb + 1) * WINDOW, :], kcat[jb * WINDOW:(jb + 2) * WINDOW])
        o = _swa_finish(logits, vcat[jb * WINDOW:(jb + 2) * WINDOW], mask, sink_ref)
        o_ref[jb * WINDOW:(jb + 1) * WINDOW, :] = o.astype(BF16)


def _swa_prompt(sink, qa, ka, va, batch, seq, tq):
    nq = seq // tq
    nblk = tq // WINDOW
    nw = seq // WINDOW
    cur = lambda n: pl.BlockSpec((tq, n), lambda b, i: (b * nq + i, 0))
    prev = pl.BlockSpec((WINDOW, LANES), lambda b, i: (jnp.maximum(b * nw + i * nblk - 1, 0), 0))
    return pl.pallas_call(
        functools.partial(_swa_prompt_kernel, nblk=nblk),
        grid=(batch, nq),
        in_specs=[pl.BlockSpec(memory_space=pltpu.SMEM), cur(512), cur(LANES), prev, cur(LANES), prev],
        out_specs=cur(512),
        out_shape=jax.ShapeDtypeStruct(qa.shape, BF16),
        compiler_params=_cparams(("arbitrary", "arbitrary")),
        name="swa_prompt",
    )(sink, qa, ka, ka, va, va)


ROW_UNROLL = 8


def _rows_loop(rows, load, stages, store):
    def body(g, carry):
        idx = [g * ROW_UNROLL + j for j in range(ROW_UNROLL)]
        vals = [load(r) for r in idx]
        for stage in stages:
            vals = [stage(v) for v in vals]
        for r, o in zip(idx, vals):
            store(r, o)
        return carry

    lax.fori_loop(0, rows // ROW_UNROLL, body, 0)


def _swa_decode_kernel(sink_ref, q_ref, kn_ref, vn_ref, wk_ref, wv_ref, o_ref, wko_ref, wvo_ref,
                       *, rows, new_len):
    ii = lax.broadcasted_iota(jnp.int32, (DEC_PAD, 2 * WINDOW), 0)
    jj = lax.broadcasted_iota(jnp.int32, (DEC_PAD, 2 * WINDOW), 1)
    mask = ((jj < WINDOW) & (jj > ii)) | ((jj >= WINDOW) & (jj - WINDOW <= ii))
    pad = jnp.zeros((WINDOW - DEC_PAD, LANES), F32)
    keep = lax.broadcasted_iota(jnp.int32, (WINDOW, LANES), 0) < WINDOW - new_len

    def slide(win, new):
        tail = jnp.concatenate([pad, new], axis=0)
        return jnp.where(keep, pltpu.roll(win, WINDOW - new_len, 0), pltpu.roll(tail, DEC_PAD - new_len, 0))

    def load(r):
        return q_ref[r], wk_ref[r], kn_ref[r], wv_ref[r], vn_ref[r]

    def scores(args):
        q, wk, kn, wv, vn = args
        kk = jnp.concatenate([wk, kn, pad], axis=0).astype(BF16)
        return _swa_scores(q, kk), wk, kn, wv, vn

    def finish(args):
        logits, wk, kn, wv, vn = args
        vv = jnp.concatenate([wv, vn, pad], axis=0).astype(BF16)
        return _swa_finish(logits, vv, mask, sink_ref).astype(BF16), slide(wk, kn), slide(wv, vn)

    def store(r, outs):
        o_ref[r], wko_ref[r], wvo_ref[r] = outs

    _rows_loop(rows, load, [scores, finish], store)


def _swa_decode(sink, qa, kn, vn, wk, wv, rows, new_len):
    nb = qa.shape[0]
    blk = lambda a, n: pl.BlockSpec((rows, a, n), lambda i: (i, 0, 0))
    return pl.pallas_call(
        functools.partial(_swa_decode_kernel, rows=rows, new_len=new_len),
        grid=(nb // rows,),
        in_specs=[pl.BlockSpec(memory_space=pltpu.SMEM), blk(DEC_PAD, 512), blk(DEC_PAD, LANES),
                  blk(DEC_PAD, LANES), blk(WINDOW, LANES), blk(WINDOW, LANES)],
        out_specs=[blk(DEC_PAD, 512), blk(WINDOW, LANES), blk(WINDOW, LANES)],
        out_shape=[jax.ShapeDtypeStruct(qa.shape, BF16), jax.ShapeDtypeStruct(wk.shape, F32),
                   jax.ShapeDtypeStruct(wv.shape, F32)],
        compiler_params=_cparams(("arbitrary",)),
        name="swa_decode",
    )(sink, qa, kn, vn, wk, wv)


def _gdn_gates(ba, alog, dtb, valid):
    beta = jax.nn.sigmoid(ba)
    g = -jnp.exp(alog) * _softplus(ba + dtb)
    if valid is not None:
        beta = jnp.where(valid, beta, 0.0)
        g = jnp.where(valid, g, 0.0)
    return beta, g


def _chunk_masks(chunk):
    sh = chunk.bit_length() - 1
    ri = lax.broadcasted_iota(jnp.int32, (LANES, LANES), 0)
    ci = lax.broadcasted_iota(jnp.int32, (LANES, LANES), 1)
    same = (ri >> sh) == (ci >> sh)
    return same, same & (ri >= ci), same & (ri > ci), ri == ci


def _gdn_cumsums(g_all, chunk):
    n = g_all.shape[0]
    pos = lax.broadcasted_iota(jnp.int32, g_all.shape, 0) & (chunk - 1)
    gcol = g_all
    step = 1
    while step < chunk:
        gcol = gcol + jnp.where(pos >= step, pltpu.roll(gcol, step, 0), 0.0)
        step *= 2
    gtot = jnp.where(pos == chunk - 1, gcol, 0.0)
    step = 1
    while step < chunk:
        gtot = gtot + jnp.where(pos + step < chunk, pltpu.roll(gtot, n - step, 0), 0.0)
        step *= 2
    return gcol, gcol.T, gtot


def _sibling_mask(ri, ci, lvl):
    rb = ri >> lvl
    return ((rb & 1) == 1) & ((ci >> lvl) == rb - 1)


def _gdn_phase_a(heads, chunk):
    _, tri, strict, _ = _chunk_masks(chunk)
    ri = lax.broadcasted_iota(jnp.int32, (LANES, LANES), 0)
    ci = lax.broadcasted_iota(jnp.int32, (LANES, LANES), 1)
    kbs = [k * beta for q, k, v, beta, gcol, grow, gtot in heads]
    kkqk = [_bdot_nt(jnp.concatenate([kb, hd[0]], axis=0), hd[1]) for kb, hd in zip(kbs, heads)]
    a, qk = [], []
    for r, (q, k, v, beta, gcol, grow, gtot) in zip(kkqk, heads):
        decay = jnp.exp(jnp.where(tri, gcol - grow, NEG))
        a.append(jnp.where(strict, r[:LANES] * decay, 0.0))
        qk.append(r[LANES:] * decay)
    n = [-jnp.where(_sibling_mask(ri, ci, 0), ah, 0.0) for ah in a]
    for lvl in range(1, chunk.bit_length() - 1):
        sib = _sibling_mask(ri, ci, lvl)
        al = [jnp.where(sib, ah, 0.0) for ah in a]
        x = [alh + _bdot(nh, alh) for alh, nh in zip(al, n)]
        n = [nh - (xh + _bdot(xh, nh)) for xh, nh in zip(x, n)]
    outs = []
    rhs = [jnp.concatenate([hd[2] * hd[3], kb * jnp.exp(hd[4])], axis=1) for kb, hd in zip(kbs, heads)]
    sol = [r + _bdot(nh, r) for r, nh in zip(rhs, n)]
    for s, qkh, (q, k, v, beta, gcol, grow, gtot) in zip(sol, qk, heads):
        outs.append((s[:, :B_DV], s[:, B_DV:], q * jnp.exp(gcol), k * jnp.exp(gtot - gcol), qkh, jnp.exp(gtot)))
    return outs


def _gdn_qkv_heads(qkv, h):
    q = qkv[:, h * B_DK:(h + 1) * B_DK]
    k = qkv[:, (B_HEADS + h) * B_DK:(B_HEADS + h + 1) * B_DK]
    v = qkv[:, (2 * B_HEADS + h) * B_DK:(2 * B_HEADS + h + 1) * B_DK]
    q = q * lax.rsqrt(jnp.sum(q * q, axis=-1, keepdims=True) + EPS) * (B_DK ** -0.5)
    k = k * lax.rsqrt(jnp.sum(k * k, axis=-1, keepdims=True) + EPS)
    return q, k, v


def _gdn_out(o, z, nw):
    return _rms(o, nw) * _silu(z)


def _pair_bd(x2):
    lo = lax.broadcasted_iota(jnp.int32, x2.shape, 1) < LANES
    return jnp.concatenate([jnp.where(lo, x2, 0.0), jnp.where(lo, 0.0, x2)], axis=0).astype(BF16)


def _quad_bd(x4):
    blk = lax.broadcasted_iota(jnp.int32, x4.shape, 1) >> (GDN_CHUNK.bit_length() - 1)
    return jnp.concatenate([jnp.where(blk == j, x4, 0.0) for j in range(PAIR // GDN_CHUNK)],
                           axis=0).astype(BF16)


def _pair_cols(x, l0, l1):
    lo = lax.broadcasted_iota(jnp.int32, (x.shape[0], PAIR), 1) < LANES
    return jnp.where(lo, x[:, l0:l0 + 1], x[:, l1:l1 + 1])


def _gdn_prompt_kernel(q_ref, k_ref, v_ref, ba_ref, z_ref, alog_ref, dtb_ref, nw_ref,
                       ob_ref, s_ref, *, nb, ntb):
    t = pl.program_id(0)
    npair = B_HEADS // 2

    @pl.when(t == 0)
    def _():
        s_ref[...] = jnp.zeros(s_ref.shape, F32)

    alog = alog_ref[...]
    dtb = dtb_ref[...]
    nw = nw_ref[...]
    nchunk = LANES // GDN_CHUNK
    _, tri, strict, _ = _chunk_masks(GDN_CHUNK)
    tri2 = jnp.concatenate([tri, tri], axis=1)
    strict2 = jnp.concatenate([strict, strict], axis=1)

    items = []
    blocks = [(b, tb) for b in range(nb) for tb in range(ntb)]
    assert len(blocks) * GATE_LANES <= LANES
    packed = None
    for j, (b, tb) in enumerate(blocks):
        blk = ba_ref[b, tb * LANES:(tb + 1) * LANES, :]
        packed = blk if j == 0 else packed + pltpu.roll(blk, GATE_LANES * j, 1)
    beta_all, g_all = _gdn_gates(packed, alog, dtb, None)
    gcol_all, grow_all, gtot_all = _gdn_cumsums(g_all, GDN_CHUNK)
    eg_all = jnp.exp(gcol_all)
    ed_all = jnp.exp(gtot_all - gcol_all)
    et_all = jnp.exp(gtot_all)
    for j, (b, tb) in enumerate(blocks):
        rsl = slice(tb * LANES, (tb + 1) * LANES)
        for p in range(npair):
            h0, h1 = GATE_LANES * j + 2 * p, GATE_LANES * j + 2 * p + 1
            g0, g1 = B_HEADS + h0, B_HEADS + h1
            psl = slice(p * PAIR, (p + 1) * PAIR)
            q2 = q_ref[b, rsl, psl]
            k2 = k_ref[b, rsl, psl]
            v2 = v_ref[b, rsl, psl]
            beta2 = _pair_cols(beta_all, h0, h1)
            grow2 = jnp.concatenate([grow_all[g0:g0 + 1, :], grow_all[g1:g1 + 1, :]], axis=1)
            decay2 = jnp.exp(jnp.where(tri2, _pair_cols(gcol_all, g0, g1) - grow2, NEG))
            kb2 = k2 * beta2
            items.append(dict(
                b=b, tb=tb, p=p, rsl=rsl, decay2=decay2, kb2=kb2, v2b=v2 * beta2,
                kbe2=kb2 * _pair_cols(eg_all, g0, g1),
                lhs=jnp.concatenate([kb2, q2], axis=0).astype(BF16),
                kbd=_pair_bd(k2),
                qe2=(q2 * _pair_cols(eg_all, g0, g1)).astype(BF16),
                kd2=(k2 * _pair_cols(ed_all, g0, g1)).astype(BF16),
                et2=_pair_cols(et_all, g0, g1)))

    for it in items:
        kkqk = lax.dot_general(it["lhs"], it["kbd"], (((1,), (1,)), ((), ())),
                               preferred_element_type=F32)
        it["a2"] = jnp.where(strict2, kkqk[:LANES] * it["decay2"], 0.0)
        it["qk2"] = (kkqk[LANES:] * it["decay2"]).astype(BF16)
    qlane = lax.broadcasted_iota(jnp.int32, (GDN_CHUNK, PAIR), 1)
    qrow = lax.broadcasted_iota(jnp.int32, (GDN_CHUNK, PAIR), 0)
    qcol = qlane & (GDN_CHUNK - 1)
    top = (qlane & GDN_CHUNK) == 0
    for it in items:
        it["a64"] = it["a2"][:GDN_CHUNK] + it["a2"][GDN_CHUNK:]
        it["n"] = -jnp.where(_sibling_mask(qrow, qcol, 0), it["a64"], 0.0)
    for lvl in range(1, GDN_CHUNK.bit_length() - 1):
        sib = _sibling_mask(qrow, qcol, lvl)
        for it in items:
            it["al"] = jnp.where(sib, it["a64"], 0.0)
            it["x"] = it["al"] + jnp.dot(it["n"].astype(BF16), _quad_bd(it["al"]), preferred_element_type=F32)
        for it in items:
            x = it["x"]
            it["n"] = it["n"] - (x + jnp.dot(x.astype(BF16), _quad_bd(it["n"]), preferred_element_type=F32))
    for it in items:
        n64 = it["n"]
        it["n"] = jnp.concatenate([jnp.where(top, n64, 0.0), jnp.where(top, 0.0, n64)], axis=0)
    for it in items:
        us, ws = [], []
        for s in range(2):
            sl = slice(s * LANES, (s + 1) * LANES)
            rhs = jnp.concatenate([it["v2b"][:, sl], it["kbe2"][:, sl]], axis=1)
            sol = rhs + _bdot(it["n"][:, sl], rhs)
            us.append(sol[:, :B_DV])
            ws.append(sol[:, B_DV:])
        it["u2"] = jnp.concatenate(us, axis=1)
        it["w2"] = jnp.concatenate(ws, axis=1).astype(BF16)

    nchain = nb * npair
    state = [[s_ref[ch // npair, 2 * (ch % npair) + s] for s in range(2)] for ch in range(nchain)]
    zeros16 = jnp.zeros((LANES, LANES), BF16)
    outs = {}
    for tb in range(ntb):
        cur = [(it["b"] * npair + it["p"], i, it) for i, it in enumerate(items) if it["tb"] == tb]
        for c in range(nchunk):
            rows = slice(c * GDN_CHUNK, (c + 1) * GDN_CHUNK)
            rs = []
            for ch, _, it in cur:
                sa, sb = (s.astype(BF16) for s in state[ch])
                sbd = jnp.concatenate([jnp.concatenate([sa, zeros16], axis=1),
                                       jnp.concatenate([zeros16, sb], axis=1)], axis=0)
                rs.append(jnp.dot(jnp.concatenate([it["w2"][rows], it["qe2"][rows]], axis=0), sbd,
                                  preferred_element_type=F32))
            for (ch, i, it), r in zip(cur, rs):
                vn2 = it["u2"][rows] - r[:GDN_CHUNK]
                vnb = vn2.astype(BF16)
                vt = _pair_bd(jnp.concatenate([vn2] * nchunk, axis=0))
                outs[(i, c)] = r[GDN_CHUNK:] + jnp.dot(it["qk2"][rows], vt, preferred_element_type=F32)
                for s in range(2):
                    hl = slice(s * LANES, (s + 1) * LANES)
                    upd = lax.dot_general(it["kd2"][rows, hl], vnb[:, hl], (((0,), (0,)), ((), ())),
                                          preferred_element_type=F32)
                    state[ch][s] = state[ch][s] * it["et2"][c * GDN_CHUNK:c * GDN_CHUNK + 1, hl] + upd
    for ch in range(nchain):
        for s in range(2):
            s_ref[ch // npair, 2 * (ch % npair) + s] = state[ch][s]

    for i, it in enumerate(items):
        o2 = jnp.concatenate([outs[(i, c)] for c in range(nchunk)], axis=0)
        for s in range(2):
            h = 2 * it["p"] + s
            sl = slice(h * B_DV, (h + 1) * B_DV)
            o = o2[:, s * LANES:(s + 1) * LANES]
            ob_ref[it["b"], it["rsl"], sl] = _gdn_out(o, z_ref[it["b"], it["rsl"], sl], nw).astype(BF16)


GDN_TOKEN_BLOCKS = 2


def _gdn_prompt(q, k, v, ba, z, alog, dtb, nw):
    batch, seq, _ = q.shape
    tm = GDN_TOKEN_BLOCKS * LANES
    tok = lambda n: pl.BlockSpec((batch, tm, n), lambda t: (0, t, 0))
    state = pl.BlockSpec((batch, B_HEADS, B_DK, B_DV), lambda t: (0, 0, 0, 0))
    return pl.pallas_call(
        functools.partial(_gdn_prompt_kernel, nb=batch, ntb=GDN_TOKEN_BLOCKS),
        grid=(seq // tm,),
        in_specs=[tok(512), tok(512), tok(512), tok(LANES), tok(512),
                  _const_spec((1, LANES)), _const_spec((1, LANES)), _const_spec((1, B_DV))],
        out_specs=[tok(512), state],
        out_shape=[jax.ShapeDtypeStruct((batch, seq, 512), BF16),
                   jax.ShapeDtypeStruct((batch, B_HEADS, B_DK, B_DV), F32)],
        compiler_params=_cparams(("arbitrary",)),
        name="gdn_prompt",
    )(q, k, v, ba, z, alog, dtb, nw)


def _gdn_decode_kernel(raw_ref, hist_ref, ba_ref, z_ref, rec_ref, cw_ref, alog_ref, dtb_ref, nw_ref,
                       ob_ref, s_ref, buf_ref, u_s, w_s, qe_s, kd_s, qk_s, eg_s, o_s, *, rows, valid_len):
    buf_ref[:, 0:DEC_PAD, :] = hist_ref[...]
    buf_ref[:, DEC_PAD:2 * DEC_PAD, :] = raw_ref[...]
    conv = None
    for i in range(CONV_W):
        off = DEC_PAD - (CONV_W - 1) + i
        term = buf_ref[:, off:off + DEC_PAD, :] * cw_ref[i:i + 1, :]
        conv = term if conv is None else conv + term
    qkv = _silu(conv.reshape(rows * DEC_PAD, B_CONV_CH))
    tok = lax.broadcasted_iota(jnp.int32, (LANES, LANES), 0) & (DEC_PAD - 1)
    beta_all, g_all = _gdn_gates(ba_ref[...], alog_ref[...], dtb_ref[...], tok < valid_len)
    gcol_all, grow_all, gtot_all = _gdn_cumsums(g_all, DEC_PAD)
    heads = []
    for h in range(B_HEADS):
        gl = B_HEADS + h
        heads.append(_gdn_qkv_heads(qkv, h) + (beta_all[:, h:h + 1], gcol_all[:, gl:gl + 1],
                                               grow_all[gl:gl + 1, :], gtot_all[:, gl:gl + 1]))
    for h, (u, w, qe, kd, qk, egt) in enumerate(_gdn_phase_a(heads, DEC_PAD)):
        u_s[h] = u
        w_s[h] = w
        qe_s[h] = qe
        kd_s[h] = kd
        qk_s[h] = qk
        eg_s[h] = jnp.broadcast_to(egt, (LANES, LANES))

    def load(r):
        r0 = pl.multiple_of(r * DEC_PAD, DEC_PAD)
        rr = pl.ds(r0, DEC_PAD)
        return [(rec_ref[r, h], w_s[h, rr, :], qe_s[h, rr, :], u_s[h, rr, :], qk_s[h, rr, :],
                 kd_s[h, rr, :], eg_s[h, pl.ds(r0, 1), :]) for h in range(B_HEADS)]

    def read_state(heads):
        return [(_bdot(jnp.concatenate([w, qe], axis=0), s), s, u, qk, kd, eg)
                for s, w, qe, u, qk, kd, eg in heads]

    def update(heads):
        outs = []
        for res, s, u, qk, kd, eg in heads:
            v_new = u - res[:DEC_PAD]
            vt = jnp.concatenate([v_new] * (LANES // DEC_PAD), axis=0)
            outs.append((res[DEC_PAD:] + _bdot(qk, vt), s * eg + _bdot_tn(kd, v_new)))
        return outs

    def store(r, outs):
        rr = pl.ds(pl.multiple_of(r * DEC_PAD, DEC_PAD), DEC_PAD)
        for h, (o, s_new) in enumerate(outs):
            o_s[h, rr, :] = o
            s_ref[r, h] = s_new

    _rows_loop(rows, load, [read_state, update], store)
    nw = nw_ref[...]
    for h in range(B_HEADS):
        sl = slice(h * B_DV, (h + 1) * B_DV)
        ob_ref[:, sl] = _gdn_out(o_s[h], z_ref[:, sl], nw).astype(BF16)


def _gdn_decode(raw, histp, ba, z, rec, cw, alog, dtb, nw, rows, valid_len):
    nb = raw.shape[0]
    flat = rows * DEC_PAD
    assert flat == LANES
    sq = lambda: pltpu.VMEM((B_HEADS, LANES, LANES), F32)
    return pl.pallas_call(
        functools.partial(_gdn_decode_kernel, rows=rows, valid_len=valid_len),
        grid=(nb // rows,),
        in_specs=[pl.BlockSpec((rows, DEC_PAD, B_CONV_CH), lambda i: (i, 0, 0)),
                  pl.BlockSpec((rows, DEC_PAD, B_CONV_CH), lambda i: (i, 0, 0)),
                  pl.BlockSpec((flat, LANES), lambda i: (i, 0)),
                  pl.BlockSpec((flat, 512), lambda i: (i, 0)),
                  pl.BlockSpec((rows, B_HEADS, B_DK, B_DV), lambda i: (i, 0, 0, 0)),
                  _const_spec((CONV_W, B_CONV_CH)), _const_spec((1, LANES)), _const_spec((1, LANES)),
                  _const_spec((1, B_DV))],
        out_specs=[pl.BlockSpec((flat, 512), lambda i: (i, 0)),
                   pl.BlockSpec((rows, B_HEADS, B_DK, B_DV), lambda i: (i, 0, 0, 0))],
        out_shape=[jax.ShapeDtypeStruct((nb * DEC_PAD, 512), BF16),
                   jax.ShapeDtypeStruct(rec.shape, F32)],
        scratch_shapes=[pltpu.VMEM((rows, 2 * DEC_PAD, B_CONV_CH), F32)] + [sq() for _ in range(7)],
        compiler_params=_cparams(("arbitrary",)),
        name="gdn_decode",
    )(raw, histp, ba, z, rec, cw, alog, dtb, nw)


def _memkv_kernel(m_ref, g_ref, w_ref, k_ref, v_ref):
    h = _rms(m_ref[...], g_ref[...]).astype(BF16)
    n = C_HEADS * C_HD
    k_ref[...] = jnp.dot(h, w_ref[:, :n], preferred_element_type=F32)
    v_ref[...] = jnp.dot(h, w_ref[:, n:], preferred_element_type=F32)


def _memkv(mem, gain, w):
    t = mem.shape[0]
    tm = 512
    n = C_HEADS * C_HD
    return pl.pallas_call(
        _memkv_kernel,
        grid=(t // tm,),
        in_specs=[pl.BlockSpec((tm, D_MODEL), lambda i: (i, 0)), _const_spec((1, D_MODEL)),
                  _const_spec((D_MODEL, 2 * n))],
        out_specs=[pl.BlockSpec((tm, n), lambda i: (i, 0))] * 2,
        out_shape=[jax.ShapeDtypeStruct((t, n), F32)] * 2,
        compiler_params=_cparams(("arbitrary",)),
        name="memkv",
    )(mem, gain, w)


def _softmax_rows(logits):
    m = jnp.max(logits, axis=-1, keepdims=True)
    e = jnp.exp(logits - m)
    return e, 1.0 / jnp.sum(e, axis=-1, keepdims=True)


def _memattn_prompt_kernel(q_ref, k_ref, v_ref, o_ref):
    hs = lambda h: slice(h * C_HD, (h + 1) * C_HD)
    scores = lambda h: _bdot_nt(q_ref[:, hs(h)], k_ref[:, hs(h)])
    logits = scores(0)
    for h in range(C_HEADS):
        nxt = scores(h + 1) if h + 1 < C_HEADS else None
        e, inv = _softmax_rows(logits * (C_HD ** -0.5))
        o_ref[:, hs(h)] = (_bdot(e, v_ref[:, hs(h)]) * inv).astype(BF16)
        logits = nxt


def _memattn_prompt(qc, mk, mv, batch, seq, tm):
    nq = seq // tm
    n = C_HEADS * C_HD
    cur = pl.BlockSpec((tm, n), lambda b, i: (b * nq + i, 0))
    mem = pl.BlockSpec((N_MEM, n), lambda b, i: (b, 0))
    return pl.pallas_call(
        _memattn_prompt_kernel,
        grid=(batch, nq),
        in_specs=[cur, mem, mem],
        out_specs=cur,
        out_shape=jax.ShapeDtypeStruct(qc.shape, BF16),
        compiler_params=_cparams(("arbitrary", "arbitrary")),
        name="memattn_prompt",
    )(qc, mk, mv)


def _memattn_decode_kernel(q_ref, k_ref, v_ref, o_ref, *, rows):
    nk = N_MEM * C_HEADS
    col = lax.broadcasted_iota(jnp.int32, (C_HEADS * DEC_PAD, nk), 1)
    row = lax.broadcasted_iota(jnp.int32, (C_HEADS * DEC_PAD, nk), 0)
    own = (col & (C_HEADS - 1)) == (row >> (DEC_PAD.bit_length() - 1))

    def load(r):
        return q_ref[r], k_ref[r], v_ref[r]

    def scores(args):
        q, k, v = args
        q = q.astype(F32)
        lhs = jnp.concatenate([q[:, h * C_HD:(h + 1) * C_HD] for h in range(C_HEADS)], axis=0)
        return _bdot_nt(lhs, k), v

    def finish(args):
        logits, v = args
        e, inv = _softmax_rows(jnp.where(own, logits * (C_HD ** -0.5), NEG))
        pv = _bdot(e, v) * inv
        return jnp.concatenate([pv[h * DEC_PAD:(h + 1) * DEC_PAD] for h in range(C_HEADS)],
                               axis=1).astype(BF16)

    def store(r, o):
        o_ref[r] = o

    _rows_loop(rows, load, [scores, finish], store)


def _memattn_decode(qc, ck, cv, rows):
    nb = qc.shape[0]
    n = C_HEADS * C_HD
    blk = pl.BlockSpec((rows, DEC_PAD, n), lambda i: (i, 0, 0))
    cache = pl.BlockSpec((rows, N_MEM * C_HEADS, C_HD), lambda i: (i, 0, 0))
    return pl.pallas_call(
        functools.partial(_memattn_decode_kernel, rows=rows),
        grid=(nb // rows,),
        in_specs=[blk, cache, cache],
        out_specs=blk,
        out_shape=jax.ShapeDtypeStruct(qc.shape, BF16),
        compiler_params=_cparams(("arbitrary",)),
        name="memattn_decode",
    )(qc, ck, cv)


def _merge_kernel(x_ref, oa_ref, ob_ref, oc_ref, gpre_ref, wga_ref, wgb_ref, wgc_ref, wb_ref, wo_ref,
                  gpost_ref, gfpre_ref, x1_ref, h2_ref):
    x = x_ref[...]
    ups = [jnp.dot(o_ref[...], wb_ref[n], preferred_element_type=F32)
           for n, o_ref in enumerate((oa_ref, ob_ref, oc_ref))]
    h = _rms(x, gpre_ref[...]).astype(BF16)
    mix = None
    for n, wg_ref in enumerate((wga_ref, wgb_ref, wgc_ref)):
        gate = jax.nn.sigmoid(jnp.dot(h, wg_ref[...], preferred_element_type=F32))
        mix = gate * ups[n] if mix is None else mix + gate * ups[n]
    x1 = x + _rms(_bdot(mix, wo_ref[...]), gpost_ref[...])
    x1_ref[...] = x1
    h2_ref[...] = _rms(x1, gfpre_ref[...]).astype(BF16)


FF_SPLIT = 6 * PAIR


def _ffn_kernel(x1_ref, h2_ref, wfi_ref, wfo_ref, gfpost_ref, y_ref):
    h2 = h2_ref[...]
    f = None
    halves = []
    for a, b in ((0, FF_SPLIT), (FF_SPLIT, D_FF)):
        gt = jnp.dot(h2, wfi_ref[:, a:b], preferred_element_type=F32)
        uf = jnp.dot(h2, wfi_ref[:, D_FF + a:D_FF + b], preferred_element_type=F32)
        halves.append((a, b, gt, uf))
    for a, b, gt, uf in halves:
        part = _bdot(_silu(gt) * uf, wfo_ref[a:b, :])
        f = part if f is None else f + part
    y_ref[...] = x1_ref[...] + _rms(f, gfpost_ref[...])


def _merge_weight_specs():
    vec = _const_spec((1, D_MODEL))
    return ([vec] + [pl.BlockSpec((D_MODEL, D_MODEL), functools.partial(lambda n, *_: (0, n), n + 1),
                                  pipeline_mode=pl.Buffered(1)) for n in range(N_BRANCH)]
            + [_const_spec((N_BRANCH, BRANCH_W, D_MODEL)), _const_spec((D_MODEL, D_MODEL)), vec, vec])


def _merge_streamed(x, oa, ob, oc, gpre, wtail, wb, wo, gpost, gfpre, tm):
    t = x.shape[0]
    deep = lambda n: pl.BlockSpec((tm, n), lambda i: (i, 0), pipeline_mode=pl.Buffered(3))
    row = lambda n: pl.BlockSpec((tm, n), lambda i: (i, 0))

    def outer(x_hbm, oa_hbm, ob_hbm, oc_hbm, gpre_ref, wtail_ref, wb_ref, wo_ref, gpost_ref, gfpre_ref,
              x1_hbm, h2_hbm):
        gates = [wtail_ref.at[:, (n + 1) * D_MODEL:(n + 2) * D_MODEL] for n in range(N_BRANCH)]

        def body(x_ref, oa_ref, ob_ref, oc_ref, x1_ref, h2_ref):
            _merge_kernel(x_ref, oa_ref, ob_ref, oc_ref, gpre_ref, *gates, wb_ref, wo_ref,
                          gpost_ref, gfpre_ref, x1_ref, h2_ref)

        pltpu.emit_pipeline(
            body, grid=(t // tm,),
            in_specs=[deep(D_MODEL), deep(512), deep(512), deep(512)],
            out_specs=[row(D_MODEL), row(D_MODEL)],
        )(x_hbm, oa_hbm, ob_hbm, oc_hbm, x1_hbm, h2_hbm)

    hbm = pl.BlockSpec(memory_space=pl.ANY)
    vmem = pl.BlockSpec(memory_space=pltpu.VMEM)
    return pl.pallas_call(
        outer,
        in_specs=[hbm] * 4 + [vmem] * 6,
        out_specs=[hbm, hbm],
        out_shape=[jax.ShapeDtypeStruct(x.shape, F32), jax.ShapeDtypeStruct(x.shape, BF16)],
        compiler_params=pltpu.CompilerParams(vmem_limit_bytes=VMEM_LIMIT),
        name="merge_streamed",
    )(x, oa, ob, oc, gpre, wtail, wb, wo, gpost, gfpre)


def _merge(x, oa, ob, oc, gpre, wtail, wb, wo, gpost, gfpre, tm):
    t = x.shape[0]
    row = lambda n: pl.BlockSpec((tm, n), lambda i: (i, 0))
    if t // tm >= 8:
        return _merge_streamed(x, oa, ob, oc, gpre, wtail, wb, wo, gpost, gfpre, tm)
    return pl.pallas_call(
        _merge_kernel,
        grid=(t // tm,),
        in_specs=[row(D_MODEL), row(512), row(512), row(512)] + _merge_weight_specs(),
        out_specs=[row(D_MODEL), row(D_MODEL)],
        out_shape=[jax.ShapeDtypeStruct(x.shape, F32), jax.ShapeDtypeStruct(x.shape, BF16)],
        compiler_params=_cparams(("arbitrary",)),
        name="merge",
    )(x, oa, ob, oc, gpre, wtail, wtail, wtail, wb, wo, gpost, gfpre)


def _ffn(x1, h2, wfi, wfo, gfpost, tm):
    t = x1.shape[0]
    vec = _const_spec((1, D_MODEL))
    frow = pl.BlockSpec((tm, D_MODEL), lambda i: (i, 0))
    return pl.pallas_call(
        _ffn_kernel,
        grid=(t // tm,),
        in_specs=[frow, frow, _const_spec((D_MODEL, 2 * D_FF)),
                  _const_spec((D_FF, D_MODEL)), vec],
        out_specs=frow,
        out_shape=jax.ShapeDtypeStruct(x1.shape, F32),
        compiler_params=_cparams(("arbitrary",)),
        name="ffn",
    )(x1, h2, wfi, wfo, gfpost)


def _rope_tables(pos):
    half = A_HD // 2
    inv = ROPE_THETA ** (-jnp.arange(half, dtype=F32) / half)
    ang = pos.astype(F32)[:, None] * inv[None, :]
    cos, sin = jnp.cos(ang), jnp.sin(ang)
    cos = jnp.concatenate([cos, cos], axis=-1)
    sin = jnp.concatenate([-sin, sin], axis=-1)
    return jnp.tile(cos, (1, LANES // A_HD)), jnp.tile(sin, (1, LANES // A_HD))


def _lane_row(vals, offset):
    grp = jnp.zeros((GATE_LANES,), F32).at[offset:offset + vals.shape[0]].set(vals.astype(F32))
    return jnp.tile(grp, LANES // GATE_LANES).reshape(1, LANES)


def kernel(x_prompt, x_sample, mem_prompt, state_win_k, state_win_v, state_conv, state_rec,
           cache_mem_k, cache_mem_v, ln_mix_pre, w_in, attn_sink, gdn_conv_w, gdn_a_log,
           gdn_dt_bias, gdn_norm_w, ln_mem, w_mem_kv, w_branch, w_out, ln_mix_post,
           ln_ffn_pre, w_ffn_in, w_ffn_out, ln_ffn_post):
    bp, lp, _ = x_prompt.shape
    bs, ls, _ = x_sample.shape

    sizes = [512, 128, 128, B_CONV_CH, B_HEADS, B_HEADS, 512, 512, N_BRANCH * D_MODEL]
    o = np.cumsum([0] + sizes)
    hperm = np.concatenate([np.r_[j * A_HD:(j + 1) * A_HD, (j + 4) * A_HD:(j + 5) * A_HD] for j in range(4)])
    ws = (w_in[:, hperm].astype(BF16), w_in[:, o[1]:o[4]].astype(BF16), w_in[:, o[6]:o[9]].astype(BF16),
          jnp.pad(w_in[:, o[4]:o[6]], ((0, 0), (0, LANES - 2 * B_HEADS))).astype(BF16))
    wb = jnp.concatenate([w_branch[0:1][:, hperm], w_branch[1:]], axis=0).astype(BF16)
    wo = w_out.astype(BF16)
    wfi = w_ffn_in.astype(BF16)
    wfo = w_ffn_out.astype(BF16)
    wmem = w_mem_kv.astype(BF16)
    sink = attn_sink.astype(F32)[np.array([0, 4, 1, 5, 2, 6, 3, 7])]
    vec = lambda g: g.astype(F32).reshape(1, -1)
    alog = _lane_row(gdn_a_log, B_HEADS)
    dtb = _lane_row(gdn_dt_bias, B_HEADS)
    cw = gdn_conv_w.astype(F32)
    nw = vec(gdn_norm_w)

    merge_w = (vec(ln_mix_pre), ws[2], wb, wo, vec(ln_mix_post), vec(ln_ffn_pre))
    ffn = lambda x1, h2: _ffn(x1, h2, wfi, wfo, vec(ln_ffn_post), min(1024, x1.shape[0]))

    tp = bp * lp
    xp = x_prompt.reshape(tp, D_MODEL)
    cos_p, sin_p = _rope_tables(jnp.arange(lp, dtype=jnp.int32))
    qa, ka, va, z, qc, ba, qn, kn, vv, tail = _proj(xp, vec(ln_mix_pre), ws, cos_p, sin_p, 1024, cw, lp)
    b3 = lambda a: a.reshape(bp, lp, a.shape[-1])
    ob, rec_p = _gdn_prompt(b3(qn), b3(kn), b3(vv), b3(ba), b3(z), alog, dtb, nw)
    mk, mv = _memkv(mem_prompt.reshape(bp * N_MEM, D_MODEL), vec(ln_mem), wmem)
    oa = _swa_prompt(sink, qa, ka, va, bp, lp, 2048)
    oc = _memattn_prompt(qc, mk, mv, bp, lp, 2048)
    x1, h2 = _merge(xp, oa, ob.reshape(tp, 512), oc, *merge_w, 256)
    y_p = ffn(x1, h2).reshape(bp, lp, D_MODEL)
    wk_p = ka.reshape(bp, lp, LANES)[:, -WINDOW:].reshape(bp, WINDOW, A_KV, A_HD)
    wv_p = va.reshape(bp, lp, LANES)[:, -WINDOW:].reshape(bp, WINDOW, A_KV, A_HD)
    conv_p = tail[:, -(CONV_W - 1):]
    mem_k_p = mk.reshape(bp, N_MEM, C_HEADS, C_HD)
    mem_v_p = mv.reshape(bp, N_MEM, C_HEADS, C_HD)

    ts = bs * DEC_PAD
    xs = jnp.pad(x_sample, ((0, 0), (0, DEC_PAD - ls), (0, 0))).reshape(ts, D_MODEL)
    cos_s, sin_s = _rope_tables(PAST_LEN + jnp.arange(DEC_PAD, dtype=jnp.int32))
    cos_s, sin_s = jnp.tile(cos_s, (bs, 1)), jnp.tile(sin_s, (bs, 1))
    qa, ka, va, z, qc, ba, qkv = _proj(xs, vec(ln_mix_pre), ws, cos_s, sin_s, 512)
    r3 = lambda a: a.reshape(bs, DEC_PAD, a.shape[-1])
    oa, wk_s, wv_s = _swa_decode(sink, r3(qa), r3(ka), r3(va), state_win_k.reshape(bs, WINDOW, LANES),
                                 state_win_v.reshape(bs, WINDOW, LANES), 16, ls)
    oa = oa.reshape(ts, 512)
    wk_s = wk_s.reshape(state_win_k.shape)
    wv_s = wv_s.reshape(state_win_v.shape)
    histp = jnp.pad(state_conv, ((0, 0), (DEC_PAD - (CONV_W - 1), 0), (0, 0)))
    ob, rec_s = _gdn_decode(r3(qkv), histp, ba, z, state_rec, cw, alog, dtb, nw,
                            LANES // DEC_PAD, ls)
    oc = _memattn_decode(r3(qc), cache_mem_k.reshape(bs, N_MEM * C_HEADS, C_HD),
                         cache_mem_v.reshape(bs, N_MEM * C_HEADS, C_HD), 8).reshape(ts, 512)
    real = lambda a: a.reshape(bs, DEC_PAD, 512)[:, :ls].reshape(bs * ls, 512)
    x1, h2 = _merge(x_sample.reshape(bs * ls, D_MODEL), real(oa), real(ob), real(oc), *merge_w, 256)
    y_s = ffn(x1, h2).reshape(bs, ls, D_MODEL)
    conv_s = r3(qkv)[:, ls - (CONV_W - 1):ls]

    return (y_p, y_s, wk_p, wv_p, conv_p, rec_p, mem_k_p, mem_v_p, wk_s, wv_s, conv_s, rec_s)
```
